```python
import jax, jax.numpy as jnp
from jax import lax
import numpy as np

D_MODEL = 1024
BATCH = 8
SEQ = 4096
DEPTH = 2

N_A = DEPTH // 2
N_B = DEPTH - N_A
HEAD_DIM = 64
N_MAIN_HEADS = 12
MAIN_WIDTH = N_MAIN_HEADS * HEAD_DIM
N_MEM_HEADS = 4
MEM_WIDTH = N_MEM_HEADS * HEAD_DIM
N_MEM = 256
MIX_WIDTH = MAIN_WIDTH + MEM_WIDTH
RET_CHUNK = 128
ROPE_THETA = 10000.0
Q_LORA = 384
KV_LORA = 256
QK_NOPE = 64
QK_ROPE = 32
QK_HEAD = QK_NOPE + QK_ROPE
V_HEAD = 64
Q_BLOCK = 128
N_GROUPS = 4
EXPERTS_PER_GROUP = 8
N_EXPERTS = N_GROUPS * EXPERTS_PER_GROUP
TOP_K = 2
D_EXPERT = 256
MOE_BLOCK = 128
EPS = 1e-6
A_IN_WIDTH = 4 * MAIN_WIDTH + MEM_WIDTH
B_IN_WIDTH = Q_LORA + MEM_WIDTH

kernel_name = "yoco_retention_mla_hier_moe"


def rms_norm(x, g):
    xf = x.astype(jnp.float32)
    y = xf * lax.rsqrt(jnp.mean(xf * xf, axis=-1, keepdims=True) + EPS)
    return (y * g.astype(jnp.float32)).astype(x.dtype)


def rope(x, pos):
    half = x.shape[-1] // 2
    inv = ROPE_THETA ** (-jnp.arange(half, dtype=jnp.float32) / half)
    ang = pos.astype(jnp.float32)[..., None] * inv
    cos = jnp.cos(ang)[:, :, None, :]
    sin = jnp.sin(ang)[:, :, None, :]
    x1 = x[..., :half].astype(jnp.float32)
    x2 = x[..., half:].astype(jnp.float32)
    out = jnp.concatenate([x1 * cos - x2 * sin, x2 * cos + x1 * sin], axis=-1)
    return out.astype(x.dtype)


def retention_chunkwise(q, k, v):
    B, S, H, d = q.shape
    C = RET_CHUNK
    N = S // C
    f32 = jnp.float32
    log_g = jnp.log1p(-jnp.exp2(-5.0 - jnp.arange(H, dtype=f32)))
    qc = q.astype(f32).reshape(B, N, C, H, d)
    kc = k.astype(f32).reshape(B, N, C, H, d) * (d ** -0.5)
    vc = v.astype(f32).reshape(B, N, C, H, d)
    idx = jnp.arange(C, dtype=f32)
    rel = idx[:, None] - idx[None, :]
    decay_in = jnp.where(rel[None] >= 0,
                         jnp.exp(log_g[:, None, None] * jnp.maximum(rel, 0.0)[None]), 0.0)
    scores = jnp.einsum('bnihd,bnjhd->bnhij', qc, kc) * decay_in
    inner = jnp.einsum('bnhij,bnjhe->bnihe', scores, vc)
    k_dec = kc * jnp.exp(log_g[None, :] * (C - 1.0 - idx)[:, None])[:, :, None]
    U = jnp.einsum('bnjhd,bnjhe->nbhde', k_dec, vc)
    chunk_decay = jnp.exp(log_g * C)[None, :, None, None]

    def step(R, u):
        return chunk_decay * R + u, R

    _, R_prev = lax.scan(step, jnp.zeros((B, H, d, d), f32), U)
    q_dec = qc * jnp.exp(log_g[None, :] * (idx + 1.0)[:, None])[:, :, None]
    cross = jnp.einsum('bnihd,nbhde->bnihe', q_dec, R_prev)
    return (inner + cross).reshape(B, S, H, d)


def head_group_norm(y, g):
    B, S, H, d = y.shape
    mu = jnp.mean(y, axis=-1, keepdims=True)
    yc = y - mu
    var = jnp.mean(yc * yc, axis=-1, keepdims=True)
    return (yc * lax.rsqrt(var + EPS)).reshape(B, S, H * d) * g.astype(jnp.float32)


def memory_attention(qm, mem, w_mem_kv, q_g, k_g):
    B, S, _ = qm.shape
    M = mem.shape[1]
    q = rms_norm(qm.reshape(B, S, N_MEM_HEADS, HEAD_DIM), q_g)
    kv = mem @ w_mem_kv
    k = rms_norm(kv[..., :MEM_WIDTH].reshape(B, M, N_MEM_HEADS, HEAD_DIM), k_g)
    v = kv[..., MEM_WIDTH:].reshape(B, M, N_MEM_HEADS, HEAD_DIM)
    s = jnp.einsum('bshd,bmhd->bhsm', q, k).astype(jnp.float32) * (HEAD_DIM ** -0.5)
    p = jax.nn.softmax(s, axis=-1).astype(v.dtype)
    o = jnp.einsum('bhsm,bmhd->bshd', p, v)
    return o.reshape(B, S, MEM_WIDTH)


def shared_latent_kv(x, kv_ln, w_dkv, kv_lora_g, w_ukv, k_g, pos):
    B, S, _ = x.shape
    h = rms_norm(x, kv_ln)
    ckr = h @ w_dkv
    c_kv = rms_norm(ckr[..., :KV_LORA], kv_lora_g)
    k_rope = ckr[..., KV_LORA:]
    kv = (c_kv @ w_ukv).reshape(B, S, N_MAIN_HEADS, QK_NOPE + V_HEAD)
    k_nope = kv[..., :QK_NOPE]
    v = kv[..., QK_NOPE:]
    k = jnp.concatenate(
        [k_nope, jnp.broadcast_to(k_rope[:, :, None, :], (B, S, N_MAIN_HEADS, QK_ROPE))], axis=-1)
    k = rms_norm(k, k_g)
    k = jnp.concatenate([k[..., :QK_NOPE], rope(k[..., QK_NOPE:], pos)], axis=-1)
    return k, v


def mla_attention(c_q, q_lora_g, w_uq, q_g, k, v, pos):
    B, S, _ = c_q.shape
    q = (rms_norm(c_q, q_lora_g) @ w_uq).reshape(B, S, N_MAIN_HEADS, QK_HEAD)
    q = rms_norm(q, q_g)
    q = jnp.concatenate([q[..., :QK_NOPE], rope(q[..., QK_NOPE:], pos)], axis=-1)
    nb = S // Q_BLOCK
    qb = q.reshape(B, nb, Q_BLOCK, N_MAIN_HEADS, QK_HEAD).transpose(1, 0, 2, 3, 4)
    k_idx = jnp.arange(S)
    scale = QK_HEAD ** -0.5

    def block(args):
        qi, bi = args
        s = jnp.einsum('bqhd,bkhd->bhqk', qi, k).astype(jnp.float32) * scale
        q_idx = bi * Q_BLOCK + jnp.arange(Q_BLOCK)
        s = jnp.where(k_idx[None, :] <= q_idx[:, None], s, -jnp.inf)
        p = jax.nn.softmax(s, axis=-1).astype(v.dtype)
        return jnp.einsum('bhqk,bkhd->bqhd', p, v)

    o = lax.map(block, (qb, jnp.arange(nb)))
    return o.transpose(1, 0, 2, 3, 4).reshape(B, S, N_MAIN_HEADS * V_HEAD)


def hier_moe(h, w_grp, b_grp, w_exp, b_exp, w_gate, w_up, w_down):
    B, S, D = h.shape
    T = B * S
    xf = h.reshape(T, D)
    g_prob = jax.nn.softmax((xf @ w_grp).astype(jnp.float32) + b_grp.astype(jnp.float32), axis=-1)
    grp = jnp.argmax(g_prob, axis=-1)
    p_grp = jnp.max(g_prob, axis=-1, keepdims=True)
    e_logits = ((xf @ w_exp).astype(jnp.float32) + b_exp.astype(jnp.float32)).reshape(
        T, N_GROUPS, EXPERTS_PER_GROUP)
    e_logits = e_logits[jnp.arange(T), grp]
    e_prob = jax.nn.softmax(e_logits, axis=-1)
    top_p, top_i = lax.top_k(e_prob, TOP_K)
    gate = p_grp * top_p / jnp.sum(top_p, axis=-1, keepdims=True)
    expert = grp[:, None] * EXPERTS_PER_GROUP + top_i
    A = T * TOP_K
    flat_e = expert.reshape(A).astype(jnp.int32)
    flat_tok = jnp.repeat(jnp.arange(T, dtype=jnp.int32), TOP_K)
    flat_w = gate.reshape(A)
    order = jnp.argsort(flat_e)
    sorted_e = flat_e[order]
    counts = jnp.zeros((N_EXPERTS,), jnp.int32).at[flat_e].add(1)
    padded = (counts + MOE_BLOCK - 1) // MOE_BLOCK * MOE_BLOCK
    pad_end = jnp.cumsum(padded)
    pad_start = pad_end - padded
    start = jnp.cumsum(counts) - counts
    dest = pad_start[sorted_e] + (jnp.arange(A, dtype=jnp.int32) - start[sorted_e])
    P = A + N_EXPERTS * MOE_BLOCK
    row_tok = jnp.zeros((P,), jnp.int32).at[dest].set(flat_tok[order])
    row_w = jnp.zeros((P,), jnp.float32).at[dest].set(flat_w[order])
    nblk = P // MOE_BLOCK
    blk_e = jnp.minimum(
        jnp.searchsorted(pad_end, jnp.arange(nblk, dtype=jnp.int32) * MOE_BLOCK, side='right'),
        N_EXPERTS - 1)
    xs = xf[row_tok].reshape(nblk, MOE_BLOCK, D)

    def expert_block(args):
        xb, e = args
        hid = jax.nn.silu(xb @ w_gate[e]) * (xb @ w_up[e])
        return hid @ w_down[e]

    ys = lax.map(expert_block, (xs, blk_e)).reshape(P, D)
    out = jax.ops.segment_sum(ys.astype(jnp.float32) * row_w[:, None], row_tok, num_segments=T)
    return out.astype(h.dtype).reshape(B, S, D)


def retention_mixer(x, pos, mem, ln1, w_in, ret_gn, w_mem_kv, mq_g, mk_g, w_out):
    B, S, _ = x.shape
    h = rms_norm(x, ln1)
    proj = h @ w_in
    q, k, v, g, qm = jnp.split(proj, [MAIN_WIDTH, 2 * MAIN_WIDTH, 3 * MAIN_WIDTH, 4 * MAIN_WIDTH], axis=-1)
    q = rope(q.reshape(B, S, N_MAIN_HEADS, HEAD_DIM), pos)
    k = rope(k.reshape(B, S, N_MAIN_HEADS, HEAD_DIM), pos)
    v = v.reshape(B, S, N_MAIN_HEADS, HEAD_DIM)
    y = head_group_norm(retention_chunkwise(q, k, v), ret_gn)
    y = (jax.nn.silu(g.astype(jnp.float32)) * y).astype(x.dtype)
    m = memory_attention(qm, mem, w_mem_kv, mq_g, mk_g)
    return x + jnp.concatenate([y, m], axis=-1) @ w_out


def mla_mixer(x, pos, mem, k_sh, v_sh, ln1, w_in, q_lora_g, w_uq, q_g, w_mem_kv, mq_g, mk_g, w_out):
    h = rms_norm(x, ln1)
    proj = h @ w_in
    c_q, qm = proj[..., :Q_LORA], proj[..., Q_LORA:]
    y = mla_attention(c_q, q_lora_g, w_uq, q_g, k_sh, v_sh, pos)
    m = memory_attention(qm, mem, w_mem_kv, mq_g, mk_g)
    return x + jnp.concatenate([y, m], axis=-1) @ w_out


def setup_inputs(seed: int = 0) -> dict:
    key = jax.random.key(seed)
    ks = iter(jax.random.split(key, 40))
    f32 = jnp.float32

    def nrm(shape, fan_in):
        return jax.random.normal(next(ks), shape, f32) * (fan_in ** -0.5)

    def gain(shape):
        return 1.0 + 0.02 * jax.random.normal(next(ks), shape, f32)

    def bias(shape):
        return 0.01 * jax.random.normal(next(ks), shape, f32)

    x = jax.random.normal(next(ks), (BATCH, SEQ, D_MODEL), f32)
    mem = jax.random.normal(next(ks), (BATCH, N_MEM, D_MODEL), f32)
    offs = jax.random.randint(next(ks), (BATCH, 1), 0, 2048, dtype=jnp.int32)
    positions = offs + jnp.arange(SEQ, dtype=jnp.int32)[None, :]
    return {
        "x": x, "mem": mem, "positions": positions,
        "ln1": gain((DEPTH, D_MODEL)),
        "ln2": gain((DEPTH, D_MODEL)),
        "w_out": nrm((DEPTH, MIX_WIDTH, D_MODEL), MIX_WIDTH),
        "mem_w_kv": nrm((DEPTH, D_MODEL, 2 * MEM_WIDTH), D_MODEL),
        "mem_q_norm": gain((DEPTH, HEAD_DIM)),
        "mem_k_norm": gain((DEPTH, HEAD_DIM)),
        "router_group_w": nrm((DEPTH, D_MODEL, N_GROUPS), D_MODEL),
        "router_group_b": bias((DEPTH, N_GROUPS)),
        "router_expert_w": nrm((DEPTH, D_MODEL, N_EXPERTS), D_MODEL),
        "router_expert_b": bias((DEPTH, N_EXPERTS)),
        "expert_w_gate": nrm((DEPTH, N_EXPERTS, D_MODEL, D_EXPERT), D_MODEL),
        "expert_w_up": nrm((DEPTH, N_EXPERTS, D_MODEL, D_EXPERT), D_MODEL),
        "expert_w_down": nrm((DEPTH, N_EXPERTS, D_EXPERT, D_MODEL), D_EXPERT),
        "ret_w_in": nrm((N_A, D_MODEL, A_IN_WIDTH), D_MODEL),
        "ret_gn": gain((N_A, MAIN_WIDTH)),
        "kv_ln": gain((D_MODEL,)),
        "kv_w_down": nrm((D_MODEL, KV_LORA + QK_ROPE), D_MODEL),
        "kv_lora_norm": gain((KV_LORA,)),
        "kv_w_up": nrm((KV_LORA, N_MAIN_HEADS * (QK_NOPE + V_HEAD)), KV_LORA),
        "k_norm": gain((QK_HEAD,)),
        "mla_w_in": nrm((N_B, D_MODEL, B_IN_WIDTH), D_MODEL),
        "q_lora_norm": gain((N_B, Q_LORA)),
        "mla_w_uq": nrm((N_B, Q_LORA, N_MAIN_HEADS * QK_HEAD), Q_LORA),
        "q_norm": gain((N_B, QK_HEAD)),
    }


def reference(x, mem, positions, ln1, ln2, w_out, mem_w_kv, mem_q_norm, mem_k_norm,
              router_group_w, router_group_b, router_expert_w, router_expert_b,
              expert_w_gate, expert_w_up, expert_w_down, ret_w_in, ret_gn,
              kv_ln, kv_w_down, kv_lora_norm, kv_w_up, k_norm,
              mla_w_in, q_lora_norm, mla_w_uq, q_norm):
    k_sh = None
    v_sh = None
    for i in range(DEPTH):
        if i < N_A:
            x = retention_mixer(x, positions, mem, ln1[i], ret_w_in[i], ret_gn[i],
                                mem_w_kv[i], mem_q_norm[i], mem_k_norm[i], w_out[i])
        else:
            j = i - N_A
            x = mla_mixer(x, positions, mem, k_sh, v_sh, ln1[i], mla_w_in[j], q_lora_norm[j],
                          mla_w_uq[j], q_norm[j], mem_w_kv[i], mem_q_norm[i], mem_k_norm[i], w_out[i])
        x = x + hier_moe(rms_norm(x, ln2[i]), router_group_w[i], router_group_b[i],
                         router_expert_w[i], router_expert_b[i],
                         expert_w_gate[i], expert_w_up[i], expert_w_down[i])
        if i == N_A - 1:
            k_sh, v_sh = shared_latent_kv(x, kv_ln, kv_w_down, kv_lora_norm, kv_w_up, k_norm, positions)
    return x
```

```python
import functools
import math

import jax
import jax.numpy as jnp
from jax import lax
from jax.experimental import pallas as pl
from jax.experimental.pallas import tpu as pltpu

F32 = jnp.float32
BF16 = jnp.bfloat16

HEAD_DIM = 64
N_MAIN_HEADS = 12
MAIN_WIDTH = N_MAIN_HEADS * HEAD_DIM
N_MEM_HEADS = 4
MEM_WIDTH = N_MEM_HEADS * HEAD_DIM
RET_CHUNK = 128
ROPE_THETA = 10000.0
Q_LORA = 384
KV_LORA = 256
QK_NOPE = 64
QK_ROPE = 32
QK_HEAD = QK_NOPE + QK_ROPE
V_HEAD = 64
N_GROUPS = 4
EXPERTS_PER_GROUP = 8
N_EXPERTS = N_GROUPS * EXPERTS_PER_GROUP
D_EXPERT = 256
MOE_BLOCK = 128
EPS = 1e-6

LANES = 128
SUBLANES = 8
VMEM_LIMIT = 48 * 1024 * 1024
NEG_BIG = -1e30

N_PAIRS = N_MAIN_HEADS // 2
ROW_TILE = 512
MOE_TILE = 256


def _cparams(sem):
    return pltpu.CompilerParams(dimension_semantics=sem, vmem_limit_bytes=VMEM_LIMIT)


def _dot(a, b):
    return jnp.dot(a, b, preferred_element_type=F32)


def _dot_nt(a, b):
    return lax.dot_general(a, b, (((1,), (1,)), ((), ())), preferred_element_type=F32)


def _dot_tn(a, b):
    return lax.dot_general(a, b, (((0,), (0,)), ((), ())), preferred_element_type=F32)


def _rmsnorm(xf, g):
    return xf * lax.rsqrt(jnp.mean(xf * xf, axis=-1, keepdims=True) + EPS) * g


def _rope_tables(pos_col, inv_row, sgn_row):
    ang = pos_col.astype(F32) * inv_row
    return jnp.cos(ang), jnp.sin(ang) * sgn_row


def _rope_apply(y, cos, sin_signed, first, half):
    partner = jnp.where(first, pltpu.roll(y, LANES - half, 1), pltpu.roll(y, half, 1))
    return y * cos + partner * sin_signed


def _ret_inproj_kernel(x_ref, pos_ref, g_ref, inv_ref, sgn_ref, w_ref,
                       q_ref, k_ref, v_ref, gt_ref, qm_ref):
    h = _rmsnorm(x_ref[...], g_ref[...]).astype(BF16)
    cos, sin_s = _rope_tables(pos_ref[...], inv_ref[...], sgn_ref[...])
    first = sgn_ref[...] < 0.0
    mw = MAIN_WIDTH
    for idx, ref in ((0, q_ref), (1, k_ref)):
        y = _dot(h, w_ref[:, idx * mw:(idx + 1) * mw])
        for j in range(mw // LANES):
            sl = slice(j * LANES, (j + 1) * LANES)
            ref[:, sl] = _rope_apply(y[:, sl], cos, sin_s, first, HEAD_DIM // 2).astype(BF16)
    v_ref[...] = _dot(h, w_ref[:, 2 * mw:3 * mw]).astype(BF16)
    gt_ref[...] = _dot(h, w_ref[:, 3 * mw:4 * mw]).astype(BF16)
    qm_ref[...] = _dot(h, w_ref[:, 4 * mw:4 * mw + MEM_WIDTH]).astype(BF16)


def _ret_inproj(x2, pos_col, g, w_in):
    T, D = x2.shape
    tm = min(ROW_TILE, T)
    half = HEAD_DIM // 2
    inv = ROPE_THETA ** (-jnp.arange(half, dtype=F32) / half)
    lane = jnp.arange(LANES)
    inv_row = inv[lane % half][None, :]
    sgn_row = jnp.where((lane % HEAD_DIM) < half, -1.0, 1.0).astype(F32)[None, :]
    row = lambda i: (i, 0)
    fixed = lambda i: (0, 0)
    n_in = w_in.shape[1]
    outs = pl.pallas_call(
        _ret_inproj_kernel,
        grid=(T // tm,),
        in_specs=[
            pl.BlockSpec((tm, D), row),
            pl.BlockSpec((tm, 1), row),
            pl.BlockSpec((1, D), fixed),
            pl.BlockSpec((1, LANES), fixed),
            pl.BlockSpec((1, LANES), fixed),
            pl.BlockSpec((D, n_in), fixed),
        ],
        out_specs=[pl.BlockSpec((tm, MAIN_WIDTH), row)] * 4 + [pl.BlockSpec((tm, MEM_WIDTH), row)],
        out_shape=[jax.ShapeDtypeStruct((T, MAIN_WIDTH), BF16)] * 4
        + [jax.ShapeDtypeStruct((T, MEM_WIDTH), BF16)],
        compiler_params=_cparams(("parallel",)),
        name="ret_inproj",
    )(x2, pos_col, g[None, :], inv_row, sgn_row, w_in.astype(BF16))
    return outs


def _retention_tables():
    H, C, d = N_MAIN_HEADS, RET_CHUNK, HEAD_DIM
    log_g = jnp.log1p(-jnp.exp2(-5.0 - jnp.arange(H, dtype=F32)))
    idx = jnp.arange(C, dtype=F32)
    rel = idx[:, None] - idx[None, :]
    scale = d ** -0.5
    decay_in = jnp.where(rel[None] >= 0,
                         jnp.exp(log_g[:, None, None] * jnp.maximum(rel, 0.0)[None]), 0.0) * scale
    kdec = jnp.exp(log_g[None, :] * (C - 1.0 - idx)[:, None]) * scale
    qdec = jnp.exp(log_g[None, :] * (idx + 1.0)[:, None])
    cdec = jnp.exp(log_g * C)

    def lanes(t):
        return jnp.repeat(t, d, axis=1).reshape(C, N_PAIRS, 2 * d).transpose(1, 0, 2)

    head_of_lane = jnp.arange(2 * d) // d
    same = (head_of_lane[:, None] == head_of_lane[None, :]).astype(F32)
    cd_lane = jnp.repeat(cdec, d).reshape(N_PAIRS, 2 * d)
    state_decay = cd_lane[:, :, None] * same[None]
    decay_in = decay_in.reshape(N_PAIRS, 2, C, C)
    return decay_in, lanes(kdec), lanes(qdec), state_decay, same


def _retention_kernel(q_ref, k_ref, v_ref, gt_ref, dm_ref, kd_ref, qd_ref, sd_ref, same_ref,
                      gn_ref, o_ref, r_ref):
    n = pl.program_id(2)

    @pl.when(n == 0)
    def _():
        r_ref[...] = jnp.zeros_like(r_ref)

    q = q_ref[0]
    k = k_ref[0]
    v = v_ref[0]
    lane = lax.broadcasted_iota(jnp.int32, (1, LANES), 1)
    head_a = lane < HEAD_DIM
    zero = jnp.zeros_like(q)
    s_a = _dot_nt(jnp.where(head_a, q, zero), k) * dm_ref[0, 0]
    s_b = _dot_nt(jnp.where(head_a, zero, q), k) * dm_ref[0, 1]
    inner = (_dot(s_a.astype(BF16), jnp.where(head_a, v, zero))
             + _dot(s_b.astype(BF16), jnp.where(head_a, zero, v)))
    r_prev = r_ref[...]
    cross = _dot((q.astype(F32) * qd_ref[0]).astype(BF16), r_prev.astype(BF16))
    y = inner + cross
    u = _dot_tn((k.astype(F32) * kd_ref[0]).astype(BF16), v)
    r_ref[...] = sd_ref[0] * r_prev + same_ref[...] * u

    inv_d = 1.0 / HEAD_DIM
    mu_a = jnp.sum(jnp.where(head_a, y, 0.0), axis=-1, keepdims=True) * inv_d
    mu_b = jnp.sum(jnp.where(head_a, 0.0, y), axis=-1, keepdims=True) * inv_d
    yc = y - jnp.where(head_a, mu_a, mu_b)
    yc2 = yc * yc
    var_a = jnp.sum(jnp.where(head_a, yc2, 0.0), axis=-1, keepdims=True) * inv_d
    var_b = jnp.sum(jnp.where(head_a, 0.0, yc2), axis=-1, keepdims=True) * inv_d
    yn = yc * lax.rsqrt(jnp.where(head_a, var_a, var_b) + EPS) * gn_ref[...]
    g = gt_ref[0].astype(F32)
    o_ref[0] = (g / (1.0 + jnp.exp(-g)) * yn).astype(BF16)


def _retention(q, k, v, gt, ret_gn):
    B, S, _ = q.shape
    C = RET_CHUNK
    dm, kd, qd, sd, same = _retention_tables()
    tok = lambda b, p, n: (b, n, p)
    per_pair3 = lambda b, p, n: (p, 0, 0)
    return pl.pallas_call(
        _retention_kernel,
        grid=(B, N_PAIRS, S // C),
        in_specs=[pl.BlockSpec((1, C, LANES), tok)] * 4 + [
            pl.BlockSpec((1, 2, C, C), lambda b, p, n: (p, 0, 0, 0)),
            pl.BlockSpec((1, C, LANES), per_pair3),
            pl.BlockSpec((1, C, LANES), per_pair3),
            pl.BlockSpec((1, LANES, LANES), per_pair3),
            pl.BlockSpec((LANES, LANES), lambda b, p, n: (0, 0)),
            pl.BlockSpec((1, LANES), lambda b, p, n: (0, p)),
        ],
        out_specs=pl.BlockSpec((1, C, LANES), tok),
        out_shape=jax.ShapeDtypeStruct((B, S, MAIN_WIDTH), BF16),
        scratch_shapes=[pltpu.VMEM((LANES, LANES), F32)],
        compiler_params=_cparams(("parallel", "parallel", "arbitrary")),
        name="retention",
    )(q, k, v, gt, dm, kd, qd, sd, same, ret_gn[None, :])


def _mem_kv_kernel(mem_ref, w_ref, kg_ref, k_ref, v_ref):
    kv = _dot(mem_ref[...].astype(BF16), w_ref[...])
    lane = lax.broadcasted_iota(jnp.int32, (1, LANES), 1)
    head_a = lane < HEAD_DIM
    inv_d = 1.0 / HEAD_DIM
    for j in range(MEM_WIDTH // LANES):
        kj = kv[:, j * LANES:(j + 1) * LANES]
        k2 = kj * kj
        ms_a = jnp.sum(jnp.where(head_a, k2, 0.0), axis=-1, keepdims=True) * inv_d
        ms_b = jnp.sum(jnp.where(head_a, 0.0, k2), axis=-1, keepdims=True) * inv_d
        kn = kj * lax.rsqrt(jnp.where(head_a, ms_a, ms_b) + EPS) * kg_ref[...]
        k_ref[:, j * LANES:(j + 1) * LANES] = kn.astype(BF16)
    v_ref[...] = kv[:, MEM_WIDTH:].astype(BF16)


def _mem_kv(mem2, w_mem_kv, k_g):
    TM, D = mem2.shape
    tm = min(ROW_TILE, TM)
    kg_row = jnp.tile(k_g, LANES // HEAD_DIM)[None, :]
    row = lambda i: (i, 0)
    fixed = lambda i: (0, 0)
    return pl.pallas_call(
        _mem_kv_kernel,
        grid=(TM // tm,),
        in_specs=[pl.BlockSpec((tm, D), row), pl.BlockSpec((D, 2 * MEM_WIDTH), fixed),
                  pl.BlockSpec((1, LANES), fixed)],
        out_specs=[pl.BlockSpec((tm, MEM_WIDTH), row)] * 2,
        out_shape=[jax.ShapeDtypeStruct((TM, MEM_WIDTH), BF16)] * 2,
        compiler_params=_cparams(("parallel",)),
        name="mem_kv",
    )(mem2, w_mem_kv.astype(BF16), kg_row)


def _mem_attn_kernel(qm_ref, k_ref, v_ref, qg_ref, o_ref):
    lane = lax.broadcasted_iota(jnp.int32, (1, LANES), 1)
    head_a = lane < HEAD_DIM
    inv_d = 1.0 / HEAD_DIM
    scale = HEAD_DIM ** -0.5
    for j in range(MEM_WIDTH // LANES):
        sl = slice(j * LANES, (j + 1) * LANES)
        qj = qm_ref[0, :, sl].astype(F32)
        q2 = qj * qj
        ms_a = jnp.sum(jnp.where(head_a, q2, 0.0), axis=-1, keepdims=True) * inv_d
        ms_b = jnp.sum(jnp.where(head_a, 0.0, q2), axis=-1, keepdims=True) * inv_d
        qn = qj * lax.rsqrt(jnp.where(head_a, ms_a, ms_b) + EPS) * (qg_ref[...] * scale)
        kj = k_ref[0, :, sl]
        vj = v_ref[0, :, sl]
        out = None
        for sel in (head_a, jnp.logical_not(head_a)):
            s = _dot_nt(jnp.where(sel, qn, 0.0).astype(BF16), kj)
            p = jnp.exp(s - jnp.max(s, axis=-1, keepdims=True))
            p = p / jnp.sum(p, axis=-1, keepdims=True)
            o = _dot(p.astype(BF16), jnp.where(sel, vj, jnp.zeros_like(vj)))
            out = o if out is None else out + o
        o_ref[0, :, sl] = out.astype(BF16)


def _mem_attn(qm, k_m, v_m, q_g):
    B, S, _ = qm.shape
    M = k_m.shape[1]
    tm = min(ROW_TILE, S)
    qg_row = jnp.tile(q_g, LANES // HEAD_DIM)[None, :]
    return pl.pallas_call(
        _mem_attn_kernel,
        grid=(B, S // tm),
        in_specs=[pl.BlockSpec((1, tm, MEM_WIDTH), lambda b, i: (b, i, 0)),
                  pl.BlockSpec((1, M, MEM_WIDTH), lambda b, i: (b, 0, 0)),
                  pl.BlockSpec((1, M, MEM_WIDTH), lambda b, i: (b, 0, 0)),
                  pl.BlockSpec((1, LANES), lambda b, i: (0, 0))],
        out_specs=pl.BlockSpec((1, tm, MEM_WIDTH), lambda b, i: (b, i, 0)),
        out_shape=jax.ShapeDtypeStruct((B, S, MEM_WIDTH), BF16),
        compiler_params=_cparams(("parallel", "parallel")),
        name="mem_attn",
    )(qm, k_m, v_m, qg_row)


def _out_proj_kernel(x_ref, y_ref, m_ref, wy_ref, wm_ref, o_ref):
    o_ref[...] = x_ref[...] + _dot(y_ref[...], wy_ref[...]) + _dot(m_ref[...], wm_ref[...])


def _out_proj(x2, y2, m2, w_out):
    T, D = x2.shape
    tm = min(ROW_TILE, T)
    row = lambda i: (i, 0)
    fixed = lambda i: (0, 0)
    w = w_out.astype(BF16)
    return pl.pallas_call(
        _out_proj_kernel,
        grid=(T // tm,),
        in_specs=[pl.BlockSpec((tm, D), row), pl.BlockSpec((tm, MAIN_WIDTH), row),
                  pl.BlockSpec((tm, MEM_WIDTH), row),
                  pl.BlockSpec((MAIN_WIDTH, D), fixed), pl.BlockSpec((MEM_WIDTH, D), fixed)],
        out_specs=pl.BlockSpec((tm, D), row),
        out_shape=jax.ShapeDtypeStruct((T, D), F32),
        compiler_params=_cparams(("parallel",)),
        name="out_proj",
    )(x2, y2, m2, w[:MAIN_WIDTH], w[MAIN_WIDTH:])


ROUTER_LANE0 = N_GROUPS


def _router_kernel(x_ref, g_ref, whi_ref, wlo_ref, b_ref, info_ref, cnt_ref, base_ref):
    i = pl.program_id(0)

    @pl.when(i == 0)
    def _():
        base_ref[...] = jnp.zeros_like(base_ref)

    tm = x_ref.shape[0]
    h = _rmsnorm(x_ref[...], g_ref[...])
    h_hi = h.astype(BF16)
    h_lo = (h - h_hi.astype(F32)).astype(BF16)
    logits = (_dot(h_hi, whi_ref[...]) + _dot(h_lo, whi_ref[...]) + _dot(h_hi, wlo_ref[...])
              + b_ref[...])
    lane_i = lax.broadcasted_iota(jnp.int32, (tm, LANES), 1)
    lane = lane_i.astype(F32)
    big = float(LANES)

    is_g = lane_i < N_GROUPS
    lg = jnp.where(is_g, logits, NEG_BIG)
    mg = jnp.max(lg, axis=-1, keepdims=True)
    zg = jnp.sum(jnp.where(is_g, jnp.exp(lg - mg), 0.0), axis=-1, keepdims=True)
    p_grp = 1.0 / zg
    grp = jnp.min(jnp.where(is_g & (lg == mg), lane, big), axis=-1, keepdims=True)

    e_lane = lane_i - ROUTER_LANE0
    e_grp = (e_lane >> int(math.log2(EXPERTS_PER_GROUP))).astype(F32)
    is_e = (e_lane >= 0) & (e_lane < N_EXPERTS) & (e_grp == grp)
    le = jnp.where(is_e, logits, NEG_BIG)
    me = jnp.max(le, axis=-1, keepdims=True)
    ee = jnp.where(is_e, jnp.exp(le - me), 0.0)
    prob = ee / jnp.sum(ee, axis=-1, keepdims=True)
    p1 = jnp.max(prob, axis=-1, keepdims=True)
    i1 = jnp.min(jnp.where(is_e & (prob == p1), lane, big), axis=-1, keepdims=True)
    rest = is_e & (lane != i1)
    p2 = jnp.max(jnp.where(rest, prob, -1.0), axis=-1, keepdims=True)
    i2 = jnp.min(jnp.where(rest & (prob == p2), lane, big), axis=-1, keepdims=True)
    gate1 = p_grp * p1 / (p1 + p2)
    gate2 = p_grp * p2 / (p1 + p2)

    sel1 = lane == i1
    sel2 = lane == i2
    onehot = jnp.where(sel1 | sel2, 1.0, 0.0)
    r_io = lax.broadcasted_iota(jnp.int32, (tm, tm), 0)
    c_io = lax.broadcasted_iota(jnp.int32, (tm, tm), 1)
    lower = jnp.where(r_io > c_io, 1.0, 0.0).astype(BF16)
    before = _dot(lower, onehot.astype(BF16)) + base_ref[...]
    rank1 = jnp.sum(jnp.where(sel1, before, 0.0), axis=-1, keepdims=True)
    rank2 = jnp.sum(jnp.where(sel2, before, 0.0), axis=-1, keepdims=True)
    base_ref[...] += jnp.sum(onehot, axis=0, keepdims=True)
    cnt_ref[...] = base_ref[...]

    e1 = i1 - float(ROUTER_LANE0)
    e2 = i2 - float(ROUTER_LANE0)
    info = jnp.zeros((tm, LANES), F32)
    for col, val in enumerate((gate1, gate2, e1, e2, rank1, rank2)):
        info = jnp.where(lane_i == col, val, info)
    info_ref[...] = info


def _router(x2, g, w_grp, b_grp, w_exp, b_exp):
    T, D = x2.shape
    tm = min(ROW_TILE, T)
    w = jnp.zeros((D, LANES), F32)
    w = w.at[:, :N_GROUPS].set(w_grp).at[:, ROUTER_LANE0:ROUTER_LANE0 + N_EXPERTS].set(w_exp)
    b = jnp.zeros((1, LANES), F32)
    b = b.at[0, :N_GROUPS].set(b_grp).at[0, ROUTER_LANE0:ROUTER_LANE0 + N_EXPERTS].set(b_exp)
    w_hi = w.astype(BF16)
    w_lo = (w - w_hi.astype(F32)).astype(BF16)
    row = lambda i: (i, 0)
    fixed = lambda i: (0, 0)
    info, cnt = pl.pallas_call(
        _router_kernel,
        grid=(T // tm,),
        in_specs=[pl.BlockSpec((tm, D), row), pl.BlockSpec((1, D), fixed),
                  pl.BlockSpec((D, LANES), fixed), pl.BlockSpec((D, LANES), fixed),
                  pl.BlockSpec((1, LANES), fixed)],
        out_specs=[pl.BlockSpec((tm, LANES), row), pl.BlockSpec((1, LANES), fixed)],
        out_shape=[jax.ShapeDtypeStruct((T, LANES), F32), jax.ShapeDtypeStruct((1, LANES), F32)],
        scratch_shapes=[pltpu.VMEM((1, LANES), F32)],
        compiler_params=_cparams(("arbitrary",)),
        name="moe_router",
    )(x2, g[None, :], w_hi, w_lo, b)
    return info, cnt


def _row_copy(src, src_row, dst, dst_row, sem):
    def rows(r):
        start = r * SUBLANES
        if not isinstance(start, int):
            start = pl.multiple_of(start, SUBLANES)
        return pl.ds(start, SUBLANES)
    return pltpu.make_async_copy(src.at[rows(src_row), :], dst.at[rows(dst_row), :], sem)


def _block_copy(src, dst, blk, sem):
    rows = MOE_BLOCK * SUBLANES
    return pltpu.make_async_copy(src, dst.at[pl.ds(pl.multiple_of(blk * rows, rows), rows), :], sem)


def _dispatch_kernel(dest_ref, zrow_ref, zcnt_ref, nblk_ref, x_ref, g_ref, xs_ref,
                     buf_ref, zbuf_ref, sem, zsem):
    i = pl.program_id(0)
    tm = x_ref.shape[0]

    @pl.when(i == 0)
    def _():
        zbuf_ref[...] = jnp.zeros_like(zbuf_ref)
        n_blocks = xs_ref.shape[0] // (MOE_BLOCK * SUBLANES)

        def tail_start(b, c):
            _block_copy(zbuf_ref, xs_ref, b, zsem).start()
            return c
        lax.fori_loop(nblk_ref[0], n_blocks, tail_start, 0)

        def tail_wait(b, c):
            _block_copy(zbuf_ref, xs_ref, b, zsem).wait()
            return c
        lax.fori_loop(nblk_ref[0], n_blocks, tail_wait, 0)

        def per_expert(e, c):
            def start(r, c2):
                _row_copy(zbuf_ref, 0, xs_ref, zrow_ref[e] + r, zsem).start()
                return c2
            lax.fori_loop(0, zcnt_ref[e], start, 0)

            def wait(r, c2):
                _row_copy(zbuf_ref, 0, xs_ref, zrow_ref[e] + r, zsem).wait()
                return c2
            lax.fori_loop(0, zcnt_ref[e], wait, 0)
            return c
        lax.fori_loop(0, N_EXPERTS, per_expert, 0)

    h = _rmsnorm(x_ref[...], g_ref[...])
    for s in range(SUBLANES):
        buf_ref[pl.ds(s, tm, stride=SUBLANES), :] = h[:, s * LANES:(s + 1) * LANES]

    def start(t, c):
        for kk in range(2):
            _row_copy(buf_ref, t, xs_ref, dest_ref[(i * tm + t) * 2 + kk], sem).start()
        return c
    lax.fori_loop(0, tm, start, 0)

    def wait(t, c):
        for kk in range(2):
            _row_copy(buf_ref, t, xs_ref, dest_ref[(i * tm + t) * 2 + kk], sem).wait()
        return c
    lax.fori_loop(0, tm, wait, 0)


def _dispatch(x2, g, dest_flat, zrow, zcnt, nblk_used, n_rows):
    T, D = x2.shape
    tm = min(MOE_TILE, T)
    return pl.pallas_call(
        _dispatch_kernel,
        grid_spec=pltpu.PrefetchScalarGridSpec(
            num_scalar_prefetch=4,
            grid=(T // tm,),
            in_specs=[pl.BlockSpec((tm, D), lambda i, *_: (i, 0)),
                      pl.BlockSpec((1, D), lambda i, *_: (0, 0))],
            out_specs=pl.BlockSpec(memory_space=pl.ANY),
            scratch_shapes=[pltpu.VMEM((tm * SUBLANES, LANES), F32),
                            pltpu.VMEM((MOE_BLOCK * SUBLANES, LANES), F32),
                            pltpu.SemaphoreType.DMA(()), pltpu.SemaphoreType.DMA(())],
        ),
        out_shape=jax.ShapeDtypeStruct((n_rows * SUBLANES, LANES), F32),
        compiler_params=_cparams(("arbitrary",)),
        name="moe_dispatch",
    )(dest_flat, zrow, zcnt, nblk_used, x2, g[None, :])


def _expert_kernel(blk_e_ref, nblk_ref, xs_ref, wg_ref, wu_ref, wd_ref, ys_ref):
    b = pl.program_id(0)

    @pl.when(b < nblk_ref[0])
    def _():
        nb = MOE_BLOCK
        x = jnp.concatenate(
            [xs_ref[pl.ds(s, nb, stride=SUBLANES), :] for s in range(SUBLANES)], axis=-1
        ).astype(BF16)
        a = _dot(x, wg_ref[0])
        u = _dot(x, wu_ref[0])
        hid = (a / (1.0 + jnp.exp(-a)) * u).astype(BF16)
        y = _dot(hid, wd_ref[0])
        for s in range(SUBLANES):
            ys_ref[pl.ds(s, nb, stride=SUBLANES), :] = y[:, s * LANES:(s + 1) * LANES]

    @pl.when(b >= nblk_ref[0])
    def _():
        ys_ref[...] = jnp.zeros_like(ys_ref)


def _experts(xs, blk_e, nblk_used, w_gate, w_up, w_down):
    rows = xs.shape[0]
    nblk = rows // (MOE_BLOCK * SUBLANES)
    D = w_gate.shape[1]
    blk = lambda b, be, nb: (jnp.minimum(b, nb[0] - 1), 0)
    out_blk = lambda b, be, nb: (b, 0)
    wsel = lambda b, be, nb: (be[b], 0, 0)
    return pl.pallas_call(
        _expert_kernel,
        grid_spec=pltpu.PrefetchScalarGridSpec(
            num_scalar_prefetch=2,
            grid=(nblk,),
            in_specs=[pl.BlockSpec((MOE_BLOCK * SUBLANES, LANES), blk),
                      pl.BlockSpec((1, D, D_EXPERT), wsel),
                      pl.BlockSpec((1, D, D_EXPERT), wsel),
                      pl.BlockSpec((1, D_EXPERT, D), wsel)],
            out_specs=pl.BlockSpec((MOE_BLOCK * SUBLANES, LANES), out_blk),
        ),
        out_shape=jax.ShapeDtypeStruct((rows, LANES), F32),
        compiler_params=_cparams(("arbitrary",)),
        name="moe_experts",
    )(blk_e, nblk_used, xs, w_gate.astype(BF16), w_up.astype(BF16), w_down.astype(BF16))


def _combine_kernel(dest_ref, x_ref, info_ref, ys_ref, o_ref, buf0_ref, buf1_ref, sem):
    i = pl.program_id(0)
    tm = x_ref.shape[0]
    bufs = (buf0_ref, buf1_ref)

    def start(t, c):
        for kk in range(2):
            _row_copy(ys_ref, dest_ref[(i * tm + t) * 2 + kk], bufs[kk], t, sem).start()
        return c
    lax.fori_loop(0, tm, start, 0)

    def wait(t, c):
        for kk in range(2):
            _row_copy(ys_ref, dest_ref[(i * tm + t) * 2 + kk], bufs[kk], t, sem).wait()
        return c
    lax.fori_loop(0, tm, wait, 0)

    info = info_ref[...]
    g0 = info[:, 0:1]
    g1 = info[:, 1:2]
    for s in range(SUBLANES):
        sl = slice(s * LANES, (s + 1) * LANES)
        y0 = buf0_ref[pl.ds(s, tm, stride=SUBLANES), :]
        y1 = buf1_ref[pl.ds(s, tm, stride=SUBLANES), :]
        o_ref[:, sl] = x_ref[:, sl] + (y0 * g0 + y1 * g1)


def _combine(x2, info, ys, dest_flat):
    T, D = x2.shape
    tm = min(MOE_TILE, T)
    return pl.pallas_call(
        _combine_kernel,
        grid_spec=pltpu.PrefetchScalarGridSpec(
            num_scalar_prefetch=1,
            grid=(T // tm,),
            in_specs=[pl.BlockSpec((tm, D), lambda i, *_: (i, 0)),
                      pl.BlockSpec((tm, LANES), lambda i, *_: (i, 0)),
                      pl.BlockSpec(memory_space=pl.ANY)],
            out_specs=pl.BlockSpec((tm, D), lambda i, *_: (i, 0)),
            scratch_shapes=[pltpu.VMEM((tm * SUBLANES, LANES), F32),
                            pltpu.VMEM((tm * SUBLANES, LANES), F32),
                            pltpu.SemaphoreType.DMA(())],
        ),
        out_shape=jax.ShapeDtypeStruct((T, D), F32),
        compiler_params=_cparams(("arbitrary",)),
        name="moe_combine",
    )(dest_flat, x2, info, ys)


def _hier_moe(x2, ln2, w_grp, b_grp, w_exp, b_exp, w_gate, w_up, w_down):
    T, D = x2.shape
    info, cnt = _router(x2, ln2, w_grp, b_grp, w_exp, b_exp)
    counts = cnt[0, ROUTER_LANE0:ROUTER_LANE0 + N_EXPERTS].astype(jnp.int32)
    padded = (counts + MOE_BLOCK - 1) // MOE_BLOCK * MOE_BLOCK
    pad_end = jnp.cumsum(padded)
    pad_start = pad_end - padded
    eid = info[:, 2:4].astype(jnp.int32)
    rank = info[:, 4:6].astype(jnp.int32)
    dest_flat = (pad_start[eid] + rank).reshape(T * 2)
    n_rows = T * 2 + N_EXPERTS * MOE_BLOCK
    nblk = n_rows // MOE_BLOCK
    blk_e = jnp.minimum(
        jnp.searchsorted(pad_end, jnp.arange(nblk, dtype=jnp.int32) * MOE_BLOCK, side='right'),
        N_EXPERTS - 1).astype(jnp.int32)
    nblk_used = (pad_end[-1:] // MOE_BLOCK).astype(jnp.int32)
    zrow = (pad_start + counts).astype(jnp.int32)
    zcnt = (padded - counts).astype(jnp.int32)

    xs = _dispatch(x2, ln2, dest_flat, zrow, zcnt, nblk_used, n_rows)
    ys = _experts(xs, blk_e, nblk_used, w_gate, w_up, w_down)
    return _combine(x2, info, ys, dest_flat)


HEAD_PAD = LANES


def _mla_rope_rows():
    half = QK_ROPE // 2
    inv = ROPE_THETA ** (-jnp.arange(half, dtype=F32) / half)
    lane = jnp.arange(LANES)
    r = lane - QK_NOPE
    in_rope = (r >= 0) & (r < QK_ROPE)
    inv_row = jnp.where(in_rope, inv[jnp.clip(r, 0, QK_ROPE - 1) % half], 0.0)[None, :]
    sgn_row = jnp.where(in_rope, jnp.where(r < half, -1.0, 1.0), 0.0).astype(F32)[None, :]
    first_row = jnp.where(in_rope & (r < half), -1.0, 1.0).astype(F32)[None, :]
    return inv_row, sgn_row, first_row


def _head_norm_rope(y, gain_row, cos, sin_s, first, out_scale):
    ms = jnp.sum(y * y, axis=-1, keepdims=True) * (1.0 / QK_HEAD)
    yn = y * lax.rsqrt(ms + EPS) * gain_row
    return _rope_apply(yn, cos, sin_s, first, QK_ROPE // 2) * out_scale


def _shared_kv_kernel(x_ref, pos_ref, ln_ref, wd_ref, scale_ref, wkv_ref, kg_ref,
                      inv_ref, sgn_ref, first_ref, k_ref, v_ref):
    h = _rmsnorm(x_ref[...], ln_ref[...]).astype(BF16)
    ckr = _dot(h, wd_ref[...])
    c = ckr[:, :KV_LORA]
    r = lax.rsqrt(jnp.mean(c * c, axis=-1, keepdims=True) + EPS)
    lane = lax.broadcasted_iota(jnp.int32, (1, ckr.shape[1]), 1)
    lhs = (ckr * jnp.where(lane < KV_LORA, r * scale_ref[...], 1.0)).astype(BF16)
    kv = _dot(lhs, wkv_ref[...])
    cos, sin_s = _rope_tables(pos_ref[...], inv_ref[...], sgn_ref[...])
    first = first_ref[...] < 0.0
    for hh in range(N_MAIN_HEADS):
        sl = slice(hh * HEAD_PAD, (hh + 1) * HEAD_PAD)
        k_ref[:, sl] = _head_norm_rope(kv[:, sl], kg_ref[...], cos, sin_s, first, 1.0).astype(BF16)
    v_ref[...] = kv[:, N_MAIN_HEADS * HEAD_PAD:].astype(BF16)


def _pad_head_cols(w):
    K = w.shape[0]
    w3 = w.reshape(K, N_MAIN_HEADS, QK_HEAD)
    return jnp.pad(w3, ((0, 0), (0, 0), (0, HEAD_PAD - QK_HEAD))).reshape(K, N_MAIN_HEADS * HEAD_PAD)


def _shared_kv(x2, pos_col, kv_ln, w_dkv, kv_lora_g, w_ukv, k_g):
    T, D = x2.shape
    tm = min(ROW_TILE, T)
    lat = 3 * LANES
    wd = jnp.pad(w_dkv, ((0, 0), (0, lat - w_dkv.shape[1]))).astype(BF16)
    scale_row = jnp.pad(kv_lora_g, (0, lat - KV_LORA), constant_values=1.0)[None, :]
    w3 = w_ukv.reshape(KV_LORA, N_MAIN_HEADS, QK_NOPE + V_HEAD)
    wk = jnp.zeros((lat, N_MAIN_HEADS, HEAD_PAD), F32)
    wk = wk.at[:KV_LORA, :, :QK_NOPE].set(w3[:, :, :QK_NOPE])
    eye = jnp.eye(QK_ROPE, dtype=F32)
    wk = wk.at[KV_LORA:KV_LORA + QK_ROPE, :, QK_NOPE:QK_HEAD].set(
        jnp.broadcast_to(eye[:, None, :], (QK_ROPE, N_MAIN_HEADS, QK_ROPE)))
    wv = jnp.zeros((lat, N_MAIN_HEADS * V_HEAD), F32)
    wv = wv.at[:KV_LORA].set(w3[:, :, QK_NOPE:].reshape(KV_LORA, N_MAIN_HEADS * V_HEAD))
    wkv = jnp.concatenate([wk.reshape(lat, N_MAIN_HEADS * HEAD_PAD), wv], axis=1).astype(BF16)
    kg_row = jnp.pad(k_g, (0, HEAD_PAD - QK_HEAD))[None, :]
    inv_row, sgn_row, first_row = _mla_rope_rows()
    row = lambda i: (i, 0)
    fixed = lambda i: (0, 0)
    kw = N_MAIN_HEADS * HEAD_PAD
    return pl.pallas_call(
        _shared_kv_kernel,
        grid=(T // tm,),
        in_specs=[pl.BlockSpec((tm, D), row), pl.BlockSpec((tm, 1), row),
                  pl.BlockSpec((1, D), fixed), pl.BlockSpec((D, lat), fixed),
                  pl.BlockSpec((1, lat), fixed), pl.BlockSpec((lat, kw + MAIN_WIDTH), fixed),
                  pl.BlockSpec((1, LANES), fixed), pl.BlockSpec((1, LANES), fixed),
                  pl.BlockSpec((1, LANES), fixed), pl.BlockSpec((1, LANES), fixed)],
        out_specs=[pl.BlockSpec((tm, kw), row), pl.BlockSpec((tm, MAIN_WIDTH), row)],
        out_shape=[jax.ShapeDtypeStruct((T, kw), BF16), jax.ShapeDtypeStruct((T, MAIN_WIDTH), BF16)],
        compiler_params=_cparams(("parallel",)),
        name="shared_kv",
    )(x2, pos_col, kv_ln[None, :], wd, scale_row, wkv, kg_row, inv_row, sgn_row, first_row)


def _mla_inproj_kernel(x_ref, pos_ref, ln_ref, win_ref, qlg_ref, wuq_ref, qg_ref,
                       inv_ref, sgn_ref, first_ref, q_ref, qm_ref):
    h = _rmsnorm(x_ref[...], ln_ref[...]).astype(BF16)
    proj = _dot(h, win_ref[...])
    qm_ref[...] = proj[:, Q_LORA:].astype(BF16)
    cq = _rmsnorm(proj[:, :Q_LORA], qlg_ref[...]).astype(BF16)
    q = _dot(cq, wuq_ref[...])
    cos, sin_s = _rope_tables(pos_ref[...], inv_ref[...], sgn_ref[...])
    first = first_ref[...] < 0.0
    scale = QK_HEAD ** -0.5
    for hh in range(N_MAIN_HEADS):
        sl = slice(hh * HEAD_PAD, (hh + 1) * HEAD_PAD)
        q_ref[:, sl] = _head_norm_rope(q[:, sl], qg_ref[...], cos, sin_s, first, scale).astype(BF16)


def _mla_inproj(x2, pos_col, ln1, w_in, q_lora_g, w_uq, q_g):
    T, D = x2.shape
    tm = min(ROW_TILE, T)
    wuq = _pad_head_cols(w_uq).astype(BF16)
    qg_row = jnp.pad(q_g, (0, HEAD_PAD - QK_HEAD))[None, :]
    inv_row, sgn_row, first_row = _mla_rope_rows()
    row = lambda i: (i, 0)
    fixed = lambda i: (0, 0)
    kw = N_MAIN_HEADS * HEAD_PAD
    n_in = w_in.shape[1]
    return pl.pallas_call(
        _mla_inproj_kernel,
        grid=(T // tm,),
        in_specs=[pl.BlockSpec((tm, D), row), pl.BlockSpec((tm, 1), row),
                  pl.BlockSpec((1, D), fixed), pl.BlockSpec((D, n_in), fixed),
                  pl.BlockSpec((1, Q_LORA), fixed), pl.BlockSpec((Q_LORA, kw), fixed),
                  pl.BlockSpec((1, LANES), fixed), pl.BlockSpec((1, LANES), fixed),
                  pl.BlockSpec((1, LANES), fixed), pl.BlockSpec((1, LANES), fixed)],
        out_specs=[pl.BlockSpec((tm, kw), row), pl.BlockSpec((tm, MEM_WIDTH), row)],
        out_shape=[jax.ShapeDtypeStruct((T, kw), BF16), jax.ShapeDtypeStruct((T, MEM_WIDTH), BF16)],
        compiler_params=_cparams(("parallel",)),
        name="mla_inproj",
    )(x2, pos_col, ln1[None, :], w_in.astype(BF16), q_lora_g[None, :], wuq, qg_row,
      inv_row, sgn_row, first_row)


ATTN_TILE = 512


def _flash_kernel(q_ref, k_ref, v_ref, o_ref, m_ref, l_ref, acc_ref):
    i = pl.program_id(2)
    j = pl.program_id(3)
    tq = q_ref.shape[1]
    tk = k_ref.shape[1]

    @pl.when(j == 0)
    def _():
        m_ref[...] = jnp.full_like(m_ref, NEG_BIG)
        l_ref[...] = jnp.zeros_like(l_ref)
        acc_ref[...] = jnp.zeros_like(acc_ref)

    @pl.when(j <= i)
    def _():
        lane = lax.broadcasted_iota(jnp.int32, (1, LANES), 1)
        head_a = lane < V_HEAD
        v = v_ref[0]
        zero = jnp.zeros_like(v)
        q_idx = i * tq + lax.broadcasted_iota(jnp.int32, (tq, tk), 0)
        k_idx = j * tk + lax.broadcasted_iota(jnp.int32, (tq, tk), 1)
        visible = k_idx <= q_idx
        for hh in range(2):
            sl = slice(hh * HEAD_PAD, (hh + 1) * HEAD_PAD)
            s = _dot_nt(q_ref[0, :, sl], k_ref[0, :, sl])
            s = jnp.where(visible, s, NEG_BIG)
            m_prev = m_ref[hh]
            m_new = jnp.maximum(m_prev, jnp.max(s, axis=-1, keepdims=True))
            alpha = jnp.exp(m_prev - m_new)
            p = jnp.exp(s - m_new)
            l_ref[hh] = alpha * l_ref[hh] + jnp.sum(p, axis=-1, keepdims=True)
            vh = jnp.where(head_a, v, zero) if hh == 0 else jnp.where(head_a, zero, v)
            acc_ref[hh] = alpha * acc_ref[hh] + _dot(p.astype(BF16), vh)
            m_ref[hh] = m_new

    @pl.when(j == pl.num_programs(3) - 1)
    def _():
        o_ref[0] = (acc_ref[0] / l_ref[0] + acc_ref[1] / l_ref[1]).astype(BF16)


def _flash(q, k, v):
    B, S, _ = q.shape
    t = min(ATTN_TILE, S)
    n = S // t
    return pl.pallas_call(
        _flash_kernel,
        grid=(B, N_PAIRS, n, n),
        in_specs=[pl.BlockSpec((1, t, 2 * HEAD_PAD), lambda b, p, i, j: (b, i, p)),
                  pl.BlockSpec((1, t, 2 * HEAD_PAD), lambda b, p, i, j: (b, jnp.minimum(j, i), p)),
                  pl.BlockSpec((1, t, LANES), lambda b, p, i, j: (b, jnp.minimum(j, i), p))],
        out_specs=pl.BlockSpec((1, t, LANES), lambda b, p, i, j: (b, i, p)),
        out_shape=jax.ShapeDtypeStruct((B, S, MAIN_WIDTH), BF16),
        scratch_shapes=[pltpu.VMEM((2, t, 1), F32), pltpu.VMEM((2, t, 1), F32),
                        pltpu.VMEM((2, t, LANES), F32)],
        compiler_params=_cparams(("parallel", "parallel", "parallel", "arbitrary")),
        name="mla_flash",
    )(q, k, v)


def kernel(x, mem, positions, ln1, ln2, w_out, mem_w_kv, mem_q_norm, mem_k_norm, router_group_w, router_group_b, router_expert_w, router_expert_b, expert_w_gate, expert_w_up, expert_w_down, ret_w_in, ret_gn, kv_ln, kv_w_down, kv_lora_norm, kv_w_up, k_norm, mla_w_in, q_lora_norm, mla_w_uq, q_norm):
    B, S, D = x.shape
    M = mem.shape[1]
    T = B * S
    x2 = x.reshape(T, D)
    mem2 = mem.reshape(B * M, D)
    pos_col = positions.reshape(T, 1).astype(jnp.int32)

    def mem_path(i, qm):
        k_m, v_m = _mem_kv(mem2, mem_w_kv[i], mem_k_norm[i])
        return _mem_attn(qm.reshape(B, S, MEM_WIDTH), k_m.reshape(B, M, MEM_WIDTH),
                         v_m.reshape(B, M, MEM_WIDTH), mem_q_norm[i]).reshape(T, MEM_WIDTH)

    def moe(i, xin):
        return _hier_moe(xin, ln2[i], router_group_w[i], router_group_b[i], router_expert_w[i],
                         router_expert_b[i], expert_w_gate[i], expert_w_up[i], expert_w_down[i])

    q, k, v, gt, qm = _ret_inproj(x2, pos_col, ln1[0], ret_w_in[0])
    shp = (B, S, MAIN_WIDTH)
    y = _retention(q.reshape(shp), k.reshape(shp), v.reshape(shp), gt.reshape(shp), ret_gn[0])
    x2 = _out_proj(x2, y.reshape(T, MAIN_WIDTH), mem_path(0, qm), w_out[0])
    x2 = moe(0, x2)

    k_sh, v_sh = _shared_kv(x2, pos_col, kv_ln, kv_w_down, kv_lora_norm, kv_w_up, k_norm)

    q1, qm1 = _mla_inproj(x2, pos_col, ln1[1], mla_w_in[0], q_lora_norm[0], mla_w_uq[0], q_norm[0])
    kw = N_MAIN_HEADS * HEAD_PAD
    y1 = _flash(q1.reshape(B, S, kw), k_sh.reshape(B, S, kw), v_sh.reshape(shp))
    x2 = _out_proj(x2, y1.reshape(T, MAIN_WIDTH), mem_path(1, qm1), w_out[1])
    x2 = moe(1, x2)
    return x2.reshape(B, S, D)
```

```python
import functools
import math

import jax
import jax.numpy as jnp
from jax import lax
from jax.experimental import pallas as pl
from jax.experimental.pallas import tpu as pltpu

F32 = jnp.float32
BF16 = jnp.bfloat16

HEAD_DIM = 64
N_MAIN_HEADS = 12
MAIN_WIDTH = N_MAIN_HEADS * HEAD_DIM
N_MEM_HEADS = 4
MEM_WIDTH = N_MEM_HEADS * HEAD_DIM
RET_CHUNK = 128
ROPE_THETA = 10000.0
Q_LORA = 384
KV_LORA = 256
QK_NOPE = 64
QK_ROPE = 32
QK_HEAD = QK_NOPE + QK_ROPE
V_HEAD = 64
N_GROUPS = 4
EXPERTS_PER_GROUP = 8
N_EXPERTS = N_GROUPS * EXPERTS_PER_GROUP
D_EXPERT = 256
MOE_BLOCK = 256
EPS = 1e-6

LANES = 128
SUBLANES = 8
VMEM_LIMIT = 48 * 1024 * 1024
NEG_BIG = -1e30

N_PAIRS = N_MAIN_HEADS // 2
ROW_TILE = 512
MOE_TILE = 256
DMA_UNROLL = 8


def _cparams(sem):
    return pltpu.CompilerParams(dimension_semantics=sem, vmem_limit_bytes=VMEM_LIMIT)


def _dot(a, b):
    return jnp.dot(a, b, preferred_element_type=F32)


def _dot_nt(a, b):
    return lax.dot_general(a, b, (((1,), (1,)), ((), ())), preferred_element_type=F32)


def _dot_tn(a, b):
    return lax.dot_general(a, b, (((0,), (0,)), ((), ())), preferred_element_type=F32)


def _rmsnorm(xf, g):
    return xf * lax.rsqrt(jnp.mean(xf * xf, axis=-1, keepdims=True) + EPS) * g


def _rope_tables(pos_col, inv_row, sgn_row):
    ang = pos_col.astype(F32) * inv_row
    return jnp.cos(ang), jnp.sin(ang) * sgn_row


def _rope_apply(y, cos, sin_signed, first, half):
    partner = jnp.where(first, pltpu.roll(y, LANES - half, 1), pltpu.roll(y, half, 1))
    return y * cos + partner * sin_signed


def _ret_inproj_kernel(x_ref, pos_ref, g_ref, inv_ref, sgn_ref, w_ref,
                       q_ref, k_ref, v_ref, gt_ref, qm_ref):
    h = _rmsnorm(x_ref[...], g_ref[...]).astype(BF16)
    cos, sin_s = _rope_tables(pos_ref[...], inv_ref[...], sgn_ref[...])
    first = sgn_ref[...] < 0.0
    mw = MAIN_WIDTH
    half = HEAD_DIM // 2
    qk = _dot(h, w_ref[:, :2 * mw])
    groups = range(2 * mw // LANES)
    ys = [qk[:, j * LANES:(j + 1) * LANES] for j in groups]
    fwd = [pltpu.roll(ys[j], LANES - half, 1) for j in groups]
    bwd = [pltpu.roll(ys[j], half, 1) for j in groups]
    outs = [(ys[j] * cos + jnp.where(first, fwd[j], bwd[j]) * sin_s).astype(BF16) for j in groups]
    n_q = mw // LANES
    for j in range(n_q):
        q_ref[:, j * LANES:(j + 1) * LANES] = outs[j]
        k_ref[:, j * LANES:(j + 1) * LANES] = outs[n_q + j]
    v_ref[...] = _dot(h, w_ref[:, 2 * mw:3 * mw]).astype(BF16)
    gt_ref[...] = _dot(h, w_ref[:, 3 * mw:4 * mw]).astype(BF16)
    qm_ref[...] = _dot(h, w_ref[:, 4 * mw:4 * mw + MEM_WIDTH]).astype(BF16)


def _ret_inproj(x2, pos_col, g, w_in):
    T, D = x2.shape
    tm = min(ROW_TILE, T)
    half = HEAD_DIM // 2
    inv = ROPE_THETA ** (-jnp.arange(half, dtype=F32) / half)
    lane = jnp.arange(LANES)
    inv_row = inv[lane % half][None, :]
    sgn_row = jnp.where((lane % HEAD_DIM) < half, -1.0, 1.0).astype(F32)[None, :]
    row = lambda i: (i, 0)
    fixed = lambda i: (0, 0)
    n_in = w_in.shape[1]
    outs = pl.pallas_call(
        _ret_inproj_kernel,
        grid=(T // tm,),
        in_specs=[
            pl.BlockSpec((tm, D), row),
            pl.BlockSpec((tm, 1), row),
            pl.BlockSpec((1, D), fixed),
            pl.BlockSpec((1, LANES), fixed),
            pl.BlockSpec((1, LANES), fixed),
            pl.BlockSpec((D, n_in), fixed),
        ],
        out_specs=[pl.BlockSpec((tm, MAIN_WIDTH), row)] * 4 + [pl.BlockSpec((tm, MEM_WIDTH), row)],
        out_shape=[jax.ShapeDtypeStruct((T, MAIN_WIDTH), BF16)] * 4
        + [jax.ShapeDtypeStruct((T, MEM_WIDTH), BF16)],
        compiler_params=_cparams(("parallel",)),
        name="ret_inproj",
    )(x2, pos_col, g[None, :], inv_row, sgn_row, w_in.astype(BF16))
    return outs


def _retention_tables():
    H, C, d = N_MAIN_HEADS, RET_CHUNK, HEAD_DIM
    log_g = jnp.log1p(-jnp.exp2(-5.0 - jnp.arange(H, dtype=F32)))
    idx = jnp.arange(C, dtype=F32)
    rel = idx[:, None] - idx[None, :]
    scale = d ** -0.5
    decay_in = jnp.where(rel[None] >= 0,
                         jnp.exp(log_g[:, None, None] * jnp.maximum(rel, 0.0)[None]), 0.0) * scale
    kdec = jnp.exp(log_g[None, :] * (C - 1.0 - idx)[:, None]) * scale
    qdec = jnp.exp(log_g[None, :] * (idx + 1.0)[:, None])
    cdec = jnp.exp(log_g * C)

    def lanes(t):
        return jnp.repeat(t, d, axis=1).reshape(C, N_PAIRS, 2 * d).transpose(1, 0, 2)

    head_of_lane = jnp.arange(2 * d) // d
    same = (head_of_lane[:, None] == head_of_lane[None, :]).astype(F32)
    cd_lane = jnp.repeat(cdec, d).reshape(N_PAIRS, 2 * d)
    state_decay = cd_lane[:, :, None] * same[None]
    decay_in = decay_in.reshape(N_PAIRS, 2, C, C)
    return decay_in, lanes(kdec), lanes(qdec), state_decay, same


def _retention_kernel(q_ref, k_ref, v_ref, gt_ref, dm_ref, kd_ref, qd_ref, sd_ref, same_ref,
                      gn_ref, o_ref, r_ref):
    n = pl.program_id(1)

    @pl.when(n == 0)
    def _():
        r_ref[...] = jnp.zeros_like(r_ref)

    lane = lax.broadcasted_iota(jnp.int32, (1, LANES), 1)
    head_a = lane < HEAD_DIM
    keep_a = jnp.where(head_a, 1.0, 0.0).astype(BF16)
    keep_b = jnp.where(head_a, 0.0, 1.0).astype(BF16)
    same = same_ref[...]
    avg = (same * (1.0 / HEAD_DIM)).astype(BF16)
    P = range(N_PAIRS)
    sl = [slice(p * LANES, (p + 1) * LANES) for p in P]
    q = [q_ref[0, :, sl[p]] for p in P]
    k = [k_ref[0, :, sl[p]] for p in P]
    v = [v_ref[0, :, sl[p]] for p in P]
    r_prev = [r_ref[p] for p in P]
    s_a = [_dot_nt(q[p] * keep_a, k[p]) * dm_ref[p, 0] for p in P]
    s_b = [_dot_nt(q[p] * keep_b, k[p]) * dm_ref[p, 1] for p in P]
    cross = [_dot((q[p].astype(F32) * qd_ref[p]).astype(BF16), r_prev[p].astype(BF16)) for p in P]
    u = [_dot_tn((k[p].astype(F32) * kd_ref[p]).astype(BF16), v[p]) for p in P]
    for p in P:
        r_ref[p] = sd_ref[p] * r_prev[p] + same * u[p]
    y = [_dot(s_a[p].astype(BF16), v[p] * keep_a) + _dot(s_b[p].astype(BF16), v[p] * keep_b) + cross[p]
         for p in P]
    yc = [y[p] - _dot(y[p].astype(BF16), avg) for p in P]
    var = [_dot((yc[p] * yc[p]).astype(BF16), avg) for p in P]
    for p in P:
        yn = yc[p] * lax.rsqrt(var[p] + EPS) * gn_ref[:, sl[p]]
        g = gt_ref[0, :, sl[p]].astype(F32)
        o_ref[0, :, sl[p]] = (g / (1.0 + jnp.exp(-g)) * yn).astype(BF16)


def _retention(q, k, v, gt, ret_gn):
    B, S, W = q.shape
    C = RET_CHUNK
    dm, kd, qd, sd, same = _retention_tables()
    tok = lambda b, n: (b, n, 0)
    fixed3 = lambda b, n: (0, 0, 0)
    return pl.pallas_call(
        _retention_kernel,
        grid=(B, S // C),
        in_specs=[pl.BlockSpec((1, C, W), tok)] * 4 + [
            pl.BlockSpec((N_PAIRS, 2, C, C), lambda b, n: (0, 0, 0, 0)),
            pl.BlockSpec((N_PAIRS, C, LANES), fixed3),
            pl.BlockSpec((N_PAIRS, C, LANES), fixed3),
            pl.BlockSpec((N_PAIRS, LANES, LANES), fixed3),
            pl.BlockSpec((LANES, LANES), lambda b, n: (0, 0)),
            pl.BlockSpec((1, W), lambda b, n: (0, 0)),
        ],
        out_specs=pl.BlockSpec((1, C, W), tok),
        out_shape=jax.ShapeDtypeStruct((B, S, W), BF16),
        scratch_shapes=[pltpu.VMEM((N_PAIRS, LANES, LANES), F32)],
        compiler_params=_cparams(("parallel", "arbitrary")),
        name="retention",
    )(q, k, v, gt, dm, kd, qd, sd, same, ret_gn[None, :])


def _mem_kv_kernel(mem_ref, w_ref, kg_ref, k_ref, v_ref):
    kv = _dot(mem_ref[...].astype(BF16), w_ref[...])
    lane = lax.broadcasted_iota(jnp.int32, (1, LANES), 1)
    head_a = lane < HEAD_DIM
    inv_d = 1.0 / HEAD_DIM
    for j in range(MEM_WIDTH // LANES):
        kj = kv[:, j * LANES:(j + 1) * LANES]
        k2 = kj * kj
        ms_a = jnp.sum(jnp.where(head_a, k2, 0.0), axis=-1, keepdims=True) * inv_d
        ms_b = jnp.sum(jnp.where(head_a, 0.0, k2), axis=-1, keepdims=True) * inv_d
        kn = kj * lax.rsqrt(jnp.where(head_a, ms_a, ms_b) + EPS) * kg_ref[...]
        k_ref[:, j * LANES:(j + 1) * LANES] = kn.astype(BF16)
    v_ref[...] = kv[:, MEM_WIDTH:].astype(BF16)


def _mem_kv(mem2, w_mem_kv, k_g):
    TM, D = mem2.shape
    tm = min(ROW_TILE, TM)
    kg_row = jnp.tile(k_g, LANES // HEAD_DIM)[None, :]
    row = lambda i: (i, 0)
    fixed = lambda i: (0, 0)
    return pl.pallas_call(
        _mem_kv_kernel,
        grid=(TM // tm,),
        in_specs=[pl.BlockSpec((tm, D), row), pl.BlockSpec((D, 2 * MEM_WIDTH), fixed),
                  pl.BlockSpec((1, LANES), fixed)],
        out_specs=[pl.BlockSpec((tm, MEM_WIDTH), row)] * 2,
        out_shape=[jax.ShapeDtypeStruct((TM, MEM_WIDTH), BF16)] * 2,
        compiler_params=_cparams(("parallel",)),
        name="mem_kv",
    )(mem2, w_mem_kv.astype(BF16), kg_row)


def _mem_attn_kernel(qm_ref, k_ref, v_ref, qg_ref, o_ref):
    lane = lax.broadcasted_iota(jnp.int32, (1, LANES), 1)
    head_a = lane < HEAD_DIM
    inv_d = 1.0 / HEAD_DIM
    scale = HEAD_DIM ** -0.5
    for j in range(MEM_WIDTH // LANES):
        sl = slice(j * LANES, (j + 1) * LANES)
        qj = qm_ref[0, :, sl].astype(F32)
        q2 = qj * qj
        ms_a = jnp.sum(jnp.where(head_a, q2, 0.0), axis=-1, keepdims=True) * inv_d
        ms_b = jnp.sum(jnp.where(head_a, 0.0, q2), axis=-1, keepdims=True) * inv_d
        qn = qj * lax.rsqrt(jnp.where(head_a, ms_a, ms_b) + EPS) * (qg_ref[...] * scale)
        kj = k_ref[0, :, sl]
        vj = v_ref[0, :, sl]
        out = None
        for sel in (head_a, jnp.logical_not(head_a)):
            s = _dot_nt(jnp.where(sel, qn, 0.0).astype(BF16), kj)
            p = jnp.exp(s - jnp.max(s, axis=-1, keepdims=True))
            p = p / jnp.sum(p, axis=-1, keepdims=True)
            o = _dot(p.astype(BF16), jnp.where(sel, vj, jnp.zeros_like(vj)))
            out = o if out is None else out + o
        o_ref[0, :, sl] = out.astype(BF16)


def _mem_attn(qm, k_m, v_m, q_g):
    B, S, _ = qm.shape
    M = k_m.shape[1]
    tm = min(ROW_TILE, S)
    qg_row = jnp.tile(q_g, LANES // HEAD_DIM)[None, :]
    return pl.pallas_call(
        _mem_attn_kernel,
        grid=(B, S // tm),
        in_specs=[pl.BlockSpec((1, tm, MEM_WIDTH), lambda b, i: (b, i, 0)),
                  pl.BlockSpec((1, M, MEM_WIDTH), lambda b, i: (b, 0, 0)),
                  pl.BlockSpec((1, M, MEM_WIDTH), lambda b, i: (b, 0, 0)),
                  pl.BlockSpec((1, LANES), lambda b, i: (0, 0))],
        out_specs=pl.BlockSpec((1, tm, MEM_WIDTH), lambda b, i: (b, i, 0)),
        out_shape=jax.ShapeDtypeStruct((B, S, MEM_WIDTH), BF16),
        compiler_params=_cparams(("parallel", "parallel")),
        name="mem_attn",
    )(qm, k_m, v_m, qg_row)


def _out_proj_kernel(x_ref, y_ref, m_ref, wy_ref, wm_ref, o_ref):
    o_ref[...] = x_ref[...] + _dot(y_ref[...], wy_ref[...]) + _dot(m_ref[...], wm_ref[...])


def _out_proj(x2, y2, m2, w_out):
    T, D = x2.shape
    tm = min(ROW_TILE, T)
    row = lambda i: (i, 0)
    fixed = lambda i: (0, 0)
    w = w_out.astype(BF16)
    return pl.pallas_call(
        _out_proj_kernel,
        grid=(T // tm,),
        in_specs=[pl.BlockSpec((tm, D), row), pl.BlockSpec((tm, MAIN_WIDTH), row),
                  pl.BlockSpec((tm, MEM_WIDTH), row),
                  pl.BlockSpec((MAIN_WIDTH, D), fixed), pl.BlockSpec((MEM_WIDTH, D), fixed)],
        out_specs=pl.BlockSpec((tm, D), row),
        out_shape=jax.ShapeDtypeStruct((T, D), F32),
        compiler_params=_cparams(("parallel",)),
        name="out_proj",
    )(x2, y2, m2, w[:MAIN_WIDTH], w[MAIN_WIDTH:])


ROUTER_LANE0 = N_GROUPS


def _router_kernel(x_ref, g_ref, whi_ref, wlo_ref, b_ref, info_ref, cnt_ref, base_ref):
    i = pl.program_id(0)

    @pl.when(i == 0)
    def _():
        base_ref[...] = jnp.zeros_like(base_ref)

    tm = x_ref.shape[0]
    h = _rmsnorm(x_ref[...], g_ref[...])
    h_hi = h.astype(BF16)
    h_lo = (h - h_hi.astype(F32)).astype(BF16)
    logits = (_dot(h_hi, whi_ref[...]) + _dot(h_lo, whi_ref[...]) + _dot(h_hi, wlo_ref[...])
              + b_ref[...])
    lane_i = lax.broadcasted_iota(jnp.int32, (tm, LANES), 1)
    lane = lane_i.astype(F32)
    big = float(LANES)

    is_g = lane_i < N_GROUPS
    lg = jnp.where(is_g, logits, NEG_BIG)
    mg = jnp.max(lg, axis=-1, keepdims=True)
    zg = jnp.sum(jnp.where(is_g, jnp.exp(lg - mg), 0.0), axis=-1, keepdims=True)
    p_grp = 1.0 / zg
    grp = jnp.min(jnp.where(is_g & (lg == mg), lane, big), axis=-1, keepdims=True)

    e_lane = lane_i - ROUTER_LANE0
    e_grp = (e_lane >> int(math.log2(EXPERTS_PER_GROUP))).astype(F32)
    is_e = (e_lane >= 0) & (e_lane < N_EXPERTS) & (e_grp == grp)
    le = jnp.where(is_e, logits, NEG_BIG)
    me = jnp.max(le, axis=-1, keepdims=True)
    ee = jnp.where(is_e, jnp.exp(le - me), 0.0)
    prob = ee / jnp.sum(ee, axis=-1, keepdims=True)
    p1 = jnp.max(prob, axis=-1, keepdims=True)
    i1 = jnp.min(jnp.where(is_e & (prob == p1), lane, big), axis=-1, keepdims=True)
    rest = is_e & (lane != i1)
    p2 = jnp.max(jnp.where(rest, prob, -1.0), axis=-1, keepdims=True)
    i2 = jnp.min(jnp.where(rest & (prob == p2), lane, big), axis=-1, keepdims=True)
    gate1 = p_grp * p1 / (p1 + p2)
    gate2 = p_grp * p2 / (p1 + p2)

    sel1 = lane == i1
    sel2 = lane == i2
    onehot = jnp.where(sel1 | sel2, 1.0, 0.0)
    r_io = lax.broadcasted_iota(jnp.int32, (tm, tm), 0)
    c_io = lax.broadcasted_iota(jnp.int32, (tm, tm), 1)
    lower = jnp.where(r_io > c_io, 1.0, 0.0).astype(BF16)
    before = _dot(lower, onehot.astype(BF16)) + base_ref[...]
    rank1 = jnp.sum(jnp.where(sel1, before, 0.0), axis=-1, keepdims=True)
    rank2 = jnp.sum(jnp.where(sel2, before, 0.0), axis=-1, keepdims=True)
    base_ref[...] += jnp.sum(onehot, axis=0, keepdims=True)
    cnt_ref[...] = base_ref[...]

    e1 = i1 - float(ROUTER_LANE0)
    e2 = i2 - float(ROUTER_LANE0)
    info = jnp.zeros((tm, LANES), F32)
    for col, val in enumerate((gate1, gate2, e1, e2, rank1, rank2)):
        info = jnp.where(lane_i == col, val, info)
    info_ref[...] = info


def _router(x2, g, w_grp, b_grp, w_exp, b_exp):
    T, D = x2.shape
    tm = min(ROW_TILE, T)
    w = jnp.zeros((D, LANES), F32)
    w = w.at[:, :N_GROUPS].set(w_grp).at[:, ROUTER_LANE0:ROUTER_LANE0 + N_EXPERTS].set(w_exp)
    b = jnp.zeros((1, LANES), F32)
    b = b.at[0, :N_GROUPS].set(b_grp).at[0, ROUTER_LANE0:ROUTER_LANE0 + N_EXPERTS].set(b_exp)
    w_hi = w.astype(BF16)
    w_lo = (w - w_hi.astype(F32)).astype(BF16)
    row = lambda i: (i, 0)
    fixed = lambda i: (0, 0)
    info, cnt = pl.pallas_call(
        _router_kernel,
        grid=(T // tm,),
        in_specs=[pl.BlockSpec((tm, D), row), pl.BlockSpec((1, D), fixed),
                  pl.BlockSpec((D, LANES), fixed), pl.BlockSpec((D, LANES), fixed),
                  pl.BlockSpec((1, LANES), fixed)],
        out_specs=[pl.BlockSpec((tm, LANES), row), pl.BlockSpec((1, LANES), fixed)],
        out_shape=[jax.ShapeDtypeStruct((T, LANES), F32), jax.ShapeDtypeStruct((1, LANES), F32)],
        scratch_shapes=[pltpu.VMEM((1, LANES), F32)],
        compiler_params=_cparams(("arbitrary",)),
        name="moe_router",
    )(x2, g[None, :], w_hi, w_lo, b)
    return info, cnt


def _row_copy(src, src_row, dst, dst_row, sem):
    def rows(r):
        start = r * SUBLANES
        if not isinstance(start, int):
            start = pl.multiple_of(start, SUBLANES)
        return pl.ds(start, SUBLANES)
    return pltpu.make_async_copy(src.at[rows(src_row), :], dst.at[rows(dst_row), :], sem)


def _block_copy(src, dst, blk, sem):
    rows = MOE_BLOCK * SUBLANES
    return pltpu.make_async_copy(src, dst.at[pl.ds(pl.multiple_of(blk * rows, rows), rows), :], sem)


def _dispatch_kernel(dest_ref, zrow_ref, zcnt_ref, nblk_ref, x_ref, g_ref, xs_ref,
                     buf_ref, zbuf_ref, sem, zsem):
    i = pl.program_id(0)
    tm = x_ref.shape[0]

    @pl.when(i == 0)
    def _():
        zbuf_ref[...] = jnp.zeros_like(zbuf_ref)
        n_blocks = xs_ref.shape[0] // (MOE_BLOCK * SUBLANES)

        def tail_start(b, c):
            _block_copy(zbuf_ref, xs_ref, b, zsem).start()
            return c
        lax.fori_loop(nblk_ref[0], n_blocks, tail_start, 0)

        def tail_wait(b, c):
            _block_copy(zbuf_ref, xs_ref, b, zsem).wait()
            return c
        lax.fori_loop(nblk_ref[0], n_blocks, tail_wait, 0)

        def per_expert(e, c):
            def start(r, c2):
                _row_copy(zbuf_ref, 0, xs_ref, zrow_ref[e] + r, zsem).start()
                return c2
            lax.fori_loop(0, zcnt_ref[e], start, 0)

            def wait(r, c2):
                _row_copy(zbuf_ref, 0, xs_ref, zrow_ref[e] + r, zsem).wait()
                return c2
            lax.fori_loop(0, zcnt_ref[e], wait, 0)
            return c
        lax.fori_loop(0, N_EXPERTS, per_expert, 0)

    h = _rmsnorm(x_ref[...], g_ref[...])
    for s in range(SUBLANES):
        buf_ref[pl.ds(s, tm, stride=SUBLANES), :] = h[:, s * LANES:(s + 1) * LANES]

    def start(tb, c):
        for u in range(DMA_UNROLL):
            t = tb * DMA_UNROLL + u
            for kk in range(2):
                _row_copy(buf_ref, t, xs_ref, dest_ref[(i * tm + t) * 2 + kk], sem).start()
        return c
    lax.fori_loop(0, tm // DMA_UNROLL, start, 0)

    for _ in range(2):
        pltpu.make_async_copy(buf_ref, xs_ref.at[pl.ds(0, tm * SUBLANES), :], sem).wait()


def _dispatch(x2, g, dest_flat, zrow, zcnt, nblk_used, n_rows):
    T, D = x2.shape
    tm = min(MOE_TILE, T)
    return pl.pallas_call(
        _dispatch_kernel,
        grid_spec=pltpu.PrefetchScalarGridSpec(
            num_scalar_prefetch=4,
            grid=(T // tm,),
            in_specs=[pl.BlockSpec((tm, D), lambda i, *_: (i, 0)),
                      pl.BlockSpec((1, D), lambda i, *_: (0, 0))],
            out_specs=pl.BlockSpec(memory_space=pl.ANY),
            scratch_shapes=[pltpu.VMEM((tm * SUBLANES, LANES), F32),
                            pltpu.VMEM((MOE_BLOCK * SUBLANES, LANES), F32),
                            pltpu.SemaphoreType.DMA(()), pltpu.SemaphoreType.DMA(())],
        ),
        out_shape=jax.ShapeDtypeStruct((n_rows * SUBLANES, LANES), F32),
        compiler_params=_cparams(("arbitrary",)),
        name="moe_dispatch",
    )(dest_flat, zrow, zcnt, nblk_used, x2, g[None, :])


def _expert_kernel(blk_e_ref, nblk_ref, xs_ref, wg_ref, wu_ref, wd_ref, ys_ref):
    b = pl.program_id(0)

    @pl.when(b < nblk_ref[0])
    def _():
        nb = MOE_BLOCK
        x = jnp.concatenate(
            [xs_ref[pl.ds(s, nb, stride=SUBLANES), :] for s in range(SUBLANES)], axis=-1
        ).astype(BF16)
        a = _dot(x, wg_ref[0])
        u = _dot(x, wu_ref[0])
        hid = (a / (1.0 + jnp.exp(-a)) * u).astype(BF16)
        y = _dot(hid, wd_ref[0])
        for s in range(SUBLANES):
            ys_ref[pl.ds(s, nb, stride=SUBLANES), :] = y[:, s * LANES:(s + 1) * LANES]

    @pl.when(b >= nblk_ref[0])
    def _():
        ys_ref[...] = jnp.zeros_like(ys_ref)


def _experts(xs, blk_e, nblk_used, w_gate, w_up, w_down):
    rows = xs.shape[0]
    nblk = rows // (MOE_BLOCK * SUBLANES)
    D = w_gate.shape[1]
    blk = lambda b, be, nb: (jnp.minimum(b, nb[0] - 1), 0)
    out_blk = lambda b, be, nb: (b, 0)
    wsel = lambda b, be, nb: (be[b], 0, 0)
    return pl.pallas_call(
        _expert_kernel,
        grid_spec=pltpu.PrefetchScalarGridSpec(
            num_scalar_prefetch=2,
            grid=(nblk,),
            in_specs=[pl.BlockSpec((MOE_BLOCK * SUBLANES, LANES), blk),
                      pl.BlockSpec((1, D, D_EXPERT), wsel),
                      pl.BlockSpec((1, D, D_EXPERT), wsel),
                      pl.BlockSpec((1, D_EXPERT, D), wsel)],
            out_specs=pl.BlockSpec((MOE_BLOCK * SUBLANES, LANES), out_blk),
        ),
        out_shape=jax.ShapeDtypeStruct((rows, LANES), F32),
        compiler_params=_cparams(("arbitrary",)),
        name="moe_experts",
    )(blk_e, nblk_used, xs, w_gate.astype(BF16), w_up.astype(BF16), w_down.astype(BF16))


def _combine_kernel(dest_ref, x_ref, info_ref, ys_ref, o_ref, buf0_ref, buf1_ref, sem):
    i = pl.program_id(0)
    tm = x_ref.shape[0]
    bufs = (buf0_ref, buf1_ref)

    def start(tb, c):
        for u in range(DMA_UNROLL):
            t = tb * DMA_UNROLL + u
            for kk in range(2):
                _row_copy(ys_ref, dest_ref[(i * tm + t) * 2 + kk], bufs[kk], t, sem).start()
        return c
    lax.fori_loop(0, tm // DMA_UNROLL, start, 0)

    for buf in bufs:
        pltpu.make_async_copy(ys_ref.at[pl.ds(0, tm * SUBLANES), :], buf, sem).wait()

    info = info_ref[...]
    g0 = info[:, 0:1]
    g1 = info[:, 1:2]
    for s in range(SUBLANES):
        sl = slice(s * LANES, (s + 1) * LANES)
        y0 = buf0_ref[pl.ds(s, tm, stride=SUBLANES), :]
        y1 = buf1_ref[pl.ds(s, tm, stride=SUBLANES), :]
        o_ref[:, sl] = x_ref[:, sl] + (y0 * g0 + y1 * g1)


def _combine(x2, info, ys, dest_flat):
    T, D = x2.shape
    tm = min(MOE_TILE, T)
    return pl.pallas_call(
        _combine_kernel,
        grid_spec=pltpu.PrefetchScalarGridSpec(
            num_scalar_prefetch=1,
            grid=(T // tm,),
            in_specs=[pl.BlockSpec((tm, D), lambda i, *_: (i, 0)),
                      pl.BlockSpec((tm, LANES), lambda i, *_: (i, 0)),
                      pl.BlockSpec(memory_space=pl.ANY)],
            out_specs=pl.BlockSpec((tm, D), lambda i, *_: (i, 0)),
            scratch_shapes=[pltpu.VMEM((tm * SUBLANES, LANES), F32),
                            pltpu.VMEM((tm * SUBLANES, LANES), F32),
                            pltpu.SemaphoreType.DMA(())],
        ),
        out_shape=jax.ShapeDtypeStruct((T, D), F32),
        compiler_params=_cparams(("arbitrary",)),
        name="moe_combine",
    )(dest_flat, x2, info, ys)


def _hier_moe(x2, ln2, w_grp, b_grp, w_exp, b_exp, w_gate, w_up, w_down):
    T, D = x2.shape
    info, cnt = _router(x2, ln2, w_grp, b_grp, w_exp, b_exp)
    counts = cnt[0, ROUTER_LANE0:ROUTER_LANE0 + N_EXPERTS].astype(jnp.int32)
    padded = (counts + MOE_BLOCK - 1) // MOE_BLOCK * MOE_BLOCK
    pad_end = jnp.cumsum(padded)
    pad_start = pad_end - padded
    eid = info[:, 2:4].astype(jnp.int32)
    rank = info[:, 4:6].astype(jnp.int32)
    dest_flat = (pad_start[eid] + rank).reshape(T * 2)
    n_rows = T * 2 + N_EXPERTS * MOE_BLOCK
    nblk = n_rows // MOE_BLOCK
    blk_row = jnp.arange(nblk, dtype=jnp.int32) * MOE_BLOCK
    blk_e = jnp.minimum(jnp.sum((pad_end[None, :] <= blk_row[:, None]).astype(jnp.int32), axis=1),
                        N_EXPERTS - 1).astype(jnp.int32)
    nblk_used = (pad_end[-1:] // MOE_BLOCK).astype(jnp.int32)
    zrow = (pad_start + counts).astype(jnp.int32)
    zcnt = (padded - counts).astype(jnp.int32)

    xs = _dispatch(x2, ln2, dest_flat, zrow, zcnt, nblk_used, n_rows)
    ys = _experts(xs, blk_e, nblk_used, w_gate, w_up, w_down)
    return _combine(x2, info, ys, dest_flat)


HEAD_PAD = LANES
LATENT_PAD = 3 * LANES


def _rot_partner():
    half = QK_ROPE // 2
    r = jnp.arange(QK_ROPE)
    return jnp.where(r < half, r + half, r - half), jnp.where(r < half, -1.0, 1.0).astype(F32)


def _mla_rope_rows():
    half = QK_ROPE // 2
    inv = ROPE_THETA ** (-jnp.arange(half, dtype=F32) / half)
    lane = jnp.arange(LANES)
    r = lane - QK_NOPE
    in_rope = (r >= 0) & (r < QK_ROPE)
    inv_row = jnp.where(in_rope, inv[jnp.clip(r, 0, QK_ROPE - 1) % half], 0.0)[None, :]
    rope_row = in_rope.astype(F32)[None, :]
    real_row = (lane < QK_HEAD).astype(F32)[None, :]
    return inv_row, rope_row, real_row


def _head_gain_row(g):
    partner, _ = _rot_partner()
    return jnp.concatenate([g, g[QK_NOPE + partner]])[None, :]


def _with_partner_cols(w3):
    partner, sign = _rot_partner()
    rot = w3[:, :, QK_NOPE + partner] * sign
    return jnp.concatenate([w3, rot], axis=-1).reshape(w3.shape[0], N_MAIN_HEADS * HEAD_PAD)


def _heads_norm_rope(ys, gain_rows, scales, real_row, cos_real, sin_rope):
    n = range(len(ys))
    row_id = lax.broadcasted_iota(jnp.int32, (LANES, LANES), 0)
    ones_real = jnp.where(row_id < QK_HEAD, 1.0, 0.0).astype(BF16)
    ms = [_dot((ys[i] * ys[i]).astype(BF16), ones_real) * (1.0 / QK_HEAD) for i in n]
    yn = [ys[i] * (lax.rsqrt(ms[i] + EPS) * scales[i]) * gain_rows[i] for i in n]
    rolled = [pltpu.roll(yn[i], LANES - QK_ROPE, 1) for i in n]
    return [(yn[i] * cos_real + rolled[i] * sin_rope).astype(BF16) for i in n]


def _mla_qkv_kernel(x_ref, pos_ref, lnkv_ref, lnq_ref, wd_ref, scale_ref, wkv_ref, kg_ref,
                    win_ref, qlg_ref, wuq_ref, qg_ref, inv_ref, rope_ref, real_ref,
                    k_ref, v_ref, q_ref, qm_ref):
    x = x_ref[...]
    xr = x * lax.rsqrt(jnp.mean(x * x, axis=-1, keepdims=True) + EPS)
    ang = pos_ref[...].astype(F32) * inv_ref[...]
    real_row = real_ref[...]
    cos_real = jnp.cos(ang) * real_row
    sin_rope = jnp.sin(ang) * rope_ref[...]

    ckr = _dot((xr * lnkv_ref[...]).astype(BF16), wd_ref[...])
    c = ckr[:, :KV_LORA]
    r = lax.rsqrt(jnp.mean(c * c, axis=-1, keepdims=True) + EPS)
    lane = lax.broadcasted_iota(jnp.int32, (1, LATENT_PAD), 1)
    lhs = (ckr * jnp.where(lane < KV_LORA, r * scale_ref[...], 1.0)).astype(BF16)
    kv = _dot(lhs, wkv_ref[...])
    proj = _dot((xr * lnq_ref[...]).astype(BF16), win_ref[...])
    cq = _rmsnorm(proj[:, :Q_LORA], qlg_ref[...]).astype(BF16)
    q = _dot(cq, wuq_ref[...])
    qm_ref[...] = proj[:, Q_LORA:].astype(BF16)
    v_ref[...] = kv[:, N_MAIN_HEADS * HEAD_PAD:].astype(BF16)
    sls = [slice(hh * HEAD_PAD, (hh + 1) * HEAD_PAD) for hh in range(N_MAIN_HEADS)]
    nh = N_MAIN_HEADS
    outs = _heads_norm_rope([kv[:, sl] for sl in sls] + [q[:, sl] for sl in sls],
                            [kg_ref[...]] * nh + [qg_ref[...]] * nh,
                            [1.0] * nh + [QK_HEAD ** -0.5] * nh, real_row, cos_real, sin_rope)
    for hh, sl in enumerate(sls):
        k_ref[:, sl] = outs[hh]
        q_ref[:, sl] = outs[nh + hh]


def _mla_qkv(x2, pos_col, kv_ln, w_dkv, kv_lora_g, w_ukv, k_g, ln1, w_in, q_lora_g, w_uq, q_g):
    T, D = x2.shape
    tm = min(ROW_TILE, T)
    lat = LATENT_PAD
    wd = jnp.pad(w_dkv, ((0, 0), (0, lat - w_dkv.shape[1]))).astype(BF16)
    scale_row = jnp.pad(kv_lora_g, (0, lat - KV_LORA), constant_values=1.0)[None, :]
    w3 = w_ukv.reshape(KV_LORA, N_MAIN_HEADS, QK_NOPE + V_HEAD)
    wk = jnp.zeros((lat, N_MAIN_HEADS, QK_HEAD), F32)
    wk = wk.at[:KV_LORA, :, :QK_NOPE].set(w3[:, :, :QK_NOPE])
    eye = jnp.eye(QK_ROPE, dtype=F32)
    wk = wk.at[KV_LORA:KV_LORA + QK_ROPE, :, QK_NOPE:].set(
        jnp.broadcast_to(eye[:, None, :], (QK_ROPE, N_MAIN_HEADS, QK_ROPE)))
    wv = jnp.zeros((lat, N_MAIN_HEADS * V_HEAD), F32)
    wv = wv.at[:KV_LORA].set(w3[:, :, QK_NOPE:].reshape(KV_LORA, N_MAIN_HEADS * V_HEAD))
    wkv = jnp.concatenate([_with_partner_cols(wk), wv], axis=1).astype(BF16)
    wuq = _with_partner_cols(w_uq.reshape(Q_LORA, N_MAIN_HEADS, QK_HEAD)).astype(BF16)
    inv_row, rope_row, real_row = _mla_rope_rows()
    row = lambda i: (i, 0)
    fixed = lambda i: (0, 0)
    kw = N_MAIN_HEADS * HEAD_PAD
    n_in = w_in.shape[1]
    lane_row = pl.BlockSpec((1, LANES), fixed)
    return pl.pallas_call(
        _mla_qkv_kernel,
        grid=(T // tm,),
        in_specs=[pl.BlockSpec((tm, D), row), pl.BlockSpec((tm, 1), row),
                  pl.BlockSpec((1, D), fixed), pl.BlockSpec((1, D), fixed),
                  pl.BlockSpec((D, lat), fixed), pl.BlockSpec((1, lat), fixed),
                  pl.BlockSpec((lat, kw + MAIN_WIDTH), fixed), lane_row,
                  pl.BlockSpec((D, n_in), fixed), pl.BlockSpec((1, Q_LORA), fixed),
                  pl.BlockSpec((Q_LORA, kw), fixed), lane_row,
                  lane_row, lane_row, lane_row],
        out_specs=[pl.BlockSpec((tm, kw), row), pl.BlockSpec((tm, MAIN_WIDTH), row),
                   pl.BlockSpec((tm, kw), row), pl.BlockSpec((tm, MEM_WIDTH), row)],
        out_shape=[jax.ShapeDtypeStruct((T, kw), BF16), jax.ShapeDtypeStruct((T, MAIN_WIDTH), BF16),
                   jax.ShapeDtypeStruct((T, kw), BF16), jax.ShapeDtypeStruct((T, MEM_WIDTH), BF16)],
        compiler_params=_cparams(("parallel",)),
        name="mla_qkv",
    )(x2, pos_col, kv_ln[None, :], ln1[None, :], wd, scale_row, wkv, _head_gain_row(k_g),
      w_in.astype(BF16), q_lora_g[None, :], wuq, _head_gain_row(q_g), inv_row, rope_row, real_row)


ATTN_TILE = 512


def _flash_kernel(qi_ref, kj_ref, q_ref, k_ref, v_ref, o_ref, m_ref, acc_ref):
    t = pl.program_id(2)
    i = qi_ref[t]
    j = kj_ref[t]
    tq = q_ref.shape[1]
    tk = k_ref.shape[1]

    @pl.when(j == 0)
    def _():
        m_ref[...] = jnp.full_like(m_ref, NEG_BIG)
        acc_ref[...] = jnp.zeros_like(acc_ref)

    lane = lax.broadcasted_iota(jnp.int32, (1, LANES), 1)
    head_a = lane < V_HEAD
    den_lane = (V_HEAD, 0)

    def step(masked):
        v = v_ref[0]
        if masked:
            q_idx = lax.broadcasted_iota(jnp.int32, (tq, tk), 0)
            k_idx = lax.broadcasted_iota(jnp.int32, (tq, tk), 1)
            visible = k_idx <= q_idx
        H = range(2)
        sl = [slice(hh * HEAD_PAD, (hh + 1) * HEAD_PAD) for hh in H]
        s = [_dot_nt(q_ref[0, :, sl[hh]], k_ref[0, :, sl[hh]]) for hh in H]
        if masked:
            s = [jnp.where(visible, s[hh], NEG_BIG) for hh in H]
        m_prev = [m_ref[hh] for hh in H]
        acc_prev = [acc_ref[hh] for hh in H]
        m_new = [jnp.maximum(m_prev[hh], jnp.max(s[hh], axis=-1, keepdims=True)) for hh in H]
        alpha = [jnp.exp(m_prev[hh] - m_new[hh]) for hh in H]
        m_wide = [jnp.concatenate([m_new[hh]] * (tk // LANES), axis=1) for hh in H]
        p = [jnp.exp(s[hh] - m_wide[hh]).astype(BF16) for hh in H]
        keep_row = [jnp.where(head_a, 1.0, 0.0).astype(BF16), jnp.where(head_a, 0.0, 1.0).astype(BF16)]
        den_row = [jnp.where(lane == den_lane[hh], 1.0, 0.0).astype(BF16) for hh in H]
        pv = [_dot(p[hh], v * keep_row[hh] + den_row[hh]) for hh in H]
        for hh in H:
            acc_ref[hh] = alpha[hh] * acc_prev[hh] + pv[hh]
            m_ref[hh] = m_new[hh]

    @pl.when(j < i)
    def _():
        step(False)

    @pl.when(j == i)
    def _():
        step(True)
        acc_a = acc_ref[0]
        acc_b = acc_ref[1]
        out_a = acc_a / acc_a[:, den_lane[0]:den_lane[0] + 1]
        out_b = acc_b / acc_b[:, den_lane[1]:den_lane[1] + 1]
        o_ref[0] = jnp.where(head_a, out_a, out_b).astype(BF16)


def _flash(q, k, v):
    B, S, _ = q.shape
    t = min(ATTN_TILE, S)
    n = S // t
    pairs = [(i, j) for i in range(n) for j in range(i + 1)]
    qi = jnp.array([p[0] for p in pairs], jnp.int32)
    kj = jnp.array([p[1] for p in pairs], jnp.int32)
    return pl.pallas_call(
        _flash_kernel,
        grid_spec=pltpu.PrefetchScalarGridSpec(
            num_scalar_prefetch=2,
            grid=(B, N_PAIRS, len(pairs)),
            in_specs=[pl.BlockSpec((1, t, 2 * HEAD_PAD), lambda b, p, s, qi, kj: (b, qi[s], p)),
                      pl.BlockSpec((1, t, 2 * HEAD_PAD), lambda b, p, s, qi, kj: (b, kj[s], p)),
                      pl.BlockSpec((1, t, LANES), lambda b, p, s, qi, kj: (b, kj[s], p))],
            out_specs=pl.BlockSpec((1, t, LANES), lambda b, p, s, qi, kj: (b, qi[s], p)),
            scratch_shapes=[pltpu.VMEM((2, t, LANES), F32), pltpu.VMEM((2, t, LANES), F32)],
        ),
        out_shape=jax.ShapeDtypeStruct((B, S, MAIN_WIDTH), BF16),
        compiler_params=_cparams(("parallel", "parallel", "arbitrary")),
        name="mla_flash",
    )(qi, kj, q, k, v)


def kernel(x, mem, positions, ln1, ln2, w_out, mem_w_kv, mem_q_norm, mem_k_norm, router_group_w, router_group_b, router_expert_w, router_expert_b, expert_w_gate, expert_w_up, expert_w_down, ret_w_in, ret_gn, kv_ln, kv_w_down, kv_lora_norm, kv_w_up, k_norm, mla_w_in, q_lora_norm, mla_w_uq, q_norm):
    B, S, D = x.shape
    M = mem.shape[1]
    T = B * S
    x2 = x.reshape(T, D)
    mem2 = mem.reshape(B * M, D)
    pos_col = positions.reshape(T, 1).astype(jnp.int32)

    def mem_path(i, qm):
        k_m, v_m = _mem_kv(mem2, mem_w_kv[i], mem_k_norm[i])
        return _mem_attn(qm.reshape(B, S, MEM_WIDTH), k_m.reshape(B, M, MEM_WIDTH),
                         v_m.reshape(B, M, MEM_WIDTH), mem_q_norm[i]).reshape(T, MEM_WIDTH)

    def moe(i, xin):
        return _hier_moe(xin, ln2[i], router_group_w[i], router_group_b[i], router_expert_w[i],
                         router_expert_b[i], expert_w_gate[i], expert_w_up[i], expert_w_down[i])

    q, k, v, gt, qm = _ret_inproj(x2, pos_col, ln1[0], ret_w_in[0])
    shp = (B, S, MAIN_WIDTH)
    y = _retention(q.reshape(shp), k.reshape(shp), v.reshape(shp), gt.reshape(shp), ret_gn[0])
    x2 = _out_proj(x2, y.reshape(T, MAIN_WIDTH), mem_path(0, qm), w_out[0])
    x2 = moe(0, x2)

    k_sh, v_sh, q1, qm1 = _mla_qkv(x2, pos_col, kv_ln, kv_w_down, kv_lora_norm, kv_w_up, k_norm,
                                   ln1[1], mla_w_in[0], q_lora_norm[0], mla_w_uq[0], q_norm[0])
    kw = N_MAIN_HEADS * HEAD_PAD
    y1 = _flash(q1.reshape(B, S, kw), k_sh.reshape(B, S, kw), v_sh.reshape(shp))
    x2 = _out_proj(x2, y1.reshape(T, MAIN_WIDTH), mem_path(1, qm1), w_out[1])
    x2 = moe(1, x2)
    return x2.reshape(B, S, D)
```

```python
import functools
import math

import jax
import jax.numpy as jnp
from jax import lax
from jax.experimental import pallas as pl
from jax.experimental.pallas import tpu as pltpu

F32 = jnp.float32
BF16 = jnp.bfloat16

HEAD_DIM = 64
N_MAIN_HEADS = 12
MAIN_WIDTH = N_MAIN_HEADS * HEAD_DIM
N_MEM_HEADS = 4
MEM_WIDTH = N_MEM_HEADS * HEAD_DIM
RET_CHUNK = 128
ROPE_THETA = 10000.0
Q_LORA = 384
KV_LORA = 256
QK_NOPE = 64
QK_ROPE = 32
QK_HEAD = QK_NOPE + QK_ROPE
V_HEAD = 64
N_GROUPS = 4
EXPERTS_PER_GROUP = 8
N_EXPERTS = N_GROUPS * EXPERTS_PER_GROUP
D_EXPERT = 256
MOE_BLOCK = 256
EPS = 1e-6

LANES = 128
SUBLANES = 8
VMEM_LIMIT = 48 * 1024 * 1024
NEG_BIG = -1e30

N_PAIRS = N_MAIN_HEADS // 2
ROW_TILE = 512
MOE_TILE = 256
DMA_UNROLL = 8


def _cparams(sem):
    return pltpu.CompilerParams(dimension_semantics=sem, vmem_limit_bytes=VMEM_LIMIT)


def _dot(a, b):
    return jnp.dot(a, b, preferred_element_type=F32)


def _dot_nt(a, b):
    return lax.dot_general(a, b, (((1,), (1,)), ((), ())), preferred_element_type=F32)


def _dot_tn(a, b):
    return lax.dot_general(a, b, (((0,), (0,)), ((), ())), preferred_element_type=F32)


def _rmsnorm(xf, g):
    return xf * lax.rsqrt(jnp.mean(xf * xf, axis=-1, keepdims=True) + EPS) * g


def _rope_tables(pos_col, inv_row, sgn_row):
    ang = pos_col.astype(F32) * inv_row
    return jnp.cos(ang), jnp.sin(ang) * sgn_row


def _rope_apply(y, cos, sin_signed, first, half):
    partner = jnp.where(first, pltpu.roll(y, LANES - half, 1), pltpu.roll(y, half, 1))
    return y * cos + partner * sin_signed


def _ret_inproj_kernel(x_ref, pos_ref, g_ref, inv_ref, sgn_ref, w_ref,
                       q_ref, k_ref, v_ref, gt_ref, qm_ref):
    h = _rmsnorm(x_ref[...], g_ref[...]).astype(BF16)
    cos, sin_s = _rope_tables(pos_ref[...], inv_ref[...], sgn_ref[...])
    first = sgn_ref[...] < 0.0
    mw = MAIN_WIDTH
    half = HEAD_DIM // 2
    qk = _dot(h, w_ref[:, :2 * mw])
    groups = range(2 * mw // LANES)
    ys = [qk[:, j * LANES:(j + 1) * LANES] for j in groups]
    fwd = [pltpu.roll(ys[j], LANES - half, 1) for j in groups]
    bwd = [pltpu.roll(ys[j], half, 1) for j in groups]
    outs = [(ys[j] * cos + jnp.where(first, fwd[j], bwd[j]) * sin_s).astype(BF16) for j in groups]
    n_q = mw // LANES
    for j in range(n_q):
        q_ref[:, j * LANES:(j + 1) * LANES] = outs[j]
        k_ref[:, j * LANES:(j + 1) * LANES] = outs[n_q + j]
    v_ref[...] = _dot(h, w_ref[:, 2 * mw:3 * mw]).astype(BF16)
    gt_ref[...] = _dot(h, w_ref[:, 3 * mw:4 * mw]).astype(BF16)
    qm_ref[...] = _dot(h, w_ref[:, 4 * mw:4 * mw + MEM_WIDTH]).astype(BF16)


def _ret_inproj(x2, pos_col, g, w_in):
    T, D = x2.shape
    tm = min(ROW_TILE, T)
    half = HEAD_DIM // 2
    inv = ROPE_THETA ** (-jnp.arange(half, dtype=F32) / half)
    lane = jnp.arange(LANES)
    inv_row = inv[lane % half][None, :]
    sgn_row = jnp.where((lane % HEAD_DIM) < half, -1.0, 1.0).astype(F32)[None, :]
    row = lambda i: (i, 0)
    fixed = lambda i: (0, 0)
    n_in = w_in.shape[1]
    outs = pl.pallas_call(
        _ret_inproj_kernel,
        grid=(T // tm,),
        in_specs=[
            pl.BlockSpec((tm, D), row),
            pl.BlockSpec((tm, 1), row),
            pl.BlockSpec((1, D), fixed),
            pl.BlockSpec((1, LANES), fixed),
            pl.BlockSpec((1, LANES), fixed),
            pl.BlockSpec((D, n_in), fixed),
        ],
        out_specs=[pl.BlockSpec((tm, MAIN_WIDTH), row)] * 4 + [pl.BlockSpec((tm, MEM_WIDTH), row)],
        out_shape=[jax.ShapeDtypeStruct((T, MAIN_WIDTH), BF16)] * 4
        + [jax.ShapeDtypeStruct((T, MEM_WIDTH), BF16)],
        compiler_params=_cparams(("parallel",)),
        name="ret_inproj",
    )(x2, pos_col, g[None, :], inv_row, sgn_row, w_in.astype(BF16))
    return outs


def _retention_tables():
    H, C, d = N_MAIN_HEADS, RET_CHUNK, HEAD_DIM
    log_g = jnp.log1p(-jnp.exp2(-5.0 - jnp.arange(H, dtype=F32)))
    idx = jnp.arange(C, dtype=F32)
    rel = idx[:, None] - idx[None, :]
    scale = d ** -0.5
    decay_in = jnp.where(rel[None] >= 0,
                         jnp.exp(log_g[:, None, None] * jnp.maximum(rel, 0.0)[None]), 0.0) * scale
    kdec = jnp.exp(log_g[None, :] * (C - 1.0 - idx)[:, None]) * scale
    qdec = jnp.exp(log_g[None, :] * (idx + 1.0)[:, None])
    cdec = jnp.exp(log_g * C)

    def lanes(t):
        return jnp.repeat(t, d, axis=1).reshape(C, N_PAIRS, 2 * d).transpose(1, 0, 2)

    head_of_lane = jnp.arange(2 * d) // d
    same = (head_of_lane[:, None] == head_of_lane[None, :]).astype(F32)
    cd_lane = jnp.repeat(cdec, d).reshape(N_PAIRS, 2 * d)
    state_decay = cd_lane[:, :, None] * same[None]
    decay_in = decay_in.reshape(N_PAIRS, 2, C, C)
    return decay_in, lanes(kdec), lanes(qdec), state_decay, same


def _retention_kernel(q_ref, k_ref, v_ref, gt_ref, dm_ref, kd_ref, qd_ref, sd_ref, same_ref,
                      gn_ref, o_ref, r_ref):
    n = pl.program_id(1)

    @pl.when(n == 0)
    def _():
        r_ref[...] = jnp.zeros_like(r_ref)

    lane = lax.broadcasted_iota(jnp.int32, (1, LANES), 1)
    head_a = lane < HEAD_DIM
    keep_a = jnp.where(head_a, 1.0, 0.0).astype(BF16)
    keep_b = jnp.where(head_a, 0.0, 1.0).astype(BF16)
    same = same_ref[...]
    avg = (same * (1.0 / HEAD_DIM)).astype(BF16)
    P = range(N_PAIRS)
    sl = [slice(p * LANES, (p + 1) * LANES) for p in P]
    q = [q_ref[0, :, sl[p]] for p in P]
    k = [k_ref[0, :, sl[p]] for p in P]
    v = [v_ref[0, :, sl[p]] for p in P]
    r_prev = [r_ref[p] for p in P]
    s_a = [_dot_nt(q[p] * keep_a, k[p]) * dm_ref[p, 0] for p in P]
    s_b = [_dot_nt(q[p] * keep_b, k[p]) * dm_ref[p, 1] for p in P]
    cross = [_dot((q[p].astype(F32) * qd_ref[p]).astype(BF16), r_prev[p].astype(BF16)) for p in P]
    u = [_dot_tn((k[p].astype(F32) * kd_ref[p]).astype(BF16), v[p]) for p in P]
    for p in P:
        r_ref[p] = sd_ref[p] * r_prev[p] + same * u[p]
    y = [_dot(s_a[p].astype(BF16), v[p] * keep_a) + _dot(s_b[p].astype(BF16), v[p] * keep_b) + cross[p]
         for p in P]
    yc = [y[p] - _dot(y[p].astype(BF16), avg) for p in P]
    var = [_dot((yc[p] * yc[p]).astype(BF16), avg) for p in P]
    for p in P:
        yn = yc[p] * lax.rsqrt(var[p] + EPS) * gn_ref[:, sl[p]]
        g = gt_ref[0, :, sl[p]].astype(F32)
        o_ref[0, :, sl[p]] = (g / (1.0 + jnp.exp(-g)) * yn).astype(BF16)


def _retention(q, k, v, gt, ret_gn):
    B, S, W = q.shape
    C = RET_CHUNK
    dm, kd, qd, sd, same = _retention_tables()
    tok = lambda b, n: (b, n, 0)
    fixed3 = lambda b, n: (0, 0, 0)
    return pl.pallas_call(
        _retention_kernel,
        grid=(B, S // C),
        in_specs=[pl.BlockSpec((1, C, W), tok)] * 4 + [
            pl.BlockSpec((N_PAIRS, 2, C, C), lambda b, n: (0, 0, 0, 0)),
            pl.BlockSpec((N_PAIRS, C, LANES), fixed3),
            pl.BlockSpec((N_PAIRS, C, LANES), fixed3),
            pl.BlockSpec((N_PAIRS, LANES, LANES), fixed3),
            pl.BlockSpec((LANES, LANES), lambda b, n: (0, 0)),
            pl.BlockSpec((1, W), lambda b, n: (0, 0)),
        ],
        out_specs=pl.BlockSpec((1, C, W), tok),
        out_shape=jax.ShapeDtypeStruct((B, S, W), BF16),
        scratch_shapes=[pltpu.VMEM((N_PAIRS, LANES, LANES), F32)],
        compiler_params=_cparams(("parallel", "arbitrary")),
        name="retention",
    )(q, k, v, gt, dm, kd, qd, sd, same, ret_gn[None, :])


def _mem_kv_kernel(mem_ref, w_ref, kg_ref, k_ref, v_ref):
    kv = _dot(mem_ref[...].astype(BF16), w_ref[...])
    lane = lax.broadcasted_iota(jnp.int32, (1, LANES), 1)
    head_a = lane < HEAD_DIM
    inv_d = 1.0 / HEAD_DIM
    for j in range(MEM_WIDTH // LANES):
        kj = kv[:, j * LANES:(j + 1) * LANES]
        k2 = kj * kj
        ms_a = jnp.sum(jnp.where(head_a, k2, 0.0), axis=-1, keepdims=True) * inv_d
        ms_b = jnp.sum(jnp.where(head_a, 0.0, k2), axis=-1, keepdims=True) * inv_d
        kn = kj * lax.rsqrt(jnp.where(head_a, ms_a, ms_b) + EPS) * kg_ref[...]
        k_ref[:, j * LANES:(j + 1) * LANES] = kn.astype(BF16)
    v_ref[...] = kv[:, MEM_WIDTH:].astype(BF16)


def _mem_kv(mem2, w_mem_kv, k_g):
    TM, D = mem2.shape
    tm = min(ROW_TILE, TM)
    kg_row = jnp.tile(k_g, LANES // HEAD_DIM)[None, :]
    row = lambda i: (i, 0)
    fixed = lambda i: (0, 0)
    return pl.pallas_call(
        _mem_kv_kernel,
        grid=(TM // tm,),
        in_specs=[pl.BlockSpec((tm, D), row), pl.BlockSpec((D, 2 * MEM_WIDTH), fixed),
                  pl.BlockSpec((1, LANES), fixed)],
        out_specs=[pl.BlockSpec((tm, MEM_WIDTH), row)] * 2,
        out_shape=[jax.ShapeDtypeStruct((TM, MEM_WIDTH), BF16)] * 2,
        compiler_params=_cparams(("parallel",)),
        name="mem_kv",
    )(mem2, w_mem_kv.astype(BF16), kg_row)


def _mem_attn_kernel(qm_ref, k_ref, v_ref, qg_ref, o_ref):
    lane = lax.broadcasted_iota(jnp.int32, (1, LANES), 1)
    head_a = lane < HEAD_DIM
    inv_d = 1.0 / HEAD_DIM
    scale = HEAD_DIM ** -0.5
    for j in range(MEM_WIDTH // LANES):
        sl = slice(j * LANES, (j + 1) * LANES)
        qj = qm_ref[0, :, sl].astype(F32)
        q2 = qj * qj
        ms_a = jnp.sum(jnp.where(head_a, q2, 0.0), axis=-1, keepdims=True) * inv_d
        ms_b = jnp.sum(jnp.where(head_a, 0.0, q2), axis=-1, keepdims=True) * inv_d
        qn = qj * lax.rsqrt(jnp.where(head_a, ms_a, ms_b) + EPS) * (qg_ref[...] * scale)
        kj = k_ref[0, :, sl]
        vj = v_ref[0, :, sl]
        out = None
        for sel in (head_a, jnp.logical_not(head_a)):
            s = _dot_nt(jnp.where(sel, qn, 0.0).astype(BF16), kj)
            p = jnp.exp(s - jnp.max(s, axis=-1, keepdims=True))
            p = p / jnp.sum(p, axis=-1, keepdims=True)
            o = _dot(p.astype(BF16), jnp.where(sel, vj, jnp.zeros_like(vj)))
            out = o if out is None else out + o
        o_ref[0, :, sl] = out.astype(BF16)


def _mem_attn(qm, k_m, v_m, q_g):
    B, S, _ = qm.shape
    M = k_m.shape[1]
    tm = min(ROW_TILE, S)
    qg_row = jnp.tile(q_g, LANES // HEAD_DIM)[None, :]
    return pl.pallas_call(
        _mem_attn_kernel,
        grid=(B, S // tm),
        in_specs=[pl.BlockSpec((1, tm, MEM_WIDTH), lambda b, i: (b, i, 0)),
                  pl.BlockSpec((1, M, MEM_WIDTH), lambda b, i: (b, 0, 0)),
                  pl.BlockSpec((1, M, MEM_WIDTH), lambda b, i: (b, 0, 0)),
                  pl.BlockSpec((1, LANES), lambda b, i: (0, 0))],
        out_specs=pl.BlockSpec((1, tm, MEM_WIDTH), lambda b, i: (b, i, 0)),
        out_shape=jax.ShapeDtypeStruct((B, S, MEM_WIDTH), BF16),
        compiler_params=_cparams(("parallel", "parallel")),
        name="mem_attn",
    )(qm, k_m, v_m, qg_row)


def _out_proj_kernel(x_ref, y_ref, m_ref, wy_ref, wm_ref, o_ref):
    o_ref[...] = x_ref[...] + _dot(y_ref[...], wy_ref[...]) + _dot(m_ref[...], wm_ref[...])


def _out_proj(x2, y2, m2, w_out):
    T, D = x2.shape
    tm = min(ROW_TILE, T)
    row = lambda i: (i, 0)
    fixed = lambda i: (0, 0)
    w = w_out.astype(BF16)
    return pl.pallas_call(
        _out_proj_kernel,
        grid=(T // tm,),
        in_specs=[pl.BlockSpec((tm, D), row), pl.BlockSpec((tm, MAIN_WIDTH), row),
                  pl.BlockSpec((tm, MEM_WIDTH), row),
                  pl.BlockSpec((MAIN_WIDTH, D), fixed), pl.BlockSpec((MEM_WIDTH, D), fixed)],
        out_specs=pl.BlockSpec((tm, D), row),
        out_shape=jax.ShapeDtypeStruct((T, D), F32),
        compiler_params=_cparams(("parallel",)),
        name="out_proj",
    )(x2, y2, m2, w[:MAIN_WIDTH], w[MAIN_WIDTH:])


ROUTER_LANE0 = N_GROUPS


RANK_BITS = 17
RANK_RADIX = 1 << RANK_BITS


def _router_kernel(x_ref, g_ref, whi_ref, wlo_ref, b_ref, info_ref, cnt_ref, tile_base_ref, base_ref):
    i = pl.program_id(0)

    @pl.when(i == 0)
    def _():
        base_ref[...] = jnp.zeros_like(base_ref)

    tm = x_ref.shape[0]
    h = _rmsnorm(x_ref[...], g_ref[...])
    h_hi = h.astype(BF16)
    h_lo = (h - h_hi.astype(F32)).astype(BF16)
    logits = (_dot(h_hi, whi_ref[...]) + _dot(h_lo, whi_ref[...]) + _dot(h_hi, wlo_ref[...])
              + b_ref[...])
    lane_i = lax.broadcasted_iota(jnp.int32, (tm, LANES), 1)
    lane = lane_i.astype(F32)
    big = float(LANES)

    is_g = lane_i < N_GROUPS
    lg = jnp.where(is_g, logits, NEG_BIG)
    mg = jnp.max(lg, axis=-1, keepdims=True)
    zg = jnp.sum(jnp.where(is_g, jnp.exp(lg - mg), 0.0), axis=-1, keepdims=True)
    p_grp = 1.0 / zg
    grp = jnp.min(jnp.where(is_g & (lg == mg), lane, big), axis=-1, keepdims=True)

    e_lane = lane_i - ROUTER_LANE0
    e_grp = (e_lane >> int(math.log2(EXPERTS_PER_GROUP))).astype(F32)
    is_e = (e_lane >= 0) & (e_lane < N_EXPERTS) & (e_grp == grp)
    le = jnp.where(is_e, logits, NEG_BIG)
    me = jnp.max(le, axis=-1, keepdims=True)
    ee = jnp.where(is_e, jnp.exp(le - me), 0.0)
    prob = ee / jnp.sum(ee, axis=-1, keepdims=True)
    p1 = jnp.max(prob, axis=-1, keepdims=True)
    i1 = jnp.min(jnp.where(is_e & (prob == p1), lane, big), axis=-1, keepdims=True)
    rest = is_e & (lane != i1)
    p2 = jnp.max(jnp.where(rest, prob, -1.0), axis=-1, keepdims=True)
    i2 = jnp.min(jnp.where(rest & (prob == p2), lane, big), axis=-1, keepdims=True)
    gate1 = p_grp * p1 / (p1 + p2)
    gate2 = p_grp * p2 / (p1 + p2)

    sel1 = lane == i1
    sel2 = lane == i2
    onehot = jnp.where(sel1 | sel2, 1.0, 0.0)
    r_io = lax.broadcasted_iota(jnp.int32, (tm, tm), 0)
    c_io = lax.broadcasted_iota(jnp.int32, (tm, tm), 1)
    lower = jnp.where(r_io > c_io, 1.0, 0.0).astype(BF16)
    tile_base_ref[0] = base_ref[...]
    before = _dot(lower, onehot.astype(BF16)) + base_ref[...]
    rank1 = jnp.sum(jnp.where(sel1, before, 0.0), axis=-1, keepdims=True)
    rank2 = jnp.sum(jnp.where(sel2, before, 0.0), axis=-1, keepdims=True)
    base_ref[...] += jnp.sum(onehot, axis=0, keepdims=True)
    cnt_ref[...] = base_ref[...]

    code1 = (i1 - float(ROUTER_LANE0)) * float(RANK_RADIX) + rank1
    code2 = (i2 - float(ROUTER_LANE0)) * float(RANK_RADIX) + rank2
    info = jnp.zeros((tm, LANES), F32)
    for col, val in enumerate((gate1, gate2, code1, code2)):
        info = jnp.where(lane_i == col, val, info)
    info_ref[...] = info


def _router(x2, g, w_grp, b_grp, w_exp, b_exp):
    T, D = x2.shape
    tm = min(ROW_TILE, T)
    w = jnp.zeros((D, LANES), F32)
    w = w.at[:, :N_GROUPS].set(w_grp).at[:, ROUTER_LANE0:ROUTER_LANE0 + N_EXPERTS].set(w_exp)
    b = jnp.zeros((1, LANES), F32)
    b = b.at[0, :N_GROUPS].set(b_grp).at[0, ROUTER_LANE0:ROUTER_LANE0 + N_EXPERTS].set(b_exp)
    w_hi = w.astype(BF16)
    w_lo = (w - w_hi.astype(F32)).astype(BF16)
    row = lambda i: (i, 0)
    fixed = lambda i: (0, 0)
    n_tiles = T // tm
    info, cnt, tile_base = pl.pallas_call(
        _router_kernel,
        grid=(n_tiles,),
        in_specs=[pl.BlockSpec((tm, D), row), pl.BlockSpec((1, D), fixed),
                  pl.BlockSpec((D, LANES), fixed), pl.BlockSpec((D, LANES), fixed),
                  pl.BlockSpec((1, LANES), fixed)],
        out_specs=[pl.BlockSpec((tm, LANES), row), pl.BlockSpec((1, LANES), fixed),
                   pl.BlockSpec((1, 1, LANES), lambda i: (i, 0, 0))],
        out_shape=[jax.ShapeDtypeStruct((T, LANES), F32), jax.ShapeDtypeStruct((1, LANES), F32),
                   jax.ShapeDtypeStruct((n_tiles, 1, LANES), F32)],
        scratch_shapes=[pltpu.VMEM((1, LANES), F32)],
        compiler_params=_cparams(("arbitrary",)),
        name="moe_router",
    )(x2, g[None, :], w_hi, w_lo, b)
    experts = slice(ROUTER_LANE0, ROUTER_LANE0 + N_EXPERTS)
    counts = cnt[0, experts].astype(jnp.int32)
    tile_base = tile_base[:, 0, experts].astype(jnp.int32)
    return info, counts, tile_base


def _tok_rows(r, n=1):
    start = r * SUBLANES
    if not isinstance(start, int):
        start = pl.multiple_of(start, SUBLANES)
    return pl.ds(start, n * SUBLANES)


def _segment_copies(n, src, src_row, dst, dst_row, sem, top, op):
    off = 0
    bit = top
    while bit >= 1:
        take = n & bit

        @pl.when(take != 0)
        def _(bit=bit, off=off):
            op(pltpu.make_async_copy(src.at[_tok_rows(src_row + off, bit), :],
                                     dst.at[_tok_rows(dst_row + off, bit), :], sem))
        off = off + take
        bit //= 2


def _dma_start(cp):
    cp.start()


def _dma_wait(cp):
    cp.wait()


def _block_copy(src, dst, blk, sem):
    return pltpu.make_async_copy(src, dst.at[_tok_rows(blk * MOE_BLOCK, MOE_BLOCK), :], sem)


def _dispatch_kernel(a_ref, n_ref, ls_ref, gb_ref, zrow_ref, zcnt_ref, nblk_ref,
                     code_ref, x_ref, g_ref, xs_ref, hbuf_ref, cbuf_ref, zbuf_ref, sem, zsem):
    s = pl.program_id(0)
    ts = x_ref.shape[0]

    @pl.when(s == 0)
    def _():
        zbuf_ref[...] = jnp.zeros_like(zbuf_ref)
        n_blocks = xs_ref.shape[0] // (MOE_BLOCK * SUBLANES)
        for op in (_dma_start, _dma_wait):
            def tail(b, c, op=op):
                op(_block_copy(zbuf_ref, xs_ref, b, zsem))
                return c
            lax.fori_loop(nblk_ref[0], n_blocks, tail, 0)

            def pad(e, c, op=op):
                _segment_copies(zcnt_ref[e], zbuf_ref, 0, xs_ref, zrow_ref[e], zsem, MOE_BLOCK // 2, op)
                return c
            lax.fori_loop(0, N_EXPERTS, pad, 0)

    for sub in range(ts // MOE_TILE):
        h = _rmsnorm(x_ref[pl.ds(sub * MOE_TILE, MOE_TILE), :], g_ref[...])
        for c in range(SUBLANES):
            hbuf_ref[pl.ds(sub * MOE_TILE * SUBLANES + c, MOE_TILE, stride=SUBLANES), :] = (
                h[:, c * LANES:(c + 1) * LANES])

    tbl = s * N_EXPERTS

    def place(tb, c):
        for u in range(DMA_UNROLL):
            t = tb * DMA_UNROLL + u
            tile = hbuf_ref[_tok_rows(t), :]
            for kk in range(2):
                code = code_ref[t * 2 + kk]
                d = a_ref[tbl + (code >> RANK_BITS)] + (code & (RANK_RADIX - 1))
                cbuf_ref[_tok_rows(d), :] = tile
        return c
    lax.fori_loop(0, ts // DMA_UNROLL, place, 0)

    for op in (_dma_start, _dma_wait):
        def run(e, c, op=op):
            _segment_copies(n_ref[tbl + e], cbuf_ref, ls_ref[tbl + e], xs_ref, gb_ref[tbl + e], sem, ts, op)
            return c
        lax.fori_loop(0, N_EXPERTS, run, 0)


def _dispatch(x2, g, codes, tables, zrow, zcnt, nblk_used, n_rows, ts):
    T, D = x2.shape
    return pl.pallas_call(
        _dispatch_kernel,
        grid_spec=pltpu.PrefetchScalarGridSpec(
            num_scalar_prefetch=7,
            grid=(T // ts,),
            in_specs=[pl.BlockSpec((ts * 2,), lambda i, *_: (i,), memory_space=pltpu.SMEM),
                      pl.BlockSpec((ts, D), lambda i, *_: (i, 0)),
                      pl.BlockSpec((1, D), lambda i, *_: (0, 0))],
            out_specs=pl.BlockSpec(memory_space=pl.ANY),
            scratch_shapes=[pltpu.VMEM((ts * SUBLANES, LANES), F32),
                            pltpu.VMEM((2 * ts * SUBLANES, LANES), F32),
                            pltpu.VMEM((MOE_BLOCK * SUBLANES, LANES), F32),
                            pltpu.SemaphoreType.DMA(()), pltpu.SemaphoreType.DMA(())],
        ),
        out_shape=jax.ShapeDtypeStruct((n_rows * SUBLANES, LANES), F32),
        compiler_params=_cparams(("arbitrary",)),
        name="moe_dispatch",
    )(*tables, zrow, zcnt, nblk_used, codes, x2, g[None, :])


def _expert_kernel(blk_e_ref, nblk_ref, xs_ref, wg_ref, wu_ref, wd_ref, ys_ref):
    b = pl.program_id(0)

    @pl.when(b < nblk_ref[0])
    def _():
        nb = MOE_BLOCK
        x = jnp.concatenate(
            [xs_ref[pl.ds(s, nb, stride=SUBLANES), :] for s in range(SUBLANES)], axis=-1
        ).astype(BF16)
        a = _dot(x, wg_ref[0])
        u = _dot(x, wu_ref[0])
        hid = (a / (1.0 + jnp.exp(-a)) * u).astype(BF16)
        y = _dot(hid, wd_ref[0])
        for s in range(SUBLANES):
            ys_ref[pl.ds(s, nb, stride=SUBLANES), :] = y[:, s * LANES:(s + 1) * LANES]

    @pl.when(b >= nblk_ref[0])
    def _():
        ys_ref[...] = jnp.zeros_like(ys_ref)


def _experts(xs, blk_e, nblk_used, w_gate, w_up, w_down):
    rows = xs.shape[0]
    nblk = rows // (MOE_BLOCK * SUBLANES)
    D = w_gate.shape[1]
    blk = lambda b, be, nb: (jnp.minimum(b, nb[0] - 1), 0)
    out_blk = lambda b, be, nb: (b, 0)
    wsel = lambda b, be, nb: (be[b], 0, 0)
    return pl.pallas_call(
        _expert_kernel,
        grid_spec=pltpu.PrefetchScalarGridSpec(
            num_scalar_prefetch=2,
            grid=(nblk,),
            in_specs=[pl.BlockSpec((MOE_BLOCK * SUBLANES, LANES), blk),
                      pl.BlockSpec((1, D, D_EXPERT), wsel),
                      pl.BlockSpec((1, D, D_EXPERT), wsel),
                      pl.BlockSpec((1, D_EXPERT, D), wsel)],
            out_specs=pl.BlockSpec((MOE_BLOCK * SUBLANES, LANES), out_blk),
        ),
        out_shape=jax.ShapeDtypeStruct((rows, LANES), F32),
        compiler_params=_cparams(("arbitrary",)),
        name="moe_experts",
    )(blk_e, nblk_used, xs, w_gate.astype(BF16), w_up.astype(BF16), w_down.astype(BF16))


def _combine_kernel(a_ref, n_ref, ls_ref, gb_ref, code_ref, x_ref, info_ref, ys_ref, o_ref,
                    ybuf_ref, pick0_ref, pick1_ref, sem):
    s = pl.program_id(0)
    ts = x_ref.shape[0]
    tbl = s * N_EXPERTS
    for op in (_dma_start, _dma_wait):
        def run(e, c, op=op):
            _segment_copies(n_ref[tbl + e], ys_ref, gb_ref[tbl + e], ybuf_ref, ls_ref[tbl + e], sem, ts, op)
            return c
        lax.fori_loop(0, N_EXPERTS, run, 0)

    picks = (pick0_ref, pick1_ref)
    for sub in range(ts // MOE_TILE):
        def pick(tb, c, sub=sub):
            for u in range(DMA_UNROLL):
                r = tb * DMA_UNROLL + u
                t = sub * MOE_TILE + r
                for kk in range(2):
                    code = code_ref[t * 2 + kk]
                    d = a_ref[tbl + (code >> RANK_BITS)] + (code & (RANK_RADIX - 1))
                    picks[kk][_tok_rows(r), :] = ybuf_ref[_tok_rows(d), :]
            return c
        lax.fori_loop(0, MOE_TILE // DMA_UNROLL, pick, 0)

        rows = pl.ds(sub * MOE_TILE, MOE_TILE)
        info = info_ref[rows, :]
        g0 = info[:, 0:1]
        g1 = info[:, 1:2]
        for c in range(SUBLANES):
            sl = slice(c * LANES, (c + 1) * LANES)
            y0 = pick0_ref[pl.ds(c, MOE_TILE, stride=SUBLANES), :]
            y1 = pick1_ref[pl.ds(c, MOE_TILE, stride=SUBLANES), :]
            o_ref[rows, sl] = x_ref[rows, sl] + (y0 * g0 + y1 * g1)


def _combine(x2, info, ys, codes, tables, ts):
    T, D = x2.shape
    return pl.pallas_call(
        _combine_kernel,
        grid_spec=pltpu.PrefetchScalarGridSpec(
            num_scalar_prefetch=4,
            grid=(T // ts,),
            in_specs=[pl.BlockSpec((ts * 2,), lambda i, *_: (i,), memory_space=pltpu.SMEM),
                      pl.BlockSpec((ts, D), lambda i, *_: (i, 0)),
                      pl.BlockSpec((ts, LANES), lambda i, *_: (i, 0)),
                      pl.BlockSpec(memory_space=pl.ANY)],
            out_specs=pl.BlockSpec((ts, D), lambda i, *_: (i, 0)),
            scratch_shapes=[pltpu.VMEM((2 * ts * SUBLANES, LANES), F32),
                            pltpu.VMEM((MOE_TILE * SUBLANES, LANES), F32),
                            pltpu.VMEM((MOE_TILE * SUBLANES, LANES), F32),
                            pltpu.SemaphoreType.DMA(())],
        ),
        out_shape=jax.ShapeDtypeStruct((T, D), F32),
        compiler_params=_cparams(("arbitrary",)),
        name="moe_combine",
    )(*tables, codes, x2, info, ys)


def _supertile_tables(tile_base, counts, pad_start, ts):
    per = ts // ROW_TILE
    base = tile_base[::per]
    nxt = jnp.concatenate([base[1:], counts[None, :]], axis=0)
    n = nxt - base
    lstart = jnp.cumsum(n, axis=1) - n
    flat = lambda a: a.reshape(-1).astype(jnp.int32)
    return flat(lstart - base), flat(n), flat(lstart), flat(pad_start[None, :] + base)


DISPATCH_TOKENS = 2048
COMBINE_TOKENS = 1024


def _hier_moe(x2, ln2, w_grp, b_grp, w_exp, b_exp, w_gate, w_up, w_down):
    T, D = x2.shape
    info, counts, tile_base = _router(x2, ln2, w_grp, b_grp, w_exp, b_exp)
    padded = (counts + MOE_BLOCK - 1) // MOE_BLOCK * MOE_BLOCK
    pad_end = jnp.cumsum(padded)
    pad_start = pad_end - padded
    codes = info[:, 2:4].astype(jnp.int32).reshape(T * 2)
    n_rows = T * 2 + N_EXPERTS * MOE_BLOCK
    nblk = n_rows // MOE_BLOCK
    blk_row = jnp.arange(nblk, dtype=jnp.int32) * MOE_BLOCK
    blk_e = jnp.minimum(jnp.sum((pad_end[None, :] <= blk_row[:, None]).astype(jnp.int32), axis=1),
                        N_EXPERTS - 1).astype(jnp.int32)
    nblk_used = (pad_end[-1:] // MOE_BLOCK).astype(jnp.int32)
    zrow = (pad_start + counts).astype(jnp.int32)
    zcnt = (padded - counts).astype(jnp.int32)

    td = min(DISPATCH_TOKENS, T)
    tc = min(COMBINE_TOKENS, T)

    xs = _dispatch(x2, ln2, codes, _supertile_tables(tile_base, counts, pad_start, td),
                   zrow, zcnt, nblk_used, n_rows, td)
    ys = _experts(xs, blk_e, nblk_used, w_gate, w_up, w_down)
    return _combine(x2, info, ys, codes, _supertile_tables(tile_base, counts, pad_start, tc), tc)


HEAD_PAD = LANES
LATENT_PAD = 3 * LANES


def _rot_partner():
    half = QK_ROPE // 2
    r = jnp.arange(QK_ROPE)
    return jnp.where(r < half, r + half, r - half), jnp.where(r < half, -1.0, 1.0).astype(F32)


def _mla_rope_rows():
    half = QK_ROPE // 2
    inv = ROPE_THETA ** (-jnp.arange(half, dtype=F32) / half)
    lane = jnp.arange(LANES)
    r = lane - QK_NOPE
    in_rope = (r >= 0) & (r < QK_ROPE)
    inv_row = jnp.where(in_rope, inv[jnp.clip(r, 0, QK_ROPE - 1) % half], 0.0)[None, :]
    rope_row = in_rope.astype(F32)[None, :]
    real_row = (lane < QK_HEAD).astype(F32)[None, :]
    return inv_row, rope_row, real_row


def _head_gain_row(g):
    partner, _ = _rot_partner()
    return jnp.concatenate([g, g[QK_NOPE + partner]])[None, :]


def _with_partner_cols(w3):
    partner, sign = _rot_partner()
    rot = w3[:, :, QK_NOPE + partner] * sign
    return jnp.concatenate([w3, rot], axis=-1).reshape(w3.shape[0], N_MAIN_HEADS * HEAD_PAD)


def _heads_norm_rope(ys, gain_rows, scales, real_row, cos_real, sin_rope):
    n = range(len(ys))
    row_id = lax.broadcasted_iota(jnp.int32, (LANES, LANES), 0)
    ones_real = jnp.where(row_id < QK_HEAD, 1.0, 0.0).astype(BF16)
    ms = [_dot((ys[i] * ys[i]).astype(BF16), ones_real) * (1.0 / QK_HEAD) for i in n]
    yn = [ys[i] * (lax.rsqrt(ms[i] + EPS) * scales[i]) * gain_rows[i] for i in n]
    rolled = [pltpu.roll(yn[i], LANES - QK_ROPE, 1) for i in n]
    return [(yn[i] * cos_real + rolled[i] * sin_rope).astype(BF16) for i in n]


def _mla_qkv_kernel(x_ref, pos_ref, lnkv_ref, lnq_ref, wd_ref, scale_ref, wkv_ref, kg_ref,
                    win_ref, qlg_ref, wuq_ref, qg_ref, inv_ref, rope_ref, real_ref,
                    k_ref, v_ref, q_ref, qm_ref):
    x = x_ref[...]
    xr = x * lax.rsqrt(jnp.mean(x * x, axis=-1, keepdims=True) + EPS)
    ang = pos_ref[...].astype(F32) * inv_ref[...]
    real_row = real_ref[...]
    cos_real = jnp.cos(ang) * real_row
    sin_rope = jnp.sin(ang) * rope_ref[...]

    ckr = _dot((xr * lnkv_ref[...]).astype(BF16), wd_ref[...])
    c = ckr[:, :KV_LORA]
    r = lax.rsqrt(jnp.mean(c * c, axis=-1, keepdims=True) + EPS)
    lane = lax.broadcasted_iota(jnp.int32, (1, LATENT_PAD), 1)
    lhs = (ckr * jnp.where(lane < KV_LORA, r * scale_ref[...], 1.0)).astype(BF16)
    kv = _dot(lhs, wkv_ref[...])
    proj = _dot((xr * lnq_ref[...]).astype(BF16), win_ref[...])
    cq = _rmsnorm(proj[:, :Q_LORA], qlg_ref[...]).astype(BF16)
    q = _dot(cq, wuq_ref[...])
    qm_ref[...] = proj[:, Q_LORA:].astype(BF16)
    v_ref[...] = kv[:, N_MAIN_HEADS * HEAD_PAD:].astype(BF16)
    sls = [slice(hh * HEAD_PAD, (hh + 1) * HEAD_PAD) for hh in range(N_MAIN_HEADS)]
    nh = N_MAIN_HEADS
    outs = _heads_norm_rope([kv[:, sl] for sl in sls] + [q[:, sl] for sl in sls],
                            [kg_ref[...]] * nh + [qg_ref[...]] * nh,
                            [1.0] * nh + [QK_HEAD ** -0.5] * nh, real_row, cos_real, sin_rope)
    for hh, sl in enumerate(sls):
        k_ref[:, sl] = outs[hh]
        q_ref[:, sl] = outs[nh + hh]


def _mla_qkv(x2, pos_col, kv_ln, w_dkv, kv_lora_g, w_ukv, k_g, ln1, w_in, q_lora_g, w_uq, q_g):
    T, D = x2.shape
    tm = min(ROW_TILE, T)
    lat = LATENT_PAD
    wd = jnp.pad(w_dkv, ((0, 0), (0, lat - w_dkv.shape[1]))).astype(BF16)
    scale_row = jnp.pad(kv_lora_g, (0, lat - KV_LORA), constant_values=1.0)[None, :]
    w3 = w_ukv.reshape(KV_LORA, N_MAIN_HEADS, QK_NOPE + V_HEAD)
    wk = jnp.zeros((lat, N_MAIN_HEADS, QK_HEAD), F32)
    wk = wk.at[:KV_LORA, :, :QK_NOPE].set(w3[:, :, :QK_NOPE])
    eye = jnp.eye(QK_ROPE, dtype=F32)
    wk = wk.at[KV_LORA:KV_LORA + QK_ROPE, :, QK_NOPE:].set(
        jnp.broadcast_to(eye[:, None, :], (QK_ROPE, N_MAIN_HEADS, QK_ROPE)))
    wv = jnp.zeros((lat, N_MAIN_HEADS * V_HEAD), F32)
    wv = wv.at[:KV_LORA].set(w3[:, :, QK_NOPE:].reshape(KV_LORA, N_MAIN_HEADS * V_HEAD))
    wkv = jnp.concatenate([_with_partner_cols(wk), wv], axis=1).astype(BF16)
    wuq = _with_partner_cols(w_uq.reshape(Q_LORA, N_MAIN_HEADS, QK_HEAD)).astype(BF16)
    inv_row, rope_row, real_row = _mla_rope_rows()
    row = lambda i: (i, 0)
    fixed = lambda i: (0, 0)
    kw = N_MAIN_HEADS * HEAD_PAD
    n_in = w_in.shape[1]
    lane_row = pl.BlockSpec((1, LANES), fixed)
    return pl.pallas_call(
        _mla_qkv_kernel,
        grid=(T // tm,),
        in_specs=[pl.BlockSpec((tm, D), row), pl.BlockSpec((tm, 1), row),
                  pl.BlockSpec((1, D), fixed), pl.BlockSpec((1, D), fixed),
                  pl.BlockSpec((D, lat), fixed), pl.BlockSpec((1, lat), fixed),
                  pl.BlockSpec((lat, kw + MAIN_WIDTH), fixed), lane_row,
                  pl.BlockSpec((D, n_in), fixed), pl.BlockSpec((1, Q_LORA), fixed),
                  pl.BlockSpec((Q_LORA, kw), fixed), lane_row,
                  lane_row, lane_row, lane_row],
        out_specs=[pl.BlockSpec((tm, kw), row), pl.BlockSpec((tm, MAIN_WIDTH), row),
                   pl.BlockSpec((tm, kw), row), pl.BlockSpec((tm, MEM_WIDTH), row)],
        out_shape=[jax.ShapeDtypeStruct((T, kw), BF16), jax.ShapeDtypeStruct((T, MAIN_WIDTH), BF16),
                   jax.ShapeDtypeStruct((T, kw), BF16), jax.ShapeDtypeStruct((T, MEM_WIDTH), BF16)],
        compiler_params=_cparams(("parallel",)),
        name="mla_qkv",
    )(x2, pos_col, kv_ln[None, :], ln1[None, :], wd, scale_row, wkv, _head_gain_row(k_g),
      w_in.astype(BF16), q_lora_g[None, :], wuq, _head_gain_row(q_g), inv_row, rope_row, real_row)


ATTN_TILE = 1024


def _flash_kernel(qi_ref, kj_ref, q_ref, k_ref, v_ref, o_ref, m_ref, acc_ref):
    t = pl.program_id(2)
    i = qi_ref[t]
    j = kj_ref[t]
    tq = q_ref.shape[1]
    tk = k_ref.shape[1]

    @pl.when(j == 0)
    def _():
        m_ref[...] = jnp.full_like(m_ref, NEG_BIG)
        acc_ref[...] = jnp.zeros_like(acc_ref)

    lane = lax.broadcasted_iota(jnp.int32, (1, LANES), 1)
    head_a = lane < V_HEAD
    den_lane = (V_HEAD, 0)

    def step(masked):
        v = v_ref[0]
        if masked:
            q_idx = lax.broadcasted_iota(jnp.int32, (tq, tk), 0)
            k_idx = lax.broadcasted_iota(jnp.int32, (tq, tk), 1)
            visible = k_idx <= q_idx
        H = range(2)
        sl = [slice(hh * HEAD_PAD, (hh + 1) * HEAD_PAD) for hh in H]
        s = [_dot_nt(q_ref[0, :, sl[hh]], k_ref[0, :, sl[hh]]) for hh in H]
        if masked:
            s = [jnp.where(visible, s[hh], NEG_BIG) for hh in H]
        m_prev = [m_ref[hh] for hh in H]
        acc_prev = [acc_ref[hh] for hh in H]
        m_new = [jnp.maximum(m_prev[hh], jnp.max(s[hh], axis=-1, keepdims=True)) for hh in H]
        alpha = [jnp.exp(m_prev[hh] - m_new[hh]) for hh in H]
        m_wide = [jnp.concatenate([m_new[hh]] * (tk // LANES), axis=1) for hh in H]
        p = [jnp.exp(s[hh] - m_wide[hh]).astype(BF16) for hh in H]
        keep_row = [jnp.where(head_a, 1.0, 0.0).astype(BF16), jnp.where(head_a, 0.0, 1.0).astype(BF16)]
        den_row = [jnp.where(lane == den_lane[hh], 1.0, 0.0).astype(BF16) for hh in H]
        pv = [_dot(p[hh], v * keep_row[hh] + den_row[hh]) for hh in H]
        for hh in H:
            acc_ref[hh] = alpha[hh] * acc_prev[hh] + pv[hh]
            m_ref[hh] = m_new[hh]

    @pl.when(j < i)
    def _():
        step(False)

    @pl.when(j == i)
    def _():
        step(True)
        acc_a = acc_ref[0]
        acc_b = acc_ref[1]
        out_a = acc_a / acc_a[:, den_lane[0]:den_lane[0] + 1]
        out_b = acc_b / acc_b[:, den_lane[1]:den_lane[1] + 1]
        o_ref[0] = jnp.where(head_a, out_a, out_b).astype(BF16)


def _flash(q, k, v):
    B, S, _ = q.shape
    t = min(ATTN_TILE, S)
    n = S // t
    pairs = [(i, j) for i in range(n) for j in range(i + 1)]
    qi = jnp.array([p[0] for p in pairs], jnp.int32)
    kj = jnp.array([p[1] for p in pairs], jnp.int32)
    return pl.pallas_call(
        _flash_kernel,
        grid_spec=pltpu.PrefetchScalarGridSpec(
            num_scalar_prefetch=2,
            grid=(B, N_PAIRS, len(pairs)),
            in_specs=[pl.BlockSpec((1, t, 2 * HEAD_PAD), lambda b, p, s, qi, kj: (b, qi[s], p)),
                      pl.BlockSpec((1, t, 2 * HEAD_PAD), lambda b, p, s, qi, kj: (b, kj[s], p)),
                      pl.BlockSpec((1, t, LANES), lambda b, p, s, qi, kj: (b, kj[s], p))],
            out_specs=pl.BlockSpec((1, t, LANES), lambda b, p, s, qi, kj: (b, qi[s], p)),
            scratch_shapes=[pltpu.VMEM((2, t, LANES), F32), pltpu.VMEM((2, t, LANES), F32)],
        ),
        out_shape=jax.ShapeDtypeStruct((B, S, MAIN_WIDTH), BF16),
        compiler_params=_cparams(("parallel", "parallel", "arbitrary")),
        name="mla_flash",
    )(qi, kj, q, k, v)


def kernel(x, mem, positions, ln1, ln2, w_out, mem_w_kv, mem_q_norm, mem_k_norm, router_group_w, router_group_b, router_expert_w, router_expert_b, expert_w_gate, expert_w_up, expert_w_down, ret_w_in, ret_gn, kv_ln, kv_w_down, kv_lora_norm, kv_w_up, k_norm, mla_w_in, q_lora_norm, mla_w_uq, q_norm):
    B, S, D = x.shape
    M = mem.shape[1]
    T = B * S
    x2 = x.reshape(T, D)
    mem2 = mem.reshape(B * M, D)
    pos_col = positions.reshape(T, 1).astype(jnp.int32)

    def mem_path(i, qm):
        k_m, v_m = _mem_kv(mem2, mem_w_kv[i], mem_k_norm[i])
        return _mem_attn(qm.reshape(B, S, MEM_WIDTH), k_m.reshape(B, M, MEM_WIDTH),
                         v_m.reshape(B, M, MEM_WIDTH), mem_q_norm[i]).reshape(T, MEM_WIDTH)

    def moe(i, xin):
        return _hier_moe(xin, ln2[i], router_group_w[i], router_group_b[i], router_expert_w[i],
                         router_expert_b[i], expert_w_gate[i], expert_w_up[i], expert_w_down[i])

    q, k, v, gt, qm = _ret_inproj(x2, pos_col, ln1[0], ret_w_in[0])
    shp = (B, S, MAIN_WIDTH)
    y = _retention(q.reshape(shp), k.reshape(shp), v.reshape(shp), gt.reshape(shp), ret_gn[0])
    x2 = _out_proj(x2, y.reshape(T, MAIN_WIDTH), mem_path(0, qm), w_out[0])
    x2 = moe(0, x2)

    k_sh, v_sh, q1, qm1 = _mla_qkv(x2, pos_col, kv_ln, kv_w_down, kv_lora_norm, kv_w_up, k_norm,
                                   ln1[1], mla_w_in[0], q_lora_norm[0], mla_w_uq[0], q_norm[0])
    kw = N_MAIN_HEADS * HEAD_PAD
    y1 = _flash(q1.reshape(B, S, kw), k_sh.reshape(B, S, kw), v_sh.reshape(shp))
    x2 = _out_proj(x2, y1.reshape(T, MAIN_WIDTH), mem_path(1, qm1), w_out[1])
    x2 = moe(1, x2)
    return x2.reshape(B, S, D)
```

```python
import functools
import math

import jax
import jax.numpy as jnp
from jax import lax
from jax.experimental import pallas as pl
from jax.experimental.pallas import tpu as pltpu

F32 = jnp.float32
BF16 = jnp.bfloat16

HEAD_DIM = 64
N_MAIN_HEADS = 12
MAIN_WIDTH = N_MAIN_HEADS * HEAD_DIM
N_MEM_HEADS = 4
MEM_WIDTH = N_MEM_HEADS * HEAD_DIM
RET_CHUNK = 128
ROPE_THETA = 10000.0
Q_LORA = 384
KV_LORA = 256
QK_NOPE = 64
QK_ROPE = 32
QK_HEAD = QK_NOPE + QK_ROPE
V_HEAD = 64
N_GROUPS = 4
EXPERTS_PER_GROUP = 8
N_EXPERTS = N_GROUPS * EXPERTS_PER_GROUP
D_EXPERT = 256
MOE_BLOCK = 256
EPS = 1e-6

LANES = 128
SUBLANES = 8
VMEM_LIMIT = 48 * 1024 * 1024
NEG_BIG = -1e30

N_PAIRS = N_MAIN_HEADS // 2
ROW_TILE = 512
MOE_TILE = 256
DMA_UNROLL = 8


def _cparams(sem):
    return pltpu.CompilerParams(dimension_semantics=sem, vmem_limit_bytes=VMEM_LIMIT)


def _dot(a, b):
    return jnp.dot(a, b, preferred_element_type=F32)


def _dot_nt(a, b):
    return lax.dot_general(a, b, (((1,), (1,)), ((), ())), preferred_element_type=F32)


def _dot_tn(a, b):
    return lax.dot_general(a, b, (((0,), (0,)), ((), ())), preferred_element_type=F32)


def _rmsnorm(xf, g):
    return xf * lax.rsqrt(jnp.mean(xf * xf, axis=-1, keepdims=True) + EPS) * g


def _rope_tables(pos_col, inv_row, sgn_row):
    ang = pos_col.astype(F32) * inv_row
    return jnp.cos(ang), jnp.sin(ang) * sgn_row


def _rope_apply(y, cos, sin_signed, first, half):
    partner = jnp.where(first, pltpu.roll(y, LANES - half, 1), pltpu.roll(y, half, 1))
    return y * cos + partner * sin_signed


def _ret_inproj_kernel(x_ref, pos_ref, g_ref, inv_ref, sgn_ref, w_ref,
                       q_ref, k_ref, v_ref, gt_ref, qm_ref):
    h = _rmsnorm(x_ref[...], g_ref[...]).astype(BF16)
    cos, sin_s = _rope_tables(pos_ref[...], inv_ref[...], sgn_ref[...])
    first = sgn_ref[...] < 0.0
    mw = MAIN_WIDTH
    half = HEAD_DIM // 2
    qk = _dot(h, w_ref[:, :2 * mw])
    groups = range(2 * mw // LANES)
    ys = [qk[:, j * LANES:(j + 1) * LANES] for j in groups]
    fwd = [pltpu.roll(ys[j], LANES - half, 1) for j in groups]
    bwd = [pltpu.roll(ys[j], half, 1) for j in groups]
    outs = [(ys[j] * cos + jnp.where(first, fwd[j], bwd[j]) * sin_s).astype(BF16) for j in groups]
    n_q = mw // LANES
    for j in range(n_q):
        q_ref[:, j * LANES:(j + 1) * LANES] = outs[j]
        k_ref[:, j * LANES:(j + 1) * LANES] = outs[n_q + j]
    v_ref[...] = _dot(h, w_ref[:, 2 * mw:3 * mw]).astype(BF16)
    gt_ref[...] = _dot(h, w_ref[:, 3 * mw:4 * mw]).astype(BF16)
    qm_ref[...] = _dot(h, w_ref[:, 4 * mw:4 * mw + MEM_WIDTH]).astype(BF16)


def _ret_inproj(x2, pos_col, g, w_in):
    T, D = x2.shape
    tm = min(ROW_TILE, T)
    half = HEAD_DIM // 2
    inv = ROPE_THETA ** (-jnp.arange(half, dtype=F32) / half)
    lane = jnp.arange(LANES)
    inv_row = inv[lane % half][None, :]
    sgn_row = jnp.where((lane % HEAD_DIM) < half, -1.0, 1.0).astype(F32)[None, :]
    row = lambda i: (i, 0)
    fixed = lambda i: (0, 0)
    n_in = w_in.shape[1]
    outs = pl.pallas_call(
        _ret_inproj_kernel,
        grid=(T // tm,),
        in_specs=[
            pl.BlockSpec((tm, D), row),
            pl.BlockSpec((tm, 1), row),
            pl.BlockSpec((1, D), fixed),
            pl.BlockSpec((1, LANES), fixed),
            pl.BlockSpec((1, LANES), fixed),
            pl.BlockSpec((D, n_in), fixed),
        ],
        out_specs=[pl.BlockSpec((tm, MAIN_WIDTH), row)] * 4 + [pl.BlockSpec((tm, MEM_WIDTH), row)],
        out_shape=[jax.ShapeDtypeStruct((T, MAIN_WIDTH), BF16)] * 4
        + [jax.ShapeDtypeStruct((T, MEM_WIDTH), BF16)],
        compiler_params=_cparams(("parallel",)),
        name="ret_inproj",
    )(x2, pos_col, g[None, :], inv_row, sgn_row, w_in.astype(BF16))
    return outs


def _retention_tables():
    H, C, d = N_MAIN_HEADS, RET_CHUNK, HEAD_DIM
    log_g = jnp.log1p(-jnp.exp2(-5.0 - jnp.arange(H, dtype=F32)))
    idx = jnp.arange(C, dtype=F32)
    rel = idx[:, None] - idx[None, :]
    scale = d ** -0.5
    decay_in = jnp.where(rel[None] >= 0,
                         jnp.exp(log_g[:, None, None] * jnp.maximum(rel, 0.0)[None]), 0.0) * scale
    kdec = jnp.exp(log_g[None, :] * (C - 1.0 - idx)[:, None]) * scale
    qdec = jnp.exp(log_g[None, :] * (idx + 1.0)[:, None])
    cdec = jnp.exp(log_g * C)

    def lanes(t):
        return jnp.repeat(t, d, axis=1).reshape(C, N_PAIRS, 2 * d).transpose(1, 0, 2)

    head_of_lane = jnp.arange(2 * d) // d
    same = (head_of_lane[:, None] == head_of_lane[None, :]).astype(F32)
    cd_lane = jnp.repeat(cdec, d).reshape(N_PAIRS, 2 * d)
    state_decay = cd_lane[:, :, None] * same[None]
    decay_in = decay_in.reshape(N_PAIRS, 2, C, C)
    return decay_in, lanes(kdec), lanes(qdec), state_decay, same


def _retention_kernel(q_ref, k_ref, v_ref, gt_ref, dm_ref, kd_ref, qd_ref, sd_ref, same_ref,
                      gn_ref, o_ref, r_ref):
    n = pl.program_id(1)

    @pl.when(n == 0)
    def _():
        r_ref[...] = jnp.zeros_like(r_ref)

    lane = lax.broadcasted_iota(jnp.int32, (1, LANES), 1)
    head_a = lane < HEAD_DIM
    keep_a = jnp.where(head_a, 1.0, 0.0).astype(BF16)
    keep_b = jnp.where(head_a, 0.0, 1.0).astype(BF16)
    same = same_ref[...]
    avg = (same * (1.0 / HEAD_DIM)).astype(BF16)
    P = range(N_PAIRS)
    sl = [slice(p * LANES, (p + 1) * LANES) for p in P]
    q = [q_ref[0, :, sl[p]] for p in P]
    k = [k_ref[0, :, sl[p]] for p in P]
    v = [v_ref[0, :, sl[p]] for p in P]
    r_prev = [r_ref[p] for p in P]
    s_a = [_dot_nt(q[p] * keep_a, k[p]) * dm_ref[p, 0] for p in P]
    s_b = [_dot_nt(q[p] * keep_b, k[p]) * dm_ref[p, 1] for p in P]
    cross = [_dot((q[p].astype(F32) * qd_ref[p]).astype(BF16), r_prev[p].astype(BF16)) for p in P]
    u = [_dot_tn((k[p].astype(F32) * kd_ref[p]).astype(BF16), v[p]) for p in P]
    for p in P:
        r_ref[p] = sd_ref[p] * r_prev[p] + same * u[p]
    y = [_dot(s_a[p].astype(BF16), v[p] * keep_a) + _dot(s_b[p].astype(BF16), v[p] * keep_b) + cross[p]
         for p in P]
    yc = [y[p] - _dot(y[p].astype(BF16), avg) for p in P]
    var = [_dot((yc[p] * yc[p]).astype(BF16), avg) for p in P]
    for p in P:
        yn = yc[p] * lax.rsqrt(var[p] + EPS) * gn_ref[:, sl[p]]
        g = gt_ref[0, :, sl[p]].astype(F32)
        o_ref[0, :, sl[p]] = (g / (1.0 + jnp.exp(-g)) * yn).astype(BF16)


def _retention(q, k, v, gt, ret_gn):
    B, S, W = q.shape
    C = RET_CHUNK
    dm, kd, qd, sd, same = _retention_tables()
    tok = lambda b, n: (b, n, 0)
    fixed3 = lambda b, n: (0, 0, 0)
    return pl.pallas_call(
        _retention_kernel,
        grid=(B, S // C),
        in_specs=[pl.BlockSpec((1, C, W), tok)] * 4 + [
            pl.BlockSpec((N_PAIRS, 2, C, C), lambda b, n: (0, 0, 0, 0)),
            pl.BlockSpec((N_PAIRS, C, LANES), fixed3),
            pl.BlockSpec((N_PAIRS, C, LANES), fixed3),
            pl.BlockSpec((N_PAIRS, LANES, LANES), fixed3),
            pl.BlockSpec((LANES, LANES), lambda b, n: (0, 0)),
            pl.BlockSpec((1, W), lambda b, n: (0, 0)),
        ],
        out_specs=pl.BlockSpec((1, C, W), tok),
        out_shape=jax.ShapeDtypeStruct((B, S, W), BF16),
        scratch_shapes=[pltpu.VMEM((N_PAIRS, LANES, LANES), F32)],
        compiler_params=_cparams(("parallel", "arbitrary")),
        name="retention",
    )(q, k, v, gt, dm, kd, qd, sd, same, ret_gn[None, :])


def _mem_kv_kernel(mem_ref, w_ref, kg_ref, k_ref, v_ref):
    kv = _dot(mem_ref[...].astype(BF16), w_ref[...])
    lane = lax.broadcasted_iota(jnp.int32, (1, LANES), 1)
    head_a = lane < HEAD_DIM
    inv_d = 1.0 / HEAD_DIM
    for j in range(MEM_WIDTH // LANES):
        kj = kv[:, j * LANES:(j + 1) * LANES]
        k2 = kj * kj
        ms_a = jnp.sum(jnp.where(head_a, k2, 0.0), axis=-1, keepdims=True) * inv_d
        ms_b = jnp.sum(jnp.where(head_a, 0.0, k2), axis=-1, keepdims=True) * inv_d
        kn = kj * lax.rsqrt(jnp.where(head_a, ms_a, ms_b) + EPS) * kg_ref[...]
        k_ref[:, j * LANES:(j + 1) * LANES] = kn.astype(BF16)
    v_ref[...] = kv[:, MEM_WIDTH:].astype(BF16)


def _mem_kv(mem2, w_mem_kv, k_g):
    TM, D = mem2.shape
    tm = min(ROW_TILE, TM)
    kg_row = jnp.tile(k_g, LANES // HEAD_DIM)[None, :]
    row = lambda i: (i, 0)
    fixed = lambda i: (0, 0)
    return pl.pallas_call(
        _mem_kv_kernel,
        grid=(TM // tm,),
        in_specs=[pl.BlockSpec((tm, D), row), pl.BlockSpec((D, 2 * MEM_WIDTH), fixed),
                  pl.BlockSpec((1, LANES), fixed)],
        out_specs=[pl.BlockSpec((tm, MEM_WIDTH), row)] * 2,
        out_shape=[jax.ShapeDtypeStruct((TM, MEM_WIDTH), BF16)] * 2,
        compiler_params=_cparams(("parallel",)),
        name="mem_kv",
    )(mem2, w_mem_kv.astype(BF16), kg_row)


def _mem_attn_kernel(qm_ref, k_ref, v_ref, qg_ref, o_ref):
    lane = lax.broadcasted_iota(jnp.int32, (1, LANES), 1)
    head_a = lane < HEAD_DIM
    inv_d = 1.0 / HEAD_DIM
    scale = HEAD_DIM ** -0.5
    for j in range(MEM_WIDTH // LANES):
        sl = slice(j * LANES, (j + 1) * LANES)
        qj = qm_ref[0, :, sl].astype(F32)
        q2 = qj * qj
        ms_a = jnp.sum(jnp.where(head_a, q2, 0.0), axis=-1, keepdims=True) * inv_d
        ms_b = jnp.sum(jnp.where(head_a, 0.0, q2), axis=-1, keepdims=True) * inv_d
        qn = qj * lax.rsqrt(jnp.where(head_a, ms_a, ms_b) + EPS) * (qg_ref[...] * scale)
        kj = k_ref[0, :, sl]
        vj = v_ref[0, :, sl]
        out = None
        for sel in (head_a, jnp.logical_not(head_a)):
            s = _dot_nt(jnp.where(sel, qn, 0.0).astype(BF16), kj)
            p = jnp.exp(s - jnp.max(s, axis=-1, keepdims=True))
            p = p / jnp.sum(p, axis=-1, keepdims=True)
            o = _dot(p.astype(BF16), jnp.where(sel, vj, jnp.zeros_like(vj)))
            out = o if out is None else out + o
        o_ref[0, :, sl] = out.astype(BF16)


def _mem_attn(qm, k_m, v_m, q_g):
    B, S, _ = qm.shape
    M = k_m.shape[1]
    tm = min(ROW_TILE, S)
    qg_row = jnp.tile(q_g, LANES // HEAD_DIM)[None, :]
    return pl.pallas_call(
        _mem_attn_kernel,
        grid=(B, S // tm),
        in_specs=[pl.BlockSpec((1, tm, MEM_WIDTH), lambda b, i: (b, i, 0)),
                  pl.BlockSpec((1, M, MEM_WIDTH), lambda b, i: (b, 0, 0)),
                  pl.BlockSpec((1, M, MEM_WIDTH), lambda b, i: (b, 0, 0)),
                  pl.BlockSpec((1, LANES), lambda b, i: (0, 0))],
        out_specs=pl.BlockSpec((1, tm, MEM_WIDTH), lambda b, i: (b, i, 0)),
        out_shape=jax.ShapeDtypeStruct((B, S, MEM_WIDTH), BF16),
        compiler_params=_cparams(("parallel", "parallel")),
        name="mem_attn",
    )(qm, k_m, v_m, qg_row)


def _out_proj_kernel(x_ref, y_ref, m_ref, wy_ref, wm_ref, o_ref):
    o_ref[...] = x_ref[...] + _dot(y_ref[...], wy_ref[...]) + _dot(m_ref[...], wm_ref[...])


def _out_proj(x2, y2, m2, w_out):
    T, D = x2.shape
    tm = min(ROW_TILE, T)
    row = lambda i: (i, 0)
    fixed = lambda i: (0, 0)
    w = w_out.astype(BF16)
    return pl.pallas_call(
        _out_proj_kernel,
        grid=(T // tm,),
        in_specs=[pl.BlockSpec((tm, D), row), pl.BlockSpec((tm, MAIN_WIDTH), row),
                  pl.BlockSpec((tm, MEM_WIDTH), row),
                  pl.BlockSpec((MAIN_WIDTH, D), fixed), pl.BlockSpec((MEM_WIDTH, D), fixed)],
        out_specs=pl.BlockSpec((tm, D), row),
        out_shape=jax.ShapeDtypeStruct((T, D), F32),
        compiler_params=_cparams(("parallel",)),
        name="out_proj",
    )(x2, y2, m2, w[:MAIN_WIDTH], w[MAIN_WIDTH:])


ROUTER_LANE0 = N_GROUPS


RANK_BITS = 17
RANK_RADIX = 1 << RANK_BITS


def _router_kernel(x_ref, g_ref, whi_ref, wlo_ref, b_ref, info_ref, cnt_ref, tile_base_ref, base_ref):
    i = pl.program_id(0)

    @pl.when(i == 0)
    def _():
        base_ref[...] = jnp.zeros_like(base_ref)

    tm = x_ref.shape[0]
    h = _rmsnorm(x_ref[...], g_ref[...])
    h_hi = h.astype(BF16)
    h_lo = (h - h_hi.astype(F32)).astype(BF16)
    logits = (_dot(h_hi, whi_ref[...]) + _dot(h_lo, whi_ref[...]) + _dot(h_hi, wlo_ref[...])
              + b_ref[...])
    lane_i = lax.broadcasted_iota(jnp.int32, (tm, LANES), 1)
    lane = lane_i.astype(F32)
    big = float(LANES)

    is_g = lane_i < N_GROUPS
    lg = jnp.where(is_g, logits, NEG_BIG)
    mg = jnp.max(lg, axis=-1, keepdims=True)
    zg = jnp.sum(jnp.where(is_g, jnp.exp(lg - mg), 0.0), axis=-1, keepdims=True)
    p_grp = 1.0 / zg
    grp = jnp.min(jnp.where(is_g & (lg == mg), lane, big), axis=-1, keepdims=True)

    e_lane = lane_i - ROUTER_LANE0
    e_grp = (e_lane >> int(math.log2(EXPERTS_PER_GROUP))).astype(F32)
    is_e = (e_lane >= 0) & (e_lane < N_EXPERTS) & (e_grp == grp)
    le = jnp.where(is_e, logits, NEG_BIG)
    me = jnp.max(le, axis=-1, keepdims=True)
    ee = jnp.where(is_e, jnp.exp(le - me), 0.0)
    prob = ee / jnp.sum(ee, axis=-1, keepdims=True)
    p1 = jnp.max(prob, axis=-1, keepdims=True)
    i1 = jnp.min(jnp.where(is_e & (prob == p1), lane, big), axis=-1, keepdims=True)
    rest = is_e & (lane != i1)
    p2 = jnp.max(jnp.where(rest, prob, -1.0), axis=-1, keepdims=True)
    i2 = jnp.min(jnp.where(rest & (prob == p2), lane, big), axis=-1, keepdims=True)
    gate1 = p_grp * p1 / (p1 + p2)
    gate2 = p_grp * p2 / (p1 + p2)

    sel1 = lane == i1
    sel2 = lane == i2
    onehot = jnp.where(sel1 | sel2, 1.0, 0.0)
    r_io = lax.broadcasted_iota(jnp.int32, (tm, tm), 0)
    c_io = lax.broadcasted_iota(jnp.int32, (tm, tm), 1)
    lower = jnp.where(r_io > c_io, 1.0, 0.0).astype(BF16)
    tile_base_ref[0] = base_ref[...]
    before = _dot(lower, onehot.astype(BF16)) + base_ref[...]
    rank1 = jnp.sum(jnp.where(sel1, before, 0.0), axis=-1, keepdims=True)
    rank2 = jnp.sum(jnp.where(sel2, before, 0.0), axis=-1, keepdims=True)
    base_ref[...] += jnp.sum(onehot, axis=0, keepdims=True)
    cnt_ref[...] = base_ref[...]

    code1 = (i1 - float(ROUTER_LANE0)) * float(RANK_RADIX) + rank1
    code2 = (i2 - float(ROUTER_LANE0)) * float(RANK_RADIX) + rank2
    info = jnp.zeros((tm, LANES), F32)
    for col, val in enumerate((gate1, gate2, code1, code2)):
        info = jnp.where(lane_i == col, val, info)
    info_ref[...] = info


def _router(x2, g, w_grp, b_grp, w_exp, b_exp):
    T, D = x2.shape
    tm = min(ROW_TILE, T)
    w = jnp.zeros((D, LANES), F32)
    w = w.at[:, :N_GROUPS].set(w_grp).at[:, ROUTER_LANE0:ROUTER_LANE0 + N_EXPERTS].set(w_exp)
    b = jnp.zeros((1, LANES), F32)
    b = b.at[0, :N_GROUPS].set(b_grp).at[0, ROUTER_LANE0:ROUTER_LANE0 + N_EXPERTS].set(b_exp)
    w_hi = w.astype(BF16)
    w_lo = (w - w_hi.astype(F32)).astype(BF16)
    row = lambda i: (i, 0)
    fixed = lambda i: (0, 0)
    n_tiles = T // tm
    info, cnt, tile_base = pl.pallas_call(
        _router_kernel,
        grid=(n_tiles,),
        in_specs=[pl.BlockSpec((tm, D), row), pl.BlockSpec((1, D), fixed),
                  pl.BlockSpec((D, LANES), fixed), pl.BlockSpec((D, LANES), fixed),
                  pl.BlockSpec((1, LANES), fixed)],
        out_specs=[pl.BlockSpec((tm, LANES), row), pl.BlockSpec((1, LANES), fixed),
                   pl.BlockSpec((1, 1, LANES), lambda i: (i, 0, 0))],
        out_shape=[jax.ShapeDtypeStruct((T, LANES), F32), jax.ShapeDtypeStruct((1, LANES), F32),
                   jax.ShapeDtypeStruct((n_tiles, 1, LANES), F32)],
        scratch_shapes=[pltpu.VMEM((1, LANES), F32)],
        compiler_params=_cparams(("arbitrary",)),
        name="moe_router",
    )(x2, g[None, :], w_hi, w_lo, b)
    experts = slice(ROUTER_LANE0, ROUTER_LANE0 + N_EXPERTS)
    counts = cnt[0, experts].astype(jnp.int32)
    tile_base = tile_base[:, 0, experts].astype(jnp.int32)
    return info, counts, tile_base


TOK_ROWS = 4
U32 = jnp.uint32


def _tok_rows(r, n=1):
    start = r * TOK_ROWS
    if not isinstance(start, int):
        start = pl.multiple_of(start, TOK_ROWS)
    return pl.ds(start, n * TOK_ROWS)


def _pack_rows(h):
    bits = lax.bitcast_convert_type(h.astype(BF16).astype(F32), U32)
    half = h.shape[1] // 2
    return (bits[:, :half] >> 16) | bits[:, half:]


def _unpack_words(w):
    return (lax.bitcast_convert_type(w << 16, F32), lax.bitcast_convert_type(w & U32(0xFFFF0000), F32))


def _store_token_rows(ref, first_row, n, words):
    for c in range(TOK_ROWS):
        ref[pl.ds(first_row * TOK_ROWS + c, n, stride=TOK_ROWS), :] = words[:, c * LANES:(c + 1) * LANES]


def _load_token_rows(ref, n):
    parts = [_unpack_words(ref[pl.ds(c, n, stride=TOK_ROWS), :]) for c in range(TOK_ROWS)]
    return jnp.concatenate([p[0] for p in parts] + [p[1] for p in parts], axis=-1)


def _segment_copies(n, src, src_row, dst, dst_row, sem, top, op):
    off = 0
    bit = top
    while bit >= 1:
        take = n & bit

        @pl.when(take != 0)
        def _(bit=bit, off=off):
            op(pltpu.make_async_copy(src.at[_tok_rows(src_row + off, bit), :],
                                     dst.at[_tok_rows(dst_row + off, bit), :], sem))
        off = off + take
        bit //= 2


ENTRIES_PER_ITER = 2 * DMA_UNROLL
ITERS_PER_CODE_ROW = LANES // ENTRIES_PER_ITER


def _local_rows(code_ref, a_ref, tbl, dvec_ref, drow_ref, dsem):
    code = code_ref[...]
    e = code >> RANK_BITS
    d = code & (RANK_RADIX - 1)
    for k in range(N_EXPERTS):
        d = d + jnp.where(e == k, a_ref[tbl + k], 0)
    dvec_ref[...] = d * TOK_ROWS
    cp = pltpu.make_async_copy(dvec_ref, drow_ref, dsem)
    cp.start()
    cp.wait()


def _entry_pos(it):
    return it // ITERS_PER_CODE_ROW, (it % ITERS_PER_CODE_ROW) * ENTRIES_PER_ITER


def _rows_at(first_row):
    return pl.ds(pl.multiple_of(first_row, TOK_ROWS), TOK_ROWS)


def _dma_start(cp):
    cp.start()


def _dma_wait(cp):
    cp.wait()


def _block_copy(src, dst, blk, sem):
    return pltpu.make_async_copy(src, dst.at[_tok_rows(blk * MOE_BLOCK, MOE_BLOCK), :], sem)


def _dispatch_kernel(a_ref, n_ref, ls_ref, gb_ref, zrow_ref, zcnt_ref, nblk_ref,
                     code_ref, x_ref, g_ref, xs_ref, hbuf_ref, cbuf_ref, zbuf_ref, dvec_ref, drow_ref,
                     sem, zsem, dsem):
    s = pl.program_id(0)
    ts = x_ref.shape[0]

    @pl.when(s == 0)
    def _():
        zbuf_ref[...] = jnp.zeros_like(zbuf_ref)
        n_blocks = xs_ref.shape[0] // (MOE_BLOCK * TOK_ROWS)
        for op in (_dma_start, _dma_wait):
            def tail(b, c, op=op):
                op(_block_copy(zbuf_ref, xs_ref, b, zsem))
                return c
            lax.fori_loop(nblk_ref[0], n_blocks, tail, 0)

            def pad(e, c, op=op):
                _segment_copies(zcnt_ref[e], zbuf_ref, 0, xs_ref, zrow_ref[e], zsem, MOE_BLOCK // 2, op)
                return c
            lax.fori_loop(0, N_EXPERTS, pad, 0)

    for sub in range(ts // MOE_TILE):
        h = _rmsnorm(x_ref[pl.ds(sub * MOE_TILE, MOE_TILE), :], g_ref[...])
        _store_token_rows(hbuf_ref, sub * MOE_TILE, MOE_TILE, _pack_rows(h))

    tbl = s * N_EXPERTS
    _local_rows(code_ref, a_ref, tbl, dvec_ref, drow_ref, dsem)

    def place(tb, c):
        row, col = _entry_pos(tb)
        for u in range(DMA_UNROLL):
            tile = hbuf_ref[_tok_rows(tb * DMA_UNROLL + u), :]
            for kk in range(2):
                cbuf_ref[_rows_at(drow_ref[row, col + u * 2 + kk]), :] = tile
        return c
    lax.fori_loop(0, ts // DMA_UNROLL, place, 0)

    for op in (_dma_start, _dma_wait):
        def run(e, c, op=op):
            _segment_copies(n_ref[tbl + e], cbuf_ref, ls_ref[tbl + e], xs_ref, gb_ref[tbl + e], sem, ts, op)
            return c
        lax.fori_loop(0, N_EXPERTS, run, 0)


def _dispatch(x2, g, codes, tables, zrow, zcnt, nblk_used, n_rows, ts):
    T, D = x2.shape
    return pl.pallas_call(
        _dispatch_kernel,
        grid_spec=pltpu.PrefetchScalarGridSpec(
            num_scalar_prefetch=7,
            grid=(T // ts,),
            in_specs=[pl.BlockSpec((ts * 2 // LANES, LANES), lambda i, *_: (i, 0)),
                      pl.BlockSpec((ts, D), lambda i, *_: (i, 0)),
                      pl.BlockSpec((1, D), lambda i, *_: (0, 0))],
            out_specs=pl.BlockSpec(memory_space=pl.ANY),
            scratch_shapes=[pltpu.VMEM((ts * TOK_ROWS, LANES), U32),
                            pltpu.VMEM((2 * ts * TOK_ROWS, LANES), U32),
                            pltpu.VMEM((MOE_BLOCK * TOK_ROWS, LANES), U32),
                            pltpu.VMEM((ts * 2 // LANES, LANES), jnp.int32),
                            pltpu.SMEM((ts * 2 // LANES, LANES), jnp.int32),
                            pltpu.SemaphoreType.DMA(()), pltpu.SemaphoreType.DMA(()),
                            pltpu.SemaphoreType.DMA(())],
        ),
        out_shape=jax.ShapeDtypeStruct((n_rows * TOK_ROWS, LANES), U32),
        compiler_params=_cparams(("arbitrary",)),
        name="moe_dispatch",
    )(*tables, zrow, zcnt, nblk_used, codes, x2, g[None, :])


def _expert_kernel(blk_e_ref, nblk_ref, xs_ref, wg_ref, wu_ref, wd_ref, ys_ref, wg_s, wu_s, wd_s):
    b = pl.program_id(0)

    @pl.when((b == 0) | (blk_e_ref[b] != blk_e_ref[jnp.maximum(b - 1, 0)]))
    def _():
        wg_s[...] = wg_ref[0].astype(BF16)
        wu_s[...] = wu_ref[0].astype(BF16)
        wd_s[...] = wd_ref[0].astype(BF16)

    @pl.when(b < nblk_ref[0])
    def _():
        x = _load_token_rows(xs_ref, MOE_BLOCK).astype(BF16)
        a = _dot(x, wg_s[...])
        u = _dot(x, wu_s[...])
        hid = (a / (1.0 + jnp.exp(-a)) * u).astype(BF16)
        y = _dot(hid, wd_s[...])
        _store_token_rows(ys_ref, 0, MOE_BLOCK, _pack_rows(y))

    @pl.when(b >= nblk_ref[0])
    def _():
        ys_ref[...] = jnp.zeros_like(ys_ref)


def _experts(xs, blk_e, nblk_used, w_gate, w_up, w_down):
    rows = xs.shape[0]
    nblk = rows // (MOE_BLOCK * TOK_ROWS)
    D = w_gate.shape[1]
    blk = lambda b, be, nb: (jnp.minimum(b, nb[0] - 1), 0)
    out_blk = lambda b, be, nb: (b, 0)
    wsel = lambda b, be, nb: (be[b], 0, 0)
    return pl.pallas_call(
        _expert_kernel,
        grid_spec=pltpu.PrefetchScalarGridSpec(
            num_scalar_prefetch=2,
            grid=(nblk,),
            in_specs=[pl.BlockSpec((MOE_BLOCK * TOK_ROWS, LANES), blk),
                      pl.BlockSpec((1, D, D_EXPERT), wsel),
                      pl.BlockSpec((1, D, D_EXPERT), wsel),
                      pl.BlockSpec((1, D_EXPERT, D), wsel)],
            out_specs=pl.BlockSpec((MOE_BLOCK * TOK_ROWS, LANES), out_blk),
            scratch_shapes=[pltpu.VMEM((D, D_EXPERT), BF16), pltpu.VMEM((D, D_EXPERT), BF16),
                            pltpu.VMEM((D_EXPERT, D), BF16)],
        ),
        out_shape=jax.ShapeDtypeStruct((rows, LANES), U32),
        compiler_params=_cparams(("arbitrary",)),
        name="moe_experts",
    )(blk_e, nblk_used, xs, w_gate, w_up, w_down)


def _combine_kernel(a_ref, n_ref, ls_ref, gb_ref, code_ref, x_ref, info_ref, ys_ref, o_ref,
                    ybuf_ref, pick0_ref, pick1_ref, dvec_ref, drow_ref, sem, dsem):
    s = pl.program_id(0)
    ts = x_ref.shape[0]
    tbl = s * N_EXPERTS
    for op in (_dma_start, _dma_wait):
        def run(e, c, op=op):
            _segment_copies(n_ref[tbl + e], ys_ref, gb_ref[tbl + e], ybuf_ref, ls_ref[tbl + e], sem, ts, op)
            return c
        lax.fori_loop(0, N_EXPERTS, run, 0)
    _local_rows(code_ref, a_ref, tbl, dvec_ref, drow_ref, dsem)

    picks = (pick0_ref, pick1_ref)
    for sub in range(ts // MOE_TILE):
        def pick(tb, c, sub=sub):
            row, col = _entry_pos(sub * (MOE_TILE // DMA_UNROLL) + tb)
            for u in range(DMA_UNROLL):
                for kk in range(2):
                    picks[kk][_tok_rows(tb * DMA_UNROLL + u), :] = (
                        ybuf_ref[_rows_at(drow_ref[row, col + u * 2 + kk]), :])
            return c
        lax.fori_loop(0, MOE_TILE // DMA_UNROLL, pick, 0)

        rows = pl.ds(sub * MOE_TILE, MOE_TILE)
        info = info_ref[rows, :]
        g0 = info[:, 0:1]
        g1 = info[:, 1:2]
        half = x_ref.shape[1] // 2
        for c in range(TOK_ROWS):
            y0 = _unpack_words(pick0_ref[pl.ds(c, MOE_TILE, stride=TOK_ROWS), :])
            y1 = _unpack_words(pick1_ref[pl.ds(c, MOE_TILE, stride=TOK_ROWS), :])
            for part in range(2):
                sl = slice(part * half + c * LANES, part * half + (c + 1) * LANES)
                o_ref[rows, sl] = x_ref[rows, sl] + (y0[part] * g0 + y1[part] * g1)


def _combine(x2, info, ys, codes, tables, ts):
    T, D = x2.shape
    return pl.pallas_call(
        _combine_kernel,
        grid_spec=pltpu.PrefetchScalarGridSpec(
            num_scalar_prefetch=4,
            grid=(T // ts,),
            in_specs=[pl.BlockSpec((ts * 2 // LANES, LANES), lambda i, *_: (i, 0)),
                      pl.BlockSpec((ts, D), lambda i, *_: (i, 0)),
                      pl.BlockSpec((ts, LANES), lambda i, *_: (i, 0)),
                      pl.BlockSpec(memory_space=pl.ANY)],
            out_specs=pl.BlockSpec((ts, D), lambda i, *_: (i, 0)),
            scratch_shapes=[pltpu.VMEM((2 * ts * TOK_ROWS, LANES), U32),
                            pltpu.VMEM((MOE_TILE * TOK_ROWS, LANES), U32),
                            pltpu.VMEM((MOE_TILE * TOK_ROWS, LANES), U32),
                            pltpu.VMEM((ts * 2 // LANES, LANES), jnp.int32),
                            pltpu.SMEM((ts * 2 // LANES, LANES), jnp.int32),
                            pltpu.SemaphoreType.DMA(()), pltpu.SemaphoreType.DMA(())],
        ),
        out_shape=jax.ShapeDtypeStruct((T, D), F32),
        compiler_params=_cparams(("arbitrary",)),
        name="moe_combine",
    )(*tables, codes, x2, info, ys)


def _supertile_tables(tile_base, counts, pad_start, ts):
    per = ts // ROW_TILE
    base = tile_base[::per]
    nxt = jnp.concatenate([base[1:], counts[None, :]], axis=0)
    n = nxt - base
    lstart = jnp.cumsum(n, axis=1) - n
    flat = lambda a: a.reshape(-1).astype(jnp.int32)
    return flat(lstart - base), flat(n), flat(lstart), flat(pad_start[None, :] + base)


DISPATCH_TOKENS = 2048
COMBINE_TOKENS = 1024


def _hier_moe(x2, ln2, w_grp, b_grp, w_exp, b_exp, w_gate, w_up, w_down):
    T, D = x2.shape
    info, counts, tile_base = _router(x2, ln2, w_grp, b_grp, w_exp, b_exp)
    padded = (counts + MOE_BLOCK - 1) // MOE_BLOCK * MOE_BLOCK
    pad_end = jnp.cumsum(padded)
    pad_start = pad_end - padded
    codes = info[:, 2:4].astype(jnp.int32).reshape(T * 2 // LANES, LANES)
    n_rows = T * 2 + N_EXPERTS * MOE_BLOCK
    nblk = n_rows // MOE_BLOCK
    blk_row = jnp.arange(nblk, dtype=jnp.int32) * MOE_BLOCK
    blk_e = jnp.minimum(jnp.sum((pad_end[None, :] <= blk_row[:, None]).astype(jnp.int32), axis=1),
                        N_EXPERTS - 1).astype(jnp.int32)
    nblk_used = (pad_end[-1:] // MOE_BLOCK).astype(jnp.int32)
    zrow = (pad_start + counts).astype(jnp.int32)
    zcnt = (padded - counts).astype(jnp.int32)

    td = min(DISPATCH_TOKENS, T)
    tc = min(COMBINE_TOKENS, T)

    xs = _dispatch(x2, ln2, codes, _supertile_tables(tile_base, counts, pad_start, td),
                   zrow, zcnt, nblk_used, n_rows, td)
    ys = _experts(xs, blk_e, nblk_used, w_gate, w_up, w_down)
    return _combine(x2, info, ys, codes, _supertile_tables(tile_base, counts, pad_start, tc), tc)


HEAD_PAD = LANES
LATENT_PAD = 3 * LANES


def _rot_partner():
    half = QK_ROPE // 2
    r = jnp.arange(QK_ROPE)
    return jnp.where(r < half, r + half, r - half), jnp.where(r < half, -1.0, 1.0).astype(F32)


def _mla_rope_rows():
    half = QK_ROPE // 2
    inv = ROPE_THETA ** (-jnp.arange(half, dtype=F32) / half)
    lane = jnp.arange(LANES)
    r = lane - QK_NOPE
    in_rope = (r >= 0) & (r < QK_ROPE)
    inv_row = jnp.where(in_rope, inv[jnp.clip(r, 0, QK_ROPE - 1) % half], 0.0)[None, :]
    rope_row = in_rope.astype(F32)[None, :]
    real_row = (lane < QK_HEAD).astype(F32)[None, :]
    return inv_row, rope_row, real_row


def _head_gain_row(g):
    partner, _ = _rot_partner()
    return jnp.concatenate([g, g[QK_NOPE + partner]])[None, :]


def _with_partner_cols(w3):
    partner, sign = _rot_partner()
    rot = w3[:, :, QK_NOPE + partner] * sign
    return jnp.concatenate([w3, rot], axis=-1).reshape(w3.shape[0], N_MAIN_HEADS * HEAD_PAD)


def _heads_norm_rope(ys, gain_rows, scales, real_row, cos_real, sin_rope):
    n = range(len(ys))
    row_id = lax.broadcasted_iota(jnp.int32, (LANES, LANES), 0)
    ones_real = jnp.where(row_id < QK_HEAD, 1.0, 0.0).astype(BF16)
    ms = [_dot((ys[i] * ys[i]).astype(BF16), ones_real) * (1.0 / QK_HEAD) for i in n]
    yn = [ys[i] * (lax.rsqrt(ms[i] + EPS) * scales[i]) * gain_rows[i] for i in n]
    rolled = [pltpu.roll(yn[i], LANES - QK_ROPE, 1) for i in n]
    return [(yn[i] * cos_real + rolled[i] * sin_rope).astype(BF16) for i in n]


def _mla_qkv_kernel(x_ref, pos_ref, lnkv_ref, lnq_ref, wd_ref, scale_ref, wkv_ref, kg_ref,
                    win_ref, qlg_ref, wuq_ref, qg_ref, inv_ref, rope_ref, real_ref,
                    k_ref, v_ref, q_ref, qm_ref):
    x = x_ref[...]
    xr = x * lax.rsqrt(jnp.mean(x * x, axis=-1, keepdims=True) + EPS)
    ang = pos_ref[...].astype(F32) * inv_ref[...]
    real_row = real_ref[...]
    cos_real = jnp.cos(ang) * real_row
    sin_rope = jnp.sin(ang) * rope_ref[...]

    ckr = _dot((xr * lnkv_ref[...]).astype(BF16), wd_ref[...])
    c = ckr[:, :KV_LORA]
    r = lax.rsqrt(jnp.mean(c * c, axis=-1, keepdims=True) + EPS)
    lane = lax.broadcasted_iota(jnp.int32, (1, LATENT_PAD), 1)
    lhs = (ckr * jnp.where(lane < KV_LORA, r * scale_ref[...], 1.0)).astype(BF16)
    kv = _dot(lhs, wkv_ref[...])
    proj = _dot((xr * lnq_ref[...]).astype(BF16), win_ref[...])
    cq = _rmsnorm(proj[:, :Q_LORA], qlg_ref[...]).astype(BF16)
    q = _dot(cq, wuq_ref[...])
    qm_ref[...] = proj[:, Q_LORA:].astype(BF16)
    v_ref[...] = kv[:, N_MAIN_HEADS * HEAD_PAD:].astype(BF16)
    sls = [slice(hh * HEAD_PAD, (hh + 1) * HEAD_PAD) for hh in range(N_MAIN_HEADS)]
    nh = N_MAIN_HEADS
    outs = _heads_norm_rope([kv[:, sl] for sl in sls] + [q[:, sl] for sl in sls],
                            [kg_ref[...]] * nh + [qg_ref[...]] * nh,
                            [1.0] * nh + [QK_HEAD ** -0.5] * nh, real_row, cos_real, sin_rope)
    for hh, sl in enumerate(sls):
        k_ref[:, sl] = outs[hh]
        q_ref[:, sl] = outs[nh + hh]


def _mla_qkv(x2, pos_col, kv_ln, w_dkv, kv_lora_g, w_ukv, k_g, ln1, w_in, q_lora_g, w_uq, q_g):
    T, D = x2.shape
    tm = min(ROW_TILE, T)
    lat = LATENT_PAD
    wd = jnp.pad(w_dkv, ((0, 0), (0, lat - w_dkv.shape[1]))).astype(BF16)
    scale_row = jnp.pad(kv_lora_g, (0, lat - KV_LORA), constant_values=1.0)[None, :]
    w3 = w_ukv.reshape(KV_LORA, N_MAIN_HEADS, QK_NOPE + V_HEAD)
    wk = jnp.zeros((lat, N_MAIN_HEADS, QK_HEAD), F32)
    wk = wk.at[:KV_LORA, :, :QK_NOPE].set(w3[:, :, :QK_NOPE])
    eye = jnp.eye(QK_ROPE, dtype=F32)
    wk = wk.at[KV_LORA:KV_LORA + QK_ROPE, :, QK_NOPE:].set(
        jnp.broadcast_to(eye[:, None, :], (QK_ROPE, N_MAIN_HEADS, QK_ROPE)))
    wv = jnp.zeros((lat, N_MAIN_HEADS * V_HEAD), F32)
    wv = wv.at[:KV_LORA].set(w3[:, :, QK_NOPE:].reshape(KV_LORA, N_MAIN_HEADS * V_HEAD))
    wkv = jnp.concatenate([_with_partner_cols(wk), wv], axis=1).astype(BF16)
    wuq = _with_partner_cols(w_uq.reshape(Q_LORA, N_MAIN_HEADS, QK_HEAD)).astype(BF16)
    inv_row, rope_row, real_row = _mla_rope_rows()
    row = lambda i: (i, 0)
    fixed = lambda i: (0, 0)
    kw = N_MAIN_HEADS * HEAD_PAD
    n_in = w_in.shape[1]
    lane_row = pl.BlockSpec((1, LANES), fixed)
    return pl.pallas_call(
        _mla_qkv_kernel,
        grid=(T // tm,),
        in_specs=[pl.BlockSpec((tm, D), row), pl.BlockSpec((tm, 1), row),
                  pl.BlockSpec((1, D), fixed), pl.BlockSpec((1, D), fixed),
                  pl.BlockSpec((D, lat), fixed), pl.BlockSpec((1, lat), fixed),
                  pl.BlockSpec((lat, kw + MAIN_WIDTH), fixed), lane_row,
                  pl.BlockSpec((D, n_in), fixed), pl.BlockSpec((1, Q_LORA), fixed),
                  pl.BlockSpec((Q_LORA, kw), fixed), lane_row,
                  lane_row, lane_row, lane_row],
        out_specs=[pl.BlockSpec((tm, kw), row), pl.BlockSpec((tm, MAIN_WIDTH), row),
                   pl.BlockSpec((tm, kw), row), pl.BlockSpec((tm, MEM_WIDTH), row)],
        out_shape=[jax.ShapeDtypeStruct((T, kw), BF16), jax.ShapeDtypeStruct((T, MAIN_WIDTH), BF16),
                   jax.ShapeDtypeStruct((T, kw), BF16), jax.ShapeDtypeStruct((T, MEM_WIDTH), BF16)],
        compiler_params=_cparams(("parallel",)),
        name="mla_qkv",
    )(x2, pos_col, kv_ln[None, :], ln1[None, :], wd, scale_row, wkv, _head_gain_row(k_g),
      w_in.astype(BF16), q_lora_g[None, :], wuq, _head_gain_row(q_g), inv_row, rope_row, real_row)


ATTN_TILE = 1024


def _flash_kernel(qi_ref, kj_ref, q_ref, k_ref, v_ref, o_ref, m_ref, acc_ref):
    t = pl.program_id(2)
    i = qi_ref[t]
    j = kj_ref[t]
    tq = q_ref.shape[1]
    tk = k_ref.shape[1]

    @pl.when(j == 0)
    def _():
        m_ref[...] = jnp.full_like(m_ref, NEG_BIG)
        acc_ref[...] = jnp.zeros_like(acc_ref)

    lane = lax.broadcasted_iota(jnp.int32, (1, LANES), 1)
    head_a = lane < V_HEAD
    den_lane = (V_HEAD, 0)

    def attend(q0, nq, nk, diag_col):
        rows = pl.ds(q0, nq)
        v = v_ref[0, pl.ds(0, nk), :]
        if diag_col is not None:
            q_idx = lax.broadcasted_iota(jnp.int32, (nq, nk), 0) + diag_col
            k_idx = lax.broadcasted_iota(jnp.int32, (nq, nk), 1)
            visible = k_idx <= q_idx
        H = range(2)
        sl = [slice(hh * HEAD_PAD, (hh + 1) * HEAD_PAD) for hh in H]
        s = [_dot_nt(q_ref[0, rows, sl[hh]], k_ref[0, pl.ds(0, nk), sl[hh]]) for hh in H]
        if diag_col is not None:
            s = [jnp.where(visible, s[hh], NEG_BIG) for hh in H]
        m_prev = [m_ref[hh, rows, :] for hh in H]
        acc_prev = [acc_ref[hh, rows, :] for hh in H]
        m_new = [jnp.maximum(m_prev[hh], jnp.max(s[hh], axis=-1, keepdims=True)) for hh in H]
        alpha = [jnp.exp(m_prev[hh] - m_new[hh]) for hh in H]
        m_wide = [jnp.concatenate([m_new[hh]] * (nk // LANES), axis=1) for hh in H]
        p = [jnp.exp(s[hh] - m_wide[hh]).astype(BF16) for hh in H]
        keep_row = [jnp.where(head_a, 1.0, 0.0).astype(BF16), jnp.where(head_a, 0.0, 1.0).astype(BF16)]
        den_row = [jnp.where(lane == den_lane[hh], 1.0, 0.0).astype(BF16) for hh in H]
        pv = [_dot(p[hh], v * keep_row[hh] + den_row[hh]) for hh in H]
        for hh in H:
            acc_ref[hh, rows, :] = alpha[hh] * acc_prev[hh] + pv[hh]
            m_ref[hh, rows, :] = m_new[hh]

    @pl.when(j < i)
    def _():
        attend(0, tq, tk, None)

    @pl.when(j == i)
    def _():
        half = tq // 2
        attend(0, half, half, 0)
        attend(half, half, tk, half)
        acc_a = acc_ref[0]
        acc_b = acc_ref[1]
        out_a = acc_a / acc_a[:, den_lane[0]:den_lane[0] + 1]
        out_b = acc_b / acc_b[:, den_lane[1]:den_lane[1] + 1]
        o_ref[0] = jnp.where(head_a, out_a, out_b).astype(BF16)


def _flash(q, k, v):
    B, S, _ = q.shape
    t = min(ATTN_TILE, S)
    n = S // t
    pairs = [(i, j) for i in range(n) for j in range(i + 1)]
    qi = jnp.array([p[0] for p in pairs], jnp.int32)
    kj = jnp.array([p[1] for p in pairs], jnp.int32)
    return pl.pallas_call(
        _flash_kernel,
        grid_spec=pltpu.PrefetchScalarGridSpec(
            num_scalar_prefetch=2,
            grid=(B, N_PAIRS, len(pairs)),
            in_specs=[pl.BlockSpec((1, t, 2 * HEAD_PAD), lambda b, p, s, qi, kj: (b, qi[s], p)),
                      pl.BlockSpec((1, t, 2 * HEAD_PAD), lambda b, p, s, qi, kj: (b, kj[s], p)),
                      pl.BlockSpec((1, t, LANES), lambda b, p, s, qi, kj: (b, kj[s], p))],
            out_specs=pl.BlockSpec((1, t, LANES), lambda b, p, s, qi, kj: (b, qi[s], p)),
            scratch_shapes=[pltpu.VMEM((2, t, LANES), F32), pltpu.VMEM((2, t, LANES), F32)],
        ),
        out_shape=jax.ShapeDtypeStruct((B, S, MAIN_WIDTH), BF16),
        compiler_params=_cparams(("parallel", "parallel", "arbitrary")),
        name="mla_flash",
    )(qi, kj, q, k, v)


def kernel(x, mem, positions, ln1, ln2, w_out, mem_w_kv, mem_q_norm, mem_k_norm, router_group_w, router_group_b, router_expert_w, router_expert_b, expert_w_gate, expert_w_up, expert_w_down, ret_w_in, ret_gn, kv_ln, kv_w_down, kv_lora_norm, kv_w_up, k_norm, mla_w_in, q_lora_norm, mla_w_uq, q_norm):
    B, S, D = x.shape
    M = mem.shape[1]
    T = B * S
    x2 = x.reshape(T, D)
    mem2 = mem.reshape(B * M, D)
    pos_col = positions.reshape(T, 1).astype(jnp.int32)

    def mem_path(i, qm):
        k_m, v_m = _mem_kv(mem2, mem_w_kv[i], mem_k_norm[i])
        return _mem_attn(qm.reshape(B, S, MEM_WIDTH), k_m.reshape(B, M, MEM_WIDTH),
                         v_m.reshape(B, M, MEM_WIDTH), mem_q_norm[i]).reshape(T, MEM_WIDTH)

    def moe(i, xin):
        return _hier_moe(xin, ln2[i], router_group_w[i], router_group_b[i], router_expert_w[i],
                         router_expert_b[i], expert_w_gate[i], expert_w_up[i], expert_w_down[i])

    q, k, v, gt, qm = _ret_inproj(x2, pos_col, ln1[0], ret_w_in[0])
    shp = (B, S, MAIN_WIDTH)
    y = _retention(q.reshape(shp), k.reshape(shp), v.reshape(shp), gt.reshape(shp), ret_gn[0])
    x2 = _out_proj(x2, y.reshape(T, MAIN_WIDTH), mem_path(0, qm), w_out[0])
    x2 = moe(0, x2)

    k_sh, v_sh, q1, qm1 = _mla_qkv(x2, pos_col, kv_ln, kv_w_down, kv_lora_norm, kv_w_up, k_norm,
                                   ln1[1], mla_w_in[0], q_lora_norm[0], mla_w_uq[0], q_norm[0])
    kw = N_MAIN_HEADS * HEAD_PAD
    y1 = _flash(q1.reshape(B, S, kw), k_sh.reshape(B, S, kw), v_sh.reshape(shp))
    x2 = _out_proj(x2, y1.reshape(T, MAIN_WIDTH), mem_path(1, qm1), w_out[1])
    x2 = moe(1, x2)
    return x2.reshape(B, S, D)
```

```python
import math

import jax
import jax.numpy as jnp
from jax import lax
from jax.experimental import pallas as pl
from jax.experimental.pallas import tpu as pltpu

F32 = jnp.float32
BF16 = jnp.bfloat16

HEAD_DIM = 64
N_MAIN_HEADS = 12
MAIN_WIDTH = N_MAIN_HEADS * HEAD_DIM
N_MEM_HEADS = 4
MEM_WIDTH = N_MEM_HEADS * HEAD_DIM
RET_CHUNK = 128
ROPE_THETA = 10000.0
Q_LORA = 384
KV_LORA = 256
QK_NOPE = 64
QK_ROPE = 32
QK_HEAD = QK_NOPE + QK_ROPE
V_HEAD = 64
N_GROUPS = 4
EXPERTS_PER_GROUP = 8
N_EXPERTS = N_GROUPS * EXPERTS_PER_GROUP
D_EXPERT = 256
MOE_BLOCK = 256
EPS = 1e-6

LANES = 128
VMEM_LIMIT = 48 * 1024 * 1024
NEG_BIG = -1e30

N_PAIRS = N_MAIN_HEADS // 2
ROW_TILE = 512
MOE_TILE = 256
DMA_UNROLL = 8


def _cparams(sem):
    return pltpu.CompilerParams(dimension_semantics=sem, vmem_limit_bytes=VMEM_LIMIT)


def _dot(a, b):
    return jnp.dot(a, b, preferred_element_type=F32)


def _dot_nt(a, b):
    return lax.dot_general(a, b, (((1,), (1,)), ((), ())), preferred_element_type=F32)


def _dot_tn(a, b):
    return lax.dot_general(a, b, (((0,), (0,)), ((), ())), preferred_element_type=F32)


def _rmsnorm(xf, g):
    return xf * lax.rsqrt(jnp.mean(xf * xf, axis=-1, keepdims=True) + EPS) * g


def _rope_tables(pos_col, inv_row, sgn_row):
    ang = pos_col.astype(F32) * inv_row
    return jnp.cos(ang), jnp.sin(ang) * sgn_row


def _ret_inproj_kernel(x_ref, pos_ref, g_ref, inv_ref, sgn_ref, w_ref,
                       q_ref, k_ref, v_ref, gt_ref, qm_ref):
    h = _rmsnorm(x_ref[...], g_ref[...]).astype(BF16)
    cos, sin_s = _rope_tables(pos_ref[...], inv_ref[...], sgn_ref[...])
    first = sgn_ref[...] < 0.0
    mw = MAIN_WIDTH
    half = HEAD_DIM // 2
    qk = _dot(h, w_ref[:, :2 * mw])
    groups = range(2 * mw // LANES)
    ys = [qk[:, j * LANES:(j + 1) * LANES] for j in groups]
    fwd = [pltpu.roll(ys[j], LANES - half, 1) for j in groups]
    bwd = [pltpu.roll(ys[j], half, 1) for j in groups]
    outs = [(ys[j] * cos + jnp.where(first, fwd[j], bwd[j]) * sin_s).astype(BF16) for j in groups]
    n_q = mw // LANES
    for j in range(n_q):
        q_ref[:, j * LANES:(j + 1) * LANES] = outs[j]
        k_ref[:, j * LANES:(j + 1) * LANES] = outs[n_q + j]
    v_ref[...] = _dot(h, w_ref[:, 2 * mw:3 * mw]).astype(BF16)
    gt_ref[...] = _dot(h, w_ref[:, 3 * mw:4 * mw]).astype(BF16)
    qm_ref[...] = _dot(h, w_ref[:, 4 * mw:4 * mw + MEM_WIDTH]).astype(BF16)


def _ret_inproj(x2, pos_col, g, w_in):
    T, D = x2.shape
    tm = min(ROW_TILE, T)
    half = HEAD_DIM // 2
    inv = ROPE_THETA ** (-jnp.arange(half, dtype=F32) / half)
    lane = jnp.arange(LANES)
    inv_row = inv[lane % half][None, :]
    sgn_row = jnp.where((lane % HEAD_DIM) < half, -1.0, 1.0).astype(F32)[None, :]
    row = lambda i: (i, 0)
    fixed = lambda i: (0, 0)
    n_in = w_in.shape[1]
    outs = pl.pallas_call(
        _ret_inproj_kernel,
        grid=(T // tm,),
        in_specs=[
            pl.BlockSpec((tm, D), row),
            pl.BlockSpec((tm, 1), row),
            pl.BlockSpec((1, D), fixed),
            pl.BlockSpec((1, LANES), fixed),
            pl.BlockSpec((1, LANES), fixed),
            pl.BlockSpec((D, n_in), fixed),
        ],
        out_specs=[pl.BlockSpec((tm, MAIN_WIDTH), row)] * 4 + [pl.BlockSpec((tm, MEM_WIDTH), row)],
        out_shape=[jax.ShapeDtypeStruct((T, MAIN_WIDTH), BF16)] * 4
        + [jax.ShapeDtypeStruct((T, MEM_WIDTH), BF16)],
        compiler_params=_cparams(("parallel",)),
        name="ret_inproj",
    )(x2, pos_col, g[None, :], inv_row, sgn_row, w_in.astype(BF16))
    return outs


def _retention_tables():
    H, C, d = N_MAIN_HEADS, RET_CHUNK, HEAD_DIM
    log_g = jnp.log1p(-jnp.exp2(-5.0 - jnp.arange(H, dtype=F32)))
    idx = jnp.arange(C, dtype=F32)
    rel = idx[:, None] - idx[None, :]
    scale = d ** -0.5
    decay_in = jnp.where(rel[None] >= 0,
                         jnp.exp(log_g[:, None, None] * jnp.maximum(rel, 0.0)[None]), 0.0) * scale
    kdec = jnp.exp(log_g[None, :] * (C - 1.0 - idx)[:, None]) * scale
    qdec = jnp.exp(log_g[None, :] * (idx + 1.0)[:, None])
    cdec = jnp.exp(log_g * C)

    def lanes(t):
        return jnp.repeat(t, d, axis=1).reshape(C, N_PAIRS, 2 * d).transpose(1, 0, 2)

    head_of_lane = jnp.arange(2 * d) // d
    same = (head_of_lane[:, None] == head_of_lane[None, :]).astype(F32)
    cd_lane = jnp.repeat(cdec, d).reshape(N_PAIRS, 2 * d)
    state_decay = cd_lane[:, :, None] * same[None]
    decay_in = decay_in.reshape(N_PAIRS, 2, C, C)
    return decay_in, lanes(kdec), lanes(qdec), state_decay, same


RET_CHUNKS_PER_STEP = 4


def _retention_kernel(q_ref, k_ref, v_ref, gt_ref, dm_ref, kd_ref, qd_ref, sd_ref, same_ref,
                      gn_ref, o_ref, r_ref):
    n = pl.program_id(1)

    @pl.when(n == 0)
    def _():
        r_ref[...] = jnp.zeros_like(r_ref)

    lane = lax.broadcasted_iota(jnp.int32, (1, LANES), 1)
    head_a = lane < HEAD_DIM
    keep_a = jnp.where(head_a, 1.0, 0.0).astype(BF16)
    keep_b = jnp.where(head_a, 0.0, 1.0).astype(BF16)
    same = same_ref[...]
    avg = (same * (1.0 / HEAD_DIM)).astype(BF16)
    C = RET_CHUNK
    n_chunks = q_ref.shape[1] // C
    P = range(N_PAIRS)
    CP = [(c, p) for c in range(n_chunks) for p in P]
    rows = [pl.ds(c * C, C) for c in range(n_chunks)]
    sl = [slice(p * LANES, (p + 1) * LANES) for p in P]
    q = {cp: q_ref[0, rows[cp[0]], sl[cp[1]]] for cp in CP}
    k = {cp: k_ref[0, rows[cp[0]], sl[cp[1]]] for cp in CP}
    v = {cp: v_ref[0, rows[cp[0]], sl[cp[1]]] for cp in CP}
    s_a = {cp: _dot_nt(q[cp] * keep_a, k[cp]) * dm_ref[cp[1], 0] for cp in CP}
    s_b = {cp: _dot_nt(q[cp] * keep_b, k[cp]) * dm_ref[cp[1], 1] for cp in CP}
    u = {cp: _dot_tn((k[cp].astype(F32) * kd_ref[cp[1]]).astype(BF16), v[cp]) for cp in CP}
    qd = {cp: (q[cp].astype(F32) * qd_ref[cp[1]]).astype(BF16) for cp in CP}
    inner = {cp: _dot(s_a[cp].astype(BF16), v[cp] * keep_a) + _dot(s_b[cp].astype(BF16), v[cp] * keep_b)
             for cp in CP}
    state = [r_ref[p] for p in P]
    y = {}
    for c in range(n_chunks):
        for p in P:
            y[(c, p)] = inner[(c, p)] + _dot(qd[(c, p)], state[p].astype(BF16))
        state = [sd_ref[p] * state[p] + same * u[(c, p)] for p in P]
    for p in P:
        r_ref[p] = state[p]
    yc = {cp: y[cp] - _dot(y[cp].astype(BF16), avg) for cp in CP}
    var = {cp: _dot((yc[cp] * yc[cp]).astype(BF16), avg) for cp in CP}
    for c, p in CP:
        yn = yc[(c, p)] * lax.rsqrt(var[(c, p)] + EPS) * gn_ref[:, sl[p]]
        g = gt_ref[0, rows[c], sl[p]].astype(F32)
        o_ref[0, rows[c], sl[p]] = (g / (1.0 + jnp.exp(-g)) * yn).astype(BF16)


def _retention(q, k, v, gt, ret_gn):
    B, S, W = q.shape
    C = RET_CHUNK
    dm, kd, qd, sd, same = _retention_tables()
    tok = lambda b, n: (b, n, 0)
    fixed3 = lambda b, n: (0, 0, 0)
    step = RET_CHUNKS_PER_STEP * C
    return pl.pallas_call(
        _retention_kernel,
        grid=(B, S // step),
        in_specs=[pl.BlockSpec((1, step, W), tok)] * 4 + [
            pl.BlockSpec((N_PAIRS, 2, C, C), lambda b, n: (0, 0, 0, 0)),
            pl.BlockSpec((N_PAIRS, C, LANES), fixed3),
            pl.BlockSpec((N_PAIRS, C, LANES), fixed3),
            pl.BlockSpec((N_PAIRS, LANES, LANES), fixed3),
            pl.BlockSpec((LANES, LANES), lambda b, n: (0, 0)),
            pl.BlockSpec((1, W), lambda b, n: (0, 0)),
        ],
        out_specs=pl.BlockSpec((1, step, W), tok),
        out_shape=jax.ShapeDtypeStruct((B, S, W), BF16),
        scratch_shapes=[pltpu.VMEM((N_PAIRS, LANES, LANES), F32)],
        compiler_params=_cparams(("parallel", "arbitrary")),
        name="retention",
    )(q, k, v, gt, dm, kd, qd, sd, same, ret_gn[None, :])


def _mem_kv_kernel(mem_ref, w_ref, kg_ref, k_ref, v_ref):
    kv = _dot(mem_ref[...].astype(BF16), w_ref[...])
    lane = lax.broadcasted_iota(jnp.int32, (1, LANES), 1)
    head_a = lane < HEAD_DIM
    inv_d = 1.0 / HEAD_DIM
    for j in range(MEM_WIDTH // LANES):
        kj = kv[:, j * LANES:(j + 1) * LANES]
        k2 = kj * kj
        ms_a = jnp.sum(jnp.where(head_a, k2, 0.0), axis=-1, keepdims=True) * inv_d
        ms_b = jnp.sum(jnp.where(head_a, 0.0, k2), axis=-1, keepdims=True) * inv_d
        kn = kj * lax.rsqrt(jnp.where(head_a, ms_a, ms_b) + EPS) * kg_ref[...]
        k_ref[:, j * LANES:(j + 1) * LANES] = kn.astype(BF16)
    v_ref[...] = kv[:, MEM_WIDTH:].astype(BF16)


def _mem_kv(mem2, w_mem_kv, k_g):
    TM, D = mem2.shape
    tm = min(ROW_TILE, TM)
    kg_row = jnp.tile(k_g, LANES // HEAD_DIM)[None, :]
    row = lambda i: (i, 0)
    fixed = lambda i: (0, 0)
    return pl.pallas_call(
        _mem_kv_kernel,
        grid=(TM // tm,),
        in_specs=[pl.BlockSpec((tm, D), row), pl.BlockSpec((D, 2 * MEM_WIDTH), fixed),
                  pl.BlockSpec((1, LANES), fixed)],
        out_specs=[pl.BlockSpec((tm, MEM_WIDTH), row)] * 2,
        out_shape=[jax.ShapeDtypeStruct((TM, MEM_WIDTH), BF16)] * 2,
        compiler_params=_cparams(("parallel",)),
        name="mem_kv",
    )(mem2, w_mem_kv.astype(BF16), kg_row)


def _mem_attn_kernel(qm_ref, k_ref, v_ref, qg_ref, o_ref):
    lane = lax.broadcasted_iota(jnp.int32, (1, LANES), 1)
    head_a = lane < HEAD_DIM
    inv_d = 1.0 / HEAD_DIM
    scale = HEAD_DIM ** -0.5
    for j in range(MEM_WIDTH // LANES):
        sl = slice(j * LANES, (j + 1) * LANES)
        qj = qm_ref[0, :, sl].astype(F32)
        q2 = qj * qj
        ms_a = jnp.sum(jnp.where(head_a, q2, 0.0), axis=-1, keepdims=True) * inv_d
        ms_b = jnp.sum(jnp.where(head_a, 0.0, q2), axis=-1, keepdims=True) * inv_d
        qn = qj * lax.rsqrt(jnp.where(head_a, ms_a, ms_b) + EPS) * (qg_ref[...] * scale)
        kj = k_ref[0, :, sl]
        vj = v_ref[0, :, sl]
        out = None
        for sel in (head_a, jnp.logical_not(head_a)):
            s = _dot_nt(jnp.where(sel, qn, 0.0).astype(BF16), kj)
            p = jnp.exp(s - jnp.max(s, axis=-1, keepdims=True))
            p = p / jnp.sum(p, axis=-1, keepdims=True)
            o = _dot(p.astype(BF16), jnp.where(sel, vj, jnp.zeros_like(vj)))
            out = o if out is None else out + o
        o_ref[0, :, sl] = out.astype(BF16)


def _mem_attn(qm, k_m, v_m, q_g):
    B, S, _ = qm.shape
    M = k_m.shape[1]
    tm = min(ROW_TILE, S)
    qg_row = jnp.tile(q_g, LANES // HEAD_DIM)[None, :]
    return pl.pallas_call(
        _mem_attn_kernel,
        grid=(B, S // tm),
        in_specs=[pl.BlockSpec((1, tm, MEM_WIDTH), lambda b, i: (b, i, 0)),
                  pl.BlockSpec((1, M, MEM_WIDTH), lambda b, i: (b, 0, 0)),
                  pl.BlockSpec((1, M, MEM_WIDTH), lambda b, i: (b, 0, 0)),
                  pl.BlockSpec((1, LANES), lambda b, i: (0, 0))],
        out_specs=pl.BlockSpec((1, tm, MEM_WIDTH), lambda b, i: (b, i, 0)),
        out_shape=jax.ShapeDtypeStruct((B, S, MEM_WIDTH), BF16),
        compiler_params=_cparams(("parallel", "parallel")),
        name="mem_attn",
    )(qm, k_m, v_m, qg_row)


ROUTER_LANE0 = N_GROUPS
RANK_BITS = 17
RANK_RADIX = 1 << RANK_BITS


def _proj_router_kernel(x_ref, y_ref, m_ref, wy_ref, wm_ref, g_ref, whi_ref, wlo_ref, b_ref,
                        o_ref, info_ref, cnt_ref, tile_base_ref, base_ref):
    x1 = x_ref[...] + _dot(y_ref[...], wy_ref[...]) + _dot(m_ref[...], wm_ref[...])
    o_ref[...] = x1
    _route_tile(x1, g_ref, whi_ref, wlo_ref, b_ref, info_ref, cnt_ref, tile_base_ref, base_ref)


def _route_tile(x, g_ref, whi_ref, wlo_ref, b_ref, info_ref, cnt_ref, tile_base_ref, base_ref):
    i = pl.program_id(0)

    @pl.when(i == 0)
    def _():
        base_ref[...] = jnp.zeros_like(base_ref)

    tm = x.shape[0]
    h = _rmsnorm(x, g_ref[...])
    h_hi = h.astype(BF16)
    h_lo = (h - h_hi.astype(F32)).astype(BF16)
    logits = (_dot(h_hi, whi_ref[...]) + _dot(h_lo, whi_ref[...]) + _dot(h_hi, wlo_ref[...])
              + b_ref[...])
    lane_i = lax.broadcasted_iota(jnp.int32, (tm, LANES), 1)
    lane = lane_i.astype(F32)
    big = float(LANES)

    is_g = lane_i < N_GROUPS
    lg = jnp.where(is_g, logits, NEG_BIG)
    mg = jnp.max(lg, axis=-1, keepdims=True)
    zg = jnp.sum(jnp.where(is_g, jnp.exp(lg - mg), 0.0), axis=-1, keepdims=True)
    p_grp = 1.0 / zg
    grp = jnp.min(jnp.where(is_g & (lg == mg), lane, big), axis=-1, keepdims=True)

    e_lane = lane_i - ROUTER_LANE0
    e_grp = (e_lane >> int(math.log2(EXPERTS_PER_GROUP))).astype(F32)
    is_e = (e_lane >= 0) & (e_lane < N_EXPERTS) & (e_grp == grp)
    le = jnp.where(is_e, logits, NEG_BIG)
    me = jnp.max(le, axis=-1, keepdims=True)
    ee = jnp.where(is_e, jnp.exp(le - me), 0.0)
    prob = ee / jnp.sum(ee, axis=-1, keepdims=True)
    p1 = jnp.max(prob, axis=-1, keepdims=True)
    i1 = jnp.min(jnp.where(is_e & (prob == p1), lane, big), axis=-1, keepdims=True)
    rest = is_e & (lane != i1)
    p2 = jnp.max(jnp.where(rest, prob, -1.0), axis=-1, keepdims=True)
    i2 = jnp.min(jnp.where(rest & (prob == p2), lane, big), axis=-1, keepdims=True)
    gate1 = p_grp * p1 / (p1 + p2)
    gate2 = p_grp * p2 / (p1 + p2)

    sel1 = lane == i1
    sel2 = lane == i2
    onehot = jnp.where(sel1 | sel2, 1.0, 0.0)
    r_io = lax.broadcasted_iota(jnp.int32, (tm, tm), 0)
    c_io = lax.broadcasted_iota(jnp.int32, (tm, tm), 1)
    lower = jnp.where(r_io > c_io, 1.0, 0.0).astype(BF16)
    tile_base_ref[0] = base_ref[...]
    before = _dot(lower, onehot.astype(BF16)) + base_ref[...]
    rank1 = jnp.sum(jnp.where(sel1, before, 0.0), axis=-1, keepdims=True)
    rank2 = jnp.sum(jnp.where(sel2, before, 0.0), axis=-1, keepdims=True)
    base_ref[...] += jnp.sum(onehot, axis=0, keepdims=True)
    cnt_ref[...] = base_ref[...]

    code1 = (i1 - float(ROUTER_LANE0)) * float(RANK_RADIX) + rank1
    code2 = (i2 - float(ROUTER_LANE0)) * float(RANK_RADIX) + rank2
    info = jnp.zeros((tm, LANES), F32)
    for col, val in enumerate((gate1, gate2, code1, code2)):
        info = jnp.where(lane_i == col, val, info)
    info_ref[...] = info


def _router_operands(D, g, w_grp, b_grp, w_exp, b_exp):
    w = jnp.zeros((D, LANES), F32)
    w = w.at[:, :N_GROUPS].set(w_grp).at[:, ROUTER_LANE0:ROUTER_LANE0 + N_EXPERTS].set(w_exp)
    b = jnp.zeros((1, LANES), F32)
    b = b.at[0, :N_GROUPS].set(b_grp).at[0, ROUTER_LANE0:ROUTER_LANE0 + N_EXPERTS].set(b_exp)
    w_hi = w.astype(BF16)
    w_lo = (w - w_hi.astype(F32)).astype(BF16)
    fixed = lambda i: (0, 0)
    specs = [pl.BlockSpec((1, D), fixed), pl.BlockSpec((D, LANES), fixed), pl.BlockSpec((D, LANES), fixed),
             pl.BlockSpec((1, LANES), fixed)]
    return (g[None, :], w_hi, w_lo, b), specs


def _router_outputs(T, tm):
    n_tiles = T // tm
    specs = [pl.BlockSpec((tm, LANES), lambda i: (i, 0)), pl.BlockSpec((1, LANES), lambda i: (0, 0)),
             pl.BlockSpec((1, 1, LANES), lambda i: (i, 0, 0))]
    shapes = [jax.ShapeDtypeStruct((T, LANES), F32), jax.ShapeDtypeStruct((1, LANES), F32),
              jax.ShapeDtypeStruct((n_tiles, 1, LANES), F32)]
    return specs, shapes


def _router_tables(cnt, tile_base):
    experts = slice(ROUTER_LANE0, ROUTER_LANE0 + N_EXPERTS)
    return cnt[0, experts].astype(jnp.int32), tile_base[:, 0, experts].astype(jnp.int32)


def _out_proj_router(x2, y2, m2, w_out, g, w_grp, b_grp, w_exp, b_exp):
    T, D = x2.shape
    tm = min(ROW_TILE, T)
    row = lambda i: (i, 0)
    fixed = lambda i: (0, 0)
    w = w_out.astype(BF16)
    ops, op_specs = _router_operands(D, g, w_grp, b_grp, w_exp, b_exp)
    out_specs, out_shapes = _router_outputs(T, tm)
    x1, info, cnt, tile_base = pl.pallas_call(
        _proj_router_kernel,
        grid=(T // tm,),
        in_specs=[pl.BlockSpec((tm, D), row), pl.BlockSpec((tm, MAIN_WIDTH), row),
                  pl.BlockSpec((tm, MEM_WIDTH), row),
                  pl.BlockSpec((MAIN_WIDTH, D), fixed), pl.BlockSpec((MEM_WIDTH, D), fixed)] + op_specs,
        out_specs=[pl.BlockSpec((tm, D), row)] + out_specs,
        out_shape=[jax.ShapeDtypeStruct((T, D), F32)] + out_shapes,
        scratch_shapes=[pltpu.VMEM((1, LANES), F32)],
        compiler_params=_cparams(("arbitrary",)),
        name="out_proj_router",
    )(x2, y2, m2, w[:MAIN_WIDTH], w[MAIN_WIDTH:], *ops)
    return (x1, info) + _router_tables(cnt, tile_base)


TOK_ROWS = 4
U32 = jnp.uint32


def _tok_rows(r, n=1):
    start = r * TOK_ROWS
    if not isinstance(start, int):
        start = pl.multiple_of(start, TOK_ROWS)
    return pl.ds(start, n * TOK_ROWS)


def _pack_rows(h):
    bits = lax.bitcast_convert_type(h.astype(BF16).astype(F32), U32)
    half = h.shape[1] // 2
    return (bits[:, :half] >> 16) | bits[:, half:]


def _unpack_words(w):
    return (lax.bitcast_convert_type(w << 16, F32), lax.bitcast_convert_type(w & U32(0xFFFF0000), F32))


def _store_token_rows(ref, first_row, n, words):
    for c in range(TOK_ROWS):
        ref[pl.ds(first_row * TOK_ROWS + c, n, stride=TOK_ROWS), :] = words[:, c * LANES:(c + 1) * LANES]


def _load_token_rows(ref, n):
    parts = [_unpack_words(ref[pl.ds(c, n, stride=TOK_ROWS), :]) for c in range(TOK_ROWS)]
    return jnp.concatenate([p[0] for p in parts] + [p[1] for p in parts], axis=-1)


def _segment_copies(n, src, src_row, dst, dst_row, sem, top, op):
    off = 0
    bit = top
    while bit >= 1:
        take = n & bit

        @pl.when(take != 0)
        def _(bit=bit, off=off):
            op(pltpu.make_async_copy(src.at[_tok_rows(src_row + off, bit), :],
                                     dst.at[_tok_rows(dst_row + off, bit), :], sem))
        off = off + take
        bit //= 2


ENTRIES_PER_ITER = 2 * DMA_UNROLL
ITERS_PER_CODE_ROW = LANES // ENTRIES_PER_ITER


def _local_rows(code_ref, a_ref, tbl, first_tok, dvec_ref, drow_ref, dsem):
    code = code_ref[...]
    e = code >> RANK_BITS
    d = (code & (RANK_RADIX - 1)) + first_tok
    for k in range(N_EXPERTS):
        d = d + jnp.where(e == k, a_ref[tbl + k], 0)
    dvec_ref[...] = d * TOK_ROWS
    cp = pltpu.make_async_copy(dvec_ref, drow_ref, dsem)
    cp.start()
    cp.wait()


def _run_copies(op, step, to_sorted, n_ref, ls_ref, gb_ref, local_ref, first_tok, sorted_ref, sem, top):
    tbl = step * N_EXPERTS

    def run(e, c):
        loc = (local_ref, first_tok + ls_ref[tbl + e])
        glob = (sorted_ref, gb_ref[tbl + e])
        (src, src_row), (dst, dst_row) = (loc, glob) if to_sorted else (glob, loc)
        _segment_copies(n_ref[tbl + e], src, src_row, dst, dst_row, sem, top, op)
        return c
    lax.fori_loop(0, N_EXPERTS, run, 0)


def _entry_pos(it):
    return it // ITERS_PER_CODE_ROW, (it % ITERS_PER_CODE_ROW) * ENTRIES_PER_ITER


def _rows_at(first_row):
    return pl.ds(pl.multiple_of(first_row, TOK_ROWS), TOK_ROWS)


def _dma_start(cp):
    cp.start()


def _dma_wait(cp):
    cp.wait()


def _block_copy(src, dst, blk, sem):
    return pltpu.make_async_copy(src, dst.at[_tok_rows(blk * MOE_BLOCK, MOE_BLOCK), :], sem)


def _dispatch_kernel(a_ref, n_ref, ls_ref, gb_ref, zrow_ref, zcnt_ref, nblk_ref,
                     code_ref, x_ref, g_ref, xs_ref, hbuf_ref, cbuf_ref, zbuf_ref, dvec_ref, drow_ref,
                     sem, zsem, dsem):
    s = pl.program_id(0)
    ts = x_ref.shape[0]

    @pl.when(s == 0)
    def _():
        zbuf_ref[...] = jnp.zeros_like(zbuf_ref)
        n_blocks = xs_ref.shape[0] // (MOE_BLOCK * TOK_ROWS)
        for op in (_dma_start, _dma_wait):
            def tail(b, c, op=op):
                op(_block_copy(zbuf_ref, xs_ref, b, zsem))
                return c
            lax.fori_loop(nblk_ref[0], n_blocks, tail, 0)

            def pad(e, c, op=op):
                _segment_copies(zcnt_ref[e], zbuf_ref, 0, xs_ref, zrow_ref[e], zsem, MOE_BLOCK // 2, op)
                return c
            lax.fori_loop(0, N_EXPERTS, pad, 0)

    for sub in range(ts // MOE_TILE):
        h = _rmsnorm(x_ref[pl.ds(sub * MOE_TILE, MOE_TILE), :], g_ref[...])
        _store_token_rows(hbuf_ref, sub * MOE_TILE, MOE_TILE, _pack_rows(h))

    slot = s % 2
    first_tok = slot * (2 * ts)
    last = pl.num_programs(0) - 1

    def runs(op, step, step_slot):
        _run_copies(op, step, True, n_ref, ls_ref, gb_ref, cbuf_ref, step_slot * (2 * ts), xs_ref,
                    sem.at[step_slot], ts)

    @pl.when(s >= 2)
    def _():
        runs(_dma_wait, s - 2, slot)

    _local_rows(code_ref, a_ref, s * N_EXPERTS, first_tok, dvec_ref, drow_ref, dsem)

    def place(tb, c):
        row, col = _entry_pos(tb)
        for u in range(DMA_UNROLL):
            tile = hbuf_ref[_tok_rows(tb * DMA_UNROLL + u), :]
            for kk in range(2):
                cbuf_ref[_rows_at(drow_ref[row, col + u * 2 + kk]), :] = tile
        return c
    lax.fori_loop(0, ts // DMA_UNROLL, place, 0)

    runs(_dma_start, s, slot)

    @pl.when(s == last)
    def _():
        @pl.when(s >= 1)
        def _():
            runs(_dma_wait, s - 1, 1 - slot)
        runs(_dma_wait, s, slot)


def _dispatch(x2, g, codes, tables, zrow, zcnt, nblk_used, n_rows, ts):
    T, D = x2.shape
    return pl.pallas_call(
        _dispatch_kernel,
        grid_spec=pltpu.PrefetchScalarGridSpec(
            num_scalar_prefetch=7,
            grid=(T // ts,),
            in_specs=[pl.BlockSpec((ts * 2 // LANES, LANES), lambda i, *_: (i, 0)),
                      pl.BlockSpec((ts, D), lambda i, *_: (i, 0)),
                      pl.BlockSpec((1, D), lambda i, *_: (0, 0))],
            out_specs=pl.BlockSpec(memory_space=pl.ANY),
            scratch_shapes=[pltpu.VMEM((ts * TOK_ROWS, LANES), U32),
                            pltpu.VMEM((2 * 2 * ts * TOK_ROWS, LANES), U32),
                            pltpu.VMEM((MOE_BLOCK * TOK_ROWS, LANES), U32),
                            pltpu.VMEM((ts * 2 // LANES, LANES), jnp.int32),
                            pltpu.SMEM((ts * 2 // LANES, LANES), jnp.int32),
                            pltpu.SemaphoreType.DMA((2,)), pltpu.SemaphoreType.DMA(()),
                            pltpu.SemaphoreType.DMA(())],
        ),
        out_shape=jax.ShapeDtypeStruct((n_rows * TOK_ROWS, LANES), U32),
        compiler_params=_cparams(("arbitrary",)),
        name="moe_dispatch",
    )(*tables, zrow, zcnt, nblk_used, codes, x2, g[None, :])


def _expert_kernel(blk_e_ref, nblk_ref, xs_ref, wg_ref, wu_ref, wd_ref, ys_ref, wg_s, wu_s, wd_s):
    b = pl.program_id(0)

    @pl.when((b == 0) | (blk_e_ref[b] != blk_e_ref[jnp.maximum(b - 1, 0)]))
    def _():
        wg_s[...] = wg_ref[0, 0].astype(BF16)
        wu_s[...] = wu_ref[0, 0].astype(BF16)
        wd_s[...] = wd_ref[0, 0].astype(BF16)

    @pl.when(b < nblk_ref[0])
    def _():
        x = _load_token_rows(xs_ref, MOE_BLOCK).astype(BF16)
        a = _dot(x, wg_s[...])
        u = _dot(x, wu_s[...])
        hid = (a / (1.0 + jnp.exp(-a)) * u).astype(BF16)
        y = _dot(hid, wd_s[...])
        _store_token_rows(ys_ref, 0, MOE_BLOCK, _pack_rows(y))

    @pl.when(b >= nblk_ref[0])
    def _():
        ys_ref[...] = jnp.zeros_like(ys_ref)


def _experts(xs, blk_e, nblk_used, w_gate, w_up, w_down, layer):
    rows = xs.shape[0]
    nblk = rows // (MOE_BLOCK * TOK_ROWS)
    D = w_gate.shape[2]
    blk = lambda b, be, nb: (jnp.minimum(b, nb[0] - 1), 0)
    out_blk = lambda b, be, nb: (b, 0)
    wsel = lambda b, be, nb: (layer, be[b], 0, 0)
    return pl.pallas_call(
        _expert_kernel,
        grid_spec=pltpu.PrefetchScalarGridSpec(
            num_scalar_prefetch=2,
            grid=(nblk,),
            in_specs=[pl.BlockSpec((MOE_BLOCK * TOK_ROWS, LANES), blk),
                      pl.BlockSpec((1, 1, D, D_EXPERT), wsel),
                      pl.BlockSpec((1, 1, D, D_EXPERT), wsel),
                      pl.BlockSpec((1, 1, D_EXPERT, D), wsel)],
            out_specs=pl.BlockSpec((MOE_BLOCK * TOK_ROWS, LANES), out_blk),
            scratch_shapes=[pltpu.VMEM((D, D_EXPERT), BF16), pltpu.VMEM((D, D_EXPERT), BF16),
                            pltpu.VMEM((D_EXPERT, D), BF16)],
        ),
        out_shape=jax.ShapeDtypeStruct((rows, LANES), U32),
        compiler_params=_cparams(("arbitrary",)),
        name="moe_experts",
    )(blk_e, nblk_used, xs, w_gate, w_up, w_down)


def _combine_kernel(a_ref, n_ref, ls_ref, gb_ref, code_ref, x_ref, info_ref, ys_ref, o_ref,
                    ybuf_ref, pick0_ref, pick1_ref, dvec_ref, drow_ref, sem, dsem):
    s = pl.program_id(0)
    ts = x_ref.shape[0]
    slot = s % 2
    first_tok = slot * (2 * ts)

    def fetch(op, step, step_slot):
        _run_copies(op, step, False, n_ref, ls_ref, gb_ref, ybuf_ref, step_slot * (2 * ts), ys_ref,
                    sem.at[step_slot], ts)

    @pl.when(s == 0)
    def _():
        fetch(_dma_start, 0, 0)

    @pl.when(s + 1 < pl.num_programs(0))
    def _():
        fetch(_dma_start, s + 1, 1 - slot)

    fetch(_dma_wait, s, slot)
    _local_rows(code_ref, a_ref, s * N_EXPERTS, first_tok, dvec_ref, drow_ref, dsem)

    picks = (pick0_ref, pick1_ref)
    for sub in range(ts // MOE_TILE):
        def pick(tb, c, sub=sub):
            row, col = _entry_pos(sub * (MOE_TILE // DMA_UNROLL) + tb)
            for u in range(DMA_UNROLL):
                for kk in range(2):
                    picks[kk][_tok_rows(tb * DMA_UNROLL + u), :] = (
                        ybuf_ref[_rows_at(drow_ref[row, col + u * 2 + kk]), :])
            return c
        lax.fori_loop(0, MOE_TILE // DMA_UNROLL, pick, 0)

        rows = pl.ds(sub * MOE_TILE, MOE_TILE)
        info = info_ref[rows, :]
        g0 = info[:, 0:1]
        g1 = info[:, 1:2]
        half = x_ref.shape[1] // 2
        for c in range(TOK_ROWS):
            y0 = _unpack_words(pick0_ref[pl.ds(c, MOE_TILE, stride=TOK_ROWS), :])
            y1 = _unpack_words(pick1_ref[pl.ds(c, MOE_TILE, stride=TOK_ROWS), :])
            for part in range(2):
                sl = slice(part * half + c * LANES, part * half + (c + 1) * LANES)
                o_ref[rows, sl] = x_ref[rows, sl] + (y0[part] * g0 + y1[part] * g1)


def _combine(x2, info, ys, codes, tables, ts):
    T, D = x2.shape
    return pl.pallas_call(
        _combine_kernel,
        grid_spec=pltpu.PrefetchScalarGridSpec(
            num_scalar_prefetch=4,
            grid=(T // ts,),
            in_specs=[pl.BlockSpec((ts * 2 // LANES, LANES), lambda i, *_: (i, 0)),
                      pl.BlockSpec((ts, D), lambda i, *_: (i, 0)),
                      pl.BlockSpec((ts, LANES), lambda i, *_: (i, 0)),
                      pl.BlockSpec(memory_space=pl.ANY)],
            out_specs=pl.BlockSpec((ts, D), lambda i, *_: (i, 0)),
            scratch_shapes=[pltpu.VMEM((2 * 2 * ts * TOK_ROWS, LANES), U32),
                            pltpu.VMEM((MOE_TILE * TOK_ROWS, LANES), U32),
                            pltpu.VMEM((MOE_TILE * TOK_ROWS, LANES), U32),
                            pltpu.VMEM((ts * 2 // LANES, LANES), jnp.int32),
                            pltpu.SMEM((ts * 2 // LANES, LANES), jnp.int32),
                            pltpu.SemaphoreType.DMA((2,)), pltpu.SemaphoreType.DMA(())],
        ),
        out_shape=jax.ShapeDtypeStruct((T, D), F32),
        compiler_params=_cparams(("arbitrary",)),
        name="moe_combine",
    )(*tables, codes, x2, info, ys)


def _supertile_tables(tile_base, counts, pad_start, ts):
    per = ts // ROW_TILE
    base = tile_base[::per]
    nxt = jnp.concatenate([base[1:], counts[None, :]], axis=0)
    n = nxt - base
    lstart = jnp.cumsum(n, axis=1) - n
    flat = lambda a: a.reshape(-1).astype(jnp.int32)
    return flat(lstart - base), flat(n), flat(lstart), flat(pad_start[None, :] + base)


DISPATCH_TOKENS = 2048
COMBINE_TOKENS = 1024


def _moe_apply(x2, ln2, info, counts, tile_base, w_gate, w_up, w_down, layer):
    T, D = x2.shape
    padded = (counts + MOE_BLOCK - 1) // MOE_BLOCK * MOE_BLOCK
    pad_end = jnp.cumsum(padded)
    pad_start = pad_end - padded
    codes = info[:, 2:4].astype(jnp.int32).reshape(T * 2 // LANES, LANES)
    n_rows = T * 2 + N_EXPERTS * MOE_BLOCK
    nblk = n_rows // MOE_BLOCK
    blk_row = jnp.arange(nblk, dtype=jnp.int32) * MOE_BLOCK
    blk_e = jnp.minimum(jnp.sum((pad_end[None, :] <= blk_row[:, None]).astype(jnp.int32), axis=1),
                        N_EXPERTS - 1).astype(jnp.int32)
    nblk_used = (pad_end[-1:] // MOE_BLOCK).astype(jnp.int32)
    zrow = (pad_start + counts).astype(jnp.int32)
    zcnt = (padded - counts).astype(jnp.int32)

    td = min(DISPATCH_TOKENS, T)
    tc = min(COMBINE_TOKENS, T)

    xs = _dispatch(x2, ln2, codes, _supertile_tables(tile_base, counts, pad_start, td),
                   zrow, zcnt, nblk_used, n_rows, td)
    ys = _experts(xs, blk_e, nblk_used, w_gate, w_up, w_down, layer)
    return _combine(x2, info, ys, codes, _supertile_tables(tile_base, counts, pad_start, tc), tc)


HEAD_PAD = LANES
LATENT_PAD = 3 * LANES


def _rot_partner():
    half = QK_ROPE // 2
    r = jnp.arange(QK_ROPE)
    return jnp.where(r < half, r + half, r - half), jnp.where(r < half, -1.0, 1.0).astype(F32)


def _mla_rope_rows():
    half = QK_ROPE // 2
    inv = ROPE_THETA ** (-jnp.arange(half, dtype=F32) / half)
    lane = jnp.arange(LANES)
    r = lane - QK_NOPE
    in_rope = (r >= 0) & (r < QK_ROPE)
    inv_row = jnp.where(in_rope, inv[jnp.clip(r, 0, QK_ROPE - 1) % half], 0.0)[None, :]
    rope_row = in_rope.astype(F32)[None, :]
    real_row = (lane < QK_HEAD).astype(F32)[None, :]
    return inv_row, rope_row, real_row


def _head_gain_row(g):
    partner, _ = _rot_partner()
    return jnp.concatenate([g, g[QK_NOPE + partner]])[None, :]


def _with_partner_cols(w3):
    partner, sign = _rot_partner()
    rot = w3[:, :, QK_NOPE + partner] * sign
    return jnp.concatenate([w3, rot], axis=-1).reshape(w3.shape[0], N_MAIN_HEADS * HEAD_PAD)


def _heads_norm_rope(ys, gain_rows, scales, real_row, cos_real, sin_rope):
    n = range(len(ys))
    row_id = lax.broadcasted_iota(jnp.int32, (LANES, LANES), 0)
    ones_real = jnp.where(row_id < QK_HEAD, 1.0, 0.0).astype(BF16)
    ms = [_dot((ys[i] * ys[i]).astype(BF16), ones_real) * (1.0 / QK_HEAD) for i in n]
    yn = [ys[i] * (lax.rsqrt(ms[i] + EPS) * scales[i]) * gain_rows[i] for i in n]
    rolled = [pltpu.roll(yn[i], LANES - QK_ROPE, 1) for i in n]
    return [(yn[i] * cos_real + rolled[i] * sin_rope).astype(BF16) for i in n]


def _mla_qkv_kernel(x_ref, pos_ref, lnkv_ref, lnq_ref, wd_ref, scale_ref, wkv_ref, kg_ref,
                    win_ref, qlg_ref, wuq_ref, qg_ref, inv_ref, rope_ref, real_ref,
                    k_ref, v_ref, q_ref, qm_ref):
    x = x_ref[...]
    xr = x * lax.rsqrt(jnp.mean(x * x, axis=-1, keepdims=True) + EPS)
    ang = pos_ref[...].astype(F32) * inv_ref[...]
    real_row = real_ref[...]
    cos_real = jnp.cos(ang) * real_row
    sin_rope = jnp.sin(ang) * rope_ref[...]

    ckr = _dot((xr * lnkv_ref[...]).astype(BF16), wd_ref[...])
    c = ckr[:, :KV_LORA]
    r = lax.rsqrt(jnp.mean(c * c, axis=-1, keepdims=True) + EPS)
    lane = lax.broadcasted_iota(jnp.int32, (1, LATENT_PAD), 1)
    lhs = (ckr * jnp.where(lane < KV_LORA, r * scale_ref[...], 1.0)).astype(BF16)
    kv = _dot(lhs, wkv_ref[...])
    proj = _dot((xr * lnq_ref[...]).astype(BF16), win_ref[...])
    cq = _rmsnorm(proj[:, :Q_LORA], qlg_ref[...]).astype(BF16)
    q = _dot(cq, wuq_ref[...])
    qm_ref[...] = proj[:, Q_LORA:].astype(BF16)
    v_ref[...] = kv[:, N_MAIN_HEADS * HEAD_PAD:].astype(BF16)
    sls = [slice(hh * HEAD_PAD, (hh + 1) * HEAD_PAD) for hh in range(N_MAIN_HEADS)]
    nh = N_MAIN_HEADS
    outs = _heads_norm_rope([kv[:, sl] for sl in sls] + [q[:, sl] for sl in sls],
                            [kg_ref[...]] * nh + [qg_ref[...]] * nh,
                            [1.0] * nh + [QK_HEAD ** -0.5] * nh, real_row, cos_real, sin_rope)
    for hh, sl in enumerate(sls):
        k_ref[:, sl] = outs[hh]
        q_ref[:, sl] = outs[nh + hh]


def _mla_qkv(x2, pos_col, kv_ln, w_dkv, kv_lora_g, w_ukv, k_g, ln1, w_in, q_lora_g, w_uq, q_g):
    T, D = x2.shape
    tm = min(ROW_TILE, T)
    lat = LATENT_PAD
    wd = jnp.pad(w_dkv, ((0, 0), (0, lat - w_dkv.shape[1]))).astype(BF16)
    scale_row = jnp.pad(kv_lora_g, (0, lat - KV_LORA), constant_values=1.0)[None, :]
    w3 = w_ukv.reshape(KV_LORA, N_MAIN_HEADS, QK_NOPE + V_HEAD)
    wk = jnp.zeros((lat, N_MAIN_HEADS, QK_HEAD), F32)
    wk = wk.at[:KV_LORA, :, :QK_NOPE].set(w3[:, :, :QK_NOPE])
    eye = jnp.eye(QK_ROPE, dtype=F32)
    wk = wk.at[KV_LORA:KV_LORA + QK_ROPE, :, QK_NOPE:].set(
        jnp.broadcast_to(eye[:, None, :], (QK_ROPE, N_MAIN_HEADS, QK_ROPE)))
    wv = jnp.zeros((lat, N_MAIN_HEADS * V_HEAD), F32)
    wv = wv.at[:KV_LORA].set(w3[:, :, QK_NOPE:].reshape(KV_LORA, N_MAIN_HEADS * V_HEAD))
    wkv = jnp.concatenate([_with_partner_cols(wk), wv], axis=1).astype(BF16)
    wuq = _with_partner_cols(w_uq.reshape(Q_LORA, N_MAIN_HEADS, QK_HEAD)).astype(BF16)
    inv_row, rope_row, real_row = _mla_rope_rows()
    row = lambda i: (i, 0)
    fixed = lambda i: (0, 0)
    kw = N_MAIN_HEADS * HEAD_PAD
    n_in = w_in.shape[1]
    lane_row = pl.BlockSpec((1, LANES), fixed)
    return pl.pallas_call(
        _mla_qkv_kernel,
        grid=(T // tm,),
        in_specs=[pl.BlockSpec((tm, D), row), pl.BlockSpec((tm, 1), row),
                  pl.BlockSpec((1, D), fixed), pl.BlockSpec((1, D), fixed),
                  pl.BlockSpec((D, lat), fixed), pl.BlockSpec((1, lat), fixed),
                  pl.BlockSpec((lat, kw + MAIN_WIDTH), fixed), lane_row,
                  pl.BlockSpec((D, n_in), fixed), pl.BlockSpec((1, Q_LORA), fixed),
                  pl.BlockSpec((Q_LORA, kw), fixed), lane_row,
                  lane_row, lane_row, lane_row],
        out_specs=[pl.BlockSpec((tm, kw), row), pl.BlockSpec((tm, MAIN_WIDTH), row),
                   pl.BlockSpec((tm, kw), row), pl.BlockSpec((tm, MEM_WIDTH), row)],
        out_shape=[jax.ShapeDtypeStruct((T, kw), BF16), jax.ShapeDtypeStruct((T, MAIN_WIDTH), BF16),
                   jax.ShapeDtypeStruct((T, kw), BF16), jax.ShapeDtypeStruct((T, MEM_WIDTH), BF16)],
        compiler_params=_cparams(("parallel",)),
        name="mla_qkv",
    )(x2, pos_col, kv_ln[None, :], ln1[None, :], wd, scale_row, wkv, _head_gain_row(k_g),
      w_in.astype(BF16), q_lora_g[None, :], wuq, _head_gain_row(q_g), inv_row, rope_row, real_row)


ATTN_TILE = 1024


def _flash_kernel(qi_ref, kj_ref, q_ref, k_ref, v_ref, o_ref, m_ref, acc_ref):
    t = pl.program_id(2)
    i = qi_ref[t]
    j = kj_ref[t]
    tq = q_ref.shape[1]
    tk = k_ref.shape[1]

    @pl.when(j == 0)
    def _():
        m_ref[...] = jnp.full_like(m_ref, NEG_BIG)
        acc_ref[...] = jnp.zeros_like(acc_ref)

    lane = lax.broadcasted_iota(jnp.int32, (1, LANES), 1)
    head_a = lane < V_HEAD
    den_lane = (V_HEAD, 0)

    def attend(q0, nq, nk, diag_col):
        rows = pl.ds(q0, nq)
        v = v_ref[0, pl.ds(0, nk), :]
        if diag_col is not None:
            q_idx = lax.broadcasted_iota(jnp.int32, (nq, nk), 0) + diag_col
            k_idx = lax.broadcasted_iota(jnp.int32, (nq, nk), 1)
            visible = k_idx <= q_idx
        H = range(2)
        sl = [slice(hh * HEAD_PAD, (hh + 1) * HEAD_PAD) for hh in H]
        s = [_dot_nt(q_ref[0, rows, sl[hh]], k_ref[0, pl.ds(0, nk), sl[hh]]) for hh in H]
        if diag_col is not None:
            s = [jnp.where(visible, s[hh], NEG_BIG) for hh in H]
        m_prev = [m_ref[hh, rows, :] for hh in H]
        acc_prev = [acc_ref[hh, rows, :] for hh in H]
        m_new = [jnp.maximum(m_prev[hh], jnp.max(s[hh], axis=-1, keepdims=True)) for hh in H]
        alpha = [jnp.exp(m_prev[hh] - m_new[hh]) for hh in H]
        m_wide = [jnp.concatenate([m_new[hh]] * (nk // LANES), axis=1) for hh in H]
        p = [jnp.exp(s[hh] - m_wide[hh]).astype(BF16) for hh in H]
        keep_row = [jnp.where(head_a, 1.0, 0.0).astype(BF16), jnp.where(head_a, 0.0, 1.0).astype(BF16)]
        den_row = [jnp.where(lane == den_lane[hh], 1.0, 0.0).astype(BF16) for hh in H]
        pv = [_dot(p[hh], v * keep_row[hh] + den_row[hh]) for hh in H]
        for hh in H:
            acc_ref[hh, rows, :] = alpha[hh] * acc_prev[hh] + pv[hh]
            m_ref[hh, rows, :] = m_new[hh]

    @pl.when(j < i)
    def _():
        attend(0, tq, tk, None)

    @pl.when(j == i)
    def _():
        half = tq // 2
        attend(0, half, half, 0)
        attend(half, half, tk, half)
        acc_a = acc_ref[0]
        acc_b = acc_ref[1]
        out_a = acc_a / acc_a[:, den_lane[0]:den_lane[0] + 1]
        out_b = acc_b / acc_b[:, den_lane[1]:den_lane[1] + 1]
        o_ref[0] = jnp.where(head_a, out_a, out_b).astype(BF16)


def _flash(q, k, v):
    B, S, _ = q.shape
    t = min(ATTN_TILE, S)
    n = S // t
    pairs = [(i, j) for i in range(n) for j in range(i + 1)]
    qi = jnp.array([p[0] for p in pairs], jnp.int32)
    kj = jnp.array([p[1] for p in pairs], jnp.int32)
    return pl.pallas_call(
        _flash_kernel,
        grid_spec=pltpu.PrefetchScalarGridSpec(
            num_scalar_prefetch=2,
            grid=(B, N_PAIRS, len(pairs)),
            in_specs=[pl.BlockSpec((1, t, 2 * HEAD_PAD), lambda b, p, s, qi, kj: (b, qi[s], p)),
                      pl.BlockSpec((1, t, 2 * HEAD_PAD), lambda b, p, s, qi, kj: (b, kj[s], p)),
                      pl.BlockSpec((1, t, LANES), lambda b, p, s, qi, kj: (b, kj[s], p))],
            out_specs=pl.BlockSpec((1, t, LANES), lambda b, p, s, qi, kj: (b, qi[s], p)),
            scratch_shapes=[pltpu.VMEM((2, t, LANES), F32), pltpu.VMEM((2, t, LANES), F32)],
        ),
        out_shape=jax.ShapeDtypeStruct((B, S, MAIN_WIDTH), BF16),
        compiler_params=_cparams(("parallel", "parallel", "arbitrary")),
        name="mla_flash",
    )(qi, kj, q, k, v)


def kernel(x, mem, positions, ln1, ln2, w_out, mem_w_kv, mem_q_norm, mem_k_norm, router_group_w, router_group_b, router_expert_w, router_expert_b, expert_w_gate, expert_w_up, expert_w_down, ret_w_in, ret_gn, kv_ln, kv_w_down, kv_lora_norm, kv_w_up, k_norm, mla_w_in, q_lora_norm, mla_w_uq, q_norm):
    B, S, D = x.shape
    M = mem.shape[1]
    T = B * S
    x2 = x.reshape(T, D)
    mem2 = mem.reshape(B * M, D)
    pos_col = positions.reshape(T, 1).astype(jnp.int32)

    def mem_path(i, qm):
        k_m, v_m = _mem_kv(mem2, mem_w_kv[i], mem_k_norm[i])
        return _mem_attn(qm.reshape(B, S, MEM_WIDTH), k_m.reshape(B, M, MEM_WIDTH),
                         v_m.reshape(B, M, MEM_WIDTH), mem_q_norm[i]).reshape(T, MEM_WIDTH)

    def mix_out_and_moe(i, xin, y, m):
        x1, info, counts, tile_base = _out_proj_router(
            xin, y, m, w_out[i], ln2[i], router_group_w[i], router_group_b[i], router_expert_w[i],
            router_expert_b[i])
        return _moe_apply(x1, ln2[i], info, counts, tile_base, expert_w_gate, expert_w_up, expert_w_down, i)

    q, k, v, gt, qm = _ret_inproj(x2, pos_col, ln1[0], ret_w_in[0])
    shp = (B, S, MAIN_WIDTH)
    y = _retention(q.reshape(shp), k.reshape(shp), v.reshape(shp), gt.reshape(shp), ret_gn[0])
    x2 = mix_out_and_moe(0, x2, y.reshape(T, MAIN_WIDTH), mem_path(0, qm))

    k_sh, v_sh, q1, qm1 = _mla_qkv(x2, pos_col, kv_ln, kv_w_down, kv_lora_norm, kv_w_up, k_norm,
                                   ln1[1], mla_w_in[0], q_lora_norm[0], mla_w_uq[0], q_norm[0])
    kw = N_MAIN_HEADS * HEAD_PAD
    y1 = _flash(q1.reshape(B, S, kw), k_sh.reshape(B, S, kw), v_sh.reshape(shp))
    x2 = mix_out_and_moe(1, x2, y1.reshape(T, MAIN_WIDTH), mem_path(1, qm1))
    return x2.reshape(B, S, D)
```

```python
import math

import jax
import jax.numpy as jnp
from jax import lax
from jax.experimental import pallas as pl
from jax.experimental.pallas import tpu as pltpu

F32 = jnp.float32
BF16 = jnp.bfloat16

HEAD_DIM = 64
N_MAIN_HEADS = 12
MAIN_WIDTH = N_MAIN_HEADS * HEAD_DIM
N_MEM_HEADS = 4
MEM_WIDTH = N_MEM_HEADS * HEAD_DIM
RET_CHUNK = 128
ROPE_THETA = 10000.0
Q_LORA = 384
KV_LORA = 256
QK_NOPE = 64
QK_ROPE = 32
QK_HEAD = QK_NOPE + QK_ROPE
V_HEAD = 64
N_GROUPS = 4
EXPERTS_PER_GROUP = 8
N_EXPERTS = N_GROUPS * EXPERTS_PER_GROUP
D_EXPERT = 256
MOE_BLOCK = 512
EPS = 1e-6

LANES = 128
VMEM_LIMIT = 48 * 1024 * 1024
NEG_BIG = -1e30

N_PAIRS = N_MAIN_HEADS // 2
ROW_TILE = 512
MOE_TILE = 256
DMA_UNROLL = 8


def _cparams(sem):
    return pltpu.CompilerParams(dimension_semantics=sem, vmem_limit_bytes=VMEM_LIMIT)


def _dot(a, b):
    return jnp.dot(a, b, preferred_element_type=F32)


def _dot_nt(a, b):
    return lax.dot_general(a, b, (((1,), (1,)), ((), ())), preferred_element_type=F32)


def _dot_tn(a, b):
    return lax.dot_general(a, b, (((0,), (0,)), ((), ())), preferred_element_type=F32)


def _rmsnorm(xf, g):
    return xf * lax.rsqrt(jnp.mean(xf * xf, axis=-1, keepdims=True) + EPS) * g


def _rope_tables(pos_col, inv_row, sgn_row):
    ang = pos_col.astype(F32) * inv_row
    return jnp.cos(ang), jnp.sin(ang) * sgn_row


def _ret_inproj_kernel(x_ref, pos_ref, g_ref, inv_ref, sgn_ref, w_ref,
                       q_ref, k_ref, v_ref, gt_ref, qm_ref):
    h = _rmsnorm(x_ref[...], g_ref[...]).astype(BF16)
    cos, sin_s = _rope_tables(pos_ref[...], inv_ref[...], sgn_ref[...])
    first = sgn_ref[...] < 0.0
    mw = MAIN_WIDTH
    half = HEAD_DIM // 2
    qk = _dot(h, w_ref[:, :2 * mw])
    groups = range(2 * mw // LANES)
    ys = [qk[:, j * LANES:(j + 1) * LANES] for j in groups]
    fwd = [pltpu.roll(ys[j], LANES - half, 1) for j in groups]
    bwd = [pltpu.roll(ys[j], half, 1) for j in groups]
    outs = [(ys[j] * cos + jnp.where(first, fwd[j], bwd[j]) * sin_s).astype(BF16) for j in groups]
    n_q = mw // LANES
    for j in range(n_q):
        q_ref[:, j * LANES:(j + 1) * LANES] = outs[j]
        k_ref[:, j * LANES:(j + 1) * LANES] = outs[n_q + j]
    v_ref[...] = _dot(h, w_ref[:, 2 * mw:3 * mw]).astype(BF16)
    gt_ref[...] = _dot(h, w_ref[:, 3 * mw:4 * mw]).astype(BF16)
    qm_ref[...] = _dot(h, w_ref[:, 4 * mw:4 * mw + MEM_WIDTH]).astype(BF16)


def _ret_inproj(x2, pos_col, g, w_in):
    T, D = x2.shape
    tm = min(ROW_TILE, T)
    half = HEAD_DIM // 2
    inv = ROPE_THETA ** (-jnp.arange(half, dtype=F32) / half)
    lane = jnp.arange(LANES)
    inv_row = inv[lane % half][None, :]
    sgn_row = jnp.where((lane % HEAD_DIM) < half, -1.0, 1.0).astype(F32)[None, :]
    row = lambda i: (i, 0)
    fixed = lambda i: (0, 0)
    n_in = w_in.shape[1]
    outs = pl.pallas_call(
        _ret_inproj_kernel,
        grid=(T // tm,),
        in_specs=[
            pl.BlockSpec((tm, D), row),
            pl.BlockSpec((tm, 1), row),
            pl.BlockSpec((1, D), fixed),
            pl.BlockSpec((1, LANES), fixed),
            pl.BlockSpec((1, LANES), fixed),
            pl.BlockSpec((D, n_in), fixed),
        ],
        out_specs=[pl.BlockSpec((tm, MAIN_WIDTH), row)] * 4 + [pl.BlockSpec((tm, MEM_WIDTH), row)],
        out_shape=[jax.ShapeDtypeStruct((T, MAIN_WIDTH), BF16)] * 4
        + [jax.ShapeDtypeStruct((T, MEM_WIDTH), BF16)],
        compiler_params=_cparams(("parallel",)),
        name="ret_inproj",
    )(x2, pos_col, g[None, :], inv_row, sgn_row, w_in.astype(BF16))
    return outs


def _retention_tables():
    H, C, d = N_MAIN_HEADS, RET_CHUNK, HEAD_DIM
    log_g = jnp.log1p(-jnp.exp2(-5.0 - jnp.arange(H, dtype=F32)))
    idx = jnp.arange(C, dtype=F32)
    rel = idx[:, None] - idx[None, :]
    scale = d ** -0.5
    decay_in = jnp.where(rel[None] >= 0,
                         jnp.exp(log_g[:, None, None] * jnp.maximum(rel, 0.0)[None]), 0.0) * scale
    kdec = jnp.exp(log_g[None, :] * (C - 1.0 - idx)[:, None]) * scale
    qdec = jnp.exp(log_g[None, :] * (idx + 1.0)[:, None])
    cdec = jnp.exp(log_g * C)

    def lanes(t):
        return jnp.repeat(t, d, axis=1).reshape(C, N_PAIRS, 2 * d).transpose(1, 0, 2)

    head_of_lane = jnp.arange(2 * d) // d
    same = (head_of_lane[:, None] == head_of_lane[None, :]).astype(F32)
    cd_lane = jnp.repeat(cdec, d).reshape(N_PAIRS, 2 * d)
    state_decay = cd_lane[:, :, None] * same[None]
    decay_in = decay_in.reshape(N_PAIRS, 2, C, C)
    return decay_in, lanes(kdec), lanes(qdec), state_decay, same


RET_CHUNKS_PER_STEP = 4


def _retention_kernel(q_ref, k_ref, v_ref, gt_ref, dm_ref, kd_ref, qd_ref, sd_ref, same_ref,
                      gn_ref, o_ref, r_ref):
    n = pl.program_id(1)

    @pl.when(n == 0)
    def _():
        r_ref[...] = jnp.zeros_like(r_ref)

    lane = lax.broadcasted_iota(jnp.int32, (1, LANES), 1)
    head_a = lane < HEAD_DIM
    keep_a = jnp.where(head_a, 1.0, 0.0).astype(BF16)
    keep_b = jnp.where(head_a, 0.0, 1.0).astype(BF16)
    same = same_ref[...]
    avg = (same * (1.0 / HEAD_DIM)).astype(BF16)
    C = RET_CHUNK
    n_chunks = q_ref.shape[1] // C
    P = range(N_PAIRS)
    CP = [(c, p) for c in range(n_chunks) for p in P]
    rows = [pl.ds(c * C, C) for c in range(n_chunks)]
    sl = [slice(p * LANES, (p + 1) * LANES) for p in P]
    q = {cp: q_ref[0, rows[cp[0]], sl[cp[1]]] for cp in CP}
    k = {cp: k_ref[0, rows[cp[0]], sl[cp[1]]] for cp in CP}
    v = {cp: v_ref[0, rows[cp[0]], sl[cp[1]]] for cp in CP}
    s_a = {cp: _dot_nt(q[cp] * keep_a, k[cp]) * dm_ref[cp[1], 0] for cp in CP}
    s_b = {cp: _dot_nt(q[cp] * keep_b, k[cp]) * dm_ref[cp[1], 1] for cp in CP}
    u = {cp: _dot_tn((k[cp].astype(F32) * kd_ref[cp[1]]).astype(BF16), v[cp]) for cp in CP}
    qd = {cp: (q[cp].astype(F32) * qd_ref[cp[1]]).astype(BF16) for cp in CP}
    inner = {cp: _dot(s_a[cp].astype(BF16), v[cp] * keep_a) + _dot(s_b[cp].astype(BF16), v[cp] * keep_b)
             for cp in CP}
    state = [r_ref[p] for p in P]
    y = {}
    for c in range(n_chunks):
        for p in P:
            y[(c, p)] = inner[(c, p)] + _dot(qd[(c, p)], state[p].astype(BF16))
        state = [sd_ref[p] * state[p] + same * u[(c, p)] for p in P]
    for p in P:
        r_ref[p] = state[p]
    yc = {cp: y[cp] - _dot(y[cp].astype(BF16), avg) for cp in CP}
    var = {cp: _dot((yc[cp] * yc[cp]).astype(BF16), avg) for cp in CP}
    for c, p in CP:
        yn = yc[(c, p)] * lax.rsqrt(var[(c, p)] + EPS) * gn_ref[:, sl[p]]
        g = gt_ref[0, rows[c], sl[p]].astype(F32)
        o_ref[0, rows[c], sl[p]] = (g / (1.0 + jnp.exp(-g)) * yn).astype(BF16)


def _retention(q, k, v, gt, ret_gn):
    B, S, W = q.shape
    C = RET_CHUNK
    dm, kd, qd, sd, same = _retention_tables()
    tok = lambda b, n: (b, n, 0)
    fixed3 = lambda b, n: (0, 0, 0)
    step = RET_CHUNKS_PER_STEP * C
    return pl.pallas_call(
        _retention_kernel,
        grid=(B, S // step),
        in_specs=[pl.BlockSpec((1, step, W), tok)] * 4 + [
            pl.BlockSpec((N_PAIRS, 2, C, C), lambda b, n: (0, 0, 0, 0)),
            pl.BlockSpec((N_PAIRS, C, LANES), fixed3),
            pl.BlockSpec((N_PAIRS, C, LANES), fixed3),
            pl.BlockSpec((N_PAIRS, LANES, LANES), fixed3),
            pl.BlockSpec((LANES, LANES), lambda b, n: (0, 0)),
            pl.BlockSpec((1, W), lambda b, n: (0, 0)),
        ],
        out_specs=pl.BlockSpec((1, step, W), tok),
        out_shape=jax.ShapeDtypeStruct((B, S, W), BF16),
        scratch_shapes=[pltpu.VMEM((N_PAIRS, LANES, LANES), F32)],
        compiler_params=_cparams(("parallel", "arbitrary")),
        name="retention",
    )(q, k, v, gt, dm, kd, qd, sd, same, ret_gn[None, :])


def _mem_kv_kernel(mem_ref, w_ref, kg_ref, k_ref, v_ref):
    kv = _dot(mem_ref[...].astype(BF16), w_ref[...])
    lane = lax.broadcasted_iota(jnp.int32, (1, LANES), 1)
    head_a = lane < HEAD_DIM
    inv_d = 1.0 / HEAD_DIM
    for j in range(MEM_WIDTH // LANES):
        kj = kv[:, j * LANES:(j + 1) * LANES]
        k2 = kj * kj
        ms_a = jnp.sum(jnp.where(head_a, k2, 0.0), axis=-1, keepdims=True) * inv_d
        ms_b = jnp.sum(jnp.where(head_a, 0.0, k2), axis=-1, keepdims=True) * inv_d
        kn = kj * lax.rsqrt(jnp.where(head_a, ms_a, ms_b) + EPS) * kg_ref[...]
        k_ref[:, j * LANES:(j + 1) * LANES] = kn.astype(BF16)
    v_ref[...] = kv[:, MEM_WIDTH:].astype(BF16)


def _mem_kv(mem2, w_mem_kv, k_g):
    TM, D = mem2.shape
    tm = min(ROW_TILE, TM)
    kg_row = jnp.tile(k_g, LANES // HEAD_DIM)[None, :]
    row = lambda i: (i, 0)
    fixed = lambda i: (0, 0)
    return pl.pallas_call(
        _mem_kv_kernel,
        grid=(TM // tm,),
        in_specs=[pl.BlockSpec((tm, D), row), pl.BlockSpec((D, 2 * MEM_WIDTH), fixed),
                  pl.BlockSpec((1, LANES), fixed)],
        out_specs=[pl.BlockSpec((tm, MEM_WIDTH), row)] * 2,
        out_shape=[jax.ShapeDtypeStruct((TM, MEM_WIDTH), BF16)] * 2,
        compiler_params=_cparams(("parallel",)),
        name="mem_kv",
    )(mem2, w_mem_kv.astype(BF16), kg_row)


def _mem_attn_kernel(qm_ref, k_ref, v_ref, qg_ref, o_ref):
    lane = lax.broadcasted_iota(jnp.int32, (1, LANES), 1)
    head_a = lane < HEAD_DIM
    inv_d = 1.0 / HEAD_DIM
    scale = HEAD_DIM ** -0.5
    for j in range(MEM_WIDTH // LANES):
        sl = slice(j * LANES, (j + 1) * LANES)
        qj = qm_ref[0, :, sl].astype(F32)
        q2 = qj * qj
        ms_a = jnp.sum(jnp.where(head_a, q2, 0.0), axis=-1, keepdims=True) * inv_d
        ms_b = jnp.sum(jnp.where(head_a, 0.0, q2), axis=-1, keepdims=True) * inv_d
        qn = qj * lax.rsqrt(jnp.where(head_a, ms_a, ms_b) + EPS) * (qg_ref[...] * scale)
        kj = k_ref[0, :, sl]
        vj = v_ref[0, :, sl]
        out = None
        for sel in (head_a, jnp.logical_not(head_a)):
            s = _dot_nt(jnp.where(sel, qn, 0.0).astype(BF16), kj)
            p = jnp.exp(s - jnp.max(s, axis=-1, keepdims=True))
            p = p / jnp.sum(p, axis=-1, keepdims=True)
            o = _dot(p.astype(BF16), jnp.where(sel, vj, jnp.zeros_like(vj)))
            out = o if out is None else out + o
        o_ref[0, :, sl] = out.astype(BF16)


def _mem_attn(qm, k_m, v_m, q_g):
    B, S, _ = qm.shape
    M = k_m.shape[1]
    tm = min(ROW_TILE, S)
    qg_row = jnp.tile(q_g, LANES // HEAD_DIM)[None, :]
    return pl.pallas_call(
        _mem_attn_kernel,
        grid=(B, S // tm),
        in_specs=[pl.BlockSpec((1, tm, MEM_WIDTH), lambda b, i: (b, i, 0)),
                  pl.BlockSpec((1, M, MEM_WIDTH), lambda b, i: (b, 0, 0)),
                  pl.BlockSpec((1, M, MEM_WIDTH), lambda b, i: (b, 0, 0)),
                  pl.BlockSpec((1, LANES), lambda b, i: (0, 0))],
        out_specs=pl.BlockSpec((1, tm, MEM_WIDTH), lambda b, i: (b, i, 0)),
        out_shape=jax.ShapeDtypeStruct((B, S, MEM_WIDTH), BF16),
        compiler_params=_cparams(("parallel", "parallel")),
        name="mem_attn",
    )(qm, k_m, v_m, qg_row)


ROUTER_LANE0 = N_GROUPS
RANK_BITS = 17
RANK_RADIX = 1 << RANK_BITS


def _proj_router_kernel(x_ref, y_ref, m_ref, wy_ref, wm_ref, g_ref, whi_ref, wlo_ref, b_ref,
                        o_ref, info_ref, cnt_ref, tile_base_ref, base_ref):
    x1 = x_ref[...] + _dot(y_ref[...], wy_ref[...]) + _dot(m_ref[...], wm_ref[...])
    o_ref[...] = x1
    _route_tile(x1, g_ref, whi_ref, wlo_ref, b_ref, info_ref, cnt_ref, tile_base_ref, base_ref)


def _route_tile(x, g_ref, whi_ref, wlo_ref, b_ref, info_ref, cnt_ref, tile_base_ref, base_ref):
    i = pl.program_id(0)

    @pl.when(i == 0)
    def _():
        base_ref[...] = jnp.zeros_like(base_ref)

    tm = x.shape[0]
    h = _rmsnorm(x, g_ref[...])
    h_hi = h.astype(BF16)
    h_lo = (h - h_hi.astype(F32)).astype(BF16)
    logits = (_dot(h_hi, whi_ref[...]) + _dot(h_lo, whi_ref[...]) + _dot(h_hi, wlo_ref[...])
              + b_ref[...])
    lane_i = lax.broadcasted_iota(jnp.int32, (tm, LANES), 1)
    lane = lane_i.astype(F32)
    big = float(LANES)

    is_g = lane_i < N_GROUPS
    lg = jnp.where(is_g, logits, NEG_BIG)
    mg = jnp.max(lg, axis=-1, keepdims=True)
    zg = jnp.sum(jnp.where(is_g, jnp.exp(lg - mg), 0.0), axis=-1, keepdims=True)
    p_grp = 1.0 / zg
    grp = jnp.min(jnp.where(is_g & (lg == mg), lane, big), axis=-1, keepdims=True)

    e_lane = lane_i - ROUTER_LANE0
    e_grp = (e_lane >> int(math.log2(EXPERTS_PER_GROUP))).astype(F32)
    is_e = (e_lane >= 0) & (e_lane < N_EXPERTS) & (e_grp == grp)
    le = jnp.where(is_e, logits, NEG_BIG)
    me = jnp.max(le, axis=-1, keepdims=True)
    ee = jnp.where(is_e, jnp.exp(le - me), 0.0)
    prob = ee / jnp.sum(ee, axis=-1, keepdims=True)
    p1 = jnp.max(prob, axis=-1, keepdims=True)
    i1 = jnp.min(jnp.where(is_e & (prob == p1), lane, big), axis=-1, keepdims=True)
    rest = is_e & (lane != i1)
    p2 = jnp.max(jnp.where(rest, prob, -1.0), axis=-1, keepdims=True)
    i2 = jnp.min(jnp.where(rest & (prob == p2), lane, big), axis=-1, keepdims=True)
    gate1 = p_grp * p1 / (p1 + p2)
    gate2 = p_grp * p2 / (p1 + p2)

    sel1 = lane == i1
    sel2 = lane == i2
    onehot = jnp.where(sel1 | sel2, 1.0, 0.0)
    r_io = lax.broadcasted_iota(jnp.int32, (tm, tm), 0)
    c_io = lax.broadcasted_iota(jnp.int32, (tm, tm), 1)
    lower = jnp.where(r_io > c_io, 1.0, 0.0).astype(BF16)
    tile_base_ref[0] = base_ref[...]
    before = _dot(lower, onehot.astype(BF16)) + base_ref[...]
    rank1 = jnp.sum(jnp.where(sel1, before, 0.0), axis=-1, keepdims=True)
    rank2 = jnp.sum(jnp.where(sel2, before, 0.0), axis=-1, keepdims=True)
    base_ref[...] += jnp.sum(onehot, axis=0, keepdims=True)
    cnt_ref[...] = base_ref[...]

    code1 = (i1 - float(ROUTER_LANE0)) * float(RANK_RADIX) + rank1
    code2 = (i2 - float(ROUTER_LANE0)) * float(RANK_RADIX) + rank2
    info = jnp.zeros((tm, LANES), F32)
    for col, val in enumerate((gate1, gate2, code1, code2)):
        info = jnp.where(lane_i == col, val, info)
    info_ref[...] = info


def _router_operands(D, g, w_grp, b_grp, w_exp, b_exp):
    w = jnp.zeros((D, LANES), F32)
    w = w.at[:, :N_GROUPS].set(w_grp).at[:, ROUTER_LANE0:ROUTER_LANE0 + N_EXPERTS].set(w_exp)
    b = jnp.zeros((1, LANES), F32)
    b = b.at[0, :N_GROUPS].set(b_grp).at[0, ROUTER_LANE0:ROUTER_LANE0 + N_EXPERTS].set(b_exp)
    w_hi = w.astype(BF16)
    w_lo = (w - w_hi.astype(F32)).astype(BF16)
    fixed = lambda i: (0, 0)
    specs = [pl.BlockSpec((1, D), fixed), pl.BlockSpec((D, LANES), fixed), pl.BlockSpec((D, LANES), fixed),
             pl.BlockSpec((1, LANES), fixed)]
    return (g[None, :], w_hi, w_lo, b), specs


def _router_outputs(T, tm):
    n_tiles = T // tm
    specs = [pl.BlockSpec((tm, LANES), lambda i: (i, 0)), pl.BlockSpec((1, LANES), lambda i: (0, 0)),
             pl.BlockSpec((1, 1, LANES), lambda i: (i, 0, 0))]
    shapes = [jax.ShapeDtypeStruct((T, LANES), F32), jax.ShapeDtypeStruct((1, LANES), F32),
              jax.ShapeDtypeStruct((n_tiles, 1, LANES), F32)]
    return specs, shapes


def _router_tables(cnt, tile_base):
    experts = slice(ROUTER_LANE0, ROUTER_LANE0 + N_EXPERTS)
    return cnt[0, experts].astype(jnp.int32), tile_base[:, 0, experts].astype(jnp.int32)


def _out_proj_router(x2, y2, m2, w_out, g, w_grp, b_grp, w_exp, b_exp):
    T, D = x2.shape
    tm = min(ROW_TILE, T)
    row = lambda i: (i, 0)
    fixed = lambda i: (0, 0)
    w = w_out.astype(BF16)
    ops, op_specs = _router_operands(D, g, w_grp, b_grp, w_exp, b_exp)
    out_specs, out_shapes = _router_outputs(T, tm)
    x1, info, cnt, tile_base = pl.pallas_call(
        _proj_router_kernel,
        grid=(T // tm,),
        in_specs=[pl.BlockSpec((tm, D), row), pl.BlockSpec((tm, MAIN_WIDTH), row),
                  pl.BlockSpec((tm, MEM_WIDTH), row),
                  pl.BlockSpec((MAIN_WIDTH, D), fixed), pl.BlockSpec((MEM_WIDTH, D), fixed)] + op_specs,
        out_specs=[pl.BlockSpec((tm, D), row)] + out_specs,
        out_shape=[jax.ShapeDtypeStruct((T, D), F32)] + out_shapes,
        scratch_shapes=[pltpu.VMEM((1, LANES), F32)],
        compiler_params=_cparams(("arbitrary",)),
        name="out_proj_router",
    )(x2, y2, m2, w[:MAIN_WIDTH], w[MAIN_WIDTH:], *ops)
    return (x1, info) + _router_tables(cnt, tile_base)


TOK_ROWS = 4
U32 = jnp.uint32


def _tok_rows(r, n=1):
    start = r * TOK_ROWS
    if not isinstance(start, int):
        start = pl.multiple_of(start, TOK_ROWS)
    return pl.ds(start, n * TOK_ROWS)


def _pack_rows(h):
    bits = lax.bitcast_convert_type(h.astype(BF16).astype(F32), U32)
    half = h.shape[1] // 2
    return (bits[:, :half] >> 16) | bits[:, half:]


def _unpack_words(w):
    return (lax.bitcast_convert_type(w << 16, F32), lax.bitcast_convert_type(w & U32(0xFFFF0000), F32))


def _store_token_rows(ref, first_row, n, words):
    for c in range(TOK_ROWS):
        ref[pl.ds(first_row * TOK_ROWS + c, n, stride=TOK_ROWS), :] = words[:, c * LANES:(c + 1) * LANES]


def _load_token_rows(ref, n):
    parts = [_unpack_words(ref[pl.ds(c, n, stride=TOK_ROWS), :]) for c in range(TOK_ROWS)]
    return jnp.concatenate([p[0] for p in parts] + [p[1] for p in parts], axis=-1)


def _segment_copies(n, src, src_row, dst, dst_row, sem, top, op):
    off = 0
    bit = top
    while bit >= 1:
        take = n & bit

        @pl.when(take != 0)
        def _(bit=bit, off=off):
            op(pltpu.make_async_copy(src.at[_tok_rows(src_row + off, bit), :],
                                     dst.at[_tok_rows(dst_row + off, bit), :], sem))
        off = off + take
        bit //= 2


ENTRIES_PER_ITER = 2 * DMA_UNROLL


def _local_rows(code_ref, a_ref, tbl, first_tok, dvec_ref, drow_ref, dsem):
    code = code_ref[...]
    e = code >> RANK_BITS
    d = (code & (RANK_RADIX - 1)) + first_tok
    for k in range(N_EXPERTS):
        d = d + jnp.where(e == k, a_ref[tbl + k], 0)
    dvec_ref[...] = d * TOK_ROWS
    copies = [pltpu.make_async_copy(dvec_ref.at[r], drow_ref.at[pl.ds(r * LANES, LANES)], dsem)
              for r in range(dvec_ref.shape[0])]
    for cp in copies:
        cp.start()
    for cp in copies:
        cp.wait()


def _run_copies(op, step, to_sorted, n_ref, ls_ref, gb_ref, local_ref, first_tok, sorted_ref, sem, top):
    tbl = step * N_EXPERTS

    def run(e, c):
        loc = (local_ref, first_tok + ls_ref[tbl + e])
        glob = (sorted_ref, gb_ref[tbl + e])
        (src, src_row), (dst, dst_row) = (loc, glob) if to_sorted else (glob, loc)
        _segment_copies(n_ref[tbl + e], src, src_row, dst, dst_row, sem, top, op)
        return c
    lax.fori_loop(0, N_EXPERTS, run, 0)


def _rows_at(first_row):
    return pl.ds(pl.multiple_of(first_row, TOK_ROWS), TOK_ROWS)


def _dma_start(cp):
    cp.start()


def _dma_wait(cp):
    cp.wait()


def _block_copy(src, dst, blk, sem):
    return pltpu.make_async_copy(src, dst.at[_tok_rows(blk * MOE_BLOCK, MOE_BLOCK), :], sem)


def _dispatch_kernel(a_ref, n_ref, ls_ref, gb_ref, zrow_ref, zcnt_ref, nblk_ref,
                     code_ref, x_ref, g_ref, xs_ref, hbuf_ref, cbuf_ref, zbuf_ref, dvec_ref, drow_ref,
                     sem, zsem, dsem):
    s = pl.program_id(0)
    ts = x_ref.shape[0]

    @pl.when(s == 0)
    def _():
        zbuf_ref[...] = jnp.zeros_like(zbuf_ref)
        n_blocks = xs_ref.shape[0] // (MOE_BLOCK * TOK_ROWS)
        for op in (_dma_start, _dma_wait):
            def tail(b, c, op=op):
                op(_block_copy(zbuf_ref, xs_ref, b, zsem))
                return c
            lax.fori_loop(nblk_ref[0], n_blocks, tail, 0)

            def pad(e, c, op=op):
                _segment_copies(zcnt_ref[e], zbuf_ref, 0, xs_ref, zrow_ref[e], zsem, MOE_BLOCK // 2, op)
                return c
            lax.fori_loop(0, N_EXPERTS, pad, 0)

    for sub in range(ts // MOE_TILE):
        h = _rmsnorm(x_ref[pl.ds(sub * MOE_TILE, MOE_TILE), :], g_ref[...])
        _store_token_rows(hbuf_ref, sub * MOE_TILE, MOE_TILE, _pack_rows(h))

    slot = s % 2
    first_tok = slot * (2 * ts)
    last = pl.num_programs(0) - 1

    def runs(op, step, step_slot):
        _run_copies(op, step, True, n_ref, ls_ref, gb_ref, cbuf_ref, step_slot * (2 * ts), xs_ref,
                    sem.at[step_slot], ts)

    @pl.when(s >= 2)
    def _():
        runs(_dma_wait, s - 2, slot)

    _local_rows(code_ref, a_ref, s * N_EXPERTS, first_tok, dvec_ref, drow_ref, dsem)

    def place(tb, c):
        first = tb * ENTRIES_PER_ITER
        for u in range(DMA_UNROLL):
            tile = hbuf_ref[_tok_rows(tb * DMA_UNROLL + u), :]
            for kk in range(2):
                cbuf_ref[_rows_at(drow_ref[first + u * 2 + kk]), :] = tile
        return c
    lax.fori_loop(0, ts // DMA_UNROLL, place, 0)

    runs(_dma_start, s, slot)

    @pl.when(s == last)
    def _():
        @pl.when(s >= 1)
        def _():
            runs(_dma_wait, s - 1, 1 - slot)
        runs(_dma_wait, s, slot)


def _dispatch(x2, g, codes, tables, zrow, zcnt, nblk_used, n_rows, ts):
    T, D = x2.shape
    return pl.pallas_call(
        _dispatch_kernel,
        grid_spec=pltpu.PrefetchScalarGridSpec(
            num_scalar_prefetch=7,
            grid=(T // ts,),
            in_specs=[pl.BlockSpec((ts * 2 // LANES, LANES), lambda i, *_: (i, 0)),
                      pl.BlockSpec((ts, D), lambda i, *_: (i, 0)),
                      pl.BlockSpec((1, D), lambda i, *_: (0, 0))],
            out_specs=pl.BlockSpec(memory_space=pl.ANY),
            scratch_shapes=[pltpu.VMEM((ts * TOK_ROWS, LANES), U32),
                            pltpu.VMEM((2 * 2 * ts * TOK_ROWS, LANES), U32),
                            pltpu.VMEM((MOE_BLOCK * TOK_ROWS, LANES), U32),
                            pltpu.VMEM((ts * 2 // LANES, LANES), jnp.int32),
                            pltpu.SMEM((ts * 2,), jnp.int32),
                            pltpu.SemaphoreType.DMA((2,)), pltpu.SemaphoreType.DMA(()),
                            pltpu.SemaphoreType.DMA(())],
        ),
        out_shape=jax.ShapeDtypeStruct((n_rows * TOK_ROWS, LANES), U32),
        compiler_params=_cparams(("arbitrary",)),
        name="moe_dispatch",
    )(*tables, zrow, zcnt, nblk_used, codes, x2, g[None, :])


def _expert_kernel(blk_e_ref, nblk_ref, xs_ref, wg_ref, wu_ref, wd_ref, ys_ref, wg_s, wu_s, wd_s):
    b = pl.program_id(0)

    @pl.when((b == 0) | (blk_e_ref[b] != blk_e_ref[jnp.maximum(b - 1, 0)]))
    def _():
        wg_s[...] = wg_ref[0, 0].astype(BF16)
        wu_s[...] = wu_ref[0, 0].astype(BF16)
        wd_s[...] = wd_ref[0, 0].astype(BF16)

    @pl.when(b < nblk_ref[0])
    def _():
        x = _load_token_rows(xs_ref, MOE_BLOCK).astype(BF16)
        a = _dot(x, wg_s[...])
        u = _dot(x, wu_s[...])
        hid = (a / (1.0 + jnp.exp(-a)) * u).astype(BF16)
        y = _dot(hid, wd_s[...])
        _store_token_rows(ys_ref, 0, MOE_BLOCK, _pack_rows(y))

    @pl.when(b >= nblk_ref[0])
    def _():
        ys_ref[...] = jnp.zeros_like(ys_ref)


def _experts(xs, blk_e, nblk_used, w_gate, w_up, w_down, layer):
    rows = xs.shape[0]
    nblk = rows // (MOE_BLOCK * TOK_ROWS)
    D = w_gate.shape[2]
    blk = lambda b, be, nb: (jnp.minimum(b, nb[0] - 1), 0)
    out_blk = lambda b, be, nb: (b, 0)
    wsel = lambda b, be, nb: (layer, be[b], 0, 0)
    return pl.pallas_call(
        _expert_kernel,
        grid_spec=pltpu.PrefetchScalarGridSpec(
            num_scalar_prefetch=2,
            grid=(nblk,),
            in_specs=[pl.BlockSpec((MOE_BLOCK * TOK_ROWS, LANES), blk),
                      pl.BlockSpec((1, 1, D, D_EXPERT), wsel),
                      pl.BlockSpec((1, 1, D, D_EXPERT), wsel),
                      pl.BlockSpec((1, 1, D_EXPERT, D), wsel)],
            out_specs=pl.BlockSpec((MOE_BLOCK * TOK_ROWS, LANES), out_blk),
            scratch_shapes=[pltpu.VMEM((D, D_EXPERT), BF16), pltpu.VMEM((D, D_EXPERT), BF16),
                            pltpu.VMEM((D_EXPERT, D), BF16)],
        ),
        out_shape=jax.ShapeDtypeStruct((rows, LANES), U32),
        compiler_params=_cparams(("arbitrary",)),
        name="moe_experts",
    )(blk_e, nblk_used, xs, w_gate, w_up, w_down)


def _combine_kernel(a_ref, n_ref, ls_ref, gb_ref, code_ref, x_ref, info_ref, ys_ref, o_ref,
                    ybuf_ref, pick0_ref, pick1_ref, dvec_ref, drow_ref, sem, dsem):
    s = pl.program_id(0)
    ts = x_ref.shape[0]
    slot = s % 2
    first_tok = slot * (2 * ts)

    def fetch(op, step, step_slot):
        _run_copies(op, step, False, n_ref, ls_ref, gb_ref, ybuf_ref, step_slot * (2 * ts), ys_ref,
                    sem.at[step_slot], ts)

    @pl.when(s == 0)
    def _():
        fetch(_dma_start, 0, 0)

    @pl.when(s + 1 < pl.num_programs(0))
    def _():
        fetch(_dma_start, s + 1, 1 - slot)

    fetch(_dma_wait, s, slot)
    _local_rows(code_ref, a_ref, s * N_EXPERTS, first_tok, dvec_ref, drow_ref, dsem)

    picks = (pick0_ref, pick1_ref)
    for sub in range(ts // MOE_TILE):
        def pick(tb, c, sub=sub):
            first = (sub * (MOE_TILE // DMA_UNROLL) + tb) * ENTRIES_PER_ITER
            for u in range(DMA_UNROLL):
                for kk in range(2):
                    picks[kk][_tok_rows(tb * DMA_UNROLL + u), :] = (
                        ybuf_ref[_rows_at(drow_ref[first + u * 2 + kk]), :])
            return c
        lax.fori_loop(0, MOE_TILE // DMA_UNROLL, pick, 0)

        rows = pl.ds(sub * MOE_TILE, MOE_TILE)
        info = info_ref[rows, :]
        g0 = info[:, 0:1]
        g1 = info[:, 1:2]
        half = x_ref.shape[1] // 2
        for c in range(TOK_ROWS):
            y0 = _unpack_words(pick0_ref[pl.ds(c, MOE_TILE, stride=TOK_ROWS), :])
            y1 = _unpack_words(pick1_ref[pl.ds(c, MOE_TILE, stride=TOK_ROWS), :])
            for part in range(2):
                sl = slice(part * half + c * LANES, part * half + (c + 1) * LANES)
                o_ref[rows, sl] = x_ref[rows, sl] + (y0[part] * g0 + y1[part] * g1)


def _combine(x2, info, ys, codes, tables, ts):
    T, D = x2.shape
    return pl.pallas_call(
        _combine_kernel,
        grid_spec=pltpu.PrefetchScalarGridSpec(
            num_scalar_prefetch=4,
            grid=(T // ts,),
            in_specs=[pl.BlockSpec((ts * 2 // LANES, LANES), lambda i, *_: (i, 0)),
                      pl.BlockSpec((ts, D), lambda i, *_: (i, 0)),
                      pl.BlockSpec((ts, LANES), lambda i, *_: (i, 0)),
                      pl.BlockSpec(memory_space=pl.ANY)],
            out_specs=pl.BlockSpec((ts, D), lambda i, *_: (i, 0)),
            scratch_shapes=[pltpu.VMEM((2 * 2 * ts * TOK_ROWS, LANES), U32),
                            pltpu.VMEM((MOE_TILE * TOK_ROWS, LANES), U32),
                            pltpu.VMEM((MOE_TILE * TOK_ROWS, LANES), U32),
                            pltpu.VMEM((ts * 2 // LANES, LANES), jnp.int32),
                            pltpu.SMEM((ts * 2,), jnp.int32),
                            pltpu.SemaphoreType.DMA((2,)), pltpu.SemaphoreType.DMA(())],
        ),
        out_shape=jax.ShapeDtypeStruct((T, D), F32),
        compiler_params=_cparams(("arbitrary",)),
        name="moe_combine",
    )(*tables, codes, x2, info, ys)


def _supertile_tables(tile_base, counts, pad_start, ts):
    per = ts // ROW_TILE
    base = tile_base[::per]
    nxt = jnp.concatenate([base[1:], counts[None, :]], axis=0)
    n = nxt - base
    lstart = jnp.cumsum(n, axis=1) - n
    flat = lambda a: a.reshape(-1).astype(jnp.int32)
    return flat(lstart - base), flat(n), flat(lstart), flat(pad_start[None, :] + base)


DISPATCH_TOKENS = 2048
COMBINE_TOKENS = 1024


def _moe_apply(x2, ln2, info, counts, tile_base, w_gate, w_up, w_down, layer):
    T, D = x2.shape
    padded = (counts + MOE_BLOCK - 1) // MOE_BLOCK * MOE_BLOCK
    pad_end = jnp.cumsum(padded)
    pad_start = pad_end - padded
    codes = info[:, 2:4].astype(jnp.int32).reshape(T * 2 // LANES, LANES)
    n_rows = T * 2 + N_EXPERTS * MOE_BLOCK
    nblk = n_rows // MOE_BLOCK
    blk_row = jnp.arange(nblk, dtype=jnp.int32) * MOE_BLOCK
    blk_e = jnp.minimum(jnp.sum((pad_end[None, :] <= blk_row[:, None]).astype(jnp.int32), axis=1),
                        N_EXPERTS - 1).astype(jnp.int32)
    nblk_used = (pad_end[-1:] // MOE_BLOCK).astype(jnp.int32)
    zrow = (pad_start + counts).astype(jnp.int32)
    zcnt = (padded - counts).astype(jnp.int32)

    td = min(DISPATCH_TOKENS, T)
    tc = min(COMBINE_TOKENS, T)

    xs = _dispatch(x2, ln2, codes, _supertile_tables(tile_base, counts, pad_start, td),
                   zrow, zcnt, nblk_used, n_rows, td)
    ys = _experts(xs, blk_e, nblk_used, w_gate, w_up, w_down, layer)
    return _combine(x2, info, ys, codes, _supertile_tables(tile_base, counts, pad_start, tc), tc)


HEAD_PAD = LANES
LATENT_PAD = 3 * LANES


def _rot_partner():
    half = QK_ROPE // 2
    r = jnp.arange(QK_ROPE)
    return jnp.where(r < half, r + half, r - half), jnp.where(r < half, -1.0, 1.0).astype(F32)


def _mla_rope_rows():
    half = QK_ROPE // 2
    inv = ROPE_THETA ** (-jnp.arange(half, dtype=F32) / half)
    lane = jnp.arange(LANES)
    r = lane - QK_NOPE
    in_rope = (r >= 0) & (r < QK_ROPE)
    inv_row = jnp.where(in_rope, inv[jnp.clip(r, 0, QK_ROPE - 1) % half], 0.0)[None, :]
    rope_row = in_rope.astype(F32)[None, :]
    real_row = (lane < QK_HEAD).astype(F32)[None, :]
    return inv_row, rope_row, real_row


def _head_gain_row(g):
    partner, _ = _rot_partner()
    return jnp.concatenate([g, g[QK_NOPE + partner]])[None, :]


def _with_partner_cols(w3):
    partner, sign = _rot_partner()
    rot = w3[:, :, QK_NOPE + partner] * sign
    return jnp.concatenate([w3, rot], axis=-1).reshape(w3.shape[0], N_MAIN_HEADS * HEAD_PAD)


def _heads_norm_rope(ys, gain_rows, scales, real_row, cos_real, sin_rope):
    n = range(len(ys))
    row_id = lax.broadcasted_iota(jnp.int32, (LANES, LANES), 0)
    ones_real = jnp.where(row_id < QK_HEAD, 1.0, 0.0).astype(BF16)
    ms = [_dot((ys[i] * ys[i]).astype(BF16), ones_real) * (1.0 / QK_HEAD) for i in n]
    yn = [ys[i] * (lax.rsqrt(ms[i] + EPS) * scales[i]) * gain_rows[i] for i in n]
    rolled = [pltpu.roll(yn[i], LANES - QK_ROPE, 1) for i in n]
    return [(yn[i] * cos_real + rolled[i] * sin_rope).astype(BF16) for i in n]


def _mla_qkv_kernel(x_ref, pos_ref, lnkv_ref, lnq_ref, wd_ref, scale_ref, wkv_ref, kg_ref,
                    win_ref, qlg_ref, wuq_ref, qg_ref, inv_ref, rope_ref, real_ref,
                    k_ref, v_ref, q_ref, qm_ref):
    x = x_ref[...]
    xr = x * lax.rsqrt(jnp.mean(x * x, axis=-1, keepdims=True) + EPS)
    ang = pos_ref[...].astype(F32) * inv_ref[...]
    real_row = real_ref[...]
    cos_real = jnp.cos(ang) * real_row
    sin_rope = jnp.sin(ang) * rope_ref[...]

    ckr = _dot((xr * lnkv_ref[...]).astype(BF16), wd_ref[...])
    c = ckr[:, :KV_LORA]
    r = lax.rsqrt(jnp.mean(c * c, axis=-1, keepdims=True) + EPS)
    lane = lax.broadcasted_iota(jnp.int32, (1, LATENT_PAD), 1)
    lhs = (ckr * jnp.where(lane < KV_LORA, r * scale_ref[...], 1.0)).astype(BF16)
    kv = _dot(lhs, wkv_ref[...])
    proj = _dot((xr * lnq_ref[...]).astype(BF16), win_ref[...])
    cq = _rmsnorm(proj[:, :Q_LORA], qlg_ref[...]).astype(BF16)
    q = _dot(cq, wuq_ref[...])
    qm_ref[...] = proj[:, Q_LORA:].astype(BF16)
    v_ref[...] = kv[:, N_MAIN_HEADS * HEAD_PAD:].astype(BF16)
    sls = [slice(hh * HEAD_PAD, (hh + 1) * HEAD_PAD) for hh in range(N_MAIN_HEADS)]
    nh = N_MAIN_HEADS
    outs = _heads_norm_rope([kv[:, sl] for sl in sls] + [q[:, sl] for sl in sls],
                            [kg_ref[...]] * nh + [qg_ref[...]] * nh,
                            [1.0] * nh + [QK_HEAD ** -0.5] * nh, real_row, cos_real, sin_rope)
    for hh, sl in enumerate(sls):
        k_ref[:, sl] = outs[hh]
        q_ref[:, sl] = outs[nh + hh]


def _mla_qkv(x2, pos_col, kv_ln, w_dkv, kv_lora_g, w_ukv, k_g, ln1, w_in, q_lora_g, w_uq, q_g):
    T, D = x2.shape
    tm = min(ROW_TILE, T)
    lat = LATENT_PAD
    wd = jnp.pad(w_dkv, ((0, 0), (0, lat - w_dkv.shape[1]))).astype(BF16)
    scale_row = jnp.pad(kv_lora_g, (0, lat - KV_LORA), constant_values=1.0)[None, :]
    w3 = w_ukv.reshape(KV_LORA, N_MAIN_HEADS, QK_NOPE + V_HEAD)
    wk = jnp.zeros((lat, N_MAIN_HEADS, QK_HEAD), F32)
    wk = wk.at[:KV_LORA, :, :QK_NOPE].set(w3[:, :, :QK_NOPE])
    eye = jnp.eye(QK_ROPE, dtype=F32)
    wk = wk.at[KV_LORA:KV_LORA + QK_ROPE, :, QK_NOPE:].set(
        jnp.broadcast_to(eye[:, None, :], (QK_ROPE, N_MAIN_HEADS, QK_ROPE)))
    wv = jnp.zeros((lat, N_MAIN_HEADS * V_HEAD), F32)
    wv = wv.at[:KV_LORA].set(w3[:, :, QK_NOPE:].reshape(KV_LORA, N_MAIN_HEADS * V_HEAD))
    wkv = jnp.concatenate([_with_partner_cols(wk), wv], axis=1).astype(BF16)
    wuq = _with_partner_cols(w_uq.reshape(Q_LORA, N_MAIN_HEADS, QK_HEAD)).astype(BF16)
    inv_row, rope_row, real_row = _mla_rope_rows()
    row = lambda i: (i, 0)
    fixed = lambda i: (0, 0)
    kw = N_MAIN_HEADS * HEAD_PAD
    n_in = w_in.shape[1]
    lane_row = pl.BlockSpec((1, LANES), fixed)
    return pl.pallas_call(
        _mla_qkv_kernel,
        grid=(T // tm,),
        in_specs=[pl.BlockSpec((tm, D), row), pl.BlockSpec((tm, 1), row),
                  pl.BlockSpec((1, D), fixed), pl.BlockSpec((1, D), fixed),
                  pl.BlockSpec((D, lat), fixed), pl.BlockSpec((1, lat), fixed),
                  pl.BlockSpec((lat, kw + MAIN_WIDTH), fixed), lane_row,
                  pl.BlockSpec((D, n_in), fixed), pl.BlockSpec((1, Q_LORA), fixed),
                  pl.BlockSpec((Q_LORA, kw), fixed), lane_row,
                  lane_row, lane_row, lane_row],
        out_specs=[pl.BlockSpec((tm, kw), row), pl.BlockSpec((tm, MAIN_WIDTH), row),
                   pl.BlockSpec((tm, kw), row), pl.BlockSpec((tm, MEM_WIDTH), row)],
        out_shape=[jax.ShapeDtypeStruct((T, kw), BF16), jax.ShapeDtypeStruct((T, MAIN_WIDTH), BF16),
                   jax.ShapeDtypeStruct((T, kw), BF16), jax.ShapeDtypeStruct((T, MEM_WIDTH), BF16)],
        compiler_params=_cparams(("parallel",)),
        name="mla_qkv",
    )(x2, pos_col, kv_ln[None, :], ln1[None, :], wd, scale_row, wkv, _head_gain_row(k_g),
      w_in.astype(BF16), q_lora_g[None, :], wuq, _head_gain_row(q_g), inv_row, rope_row, real_row)


ATTN_TILE = 1024


def _flash_kernel(qi_ref, kj_ref, q_ref, k_ref, v_ref, o_ref, m_ref, acc_ref):
    t = pl.program_id(2)
    i = qi_ref[t]
    j = kj_ref[t]
    tq = q_ref.shape[1]
    tk = k_ref.shape[1]

    @pl.when(j == 0)
    def _():
        m_ref[...] = jnp.full_like(m_ref, NEG_BIG)
        acc_ref[...] = jnp.zeros_like(acc_ref)

    lane = lax.broadcasted_iota(jnp.int32, (1, LANES), 1)
    head_a = lane < V_HEAD
    den_lane = (V_HEAD, 0)

    def attend(q0, nq, nk, diag_col):
        rows = pl.ds(q0, nq)
        v = v_ref[0, pl.ds(0, nk), :]
        if diag_col is not None:
            q_idx = lax.broadcasted_iota(jnp.int32, (nq, nk), 0) + diag_col
            k_idx = lax.broadcasted_iota(jnp.int32, (nq, nk), 1)
            visible = k_idx <= q_idx
        H = range(2)
        sl = [slice(hh * HEAD_PAD, (hh + 1) * HEAD_PAD) for hh in H]
        s = [_dot_nt(q_ref[0, rows, sl[hh]], k_ref[0, pl.ds(0, nk), sl[hh]]) for hh in H]
        if diag_col is not None:
            s = [jnp.where(visible, s[hh], NEG_BIG) for hh in H]
        m_prev = [m_ref[hh, rows, :] for hh in H]
        acc_prev = [acc_ref[hh, rows, :] for hh in H]
        m_new = [jnp.maximum(m_prev[hh], jnp.max(s[hh], axis=-1, keepdims=True)) for hh in H]
        alpha = [jnp.exp(m_prev[hh] - m_new[hh]) for hh in H]
        m_wide = [jnp.concatenate([m_new[hh]] * (nk // LANES), axis=1) for hh in H]
        p = [jnp.exp(s[hh] - m_wide[hh]).astype(BF16) for hh in H]
        keep_row = [jnp.where(head_a, 1.0, 0.0).astype(BF16), jnp.where(head_a, 0.0, 1.0).astype(BF16)]
        den_row = [jnp.where(lane == den_lane[hh], 1.0, 0.0).astype(BF16) for hh in H]
        pv = [_dot(p[hh], v * keep_row[hh] + den_row[hh]) for hh in H]
        for hh in H:
            acc_ref[hh, rows, :] = alpha[hh] * acc_prev[hh] + pv[hh]
            m_ref[hh, rows, :] = m_new[hh]

    @pl.when(j < i)
    def _():
        attend(0, tq, tk, None)

    @pl.when(j == i)
    def _():
        half = tq // 2
        attend(0, half, half, 0)
        attend(half, half, tk, half)
        acc_a = acc_ref[0]
        acc_b = acc_ref[1]
        out_a = acc_a / acc_a[:, den_lane[0]:den_lane[0] + 1]
        out_b = acc_b / acc_b[:, den_lane[1]:den_lane[1] + 1]
        o_ref[0] = jnp.where(head_a, out_a, out_b).astype(BF16)


def _flash(q, k, v):
    B, S, _ = q.shape
    t = min(ATTN_TILE, S)
    n = S // t
    pairs = [(i, j) for i in range(n) for j in range(i + 1)]
    qi = jnp.array([p[0] for p in pairs], jnp.int32)
    kj = jnp.array([p[1] for p in pairs], jnp.int32)
    return pl.pallas_call(
        _flash_kernel,
        grid_spec=pltpu.PrefetchScalarGridSpec(
            num_scalar_prefetch=2,
            grid=(B, N_PAIRS, len(pairs)),
            in_specs=[pl.BlockSpec((1, t, 2 * HEAD_PAD), lambda b, p, s, qi, kj: (b, qi[s], p)),
                      pl.BlockSpec((1, t, 2 * HEAD_PAD), lambda b, p, s, qi, kj: (b, kj[s], p)),
                      pl.BlockSpec((1, t, LANES), lambda b, p, s, qi, kj: (b, kj[s], p))],
            out_specs=pl.BlockSpec((1, t, LANES), lambda b, p, s, qi, kj: (b, qi[s], p)),
            scratch_shapes=[pltpu.VMEM((2, t, LANES), F32), pltpu.VMEM((2, t, LANES), F32)],
        ),
        out_shape=jax.ShapeDtypeStruct((B, S, MAIN_WIDTH), BF16),
        compiler_params=_cparams(("parallel", "parallel", "arbitrary")),
        name="mla_flash",
    )(qi, kj, q, k, v)


def kernel(x, mem, positions, ln1, ln2, w_out, mem_w_kv, mem_q_norm, mem_k_norm, router_group_w, router_group_b, router_expert_w, router_expert_b, expert_w_gate, expert_w_up, expert_w_down, ret_w_in, ret_gn, kv_ln, kv_w_down, kv_lora_norm, kv_w_up, k_norm, mla_w_in, q_lora_norm, mla_w_uq, q_norm):
    B, S, D = x.shape
    M = mem.shape[1]
    T = B * S
    x2 = x.reshape(T, D)
    mem2 = mem.reshape(B * M, D)
    pos_col = positions.reshape(T, 1).astype(jnp.int32)

    def mem_path(i, qm):
        k_m, v_m = _mem_kv(mem2, mem_w_kv[i], mem_k_norm[i])
        return _mem_attn(qm.reshape(B, S, MEM_WIDTH), k_m.reshape(B, M, MEM_WIDTH),
                         v_m.reshape(B, M, MEM_WIDTH), mem_q_norm[i]).reshape(T, MEM_WIDTH)

    def mix_out_and_moe(i, xin, y, m):
        x1, info, counts, tile_base = _out_proj_router(
            xin, y, m, w_out[i], ln2[i], router_group_w[i], router_group_b[i], router_expert_w[i],
            router_expert_b[i])
        return _moe_apply(x1, ln2[i], info, counts, tile_base, expert_w_gate, expert_w_up, expert_w_down, i)

    q, k, v, gt, qm = _ret_inproj(x2, pos_col, ln1[0], ret_w_in[0])
    shp = (B, S, MAIN_WIDTH)
    y = _retention(q.reshape(shp), k.reshape(shp), v.reshape(shp), gt.reshape(shp), ret_gn[0])
    x2 = mix_out_and_moe(0, x2, y.reshape(T, MAIN_WIDTH), mem_path(0, qm))

    k_sh, v_sh, q1, qm1 = _mla_qkv(x2, pos_col, kv_ln, kv_w_down, kv_lora_norm, kv_w_up, k_norm,
                                   ln1[1], mla_w_in[0], q_lora_norm[0], mla_w_uq[0], q_norm[0])
    kw = N_MAIN_HEADS * HEAD_PAD
    y1 = _flash(q1.reshape(B, S, kw), k_sh.reshape(B, S, kw), v_sh.reshape(shp))
    x2 = mix_out_and_moe(1, x2, y1.reshape(T, MAIN_WIDTH), mem_path(1, qm1))
    return x2.reshape(B, S, D)
```

```python
import math

import jax
import jax.numpy as jnp
from jax import lax
from jax.experimental import pallas as pl
from jax.experimental.pallas import tpu as pltpu

F32 = jnp.float32
BF16 = jnp.bfloat16

HEAD_DIM = 64
N_MAIN_HEADS = 12
MAIN_WIDTH = N_MAIN_HEADS * HEAD_DIM
N_MEM_HEADS = 4
MEM_WIDTH = N_MEM_HEADS * HEAD_DIM
RET_CHUNK = 128
ROPE_THETA = 10000.0
Q_LORA = 384
KV_LORA = 256
QK_NOPE = 64
QK_ROPE = 32
QK_HEAD = QK_NOPE + QK_ROPE
V_HEAD = 64
N_GROUPS = 4
EXPERTS_PER_GROUP = 8
N_EXPERTS = N_GROUPS * EXPERTS_PER_GROUP
D_EXPERT = 256
MOE_BLOCK = 512
EPS = 1e-6

LANES = 128
VMEM_LIMIT = 48 * 1024 * 1024
NEG_BIG = -1e30

N_PAIRS = N_MAIN_HEADS // 2
ROW_TILE = 512
MOE_TILE = 256
DMA_UNROLL = 8


def _cparams(sem):
    return pltpu.CompilerParams(dimension_semantics=sem, vmem_limit_bytes=VMEM_LIMIT)


def _dot(a, b):
    return jnp.dot(a, b, preferred_element_type=F32)


def _dot_nt(a, b):
    return lax.dot_general(a, b, (((1,), (1,)), ((), ())), preferred_element_type=F32)


def _dot_tn(a, b):
    return lax.dot_general(a, b, (((0,), (0,)), ((), ())), preferred_element_type=F32)


def _rmsnorm(xf, g):
    return xf * lax.rsqrt(jnp.mean(xf * xf, axis=-1, keepdims=True) + EPS) * g


def _rope_tables(pos_col, inv_row, sgn_row):
    ang = pos_col.astype(F32) * inv_row
    return jnp.cos(ang), jnp.sin(ang) * sgn_row


def _ret_inproj_kernel(x_ref, pos_ref, g_ref, inv_ref, sgn_ref, w_ref,
                       q_ref, k_ref, v_ref, gt_ref, qm_ref):
    h = _rmsnorm(x_ref[...], g_ref[...]).astype(BF16)
    cos, sin_s = _rope_tables(pos_ref[...], inv_ref[...], sgn_ref[...])
    first = sgn_ref[...] < 0.0
    mw = MAIN_WIDTH
    half = HEAD_DIM // 2
    qk = _dot(h, w_ref[:, :2 * mw])
    groups = range(2 * mw // LANES)
    ys = [qk[:, j * LANES:(j + 1) * LANES] for j in groups]
    fwd = [pltpu.roll(ys[j], LANES - half, 1) for j in groups]
    bwd = [pltpu.roll(ys[j], half, 1) for j in groups]
    outs = [(ys[j] * cos + jnp.where(first, fwd[j], bwd[j]) * sin_s).astype(BF16) for j in groups]
    n_q = mw // LANES
    for j in range(n_q):
        q_ref[:, j * LANES:(j + 1) * LANES] = outs[j]
        k_ref[:, j * LANES:(j + 1) * LANES] = outs[n_q + j]
    v_ref[...] = _dot(h, w_ref[:, 2 * mw:3 * mw]).astype(BF16)
    gt_ref[...] = _dot(h, w_ref[:, 3 * mw:4 * mw]).astype(BF16)
    qm_ref[...] = _dot(h, w_ref[:, 4 * mw:4 * mw + MEM_WIDTH]).astype(BF16)


def _ret_inproj(x2, pos_col, g, w_in):
    T, D = x2.shape
    tm = min(ROW_TILE, T)
    half = HEAD_DIM // 2
    inv = ROPE_THETA ** (-jnp.arange(half, dtype=F32) / half)
    lane = jnp.arange(LANES)
    inv_row = inv[lane % half][None, :]
    sgn_row = jnp.where((lane % HEAD_DIM) < half, -1.0, 1.0).astype(F32)[None, :]
    row = lambda i: (i, 0)
    fixed = lambda i: (0, 0)
    n_in = w_in.shape[1]
    outs = pl.pallas_call(
        _ret_inproj_kernel,
        grid=(T // tm,),
        in_specs=[
            pl.BlockSpec((tm, D), row),
            pl.BlockSpec((tm, 1), row),
            pl.BlockSpec((1, D), fixed),
            pl.BlockSpec((1, LANES), fixed),
            pl.BlockSpec((1, LANES), fixed),
            pl.BlockSpec((D, n_in), fixed),
        ],
        out_specs=[pl.BlockSpec((tm, MAIN_WIDTH), row)] * 4 + [pl.BlockSpec((tm, MEM_WIDTH), row)],
        out_shape=[jax.ShapeDtypeStruct((T, MAIN_WIDTH), BF16)] * 4
        + [jax.ShapeDtypeStruct((T, MEM_WIDTH), BF16)],
        compiler_params=_cparams(("parallel",)),
        name="ret_inproj",
    )(x2, pos_col, g[None, :], inv_row, sgn_row, w_in.astype(BF16))
    return outs


def _retention_tables():
    H, C, d = N_MAIN_HEADS, RET_CHUNK, HEAD_DIM
    log_g = jnp.log1p(-jnp.exp2(-5.0 - jnp.arange(H, dtype=F32)))
    idx = jnp.arange(C, dtype=F32)
    rel = idx[:, None] - idx[None, :]
    scale = d ** -0.5
    decay_in = jnp.where(rel[None] >= 0,
                         jnp.exp(log_g[:, None, None] * jnp.maximum(rel, 0.0)[None]), 0.0) * scale
    kdec = jnp.exp(log_g[None, :] * (C - 1.0 - idx)[:, None]) * scale
    qdec = jnp.exp(log_g[None, :] * (idx + 1.0)[:, None])
    cdec = jnp.exp(log_g * C)

    def lanes(t):
        return jnp.repeat(t, d, axis=1).reshape(C, N_PAIRS, 2 * d).transpose(1, 0, 2)

    head_of_lane = jnp.arange(2 * d) // d
    same = (head_of_lane[:, None] == head_of_lane[None, :]).astype(F32)
    cd_lane = jnp.repeat(cdec, d).reshape(N_PAIRS, 2 * d)
    state_decay = cd_lane[:, :, None] * same[None]
    decay_in = decay_in.reshape(N_PAIRS, 2, C, C)
    return decay_in, lanes(kdec), lanes(qdec), state_decay, same


RET_CHUNKS_PER_STEP = 4


def _retention_kernel(q_ref, k_ref, v_ref, gt_ref, dm_ref, kd_ref, qd_ref, sd_ref, same_ref,
                      gn_ref, o_ref, r_ref):
    n = pl.program_id(1)

    @pl.when(n == 0)
    def _():
        r_ref[...] = jnp.zeros_like(r_ref)

    lane = lax.broadcasted_iota(jnp.int32, (1, LANES), 1)
    head_a = lane < HEAD_DIM
    keep_a = jnp.where(head_a, 1.0, 0.0).astype(BF16)
    keep_b = jnp.where(head_a, 0.0, 1.0).astype(BF16)
    same = same_ref[...]
    avg = (same * (1.0 / HEAD_DIM)).astype(BF16)
    C = RET_CHUNK
    n_chunks = q_ref.shape[1] // C
    P = range(N_PAIRS)
    CP = [(c, p) for c in range(n_chunks) for p in P]
    rows = [pl.ds(c * C, C) for c in range(n_chunks)]
    sl = [slice(p * LANES, (p + 1) * LANES) for p in P]
    q = {cp: q_ref[0, rows[cp[0]], sl[cp[1]]] for cp in CP}
    k = {cp: k_ref[0, rows[cp[0]], sl[cp[1]]] for cp in CP}
    v = {cp: v_ref[0, rows[cp[0]], sl[cp[1]]] for cp in CP}
    s_a = {cp: _dot_nt(q[cp] * keep_a, k[cp]) * dm_ref[cp[1], 0] for cp in CP}
    s_b = {cp: _dot_nt(q[cp] * keep_b, k[cp]) * dm_ref[cp[1], 1] for cp in CP}
    u = {cp: _dot_tn((k[cp].astype(F32) * kd_ref[cp[1]]).astype(BF16), v[cp]) for cp in CP}
    qd = {cp: (q[cp].astype(F32) * qd_ref[cp[1]]).astype(BF16) for cp in CP}
    inner = {cp: _dot(s_a[cp].astype(BF16), v[cp] * keep_a) + _dot(s_b[cp].astype(BF16), v[cp] * keep_b)
             for cp in CP}
    state = [r_ref[p] for p in P]
    y = {}
    for c in range(n_chunks):
        for p in P:
            y[(c, p)] = inner[(c, p)] + _dot(qd[(c, p)], state[p].astype(BF16))
        state = [sd_ref[p] * state[p] + same * u[(c, p)] for p in P]
    for p in P:
        r_ref[p] = state[p]
    yc = {cp: y[cp] - _dot(y[cp].astype(BF16), avg) for cp in CP}
    var = {cp: _dot((yc[cp] * yc[cp]).astype(BF16), avg) for cp in CP}
    for c, p in CP:
        yn = yc[(c, p)] * lax.rsqrt(var[(c, p)] + EPS) * gn_ref[:, sl[p]]
        g = gt_ref[0, rows[c], sl[p]].astype(F32)
        o_ref[0, rows[c], sl[p]] = (g / (1.0 + jnp.exp(-g)) * yn).astype(BF16)


def _retention(q, k, v, gt, ret_gn):
    B, S, W = q.shape
    C = RET_CHUNK
    dm, kd, qd, sd, same = _retention_tables()
    tok = lambda b, n: (b, n, 0)
    fixed3 = lambda b, n: (0, 0, 0)
    step = RET_CHUNKS_PER_STEP * C
    return pl.pallas_call(
        _retention_kernel,
        grid=(B, S // step),
        in_specs=[pl.BlockSpec((1, step, W), tok)] * 4 + [
            pl.BlockSpec((N_PAIRS, 2, C, C), lambda b, n: (0, 0, 0, 0)),
            pl.BlockSpec((N_PAIRS, C, LANES), fixed3),
            pl.BlockSpec((N_PAIRS, C, LANES), fixed3),
            pl.BlockSpec((N_PAIRS, LANES, LANES), fixed3),
            pl.BlockSpec((LANES, LANES), lambda b, n: (0, 0)),
            pl.BlockSpec((1, W), lambda b, n: (0, 0)),
        ],
        out_specs=pl.BlockSpec((1, step, W), tok),
        out_shape=jax.ShapeDtypeStruct((B, S, W), BF16),
        scratch_shapes=[pltpu.VMEM((N_PAIRS, LANES, LANES), F32)],
        compiler_params=_cparams(("parallel", "arbitrary")),
        name="retention",
    )(q, k, v, gt, dm, kd, qd, sd, same, ret_gn[None, :])


def _mem_kv_kernel(mem_ref, w_ref, kg_ref, k_ref, v_ref):
    kv = _dot(mem_ref[...].astype(BF16), w_ref[...])
    lane = lax.broadcasted_iota(jnp.int32, (1, LANES), 1)
    head_a = lane < HEAD_DIM
    inv_d = 1.0 / HEAD_DIM
    for j in range(MEM_WIDTH // LANES):
        kj = kv[:, j * LANES:(j + 1) * LANES]
        k2 = kj * kj
        ms_a = jnp.sum(jnp.where(head_a, k2, 0.0), axis=-1, keepdims=True) * inv_d
        ms_b = jnp.sum(jnp.where(head_a, 0.0, k2), axis=-1, keepdims=True) * inv_d
        kn = kj * lax.rsqrt(jnp.where(head_a, ms_a, ms_b) + EPS) * kg_ref[...]
        k_ref[:, j * LANES:(j + 1) * LANES] = kn.astype(BF16)
    v_ref[...] = kv[:, MEM_WIDTH:].astype(BF16)


def _mem_kv(mem2, w_mem_kv, k_g):
    TM, D = mem2.shape
    tm = min(ROW_TILE, TM)
    kg_row = jnp.tile(k_g, LANES // HEAD_DIM)[None, :]
    row = lambda i: (i, 0)
    fixed = lambda i: (0, 0)
    return pl.pallas_call(
        _mem_kv_kernel,
        grid=(TM // tm,),
        in_specs=[pl.BlockSpec((tm, D), row), pl.BlockSpec((D, 2 * MEM_WIDTH), fixed),
                  pl.BlockSpec((1, LANES), fixed)],
        out_specs=[pl.BlockSpec((tm, MEM_WIDTH), row)] * 2,
        out_shape=[jax.ShapeDtypeStruct((TM, MEM_WIDTH), BF16)] * 2,
        compiler_params=_cparams(("parallel",)),
        name="mem_kv",
    )(mem2, w_mem_kv.astype(BF16), kg_row)


def _mem_attn_kernel(qm_ref, k_ref, v_ref, qg_ref, o_ref):
    lane = lax.broadcasted_iota(jnp.int32, (1, LANES), 1)
    head_a = lane < HEAD_DIM
    inv_d = 1.0 / HEAD_DIM
    scale = HEAD_DIM ** -0.5
    for j in range(MEM_WIDTH // LANES):
        sl = slice(j * LANES, (j + 1) * LANES)
        qj = qm_ref[0, :, sl].astype(F32)
        q2 = qj * qj
        ms_a = jnp.sum(jnp.where(head_a, q2, 0.0), axis=-1, keepdims=True) * inv_d
        ms_b = jnp.sum(jnp.where(head_a, 0.0, q2), axis=-1, keepdims=True) * inv_d
        qn = qj * lax.rsqrt(jnp.where(head_a, ms_a, ms_b) + EPS) * (qg_ref[...] * scale)
        kj = k_ref[0, :, sl]
        vj = v_ref[0, :, sl]
        out = None
        for sel in (head_a, jnp.logical_not(head_a)):
            s = _dot_nt(jnp.where(sel, qn, 0.0).astype(BF16), kj)
            p = jnp.exp(s - jnp.max(s, axis=-1, keepdims=True))
            p = p / jnp.sum(p, axis=-1, keepdims=True)
            o = _dot(p.astype(BF16), jnp.where(sel, vj, jnp.zeros_like(vj)))
            out = o if out is None else out + o
        o_ref[0, :, sl] = out.astype(BF16)


def _mem_attn(qm, k_m, v_m, q_g):
    B, S, _ = qm.shape
    M = k_m.shape[1]
    tm = min(ROW_TILE, S)
    qg_row = jnp.tile(q_g, LANES // HEAD_DIM)[None, :]
    return pl.pallas_call(
        _mem_attn_kernel,
        grid=(B, S // tm),
        in_specs=[pl.BlockSpec((1, tm, MEM_WIDTH), lambda b, i: (b, i, 0)),
                  pl.BlockSpec((1, M, MEM_WIDTH), lambda b, i: (b, 0, 0)),
                  pl.BlockSpec((1, M, MEM_WIDTH), lambda b, i: (b, 0, 0)),
                  pl.BlockSpec((1, LANES), lambda b, i: (0, 0))],
        out_specs=pl.BlockSpec((1, tm, MEM_WIDTH), lambda b, i: (b, i, 0)),
        out_shape=jax.ShapeDtypeStruct((B, S, MEM_WIDTH), BF16),
        compiler_params=_cparams(("parallel", "parallel")),
        name="mem_attn",
    )(qm, k_m, v_m, qg_row)


ROUTER_LANE0 = N_GROUPS
RANK_BITS = 17
RANK_RADIX = 1 << RANK_BITS


def _proj_router_kernel(x_ref, y_ref, m_ref, wy_ref, wm_ref, g_ref, w2_ref, b_ref, lower_ref,
                        o_ref, info_ref, cnt_ref, tile_base_ref, base_ref):
    x1 = x_ref[...] + _dot(y_ref[...], wy_ref[...]) + _dot(m_ref[...], wm_ref[...])
    o_ref[...] = x1
    _route_tile(x1, g_ref, w2_ref, b_ref, lower_ref, info_ref, cnt_ref, tile_base_ref, base_ref)


def _route_tile(x, g_ref, w2_ref, b_ref, lower_ref, info_ref, cnt_ref, tile_base_ref, base_ref):
    i = pl.program_id(0)

    @pl.when(i == 0)
    def _():
        base_ref[...] = jnp.zeros_like(base_ref)

    tm = x.shape[0]
    h = _rmsnorm(x, g_ref[...])
    h_hi = h.astype(BF16)
    h_lo = (h - h_hi.astype(F32)).astype(BF16)
    both = _dot(h_hi, w2_ref[...])
    logits = both[:, :LANES] + both[:, LANES:] + _dot(h_lo, w2_ref[:, :LANES]) + b_ref[...]
    lane_i = lax.broadcasted_iota(jnp.int32, (tm, LANES), 1)
    lane = lane_i.astype(F32)
    big = float(LANES)

    is_g = lane_i < N_GROUPS
    lg = jnp.where(is_g, logits, NEG_BIG)
    mg = jnp.max(lg, axis=-1, keepdims=True)
    zg = jnp.sum(jnp.where(is_g, jnp.exp(lg - mg), 0.0), axis=-1, keepdims=True)
    p_grp = 1.0 / zg
    grp = jnp.min(jnp.where(is_g & (lg == mg), lane, big), axis=-1, keepdims=True)

    e_lane = lane_i - ROUTER_LANE0
    e_grp = (e_lane >> int(math.log2(EXPERTS_PER_GROUP))).astype(F32)
    is_e = (e_lane >= 0) & (e_lane < N_EXPERTS) & (e_grp == grp)
    le = jnp.where(is_e, logits, NEG_BIG)
    me = jnp.max(le, axis=-1, keepdims=True)
    ee = jnp.where(is_e, jnp.exp(le - me), 0.0)
    prob = ee / jnp.sum(ee, axis=-1, keepdims=True)
    p1 = jnp.max(prob, axis=-1, keepdims=True)
    i1 = jnp.min(jnp.where(is_e & (prob == p1), lane, big), axis=-1, keepdims=True)
    rest = is_e & (lane != i1)
    p2 = jnp.max(jnp.where(rest, prob, -1.0), axis=-1, keepdims=True)
    i2 = jnp.min(jnp.where(rest & (prob == p2), lane, big), axis=-1, keepdims=True)
    gate1 = p_grp * p1 / (p1 + p2)
    gate2 = p_grp * p2 / (p1 + p2)

    sel1 = lane == i1
    sel2 = lane == i2
    onehot = jnp.where(sel1 | sel2, 1.0, 0.0)
    tile_base_ref[0] = base_ref[...]
    before = _dot(lower_ref[...], onehot.astype(BF16)) + base_ref[...]
    rank1 = jnp.sum(jnp.where(sel1, before, 0.0), axis=-1, keepdims=True)
    rank2 = jnp.sum(jnp.where(sel2, before, 0.0), axis=-1, keepdims=True)
    base_ref[...] += jnp.sum(onehot, axis=0, keepdims=True)
    cnt_ref[...] = base_ref[...]

    code1 = (i1 - float(ROUTER_LANE0)) * float(RANK_RADIX) + rank1
    code2 = (i2 - float(ROUTER_LANE0)) * float(RANK_RADIX) + rank2
    info = jnp.zeros((tm, LANES), F32)
    for col, val in enumerate((gate1, gate2, code1, code2)):
        info = jnp.where(lane_i == col, val, info)
    info_ref[...] = info


def _router_operands(D, tm, g, w_grp, b_grp, w_exp, b_exp):
    w = jnp.zeros((D, LANES), F32)
    w = w.at[:, :N_GROUPS].set(w_grp).at[:, ROUTER_LANE0:ROUTER_LANE0 + N_EXPERTS].set(w_exp)
    b = jnp.zeros((1, LANES), F32)
    b = b.at[0, :N_GROUPS].set(b_grp).at[0, ROUTER_LANE0:ROUTER_LANE0 + N_EXPERTS].set(b_exp)
    w_hi = w.astype(BF16)
    w_lo = (w - w_hi.astype(F32)).astype(BF16)
    lower = (jnp.arange(tm)[:, None] > jnp.arange(tm)[None, :]).astype(BF16)
    fixed = lambda i: (0, 0)
    specs = [pl.BlockSpec((1, D), fixed), pl.BlockSpec((D, 2 * LANES), fixed), pl.BlockSpec((1, LANES), fixed),
             pl.BlockSpec((tm, tm), fixed)]
    return (g[None, :], jnp.concatenate([w_hi, w_lo], axis=1), b, lower), specs


def _router_outputs(T, tm):
    n_tiles = T // tm
    specs = [pl.BlockSpec((tm, LANES), lambda i: (i, 0)), pl.BlockSpec((1, LANES), lambda i: (0, 0)),
             pl.BlockSpec((1, 1, LANES), lambda i: (i, 0, 0))]
    shapes = [jax.ShapeDtypeStruct((T, LANES), F32), jax.ShapeDtypeStruct((1, LANES), F32),
              jax.ShapeDtypeStruct((n_tiles, 1, LANES), F32)]
    return specs, shapes


def _router_tables(cnt, tile_base):
    experts = slice(ROUTER_LANE0, ROUTER_LANE0 + N_EXPERTS)
    return cnt[0, experts].astype(jnp.int32), tile_base[:, 0, experts].astype(jnp.int32)


def _out_proj_router(x2, y2, m2, w_out, g, w_grp, b_grp, w_exp, b_exp):
    T, D = x2.shape
    tm = min(ROW_TILE, T)
    row = lambda i: (i, 0)
    fixed = lambda i: (0, 0)
    w = w_out.astype(BF16)
    ops, op_specs = _router_operands(D, tm, g, w_grp, b_grp, w_exp, b_exp)
    out_specs, out_shapes = _router_outputs(T, tm)
    x1, info, cnt, tile_base = pl.pallas_call(
        _proj_router_kernel,
        grid=(T // tm,),
        in_specs=[pl.BlockSpec((tm, D), row), pl.BlockSpec((tm, MAIN_WIDTH), row),
                  pl.BlockSpec((tm, MEM_WIDTH), row),
                  pl.BlockSpec((MAIN_WIDTH, D), fixed), pl.BlockSpec((MEM_WIDTH, D), fixed)] + op_specs,
        out_specs=[pl.BlockSpec((tm, D), row)] + out_specs,
        out_shape=[jax.ShapeDtypeStruct((T, D), F32)] + out_shapes,
        scratch_shapes=[pltpu.VMEM((1, LANES), F32)],
        compiler_params=_cparams(("arbitrary",)),
        name="out_proj_router",
    )(x2, y2, m2, w[:MAIN_WIDTH], w[MAIN_WIDTH:], *ops)
    return (x1, info) + _router_tables(cnt, tile_base)


TOK_ROWS = 4
U32 = jnp.uint32


def _tok_rows(r, n=1):
    start = r * TOK_ROWS
    if not isinstance(start, int):
        start = pl.multiple_of(start, TOK_ROWS)
    return pl.ds(start, n * TOK_ROWS)


def _pack_rows(h):
    bits = lax.bitcast_convert_type(h.astype(BF16).astype(F32), U32)
    half = h.shape[1] // 2
    return (bits[:, :half] >> 16) | bits[:, half:]


def _unpack_words(w):
    return (lax.bitcast_convert_type(w << 16, F32), lax.bitcast_convert_type(w & U32(0xFFFF0000), F32))


def _store_token_rows(ref, first_row, n, words):
    for c in range(TOK_ROWS):
        ref[pl.ds(first_row * TOK_ROWS + c, n, stride=TOK_ROWS), :] = words[:, c * LANES:(c + 1) * LANES]


def _load_token_rows(ref, n):
    parts = [_unpack_words(ref[pl.ds(c, n, stride=TOK_ROWS), :]) for c in range(TOK_ROWS)]
    return jnp.concatenate([p[0] for p in parts] + [p[1] for p in parts], axis=-1)


def _segment_copies(n, src, src_row, dst, dst_row, sem, top, op, grain=1):
    groups = (n + (grain - 1)) >> int(math.log2(grain))
    off = 0
    bit = top // grain
    while bit >= 1:
        take = groups & bit

        @pl.when(take != 0)
        def _(rows=bit * grain, off=off):
            op(pltpu.make_async_copy(src.at[_tok_rows(src_row + off, rows), :],
                                     dst.at[_tok_rows(dst_row + off, rows), :], sem))
        off = off + take * grain
        bit //= 2


ENTRIES_PER_ITER = 2 * DMA_UNROLL


def _local_rows(code_ref, a_ref, tbl, first_tok, dvec_ref, drow_ref, dsem):
    code = code_ref[...]
    e = code >> RANK_BITS
    d = (code & (RANK_RADIX - 1)) + first_tok
    for k in range(N_EXPERTS):
        d = d + jnp.where(e == k, a_ref[tbl + k], 0)
    dvec_ref[...] = d * TOK_ROWS
    copies = [pltpu.make_async_copy(dvec_ref.at[r], drow_ref.at[pl.ds(r * LANES, LANES)], dsem)
              for r in range(dvec_ref.shape[0])]
    for cp in copies:
        cp.start()
    for cp in copies:
        cp.wait()


def _run_copies(op, step, to_sorted, n_ref, ls_ref, gb_ref, local_ref, first_tok, sorted_ref, sem, top,
                grain=1):
    tbl = step * N_EXPERTS

    def run(e, c):
        loc = (local_ref, first_tok + ls_ref[tbl + e])
        glob = (sorted_ref, gb_ref[tbl + e])
        (src, src_row), (dst, dst_row) = (loc, glob) if to_sorted else (glob, loc)
        _segment_copies(n_ref[tbl + e], src, src_row, dst, dst_row, sem, top, op, grain)
        return c
    lax.fori_loop(0, N_EXPERTS, run, 0)


def _rows_at(first_row):
    return pl.ds(pl.multiple_of(first_row, TOK_ROWS), TOK_ROWS)


def _dma_start(cp):
    cp.start()


def _dma_wait(cp):
    cp.wait()


def _block_copy(src, dst, blk, sem):
    return pltpu.make_async_copy(src, dst.at[_tok_rows(blk * MOE_BLOCK, MOE_BLOCK), :], sem)


def _dispatch_kernel(a_ref, n_ref, ls_ref, gb_ref, zrow_ref, zcnt_ref, nblk_ref,
                     code_ref, x_ref, g_ref, xs_ref, hbuf_ref, cbuf_ref, zbuf_ref, dvec_ref, drow_ref,
                     sem, zsem, dsem):
    s = pl.program_id(0)
    ts = x_ref.shape[0]

    @pl.when(s == 0)
    def _():
        zbuf_ref[...] = jnp.zeros_like(zbuf_ref)
        n_blocks = xs_ref.shape[0] // (MOE_BLOCK * TOK_ROWS)
        for op in (_dma_start, _dma_wait):
            def tail(b, c, op=op):
                op(_block_copy(zbuf_ref, xs_ref, b, zsem))
                return c
            lax.fori_loop(nblk_ref[0], n_blocks, tail, 0)

            def pad(e, c, op=op):
                _segment_copies(zcnt_ref[e], zbuf_ref, 0, xs_ref, zrow_ref[e], zsem, MOE_BLOCK // 2, op)
                return c
            lax.fori_loop(0, N_EXPERTS, pad, 0)

    for sub in range(ts // MOE_TILE):
        h = _rmsnorm(x_ref[pl.ds(sub * MOE_TILE, MOE_TILE), :], g_ref[...])
        _store_token_rows(hbuf_ref, sub * MOE_TILE, MOE_TILE, _pack_rows(h))

    slot = s % 2
    first_tok = slot * (2 * ts)
    last = pl.num_programs(0) - 1

    def runs(op, step, step_slot):
        _run_copies(op, step, True, n_ref, ls_ref, gb_ref, cbuf_ref, step_slot * (2 * ts), xs_ref,
                    sem.at[step_slot], ts)

    @pl.when(s >= 2)
    def _():
        runs(_dma_wait, s - 2, slot)

    _local_rows(code_ref, a_ref, s * N_EXPERTS, first_tok, dvec_ref, drow_ref, dsem)

    def place(tb, c):
        first = tb * ENTRIES_PER_ITER
        for u in range(DMA_UNROLL):
            tile = hbuf_ref[_tok_rows(tb * DMA_UNROLL + u), :]
            for kk in range(2):
                cbuf_ref[_rows_at(drow_ref[first + u * 2 + kk]), :] = tile
        return c
    lax.fori_loop(0, ts // DMA_UNROLL, place, 0)

    runs(_dma_start, s, slot)

    @pl.when(s == last)
    def _():
        @pl.when(s >= 1)
        def _():
            runs(_dma_wait, s - 1, 1 - slot)
        runs(_dma_wait, s, slot)


def _dispatch(x2, g, codes, tables, zrow, zcnt, nblk_used, n_rows, ts):
    T, D = x2.shape
    return pl.pallas_call(
        _dispatch_kernel,
        grid_spec=pltpu.PrefetchScalarGridSpec(
            num_scalar_prefetch=7,
            grid=(T // ts,),
            in_specs=[pl.BlockSpec((ts * 2 // LANES, LANES), lambda i, *_: (i, 0)),
                      pl.BlockSpec((ts, D), lambda i, *_: (i, 0)),
                      pl.BlockSpec((1, D), lambda i, *_: (0, 0))],
            out_specs=pl.BlockSpec(memory_space=pl.ANY),
            scratch_shapes=[pltpu.VMEM((ts * TOK_ROWS, LANES), U32),
                            pltpu.VMEM((2 * 2 * ts * TOK_ROWS, LANES), U32),
                            pltpu.VMEM((MOE_BLOCK * TOK_ROWS, LANES), U32),
                            pltpu.VMEM((ts * 2 // LANES, LANES), jnp.int32),
                            pltpu.SMEM((ts * 2,), jnp.int32),
                            pltpu.SemaphoreType.DMA((2,)), pltpu.SemaphoreType.DMA(()),
                            pltpu.SemaphoreType.DMA(())],
        ),
        out_shape=jax.ShapeDtypeStruct((n_rows * TOK_ROWS, LANES), U32),
        compiler_params=_cparams(("arbitrary",)),
        name="moe_dispatch",
    )(*tables, zrow, zcnt, nblk_used, codes, x2, g[None, :])


def _expert_kernel(blk_e_ref, nblk_ref, xs_ref, wg_ref, wu_ref, wd_ref, ys_ref, wg_s, wu_s, wd_s):
    b = pl.program_id(0)

    @pl.when((b == 0) | (blk_e_ref[b] != blk_e_ref[jnp.maximum(b - 1, 0)]))
    def _():
        wg_s[...] = wg_ref[0, 0].astype(BF16)
        wu_s[...] = wu_ref[0, 0].astype(BF16)
        wd_s[...] = wd_ref[0, 0].astype(BF16)

    @pl.when(b < nblk_ref[0])
    def _():
        x = _load_token_rows(xs_ref, MOE_BLOCK).astype(BF16)
        a = _dot(x, wg_s[...])
        u = _dot(x, wu_s[...])
        hid = (a / (1.0 + jnp.exp(-a)) * u).astype(BF16)
        y = _dot(hid, wd_s[...])
        _store_token_rows(ys_ref, 0, MOE_BLOCK, _pack_rows(y))

    @pl.when(b >= nblk_ref[0])
    def _():
        ys_ref[...] = jnp.zeros_like(ys_ref)


def _experts(xs, blk_e, nblk_used, w_gate, w_up, w_down, layer):
    rows = xs.shape[0]
    nblk = rows // (MOE_BLOCK * TOK_ROWS)
    D = w_gate.shape[2]
    blk = lambda b, be, nb: (jnp.minimum(b, nb[0] - 1), 0)
    out_blk = lambda b, be, nb: (b, 0)
    wsel = lambda b, be, nb: (layer, be[b], 0, 0)
    return pl.pallas_call(
        _expert_kernel,
        grid_spec=pltpu.PrefetchScalarGridSpec(
            num_scalar_prefetch=2,
            grid=(nblk,),
            in_specs=[pl.BlockSpec((MOE_BLOCK * TOK_ROWS, LANES), blk),
                      pl.BlockSpec((1, 1, D, D_EXPERT), wsel),
                      pl.BlockSpec((1, 1, D, D_EXPERT), wsel),
                      pl.BlockSpec((1, 1, D_EXPERT, D), wsel)],
            out_specs=pl.BlockSpec((MOE_BLOCK * TOK_ROWS, LANES), out_blk),
            scratch_shapes=[pltpu.VMEM((D, D_EXPERT), BF16), pltpu.VMEM((D, D_EXPERT), BF16),
                            pltpu.VMEM((D_EXPERT, D), BF16)],
        ),
        out_shape=jax.ShapeDtypeStruct((rows, LANES), U32),
        compiler_params=_cparams(("arbitrary",)),
        name="moe_experts",
    )(blk_e, nblk_used, xs, w_gate, w_up, w_down)


def _combine_kernel(a_ref, n_ref, ls_ref, gb_ref, code_ref, x_ref, info_ref, ys_ref, o_ref,
                    ybuf_ref, pick0_ref, pick1_ref, dvec_ref, drow_ref, sem, dsem):
    s = pl.program_id(0)
    ts = x_ref.shape[0]
    slot = s % 2
    slot_rows = ybuf_ref.shape[0] // (2 * TOK_ROWS)
    first_tok = slot * slot_rows

    def fetch(op, step, step_slot):
        _run_copies(op, step, False, n_ref, ls_ref, gb_ref, ybuf_ref, step_slot * slot_rows, ys_ref,
                    sem.at[step_slot], ts, FETCH_GRAIN)

    @pl.when(s == 0)
    def _():
        fetch(_dma_start, 0, 0)

    @pl.when(s + 1 < pl.num_programs(0))
    def _():
        fetch(_dma_start, s + 1, 1 - slot)

    fetch(_dma_wait, s, slot)
    _local_rows(code_ref, a_ref, s * N_EXPERTS, first_tok, dvec_ref, drow_ref, dsem)

    picks = (pick0_ref, pick1_ref)
    for sub in range(ts // MOE_TILE):
        def pick(tb, c, sub=sub):
            first = (sub * (MOE_TILE // DMA_UNROLL) + tb) * ENTRIES_PER_ITER
            for u in range(DMA_UNROLL):
                for kk in range(2):
                    picks[kk][_tok_rows(tb * DMA_UNROLL + u), :] = (
                        ybuf_ref[_rows_at(drow_ref[first + u * 2 + kk]), :])
            return c
        lax.fori_loop(0, MOE_TILE // DMA_UNROLL, pick, 0)

        rows = pl.ds(sub * MOE_TILE, MOE_TILE)
        info = info_ref[rows, :]
        g0 = info[:, 0:1]
        g1 = info[:, 1:2]
        half = x_ref.shape[1] // 2
        for c in range(TOK_ROWS):
            y0 = _unpack_words(pick0_ref[pl.ds(c, MOE_TILE, stride=TOK_ROWS), :])
            y1 = _unpack_words(pick1_ref[pl.ds(c, MOE_TILE, stride=TOK_ROWS), :])
            for part in range(2):
                sl = slice(part * half + c * LANES, part * half + (c + 1) * LANES)
                o_ref[rows, sl] = x_ref[rows, sl] + (y0[part] * g0 + y1[part] * g1)


def _combine(x2, info, ys, codes, tables, ts):
    T, D = x2.shape
    return pl.pallas_call(
        _combine_kernel,
        grid_spec=pltpu.PrefetchScalarGridSpec(
            num_scalar_prefetch=4,
            grid=(T // ts,),
            in_specs=[pl.BlockSpec((ts * 2 // LANES, LANES), lambda i, *_: (i, 0)),
                      pl.BlockSpec((ts, D), lambda i, *_: (i, 0)),
                      pl.BlockSpec((ts, LANES), lambda i, *_: (i, 0)),
                      pl.BlockSpec(memory_space=pl.ANY)],
            out_specs=pl.BlockSpec((ts, D), lambda i, *_: (i, 0)),
            scratch_shapes=[pltpu.VMEM((2 * (2 * ts + N_EXPERTS * FETCH_GRAIN) * TOK_ROWS, LANES), U32),
                            pltpu.VMEM((MOE_TILE * TOK_ROWS, LANES), U32),
                            pltpu.VMEM((MOE_TILE * TOK_ROWS, LANES), U32),
                            pltpu.VMEM((ts * 2 // LANES, LANES), jnp.int32),
                            pltpu.SMEM((ts * 2,), jnp.int32),
                            pltpu.SemaphoreType.DMA((2,)), pltpu.SemaphoreType.DMA(())],
        ),
        out_shape=jax.ShapeDtypeStruct((T, D), F32),
        compiler_params=_cparams(("arbitrary",)),
        name="moe_combine",
    )(*tables, codes, x2, info, ys)


def _supertile_tables(tile_base, counts, pad_start, ts, grain=1):
    per = ts // ROW_TILE
    base = tile_base[::per]
    nxt = jnp.concatenate([base[1:], counts[None, :]], axis=0)
    n = nxt - base
    room = (n + grain - 1) // grain * grain
    lstart = jnp.cumsum(room, axis=1) - room
    flat = lambda a: a.reshape(-1).astype(jnp.int32)
    return flat(lstart - base), flat(n), flat(lstart), flat(pad_start[None, :] + base)


DISPATCH_TOKENS = 2048
COMBINE_TOKENS = 1024
FETCH_GRAIN = 64


def _moe_apply(x2, ln2, info, counts, tile_base, w_gate, w_up, w_down, layer):
    T, D = x2.shape
    padded = (counts + MOE_BLOCK - 1) // MOE_BLOCK * MOE_BLOCK
    pad_end = jnp.cumsum(padded)
    pad_start = pad_end - padded
    codes = info[:, 2:4].astype(jnp.int32).reshape(T * 2 // LANES, LANES)
    n_rows = T * 2 + (N_EXPERTS + 1) * MOE_BLOCK
    nblk = n_rows // MOE_BLOCK
    blk_row = jnp.arange(nblk, dtype=jnp.int32) * MOE_BLOCK
    blk_e = jnp.minimum(jnp.sum((pad_end[None, :] <= blk_row[:, None]).astype(jnp.int32), axis=1),
                        N_EXPERTS - 1).astype(jnp.int32)
    nblk_used = (pad_end[-1:] // MOE_BLOCK).astype(jnp.int32)
    zrow = (pad_start + counts).astype(jnp.int32)
    zcnt = (padded - counts).astype(jnp.int32)

    td = min(DISPATCH_TOKENS, T)
    tc = min(COMBINE_TOKENS, T)

    xs = _dispatch(x2, ln2, codes, _supertile_tables(tile_base, counts, pad_start, td),
                   zrow, zcnt, nblk_used, n_rows, td)
    ys = _experts(xs, blk_e, nblk_used, w_gate, w_up, w_down, layer)
    return _combine(x2, info, ys, codes, _supertile_tables(tile_base, counts, pad_start, tc, FETCH_GRAIN), tc)


HEAD_PAD = LANES
LATENT_PAD = 3 * LANES


def _rot_partner():
    half = QK_ROPE // 2
    r = jnp.arange(QK_ROPE)
    return jnp.where(r < half, r + half, r - half), jnp.where(r < half, -1.0, 1.0).astype(F32)


def _mla_rope_rows():
    half = QK_ROPE // 2
    inv = ROPE_THETA ** (-jnp.arange(half, dtype=F32) / half)
    lane = jnp.arange(LANES)
    r = lane - QK_NOPE
    in_rope = (r >= 0) & (r < QK_ROPE)
    inv_row = jnp.where(in_rope, inv[jnp.clip(r, 0, QK_ROPE - 1) % half], 0.0)[None, :]
    rope_row = in_rope.astype(F32)[None, :]
    real_row = (lane < QK_HEAD).astype(F32)[None, :]
    return inv_row, rope_row, real_row


def _head_gain_row(g):
    partner, _ = _rot_partner()
    return jnp.concatenate([g, g[QK_NOPE + partner]])[None, :]


def _with_partner_cols(w3):
    partner, sign = _rot_partner()
    rot = w3[:, :, QK_NOPE + partner] * sign
    return jnp.concatenate([w3, rot], axis=-1).reshape(w3.shape[0], N_MAIN_HEADS * HEAD_PAD)


def _heads_norm_rope(ys, gain_rows, scales, real_row, cos_real, sin_rope):
    n = range(len(ys))
    row_id = lax.broadcasted_iota(jnp.int32, (LANES, LANES), 0)
    ones_real = jnp.where(row_id < QK_HEAD, 1.0, 0.0).astype(BF16)
    ms = [_dot((ys[i] * ys[i]).astype(BF16), ones_real) * (1.0 / QK_HEAD) for i in n]
    yn = [ys[i] * (lax.rsqrt(ms[i] + EPS) * scales[i]) * gain_rows[i] for i in n]
    rolled = [pltpu.roll(yn[i], LANES - QK_ROPE, 1) for i in n]
    return [(yn[i] * cos_real + rolled[i] * sin_rope).astype(BF16) for i in n]


def _mla_qkv_kernel(x_ref, pos_ref, lnkv_ref, lnq_ref, wd_ref, scale_ref, wkv_ref, kg_ref,
                    win_ref, qlg_ref, wuq_ref, qg_ref, inv_ref, rope_ref, real_ref,
                    k_ref, v_ref, q_ref, qm_ref):
    x = x_ref[...]
    xr = x * lax.rsqrt(jnp.mean(x * x, axis=-1, keepdims=True) + EPS)
    ang = pos_ref[...].astype(F32) * inv_ref[...]
    real_row = real_ref[...]
    cos_real = jnp.cos(ang) * real_row
    sin_rope = jnp.sin(ang) * rope_ref[...]

    ckr = _dot((xr * lnkv_ref[...]).astype(BF16), wd_ref[...])
    c = ckr[:, :KV_LORA]
    r = lax.rsqrt(jnp.mean(c * c, axis=-1, keepdims=True) + EPS)
    lane = lax.broadcasted_iota(jnp.int32, (1, LATENT_PAD), 1)
    lhs = (ckr * jnp.where(lane < KV_LORA, r * scale_ref[...], 1.0)).astype(BF16)
    kv = _dot(lhs, wkv_ref[...])
    proj = _dot((xr * lnq_ref[...]).astype(BF16), win_ref[...])
    cq = _rmsnorm(proj[:, :Q_LORA], qlg_ref[...]).astype(BF16)
    q = _dot(cq, wuq_ref[...])
    qm_ref[...] = proj[:, Q_LORA:].astype(BF16)
    v_ref[...] = kv[:, N_MAIN_HEADS * HEAD_PAD:].astype(BF16)
    sls = [slice(hh * HEAD_PAD, (hh + 1) * HEAD_PAD) for hh in range(N_MAIN_HEADS)]
    nh = N_MAIN_HEADS
    outs = _heads_norm_rope([kv[:, sl] for sl in sls] + [q[:, sl] for sl in sls],
                            [kg_ref[...]] * nh + [qg_ref[...]] * nh,
                            [1.0] * nh + [QK_HEAD ** -0.5] * nh, real_row, cos_real, sin_rope)
    for hh, sl in enumerate(sls):
        k_ref[:, sl] = outs[hh]
        q_ref[:, sl] = outs[nh + hh]


def _mla_qkv(x2, pos_col, kv_ln, w_dkv, kv_lora_g, w_ukv, k_g, ln1, w_in, q_lora_g, w_uq, q_g):
    T, D = x2.shape
    tm = min(ROW_TILE, T)
    lat = LATENT_PAD
    wd = jnp.pad(w_dkv, ((0, 0), (0, lat - w_dkv.shape[1]))).astype(BF16)
    scale_row = jnp.pad(kv_lora_g, (0, lat - KV_LORA), constant_values=1.0)[None, :]
    w3 = w_ukv.reshape(KV_LORA, N_MAIN_HEADS, QK_NOPE + V_HEAD)
    wk = jnp.zeros((lat, N_MAIN_HEADS, QK_HEAD), F32)
    wk = wk.at[:KV_LORA, :, :QK_NOPE].set(w3[:, :, :QK_NOPE])
    eye = jnp.eye(QK_ROPE, dtype=F32)
    wk = wk.at[KV_LORA:KV_LORA + QK_ROPE, :, QK_NOPE:].set(
        jnp.broadcast_to(eye[:, None, :], (QK_ROPE, N_MAIN_HEADS, QK_ROPE)))
    wv = jnp.zeros((lat, N_MAIN_HEADS * V_HEAD), F32)
    wv = wv.at[:KV_LORA].set(w3[:, :, QK_NOPE:].reshape(KV_LORA, N_MAIN_HEADS * V_HEAD))
    wkv = jnp.concatenate([_with_partner_cols(wk), wv], axis=1).astype(BF16)
    wuq = _with_partner_cols(w_uq.reshape(Q_LORA, N_MAIN_HEADS, QK_HEAD)).astype(BF16)
    inv_row, rope_row, real_row = _mla_rope_rows()
    row = lambda i: (i, 0)
    fixed = lambda i: (0, 0)
    kw = N_MAIN_HEADS * HEAD_PAD
    n_in = w_in.shape[1]
    lane_row = pl.BlockSpec((1, LANES), fixed)
    return pl.pallas_call(
        _mla_qkv_kernel,
        grid=(T // tm,),
        in_specs=[pl.BlockSpec((tm, D), row), pl.BlockSpec((tm, 1), row),
                  pl.BlockSpec((1, D), fixed), pl.BlockSpec((1, D), fixed),
                  pl.BlockSpec((D, lat), fixed), pl.BlockSpec((1, lat), fixed),
                  pl.BlockSpec((lat, kw + MAIN_WIDTH), fixed), lane_row,
                  pl.BlockSpec((D, n_in), fixed), pl.BlockSpec((1, Q_LORA), fixed),
                  pl.BlockSpec((Q_LORA, kw), fixed), lane_row,
                  lane_row, lane_row, lane_row],
        out_specs=[pl.BlockSpec((tm, kw), row), pl.BlockSpec((tm, MAIN_WIDTH), row),
                   pl.BlockSpec((tm, kw), row), pl.BlockSpec((tm, MEM_WIDTH), row)],
        out_shape=[jax.ShapeDtypeStruct((T, kw), BF16), jax.ShapeDtypeStruct((T, MAIN_WIDTH), BF16),
                   jax.ShapeDtypeStruct((T, kw), BF16), jax.ShapeDtypeStruct((T, MEM_WIDTH), BF16)],
        compiler_params=_cparams(("parallel",)),
        name="mla_qkv",
    )(x2, pos_col, kv_ln[None, :], ln1[None, :], wd, scale_row, wkv, _head_gain_row(k_g),
      w_in.astype(BF16), q_lora_g[None, :], wuq, _head_gain_row(q_g), inv_row, rope_row, real_row)


ATTN_TILE = 1024


def _flash_kernel(qi_ref, kj_ref, q_ref, k_ref, v_ref, o_ref, m_ref, acc_ref):
    t = pl.program_id(2)
    i = qi_ref[t]
    j = kj_ref[t]
    tq = q_ref.shape[1]
    tk = k_ref.shape[1]

    @pl.when(j == 0)
    def _():
        m_ref[...] = jnp.full_like(m_ref, NEG_BIG)
        acc_ref[...] = jnp.zeros_like(acc_ref)

    lane = lax.broadcasted_iota(jnp.int32, (1, LANES), 1)
    head_a = lane < V_HEAD
    den_lane = (V_HEAD, 0)

    def attend(q0, nq, nk, diag_col):
        rows = pl.ds(q0, nq)
        v = v_ref[0, pl.ds(0, nk), :]
        if diag_col is not None:
            q_idx = lax.broadcasted_iota(jnp.int32, (nq, nk), 0) + diag_col
            k_idx = lax.broadcasted_iota(jnp.int32, (nq, nk), 1)
            visible = k_idx <= q_idx
        H = range(2)
        sl = [slice(hh * HEAD_PAD, (hh + 1) * HEAD_PAD) for hh in H]
        s = [_dot_nt(q_ref[0, rows, sl[hh]], k_ref[0, pl.ds(0, nk), sl[hh]]) for hh in H]
        if diag_col is not None:
            s = [jnp.where(visible, s[hh], NEG_BIG) for hh in H]
        m_prev = [m_ref[hh, rows, :] for hh in H]
        acc_prev = [acc_ref[hh, rows, :] for hh in H]
        m_new = [jnp.maximum(m_prev[hh], jnp.max(s[hh], axis=-1, keepdims=True)) for hh in H]
        alpha = [jnp.exp(m_prev[hh] - m_new[hh]) for hh in H]
        m_wide = [jnp.concatenate([m_new[hh]] * (nk // LANES), axis=1) for hh in H]
        p = [jnp.exp((s[hh] - m_wide[hh]).astype(BF16)) for hh in H]
        keep_row = [jnp.where(head_a, 1.0, 0.0).astype(BF16), jnp.where(head_a, 0.0, 1.0).astype(BF16)]
        den_row = [jnp.where(lane == den_lane[hh], 1.0, 0.0).astype(BF16) for hh in H]
        pv = [_dot(p[hh], v * keep_row[hh] + den_row[hh]) for hh in H]
        for hh in H:
            acc_ref[hh, rows, :] = alpha[hh] * acc_prev[hh] + pv[hh]
            m_ref[hh, rows, :] = m_new[hh]

    @pl.when(j < i)
    def _():
        attend(0, tq, tk, None)

    @pl.when(j == i)
    def _():
        half = tq // 2
        attend(0, half, half, 0)
        attend(half, half, tk, half)
        acc_a = acc_ref[0]
        acc_b = acc_ref[1]
        out_a = acc_a / acc_a[:, den_lane[0]:den_lane[0] + 1]
        out_b = acc_b / acc_b[:, den_lane[1]:den_lane[1] + 1]
        o_ref[0] = jnp.where(head_a, out_a, out_b).astype(BF16)


def _flash(q, k, v):
    B, S, _ = q.shape
    t = min(ATTN_TILE, S)
    n = S // t
    pairs = [(i, j) for i in range(n) for j in range(i + 1)]
    qi = jnp.array([p[0] for p in pairs], jnp.int32)
    kj = jnp.array([p[1] for p in pairs], jnp.int32)
    return pl.pallas_call(
        _flash_kernel,
        grid_spec=pltpu.PrefetchScalarGridSpec(
            num_scalar_prefetch=2,
            grid=(B, N_PAIRS, len(pairs)),
            in_specs=[pl.BlockSpec((1, t, 2 * HEAD_PAD), lambda b, p, s, qi, kj: (b, qi[s], p)),
                      pl.BlockSpec((1, t, 2 * HEAD_PAD), lambda b, p, s, qi, kj: (b, kj[s], p)),
                      pl.BlockSpec((1, t, LANES), lambda b, p, s, qi, kj: (b, kj[s], p))],
            out_specs=pl.BlockSpec((1, t, LANES), lambda b, p, s, qi, kj: (b, qi[s], p)),
            scratch_shapes=[pltpu.VMEM((2, t, LANES), F32), pltpu.VMEM((2, t, LANES), F32)],
        ),
        out_shape=jax.ShapeDtypeStruct((B, S, MAIN_WIDTH), BF16),
        compiler_params=_cparams(("parallel", "parallel", "arbitrary")),
        name="mla_flash",
    )(qi, kj, q, k, v)


def kernel(x, mem, positions, ln1, ln2, w_out, mem_w_kv, mem_q_norm, mem_k_norm, router_group_w, router_group_b, router_expert_w, router_expert_b, expert_w_gate, expert_w_up, expert_w_down, ret_w_in, ret_gn, kv_ln, kv_w_down, kv_lora_norm, kv_w_up, k_norm, mla_w_in, q_lora_norm, mla_w_uq, q_norm):
    B, S, D = x.shape
    M = mem.shape[1]
    T = B * S
    x2 = x.reshape(T, D)
    mem2 = mem.reshape(B * M, D)
    pos_col = positions.reshape(T, 1).astype(jnp.int32)

    def mem_path(i, qm):
        k_m, v_m = _mem_kv(mem2, mem_w_kv[i], mem_k_norm[i])
        return _mem_attn(qm.reshape(B, S, MEM_WIDTH), k_m.reshape(B, M, MEM_WIDTH),
                         v_m.reshape(B, M, MEM_WIDTH), mem_q_norm[i]).reshape(T, MEM_WIDTH)

    def mix_out_and_moe(i, xin, y, m):
        x1, info, counts, tile_base = _out_proj_router(
            xin, y, m, w_out[i], ln2[i], router_group_w[i], router_group_b[i], router_expert_w[i],
            router_expert_b[i])
        return _moe_apply(x1, ln2[i], info, counts, tile_base, expert_w_gate, expert_w_up, expert_w_down, i)

    q, k, v, gt, qm = _ret_inproj(x2, pos_col, ln1[0], ret_w_in[0])
    shp = (B, S, MAIN_WIDTH)
    y = _retention(q.reshape(shp), k.reshape(shp), v.reshape(shp), gt.reshape(shp), ret_gn[0])
    x2 = mix_out_and_moe(0, x2, y.reshape(T, MAIN_WIDTH), mem_path(0, qm))

    k_sh, v_sh, q1, qm1 = _mla_qkv(x2, pos_col, kv_ln, kv_w_down, kv_lora_norm, kv_w_up, k_norm,
                                   ln1[1], mla_w_in[0], q_lora_norm[0], mla_w_uq[0], q_norm[0])
    kw = N_MAIN_HEADS * HEAD_PAD
    y1 = _flash(q1.reshape(B, S, kw), k_sh.reshape(B, S, kw), v_sh.reshape(shp))
    x2 = mix_out_and_moe(1, x2, y1.reshape(T, MAIN_WIDTH), mem_path(1, qm1))
    return x2.reshape(B, S, D)
```

```python
import math

import jax
import jax.numpy as jnp
from jax import lax
from jax.experimental import pallas as pl
from jax.experimental.pallas import tpu as pltpu

F32 = jnp.float32
BF16 = jnp.bfloat16

HEAD_DIM = 64
N_MAIN_HEADS = 12
MAIN_WIDTH = N_MAIN_HEADS * HEAD_DIM
N_MEM_HEADS = 4
MEM_WIDTH = N_MEM_HEADS * HEAD_DIM
RET_CHUNK = 128
ROPE_THETA = 10000.0
Q_LORA = 384
KV_LORA = 256
QK_NOPE = 64
QK_ROPE = 32
QK_HEAD = QK_NOPE + QK_ROPE
V_HEAD = 64
N_GROUPS = 4
EXPERTS_PER_GROUP = 8
N_EXPERTS = N_GROUPS * EXPERTS_PER_GROUP
D_EXPERT = 256
MOE_BLOCK = 512
EPS = 1e-6

LANES = 128
VMEM_LIMIT = 48 * 1024 * 1024
NEG_BIG = -1e30

N_PAIRS = N_MAIN_HEADS // 2
ROW_TILE = 512
MOE_TILE = 256
DMA_UNROLL = 8


def _cparams(sem):
    return pltpu.CompilerParams(dimension_semantics=sem, vmem_limit_bytes=VMEM_LIMIT)


def _dot(a, b):
    return jnp.dot(a, b, preferred_element_type=F32)


def _dot_nt(a, b):
    return lax.dot_general(a, b, (((1,), (1,)), ((), ())), preferred_element_type=F32)


def _dot_tn(a, b):
    return lax.dot_general(a, b, (((0,), (0,)), ((), ())), preferred_element_type=F32)


def _rmsnorm(xf, g):
    return xf * lax.rsqrt(jnp.mean(xf * xf, axis=-1, keepdims=True) + EPS) * g


def _rope_tables(pos_col, inv_row, sgn_row):
    ang = pos_col.astype(F32) * inv_row
    return jnp.cos(ang), jnp.sin(ang) * sgn_row


def _ret_inproj_kernel(x_ref, pos_ref, g_ref, inv_ref, sgn_ref, w_ref,
                       q_ref, k_ref, v_ref, gt_ref, qm_ref):
    h = _rmsnorm(x_ref[...], g_ref[...]).astype(BF16)
    cos, sin_s = _rope_tables(pos_ref[...], inv_ref[...], sgn_ref[...])
    first = sgn_ref[...] < 0.0
    mw = MAIN_WIDTH
    half = HEAD_DIM // 2
    qk = _dot(h, w_ref[:, :2 * mw])
    groups = range(2 * mw // LANES)
    ys = [qk[:, j * LANES:(j + 1) * LANES] for j in groups]
    fwd = [pltpu.roll(ys[j], LANES - half, 1) for j in groups]
    bwd = [pltpu.roll(ys[j], half, 1) for j in groups]
    outs = [(ys[j] * cos + jnp.where(first, fwd[j], bwd[j]) * sin_s).astype(BF16) for j in groups]
    n_q = mw // LANES
    for j in range(n_q):
        q_ref[:, j * LANES:(j + 1) * LANES] = outs[j]
        k_ref[:, j * LANES:(j + 1) * LANES] = outs[n_q + j]
    v_ref[...] = _dot(h, w_ref[:, 2 * mw:3 * mw]).astype(BF16)
    gt_ref[...] = _dot(h, w_ref[:, 3 * mw:4 * mw]).astype(BF16)
    qm_ref[...] = _dot(h, w_ref[:, 4 * mw:4 * mw + MEM_WIDTH]).astype(BF16)


def _ret_inproj(x2, pos_col, g, w_in):
    T, D = x2.shape
    tm = min(ROW_TILE, T)
    half = HEAD_DIM // 2
    inv = ROPE_THETA ** (-jnp.arange(half, dtype=F32) / half)
    lane = jnp.arange(LANES)
    inv_row = inv[lane % half][None, :]
    sgn_row = jnp.where((lane % HEAD_DIM) < half, -1.0, 1.0).astype(F32)[None, :]
    row = lambda i: (i, 0)
    fixed = lambda i: (0, 0)
    n_in = w_in.shape[1]
    outs = pl.pallas_call(
        _ret_inproj_kernel,
        grid=(T // tm,),
        in_specs=[
            pl.BlockSpec((tm, D), row),
            pl.BlockSpec((tm, 1), row),
            pl.BlockSpec((1, D), fixed),
            pl.BlockSpec((1, LANES), fixed),
            pl.BlockSpec((1, LANES), fixed),
            pl.BlockSpec((D, n_in), fixed),
        ],
        out_specs=[pl.BlockSpec((tm, MAIN_WIDTH), row)] * 4 + [pl.BlockSpec((tm, MEM_WIDTH), row)],
        out_shape=[jax.ShapeDtypeStruct((T, MAIN_WIDTH), BF16)] * 4
        + [jax.ShapeDtypeStruct((T, MEM_WIDTH), BF16)],
        compiler_params=_cparams(("parallel",)),
        name="ret_inproj",
    )(x2, pos_col, g[None, :], inv_row, sgn_row, w_in.astype(BF16))
    return outs


def _retention_tables():
    H, C, d = N_MAIN_HEADS, RET_CHUNK, HEAD_DIM
    log_g = jnp.log1p(-jnp.exp2(-5.0 - jnp.arange(H, dtype=F32)))
    idx = jnp.arange(C, dtype=F32)
    rel = idx[:, None] - idx[None, :]
    scale = d ** -0.5
    decay_in = jnp.where(rel[None] >= 0,
                         jnp.exp(log_g[:, None, None] * jnp.maximum(rel, 0.0)[None]), 0.0) * scale
    kdec = jnp.exp(log_g[None, :] * (C - 1.0 - idx)[:, None]) * scale
    qdec = jnp.exp(log_g[None, :] * (idx + 1.0)[:, None])
    cdec = jnp.exp(log_g * C)

    def lanes(t):
        return jnp.repeat(t, d, axis=1).reshape(C, N_PAIRS, 2 * d).transpose(1, 0, 2)

    head_of_lane = jnp.arange(2 * d) // d
    same = (head_of_lane[:, None] == head_of_lane[None, :]).astype(F32)
    cd_lane = jnp.repeat(cdec, d).reshape(N_PAIRS, 2 * d)
    state_decay = cd_lane[:, :, None] * same[None]
    decay_in = decay_in.reshape(N_PAIRS, 2, C, C)
    return decay_in, lanes(kdec), lanes(qdec), state_decay, same


RET_CHUNKS_PER_STEP = 4


def _retention_kernel(q_ref, k_ref, v_ref, gt_ref, dm_ref, kd_ref, qd_ref, sd_ref, same_ref,
                      gn_ref, o_ref, r_ref):
    n = pl.program_id(1)

    @pl.when(n == 0)
    def _():
        r_ref[...] = jnp.zeros_like(r_ref)

    lane = lax.broadcasted_iota(jnp.int32, (1, LANES), 1)
    head_a = lane < HEAD_DIM
    keep_a = jnp.where(head_a, 1.0, 0.0).astype(BF16)
    keep_b = jnp.where(head_a, 0.0, 1.0).astype(BF16)
    same = same_ref[...]
    avg = (same * (1.0 / HEAD_DIM)).astype(BF16)
    C = RET_CHUNK
    n_chunks = q_ref.shape[1] // C
    P = range(N_PAIRS)
    CP = [(c, p) for c in range(n_chunks) for p in P]
    rows = [pl.ds(c * C, C) for c in range(n_chunks)]
    sl = [slice(p * LANES, (p + 1) * LANES) for p in P]
    q = {cp: q_ref[0, rows[cp[0]], sl[cp[1]]] for cp in CP}
    k = {cp: k_ref[0, rows[cp[0]], sl[cp[1]]] for cp in CP}
    v = {cp: v_ref[0, rows[cp[0]], sl[cp[1]]] for cp in CP}
    s_a = {cp: _dot_nt(q[cp] * keep_a, k[cp]) * dm_ref[cp[1], 0] for cp in CP}
    s_b = {cp: _dot_nt(q[cp] * keep_b, k[cp]) * dm_ref[cp[1], 1] for cp in CP}
    u = {cp: _dot_tn((k[cp].astype(F32) * kd_ref[cp[1]]).astype(BF16), v[cp]) for cp in CP}
    qd = {cp: (q[cp].astype(F32) * qd_ref[cp[1]]).astype(BF16) for cp in CP}
    inner = {cp: _dot(s_a[cp].astype(BF16), v[cp] * keep_a) + _dot(s_b[cp].astype(BF16), v[cp] * keep_b)
             for cp in CP}
    state = [r_ref[p] for p in P]
    y = {}
    for c in range(n_chunks):
        for p in P:
            y[(c, p)] = inner[(c, p)] + _dot(qd[(c, p)], state[p].astype(BF16))
        state = [sd_ref[p] * state[p] + same * u[(c, p)] for p in P]
    for p in P:
        r_ref[p] = state[p]
    yc = {cp: y[cp] - _dot(y[cp].astype(BF16), avg) for cp in CP}
    var = {cp: _dot((yc[cp] * yc[cp]).astype(BF16), avg) for cp in CP}
    for c, p in CP:
        yn = yc[(c, p)] * lax.rsqrt(var[(c, p)] + EPS) * gn_ref[:, sl[p]]
        g = gt_ref[0, rows[c], sl[p]].astype(F32)
        o_ref[0, rows[c], sl[p]] = (g / (1.0 + jnp.exp(-g)) * yn).astype(BF16)


def _retention(q, k, v, gt, ret_gn):
    B, S, W = q.shape
    C = RET_CHUNK
    dm, kd, qd, sd, same = _retention_tables()
    tok = lambda b, n: (b, n, 0)
    fixed3 = lambda b, n: (0, 0, 0)
    step = RET_CHUNKS_PER_STEP * C
    return pl.pallas_call(
        _retention_kernel,
        grid=(B, S // step),
        in_specs=[pl.BlockSpec((1, step, W), tok)] * 4 + [
            pl.BlockSpec((N_PAIRS, 2, C, C), lambda b, n: (0, 0, 0, 0)),
            pl.BlockSpec((N_PAIRS, C, LANES), fixed3),
            pl.BlockSpec((N_PAIRS, C, LANES), fixed3),
            pl.BlockSpec((N_PAIRS, LANES, LANES), fixed3),
            pl.BlockSpec((LANES, LANES), lambda b, n: (0, 0)),
            pl.BlockSpec((1, W), lambda b, n: (0, 0)),
        ],
        out_specs=pl.BlockSpec((1, step, W), tok),
        out_shape=jax.ShapeDtypeStruct((B, S, W), BF16),
        scratch_shapes=[pltpu.VMEM((N_PAIRS, LANES, LANES), F32)],
        compiler_params=_cparams(("parallel", "arbitrary")),
        name="retention",
    )(q, k, v, gt, dm, kd, qd, sd, same, ret_gn[None, :])


def _mem_kv_kernel(mem_ref, w_ref, kg_ref, k_ref, v_ref):
    kv = _dot(mem_ref[...].astype(BF16), w_ref[...])
    lane = lax.broadcasted_iota(jnp.int32, (1, LANES), 1)
    head_a = lane < HEAD_DIM
    inv_d = 1.0 / HEAD_DIM
    for j in range(MEM_WIDTH // LANES):
        kj = kv[:, j * LANES:(j + 1) * LANES]
        k2 = kj * kj
        ms_a = jnp.sum(jnp.where(head_a, k2, 0.0), axis=-1, keepdims=True) * inv_d
        ms_b = jnp.sum(jnp.where(head_a, 0.0, k2), axis=-1, keepdims=True) * inv_d
        kn = kj * lax.rsqrt(jnp.where(head_a, ms_a, ms_b) + EPS) * kg_ref[...]
        k_ref[:, j * LANES:(j + 1) * LANES] = kn.astype(BF16)
    v_ref[...] = kv[:, MEM_WIDTH:].astype(BF16)


def _mem_kv(mem2, w_mem_kv, k_g):
    TM, D = mem2.shape
    tm = min(ROW_TILE, TM)
    kg_row = jnp.tile(k_g, LANES // HEAD_DIM)[None, :]
    row = lambda i: (i, 0)
    fixed = lambda i: (0, 0)
    return pl.pallas_call(
        _mem_kv_kernel,
        grid=(TM // tm,),
        in_specs=[pl.BlockSpec((tm, D), row), pl.BlockSpec((D, 2 * MEM_WIDTH), fixed),
                  pl.BlockSpec((1, LANES), fixed)],
        out_specs=[pl.BlockSpec((tm, MEM_WIDTH), row)] * 2,
        out_shape=[jax.ShapeDtypeStruct((TM, MEM_WIDTH), BF16)] * 2,
        compiler_params=_cparams(("parallel",)),
        name="mem_kv",
    )(mem2, w_mem_kv.astype(BF16), kg_row)


def _mem_attn_kernel(qm_ref, k_ref, v_ref, qg_ref, o_ref):
    lane = lax.broadcasted_iota(jnp.int32, (1, LANES), 1)
    head_a = lane < HEAD_DIM
    keep = [jnp.where(head_a, 1.0, 0.0).astype(BF16), jnp.where(head_a, 0.0, 1.0).astype(BF16)]
    r_head = lax.broadcasted_iota(jnp.int32, (LANES, LANES), 0) < HEAD_DIM
    c_head = lax.broadcasted_iota(jnp.int32, (LANES, LANES), 1) < HEAD_DIM
    avg = jnp.where(r_head == c_head, 1.0 / HEAD_DIM, 0.0).astype(BF16)
    n_mem = k_ref.shape[1]
    ones = jnp.ones((n_mem, LANES), BF16)
    scale = HEAD_DIM ** -0.5
    G = range(MEM_WIDTH // LANES)
    sl = [slice(j * LANES, (j + 1) * LANES) for j in G]
    q = [qm_ref[0, :, sl[j]].astype(F32) for j in G]
    ms = [_dot((q[j] * q[j]).astype(BF16), avg) for j in G]
    qn = [(q[j] * lax.rsqrt(ms[j] + EPS) * (qg_ref[...] * scale)).astype(BF16) for j in G]
    heads = [(j, hh) for j in G for hh in range(2)]
    s = {jh: _dot_nt(qn[jh[0]] * keep[jh[1]], k_ref[0, :, sl[jh[0]]]) for jh in heads}
    p = {jh: jnp.exp(s[jh] - jnp.max(s[jh], axis=-1, keepdims=True)).astype(BF16) for jh in heads}
    den = {jh: _dot(p[jh], ones) for jh in heads}
    pv = {jh: _dot(p[jh], v_ref[0, :, sl[jh[0]]] * keep[jh[1]]) for jh in heads}
    for j in G:
        o_ref[0, :, sl[j]] = (pv[(j, 0)] / den[(j, 0)] + pv[(j, 1)] / den[(j, 1)]).astype(BF16)


def _mem_attn(qm, k_m, v_m, q_g):
    B, S, _ = qm.shape
    M = k_m.shape[1]
    tm = min(2 * ROW_TILE, S)
    qg_row = jnp.tile(q_g, LANES // HEAD_DIM)[None, :]
    return pl.pallas_call(
        _mem_attn_kernel,
        grid=(B, S // tm),
        in_specs=[pl.BlockSpec((1, tm, MEM_WIDTH), lambda b, i: (b, i, 0)),
                  pl.BlockSpec((1, M, MEM_WIDTH), lambda b, i: (b, 0, 0)),
                  pl.BlockSpec((1, M, MEM_WIDTH), lambda b, i: (b, 0, 0)),
                  pl.BlockSpec((1, LANES), lambda b, i: (0, 0))],
        out_specs=pl.BlockSpec((1, tm, MEM_WIDTH), lambda b, i: (b, i, 0)),
        out_shape=jax.ShapeDtypeStruct((B, S, MEM_WIDTH), BF16),
        compiler_params=_cparams(("parallel", "parallel")),
        name="mem_attn",
    )(qm, k_m, v_m, qg_row)


ROUTER_LANE0 = N_GROUPS
RANK_BITS = 17
RANK_RADIX = 1 << RANK_BITS


def _proj_router_kernel(x_ref, y_ref, m_ref, wy_ref, wm_ref, g_ref, w2_ref, b_ref, lower_ref,
                        o_ref, info_ref, cnt_ref, tile_base_ref, base_ref):
    x1 = x_ref[...] + _dot(y_ref[...], wy_ref[...]) + _dot(m_ref[...], wm_ref[...])
    o_ref[...] = x1
    _route_tile(x1, g_ref, w2_ref, b_ref, lower_ref, info_ref, cnt_ref, tile_base_ref, base_ref)


def _route_tile(x, g_ref, w2_ref, b_ref, lower_ref, info_ref, cnt_ref, tile_base_ref, base_ref):
    i = pl.program_id(0)

    @pl.when(i == 0)
    def _():
        base_ref[...] = jnp.zeros_like(base_ref)

    tm = x.shape[0]
    h = _rmsnorm(x, g_ref[...])
    h_hi = h.astype(BF16)
    h_lo = (h - h_hi.astype(F32)).astype(BF16)
    both = _dot(h_hi, w2_ref[...])
    logits = both[:, :LANES] + both[:, LANES:] + _dot(h_lo, w2_ref[:, :LANES]) + b_ref[...]
    lane_i = lax.broadcasted_iota(jnp.int32, (tm, LANES), 1)
    lane = lane_i.astype(F32)
    big = float(LANES)

    is_g = lane_i < N_GROUPS
    lg = jnp.where(is_g, logits, NEG_BIG)
    mg = jnp.max(lg, axis=-1, keepdims=True)
    zg = jnp.sum(jnp.where(is_g, jnp.exp(lg - mg), 0.0), axis=-1, keepdims=True)
    p_grp = 1.0 / zg
    grp = jnp.min(jnp.where(is_g & (lg == mg), lane, big), axis=-1, keepdims=True)

    e_lane = lane_i - ROUTER_LANE0
    e_grp = (e_lane >> int(math.log2(EXPERTS_PER_GROUP))).astype(F32)
    is_e = (e_lane >= 0) & (e_lane < N_EXPERTS) & (e_grp == grp)
    le = jnp.where(is_e, logits, NEG_BIG)
    me = jnp.max(le, axis=-1, keepdims=True)
    ee = jnp.where(is_e, jnp.exp(le - me), 0.0)
    prob = ee / jnp.sum(ee, axis=-1, keepdims=True)
    p1 = jnp.max(prob, axis=-1, keepdims=True)
    i1 = jnp.min(jnp.where(is_e & (prob == p1), lane, big), axis=-1, keepdims=True)
    rest = is_e & (lane != i1)
    p2 = jnp.max(jnp.where(rest, prob, -1.0), axis=-1, keepdims=True)
    i2 = jnp.min(jnp.where(rest & (prob == p2), lane, big), axis=-1, keepdims=True)
    gate1 = p_grp * p1 / (p1 + p2)
    gate2 = p_grp * p2 / (p1 + p2)

    sel1 = lane == i1
    sel2 = lane == i2
    onehot = jnp.where(sel1 | sel2, 1.0, 0.0)
    tile_base_ref[0] = base_ref[...]
    before = _dot(lower_ref[...], onehot.astype(BF16)) + base_ref[...]
    rank1 = jnp.sum(jnp.where(sel1, before, 0.0), axis=-1, keepdims=True)
    rank2 = jnp.sum(jnp.where(sel2, before, 0.0), axis=-1, keepdims=True)
    base_ref[...] += jnp.sum(onehot, axis=0, keepdims=True)
    cnt_ref[...] = base_ref[...]

    code1 = (i1 - float(ROUTER_LANE0)) * float(RANK_RADIX) + rank1
    code2 = (i2 - float(ROUTER_LANE0)) * float(RANK_RADIX) + rank2
    info = jnp.zeros((tm, LANES), F32)
    for col, val in enumerate((gate1, gate2, code1, code2)):
        info = jnp.where(lane_i == col, val, info)
    info_ref[...] = info


def _router_operands(D, tm, g, w_grp, b_grp, w_exp, b_exp):
    w = jnp.zeros((D, LANES), F32)
    w = w.at[:, :N_GROUPS].set(w_grp).at[:, ROUTER_LANE0:ROUTER_LANE0 + N_EXPERTS].set(w_exp)
    b = jnp.zeros((1, LANES), F32)
    b = b.at[0, :N_GROUPS].set(b_grp).at[0, ROUTER_LANE0:ROUTER_LANE0 + N_EXPERTS].set(b_exp)
    w_hi = w.astype(BF16)
    w_lo = (w - w_hi.astype(F32)).astype(BF16)
    lower = (jnp.arange(tm)[:, None] > jnp.arange(tm)[None, :]).astype(BF16)
    fixed = lambda i: (0, 0)
    specs = [pl.BlockSpec((1, D), fixed), pl.BlockSpec((D, 2 * LANES), fixed), pl.BlockSpec((1, LANES), fixed),
             pl.BlockSpec((tm, tm), fixed)]
    return (g[None, :], jnp.concatenate([w_hi, w_lo], axis=1), b, lower), specs


def _router_outputs(T, tm):
    n_tiles = T // tm
    specs = [pl.BlockSpec((tm, LANES), lambda i: (i, 0)), pl.BlockSpec((1, LANES), lambda i: (0, 0)),
             pl.BlockSpec((1, 1, LANES), lambda i: (i, 0, 0))]
    shapes = [jax.ShapeDtypeStruct((T, LANES), F32), jax.ShapeDtypeStruct((1, LANES), F32),
              jax.ShapeDtypeStruct((n_tiles, 1, LANES), F32)]
    return specs, shapes


def _router_tables(cnt, tile_base):
    experts = slice(ROUTER_LANE0, ROUTER_LANE0 + N_EXPERTS)
    return cnt[0, experts].astype(jnp.int32), tile_base[:, 0, experts].astype(jnp.int32)


def _out_proj_router(x2, y2, m2, w_out, g, w_grp, b_grp, w_exp, b_exp):
    T, D = x2.shape
    tm = min(ROW_TILE, T)
    row = lambda i: (i, 0)
    fixed = lambda i: (0, 0)
    w = w_out.astype(BF16)
    ops, op_specs = _router_operands(D, tm, g, w_grp, b_grp, w_exp, b_exp)
    out_specs, out_shapes = _router_outputs(T, tm)
    x1, info, cnt, tile_base = pl.pallas_call(
        _proj_router_kernel,
        grid=(T // tm,),
        in_specs=[pl.BlockSpec((tm, D), row), pl.BlockSpec((tm, MAIN_WIDTH), row),
                  pl.BlockSpec((tm, MEM_WIDTH), row),
                  pl.BlockSpec((MAIN_WIDTH, D), fixed), pl.BlockSpec((MEM_WIDTH, D), fixed)] + op_specs,
        out_specs=[pl.BlockSpec((tm, D), row)] + out_specs,
        out_shape=[jax.ShapeDtypeStruct((T, D), F32)] + out_shapes,
        scratch_shapes=[pltpu.VMEM((1, LANES), F32)],
        compiler_params=_cparams(("arbitrary",)),
        name="out_proj_router",
    )(x2, y2, m2, w[:MAIN_WIDTH], w[MAIN_WIDTH:], *ops)
    return (x1, info) + _router_tables(cnt, tile_base)


TOK_ROWS = 4
U32 = jnp.uint32


def _tok_rows(r, n=1):
    start = r * TOK_ROWS
    if not isinstance(start, int):
        start = pl.multiple_of(start, TOK_ROWS)
    return pl.ds(start, n * TOK_ROWS)


def _pack_rows(h):
    bits = lax.bitcast_convert_type(h.astype(BF16).astype(F32), U32)
    half = h.shape[1] // 2
    return (bits[:, :half] >> 16) | bits[:, half:]


def _unpack_words(w):
    return (lax.bitcast_convert_type(w << 16, F32), lax.bitcast_convert_type(w & U32(0xFFFF0000), F32))


def _store_token_rows(ref, first_row, n, words):
    for c in range(TOK_ROWS):
        ref[pl.ds(first_row * TOK_ROWS + c, n, stride=TOK_ROWS), :] = words[:, c * LANES:(c + 1) * LANES]


def _load_token_rows(ref, n):
    parts = [_unpack_words(ref[pl.ds(c, n, stride=TOK_ROWS), :]) for c in range(TOK_ROWS)]
    return jnp.concatenate([p[0] for p in parts] + [p[1] for p in parts], axis=-1)


def _segment_copies(n, src, src_row, dst, dst_row, sem, top, op, grain=1):
    groups = (n + (grain - 1)) >> int(math.log2(grain))
    off = 0
    bit = top // grain
    while bit >= 1:
        take = groups & bit

        @pl.when(take != 0)
        def _(rows=bit * grain, off=off):
            op(pltpu.make_async_copy(src.at[_tok_rows(src_row + off, rows), :],
                                     dst.at[_tok_rows(dst_row + off, rows), :], sem))
        off = off + take * grain
        bit //= 2


ENTRIES_PER_ITER = 2 * DMA_UNROLL


def _local_rows(code_ref, a_ref, tbl, first_tok, dvec_ref, drow_ref, dsem):
    code = code_ref[...]
    e = code >> RANK_BITS
    d = (code & (RANK_RADIX - 1)) + first_tok
    for k in range(N_EXPERTS):
        d = d + jnp.where(e == k, a_ref[tbl + k], 0)
    dvec_ref[...] = d * TOK_ROWS
    copies = [pltpu.make_async_copy(dvec_ref.at[r], drow_ref.at[pl.ds(r * LANES, LANES)], dsem)
              for r in range(dvec_ref.shape[0])]
    for cp in copies:
        cp.start()
    for cp in copies:
        cp.wait()


def _run_copies(op, step, to_sorted, n_ref, ls_ref, gb_ref, local_ref, first_tok, sorted_ref, sem, top,
                grain=1):
    tbl = step * N_EXPERTS

    def run(e, c):
        loc = (local_ref, first_tok + ls_ref[tbl + e])
        glob = (sorted_ref, gb_ref[tbl + e])
        (src, src_row), (dst, dst_row) = (loc, glob) if to_sorted else (glob, loc)
        _segment_copies(n_ref[tbl + e], src, src_row, dst, dst_row, sem, top, op, grain)
        return c
    lax.fori_loop(0, N_EXPERTS, run, 0)


def _rows_at(first_row):
    return pl.ds(pl.multiple_of(first_row, TOK_ROWS), TOK_ROWS)


def _dma_start(cp):
    cp.start()


def _dma_wait(cp):
    cp.wait()


def _block_copy(src, dst, blk, sem):
    return pltpu.make_async_copy(src, dst.at[_tok_rows(blk * MOE_BLOCK, MOE_BLOCK), :], sem)


def _dispatch_kernel(a_ref, n_ref, ls_ref, gb_ref, zrow_ref, zcnt_ref, nblk_ref,
                     code_ref, x_ref, g_ref, xs_ref, hbuf_ref, cbuf_ref, zbuf_ref, dvec_ref, drow_ref,
                     sem, zsem, dsem):
    s = pl.program_id(0)
    ts = x_ref.shape[0]

    @pl.when(s == 0)
    def _():
        zbuf_ref[...] = jnp.zeros_like(zbuf_ref)
        n_blocks = xs_ref.shape[0] // (MOE_BLOCK * TOK_ROWS)
        for op in (_dma_start, _dma_wait):
            def tail(b, c, op=op):
                op(_block_copy(zbuf_ref, xs_ref, b, zsem))
                return c
            lax.fori_loop(nblk_ref[0], n_blocks, tail, 0)

            def pad(e, c, op=op):
                _segment_copies(zcnt_ref[e], zbuf_ref, 0, xs_ref, zrow_ref[e], zsem, MOE_BLOCK // 2, op)
                return c
            lax.fori_loop(0, N_EXPERTS, pad, 0)

    for sub in range(ts // MOE_TILE):
        h = _rmsnorm(x_ref[pl.ds(sub * MOE_TILE, MOE_TILE), :], g_ref[...])
        _store_token_rows(hbuf_ref, sub * MOE_TILE, MOE_TILE, _pack_rows(h))

    slot = s % 2
    first_tok = slot * (2 * ts)
    last = pl.num_programs(0) - 1

    def runs(op, step, step_slot):
        _run_copies(op, step, True, n_ref, ls_ref, gb_ref, cbuf_ref, step_slot * (2 * ts), xs_ref,
                    sem.at[step_slot], ts)

    @pl.when(s >= 2)
    def _():
        runs(_dma_wait, s - 2, slot)

    _local_rows(code_ref, a_ref, s * N_EXPERTS, first_tok, dvec_ref, drow_ref, dsem)

    def place(tb, c):
        first = tb * ENTRIES_PER_ITER
        for u in range(DMA_UNROLL):
            tile = hbuf_ref[_tok_rows(tb * DMA_UNROLL + u), :]
            for kk in range(2):
                cbuf_ref[_rows_at(drow_ref[first + u * 2 + kk]), :] = tile
        return c
    lax.fori_loop(0, ts // DMA_UNROLL, place, 0)

    runs(_dma_start, s, slot)

    @pl.when(s == last)
    def _():
        @pl.when(s >= 1)
        def _():
            runs(_dma_wait, s - 1, 1 - slot)
        runs(_dma_wait, s, slot)


def _dispatch(x2, g, codes, tables, zrow, zcnt, nblk_used, n_rows, ts):
    T, D = x2.shape
    return pl.pallas_call(
        _dispatch_kernel,
        grid_spec=pltpu.PrefetchScalarGridSpec(
            num_scalar_prefetch=7,
            grid=(T // ts,),
            in_specs=[pl.BlockSpec((ts * 2 // LANES, LANES), lambda i, *_: (i, 0)),
                      pl.BlockSpec((ts, D), lambda i, *_: (i, 0)),
                      pl.BlockSpec((1, D), lambda i, *_: (0, 0))],
            out_specs=pl.BlockSpec(memory_space=pl.ANY),
            scratch_shapes=[pltpu.VMEM((ts * TOK_ROWS, LANES), U32),
                            pltpu.VMEM((2 * 2 * ts * TOK_ROWS, LANES), U32),
                            pltpu.VMEM((MOE_BLOCK * TOK_ROWS, LANES), U32),
                            pltpu.VMEM((ts * 2 // LANES, LANES), jnp.int32),
                            pltpu.SMEM((ts * 2,), jnp.int32),
                            pltpu.SemaphoreType.DMA((2,)), pltpu.SemaphoreType.DMA(()),
                            pltpu.SemaphoreType.DMA(())],
        ),
        out_shape=jax.ShapeDtypeStruct((n_rows * TOK_ROWS, LANES), U32),
        compiler_params=_cparams(("arbitrary",)),
        name="moe_dispatch",
    )(*tables, zrow, zcnt, nblk_used, codes, x2, g[None, :])


def _expert_kernel(blk_e_ref, nblk_ref, xs_ref, wg_ref, wu_ref, wd_ref, ys_ref, wg_s, wu_s, wd_s):
    b = pl.program_id(0)

    @pl.when((b == 0) | (blk_e_ref[b] != blk_e_ref[jnp.maximum(b - 1, 0)]))
    def _():
        wg_s[...] = wg_ref[0, 0].astype(BF16)
        wu_s[...] = wu_ref[0, 0].astype(BF16)
        wd_s[...] = wd_ref[0, 0].astype(BF16)

    @pl.when(b < nblk_ref[0])
    def _():
        x = _load_token_rows(xs_ref, MOE_BLOCK).astype(BF16)
        a = _dot(x, wg_s[...])
        u = _dot(x, wu_s[...])
        hid = (a / (1.0 + jnp.exp(-a)) * u).astype(BF16)
        y = _dot(hid, wd_s[...])
        _store_token_rows(ys_ref, 0, MOE_BLOCK, _pack_rows(y))

    @pl.when(b >= nblk_ref[0])
    def _():
        ys_ref[...] = jnp.zeros_like(ys_ref)


def _experts(xs, blk_e, nblk_used, w_gate, w_up, w_down, layer):
    rows = xs.shape[0]
    nblk = rows // (MOE_BLOCK * TOK_ROWS)
    D = w_gate.shape[2]
    blk = lambda b, be, nb: (jnp.minimum(b, nb[0] - 1), 0)
    out_blk = lambda b, be, nb: (b, 0)
    wsel = lambda b, be, nb: (layer, be[b], 0, 0)
    return pl.pallas_call(
        _expert_kernel,
        grid_spec=pltpu.PrefetchScalarGridSpec(
            num_scalar_prefetch=2,
            grid=(nblk,),
            in_specs=[pl.BlockSpec((MOE_BLOCK * TOK_ROWS, LANES), blk),
                      pl.BlockSpec((1, 1, D, D_EXPERT), wsel),
                      pl.BlockSpec((1, 1, D, D_EXPERT), wsel),
                      pl.BlockSpec((1, 1, D_EXPERT, D), wsel)],
            out_specs=pl.BlockSpec((MOE_BLOCK * TOK_ROWS, LANES), out_blk),
            scratch_shapes=[pltpu.VMEM((D, D_EXPERT), BF16), pltpu.VMEM((D, D_EXPERT), BF16),
                            pltpu.VMEM((D_EXPERT, D), BF16)],
        ),
        out_shape=jax.ShapeDtypeStruct((rows, LANES), U32),
        compiler_params=_cparams(("arbitrary",)),
        name="moe_experts",
    )(blk_e, nblk_used, xs, w_gate, w_up, w_down)


def _combine_kernel(a_ref, n_ref, ls_ref, gb_ref, code_ref, x_ref, info_ref, ys_ref, o_ref,
                    ybuf_ref, pick0_ref, pick1_ref, dvec_ref, drow_ref, sem, dsem):
    s = pl.program_id(0)
    ts = x_ref.shape[0]
    slot = s % 2
    slot_rows = ybuf_ref.shape[0] // (2 * TOK_ROWS)
    first_tok = slot * slot_rows

    def fetch(op, step, step_slot):
        _run_copies(op, step, False, n_ref, ls_ref, gb_ref, ybuf_ref, step_slot * slot_rows, ys_ref,
                    sem.at[step_slot], ts, FETCH_GRAIN)

    @pl.when(s == 0)
    def _():
        fetch(_dma_start, 0, 0)

    @pl.when(s + 1 < pl.num_programs(0))
    def _():
        fetch(_dma_start, s + 1, 1 - slot)

    fetch(_dma_wait, s, slot)
    _local_rows(code_ref, a_ref, s * N_EXPERTS, first_tok, dvec_ref, drow_ref, dsem)

    picks = (pick0_ref, pick1_ref)
    for sub in range(ts // MOE_TILE):
        def pick(tb, c, sub=sub):
            first = (sub * (MOE_TILE // DMA_UNROLL) + tb) * ENTRIES_PER_ITER
            for u in range(DMA_UNROLL):
                for kk in range(2):
                    picks[kk][_tok_rows(tb * DMA_UNROLL + u), :] = (
                        ybuf_ref[_rows_at(drow_ref[first + u * 2 + kk]), :])
            return c
        lax.fori_loop(0, MOE_TILE // DMA_UNROLL, pick, 0)

        rows = pl.ds(sub * MOE_TILE, MOE_TILE)
        info = info_ref[rows, :]
        g0 = info[:, 0:1]
        g1 = info[:, 1:2]
        half = x_ref.shape[1] // 2
        for c in range(TOK_ROWS):
            y0 = _unpack_words(pick0_ref[pl.ds(c, MOE_TILE, stride=TOK_ROWS), :])
            y1 = _unpack_words(pick1_ref[pl.ds(c, MOE_TILE, stride=TOK_ROWS), :])
            for part in range(2):
                sl = slice(part * half + c * LANES, part * half + (c + 1) * LANES)
                o_ref[rows, sl] = x_ref[rows, sl] + (y0[part] * g0 + y1[part] * g1)


def _combine(x2, info, ys, codes, tables, ts):
    T, D = x2.shape
    return pl.pallas_call(
        _combine_kernel,
        grid_spec=pltpu.PrefetchScalarGridSpec(
            num_scalar_prefetch=4,
            grid=(T // ts,),
            in_specs=[pl.BlockSpec((ts * 2 // LANES, LANES), lambda i, *_: (i, 0)),
                      pl.BlockSpec((ts, D), lambda i, *_: (i, 0)),
                      pl.BlockSpec((ts, LANES), lambda i, *_: (i, 0)),
                      pl.BlockSpec(memory_space=pl.ANY)],
            out_specs=pl.BlockSpec((ts, D), lambda i, *_: (i, 0)),
            scratch_shapes=[pltpu.VMEM((2 * (2 * ts + N_EXPERTS * FETCH_GRAIN) * TOK_ROWS, LANES), U32),
                            pltpu.VMEM((MOE_TILE * TOK_ROWS, LANES), U32),
                            pltpu.VMEM((MOE_TILE * TOK_ROWS, LANES), U32),
                            pltpu.VMEM((ts * 2 // LANES, LANES), jnp.int32),
                            pltpu.SMEM((ts * 2,), jnp.int32),
                            pltpu.SemaphoreType.DMA((2,)), pltpu.SemaphoreType.DMA(())],
        ),
        out_shape=jax.ShapeDtypeStruct((T, D), F32),
        compiler_params=_cparams(("arbitrary",)),
        name="moe_combine",
    )(*tables, codes, x2, info, ys)


def _supertile_tables(tile_base, counts, pad_start, ts, grain=1):
    per = ts // ROW_TILE
    base = tile_base[::per]
    nxt = jnp.concatenate([base[1:], counts[None, :]], axis=0)
    n = nxt - base
    room = (n + grain - 1) // grain * grain
    lstart = jnp.cumsum(room, axis=1) - room
    flat = lambda a: a.reshape(-1).astype(jnp.int32)
    return flat(lstart - base), flat(n), flat(lstart), flat(pad_start[None, :] + base)


DISPATCH_TOKENS = 2048
COMBINE_TOKENS = 1024
FETCH_GRAIN = 64


def _moe_apply(x2, ln2, info, counts, tile_base, w_gate, w_up, w_down, layer):
    T, D = x2.shape
    padded = (counts + MOE_BLOCK - 1) // MOE_BLOCK * MOE_BLOCK
    pad_end = jnp.cumsum(padded)
    pad_start = pad_end - padded
    codes = info[:, 2:4].astype(jnp.int32).reshape(T * 2 // LANES, LANES)
    n_rows = T * 2 + (N_EXPERTS + 1) * MOE_BLOCK
    nblk = n_rows // MOE_BLOCK
    blk_row = jnp.arange(nblk, dtype=jnp.int32) * MOE_BLOCK
    blk_e = jnp.minimum(jnp.sum((pad_end[None, :] <= blk_row[:, None]).astype(jnp.int32), axis=1),
                        N_EXPERTS - 1).astype(jnp.int32)
    nblk_used = (pad_end[-1:] // MOE_BLOCK).astype(jnp.int32)
    zrow = (pad_start + counts).astype(jnp.int32)
    zcnt = (padded - counts).astype(jnp.int32)

    td = min(DISPATCH_TOKENS, T)
    tc = min(COMBINE_TOKENS, T)

    xs = _dispatch(x2, ln2, codes, _supertile_tables(tile_base, counts, pad_start, td),
                   zrow, zcnt, nblk_used, n_rows, td)
    ys = _experts(xs, blk_e, nblk_used, w_gate, w_up, w_down, layer)
    return _combine(x2, info, ys, codes, _supertile_tables(tile_base, counts, pad_start, tc, FETCH_GRAIN), tc)


HEAD_PAD = LANES
LATENT_PAD = 3 * LANES


def _rot_partner():
    half = QK_ROPE // 2
    r = jnp.arange(QK_ROPE)
    return jnp.where(r < half, r + half, r - half), jnp.where(r < half, -1.0, 1.0).astype(F32)


def _mla_rope_rows():
    half = QK_ROPE // 2
    inv = ROPE_THETA ** (-jnp.arange(half, dtype=F32) / half)
    lane = jnp.arange(LANES)
    r = lane - QK_NOPE
    in_rope = (r >= 0) & (r < QK_ROPE)
    inv_row = jnp.where(in_rope, inv[jnp.clip(r, 0, QK_ROPE - 1) % half], 0.0)[None, :]
    rope_row = in_rope.astype(F32)[None, :]
    real_row = (lane < QK_HEAD).astype(F32)[None, :]
    return inv_row, rope_row, real_row


def _head_gain_row(g):
    partner, _ = _rot_partner()
    return jnp.concatenate([g, g[QK_NOPE + partner]])[None, :]


def _with_partner_cols(w3):
    partner, sign = _rot_partner()
    rot = w3[:, :, QK_NOPE + partner] * sign
    return jnp.concatenate([w3, rot], axis=-1).reshape(w3.shape[0], N_MAIN_HEADS * HEAD_PAD)


def _heads_norm_rope(ys, gain_rows, scales, real_row, cos_real, sin_rope):
    n = range(len(ys))
    row_id = lax.broadcasted_iota(jnp.int32, (LANES, LANES), 0)
    ones_real = jnp.where(row_id < QK_HEAD, 1.0, 0.0).astype(BF16)
    ms = [_dot((ys[i] * ys[i]).astype(BF16), ones_real) * (1.0 / QK_HEAD) for i in n]
    yn = [ys[i] * (lax.rsqrt(ms[i] + EPS) * scales[i]) * gain_rows[i] for i in n]
    rolled = [pltpu.roll(yn[i], LANES - QK_ROPE, 1) for i in n]
    return [(yn[i] * cos_real + rolled[i] * sin_rope).astype(BF16) for i in n]


def _mla_qkv_kernel(x_ref, pos_ref, lnkv_ref, lnq_ref, wd_ref, scale_ref, wkv_ref, kg_ref,
                    win_ref, qlg_ref, wuq_ref, qg_ref, inv_ref, rope_ref, real_ref,
                    k_ref, v_ref, q_ref, qm_ref):
    x = x_ref[...]
    xr = x * lax.rsqrt(jnp.mean(x * x, axis=-1, keepdims=True) + EPS)
    ang = pos_ref[...].astype(F32) * inv_ref[...]
    real_row = real_ref[...]
    cos_real = jnp.cos(ang) * real_row
    sin_rope = jnp.sin(ang) * rope_ref[...]

    ckr = _dot((xr * lnkv_ref[...]).astype(BF16), wd_ref[...])
    c = ckr[:, :KV_LORA]
    r = lax.rsqrt(jnp.mean(c * c, axis=-1, keepdims=True) + EPS)
    lane = lax.broadcasted_iota(jnp.int32, (1, LATENT_PAD), 1)
    lhs = (ckr * jnp.where(lane < KV_LORA, r * scale_ref[...], 1.0)).astype(BF16)
    kv = _dot(lhs, wkv_ref[...])
    proj = _dot((xr * lnq_ref[...]).astype(BF16), win_ref[...])
    cq = _rmsnorm(proj[:, :Q_LORA], qlg_ref[...]).astype(BF16)
    q = _dot(cq, wuq_ref[...])
    qm_ref[...] = proj[:, Q_LORA:].astype(BF16)
    v_ref[...] = kv[:, N_MAIN_HEADS * HEAD_PAD:].astype(BF16)
    sls = [slice(hh * HEAD_PAD, (hh + 1) * HEAD_PAD) for hh in range(N_MAIN_HEADS)]
    nh = N_MAIN_HEADS
    outs = _heads_norm_rope([kv[:, sl] for sl in sls] + [q[:, sl] for sl in sls],
                            [kg_ref[...]] * nh + [qg_ref[...]] * nh,
                            [1.0] * nh + [QK_HEAD ** -0.5] * nh, real_row, cos_real, sin_rope)
    for hh, sl in enumerate(sls):
        k_ref[:, sl] = outs[hh]
        q_ref[:, sl] = outs[nh + hh]


def _mla_qkv(x2, pos_col, kv_ln, w_dkv, kv_lora_g, w_ukv, k_g, ln1, w_in, q_lora_g, w_uq, q_g):
    T, D = x2.shape
    tm = min(ROW_TILE, T)
    lat = LATENT_PAD
    wd = jnp.pad(w_dkv, ((0, 0), (0, lat - w_dkv.shape[1]))).astype(BF16)
    scale_row = jnp.pad(kv_lora_g, (0, lat - KV_LORA), constant_values=1.0)[None, :]
    w3 = w_ukv.reshape(KV_LORA, N_MAIN_HEADS, QK_NOPE + V_HEAD)
    wk = jnp.zeros((lat, N_MAIN_HEADS, QK_HEAD), F32)
    wk = wk.at[:KV_LORA, :, :QK_NOPE].set(w3[:, :, :QK_NOPE])
    eye = jnp.eye(QK_ROPE, dtype=F32)
    wk = wk.at[KV_LORA:KV_LORA + QK_ROPE, :, QK_NOPE:].set(
        jnp.broadcast_to(eye[:, None, :], (QK_ROPE, N_MAIN_HEADS, QK_ROPE)))
    wv = jnp.zeros((lat, N_MAIN_HEADS * V_HEAD), F32)
    wv = wv.at[:KV_LORA].set(w3[:, :, QK_NOPE:].reshape(KV_LORA, N_MAIN_HEADS * V_HEAD))
    wkv = jnp.concatenate([_with_partner_cols(wk), wv], axis=1).astype(BF16)
    wuq = _with_partner_cols(w_uq.reshape(Q_LORA, N_MAIN_HEADS, QK_HEAD)).astype(BF16)
    inv_row, rope_row, real_row = _mla_rope_rows()
    row = lambda i: (i, 0)
    fixed = lambda i: (0, 0)
    kw = N_MAIN_HEADS * HEAD_PAD
    n_in = w_in.shape[1]
    lane_row = pl.BlockSpec((1, LANES), fixed)
    return pl.pallas_call(
        _mla_qkv_kernel,
        grid=(T // tm,),
        in_specs=[pl.BlockSpec((tm, D), row), pl.BlockSpec((tm, 1), row),
                  pl.BlockSpec((1, D), fixed), pl.BlockSpec((1, D), fixed),
                  pl.BlockSpec((D, lat), fixed), pl.BlockSpec((1, lat), fixed),
                  pl.BlockSpec((lat, kw + MAIN_WIDTH), fixed), lane_row,
                  pl.BlockSpec((D, n_in), fixed), pl.BlockSpec((1, Q_LORA), fixed),
                  pl.BlockSpec((Q_LORA, kw), fixed), lane_row,
                  lane_row, lane_row, lane_row],
        out_specs=[pl.BlockSpec((tm, kw), row), pl.BlockSpec((tm, MAIN_WIDTH), row),
                   pl.BlockSpec((tm, kw), row), pl.BlockSpec((tm, MEM_WIDTH), row)],
        out_shape=[jax.ShapeDtypeStruct((T, kw), BF16), jax.ShapeDtypeStruct((T, MAIN_WIDTH), BF16),
                   jax.ShapeDtypeStruct((T, kw), BF16), jax.ShapeDtypeStruct((T, MEM_WIDTH), BF16)],
        compiler_params=_cparams(("parallel",)),
        name="mla_qkv",
    )(x2, pos_col, kv_ln[None, :], ln1[None, :], wd, scale_row, wkv, _head_gain_row(k_g),
      w_in.astype(BF16), q_lora_g[None, :], wuq, _head_gain_row(q_g), inv_row, rope_row, real_row)


ATTN_TILE = 1024
ATTN_PAIRS_PER_STEP = 2


def _flash_kernel(qi_ref, kj_ref, q_ref, k_ref, v_ref, o_ref, m_ref, acc_ref):
    t = pl.program_id(2)
    i = qi_ref[t]
    j = kj_ref[t]
    tq = q_ref.shape[1]
    tk = k_ref.shape[1]

    @pl.when(j == 0)
    def _():
        m_ref[...] = jnp.full_like(m_ref, NEG_BIG)
        acc_ref[...] = jnp.zeros_like(acc_ref)

    lane = lax.broadcasted_iota(jnp.int32, (1, LANES), 1)
    head_a = lane < V_HEAD
    den_lane = (V_HEAD, 0)

    def attend(pair, q0, nq, nk, diag_col):
        rows = pl.ds(q0, nq)
        v = v_ref[0, pl.ds(0, nk), pair * LANES:(pair + 1) * LANES]
        if diag_col is not None:
            q_idx = lax.broadcasted_iota(jnp.int32, (nq, nk), 0) + diag_col
            k_idx = lax.broadcasted_iota(jnp.int32, (nq, nk), 1)
            visible = k_idx <= q_idx
        H = range(2)
        st = [2 * pair + hh for hh in H]
        sl = [slice(h * HEAD_PAD, (h + 1) * HEAD_PAD) for h in st]
        s = [_dot_nt(q_ref[0, rows, sl[hh]], k_ref[0, pl.ds(0, nk), sl[hh]]) for hh in H]
        if diag_col is not None:
            s = [jnp.where(visible, s[hh], NEG_BIG) for hh in H]
        m_prev = [m_ref[st[hh], rows, :] for hh in H]
        acc_prev = [acc_ref[st[hh], rows, :] for hh in H]
        m_new = [jnp.maximum(m_prev[hh], jnp.max(s[hh], axis=-1, keepdims=True)) for hh in H]
        alpha = [jnp.exp(m_prev[hh] - m_new[hh]) for hh in H]
        m_wide = [jnp.concatenate([m_new[hh]] * (nk // LANES), axis=1) for hh in H]
        p = [jnp.exp((s[hh] - m_wide[hh]).astype(BF16)) for hh in H]
        keep_row = [jnp.where(head_a, 1.0, 0.0).astype(BF16), jnp.where(head_a, 0.0, 1.0).astype(BF16)]
        den_row = [jnp.where(lane == den_lane[hh], 1.0, 0.0).astype(BF16) for hh in H]
        pv = [_dot(p[hh], v * keep_row[hh] + den_row[hh]) for hh in H]
        for hh in H:
            acc_ref[st[hh], rows, :] = alpha[hh] * acc_prev[hh] + pv[hh]
            m_ref[st[hh], rows, :] = m_new[hh]

    n_pairs = v_ref.shape[2] // LANES

    @pl.when(j < i)
    def _():
        for pair in range(n_pairs):
            attend(pair, 0, tq, tk, None)

    @pl.when(j == i)
    def _():
        half = tq // 2
        for pair in range(n_pairs):
            attend(pair, 0, half, half, 0)
            attend(pair, half, half, tk, half)
            acc_a = acc_ref[2 * pair]
            acc_b = acc_ref[2 * pair + 1]
            out_a = acc_a / acc_a[:, den_lane[0]:den_lane[0] + 1]
            out_b = acc_b / acc_b[:, den_lane[1]:den_lane[1] + 1]
            o_ref[0, :, pair * LANES:(pair + 1) * LANES] = jnp.where(head_a, out_a, out_b).astype(BF16)


def _flash(q, k, v):
    B, S, _ = q.shape
    t = min(ATTN_TILE, S)
    n = S // t
    g = ATTN_PAIRS_PER_STEP
    pairs = [(i, j) for i in range(n) for j in range(i + 1)]
    qi = jnp.array([p[0] for p in pairs], jnp.int32)
    kj = jnp.array([p[1] for p in pairs], jnp.int32)
    return pl.pallas_call(
        _flash_kernel,
        grid_spec=pltpu.PrefetchScalarGridSpec(
            num_scalar_prefetch=2,
            grid=(B, N_PAIRS // g, len(pairs)),
            in_specs=[pl.BlockSpec((1, t, 2 * g * HEAD_PAD), lambda b, p, s, qi, kj: (b, qi[s], p)),
                      pl.BlockSpec((1, t, 2 * g * HEAD_PAD), lambda b, p, s, qi, kj: (b, kj[s], p)),
                      pl.BlockSpec((1, t, g * LANES), lambda b, p, s, qi, kj: (b, kj[s], p))],
            out_specs=pl.BlockSpec((1, t, g * LANES), lambda b, p, s, qi, kj: (b, qi[s], p)),
            scratch_shapes=[pltpu.VMEM((2 * g, t, LANES), F32), pltpu.VMEM((2 * g, t, LANES), F32)],
        ),
        out_shape=jax.ShapeDtypeStruct((B, S, MAIN_WIDTH), BF16),
        compiler_params=_cparams(("parallel", "parallel", "arbitrary")),
        name="mla_flash",
    )(qi, kj, q, k, v)


def kernel(x, mem, positions, ln1, ln2, w_out, mem_w_kv, mem_q_norm, mem_k_norm, router_group_w, router_group_b, router_expert_w, router_expert_b, expert_w_gate, expert_w_up, expert_w_down, ret_w_in, ret_gn, kv_ln, kv_w_down, kv_lora_norm, kv_w_up, k_norm, mla_w_in, q_lora_norm, mla_w_uq, q_norm):
    B, S, D = x.shape
    M = mem.shape[1]
    T = B * S
    x2 = x.reshape(T, D)
    mem2 = mem.reshape(B * M, D)
    pos_col = positions.reshape(T, 1).astype(jnp.int32)

    def mem_path(i, qm):
        k_m, v_m = _mem_kv(mem2, mem_w_kv[i], mem_k_norm[i])
        return _mem_attn(qm.reshape(B, S, MEM_WIDTH), k_m.reshape(B, M, MEM_WIDTH),
                         v_m.reshape(B, M, MEM_WIDTH), mem_q_norm[i]).reshape(T, MEM_WIDTH)

    def mix_out_and_moe(i, xin, y, m):
        x1, info, counts, tile_base = _out_proj_router(
            xin, y, m, w_out[i], ln2[i], router_group_w[i], router_group_b[i], router_expert_w[i],
            router_expert_b[i])
        return _moe_apply(x1, ln2[i], info, counts, tile_base, expert_w_gate, expert_w_up, expert_w_down, i)

    q, k, v, gt, qm = _ret_inproj(x2, pos_col, ln1[0], ret_w_in[0])
    shp = (B, S, MAIN_WIDTH)
    y = _retention(q.reshape(shp), k.reshape(shp), v.reshape(shp), gt.reshape(shp), ret_gn[0])
    x2 = mix_out_and_moe(0, x2, y.reshape(T, MAIN_WIDTH), mem_path(0, qm))

    k_sh, v_sh, q1, qm1 = _mla_qkv(x2, pos_col, kv_ln, kv_w_down, kv_lora_norm, kv_w_up, k_norm,
                                   ln1[1], mla_w_in[0], q_lora_norm[0], mla_w_uq[0], q_norm[0])
    kw = N_MAIN_HEADS * HEAD_PAD
    y1 = _flash(q1.reshape(B, S, kw), k_sh.reshape(B, S, kw), v_sh.reshape(shp))
    x2 = mix_out_and_moe(1, x2, y1.reshape(T, MAIN_WIDTH), mem_path(1, qm1))
    return x2.reshape(B, S, D)
```

```python
import math

import jax
import jax.numpy as jnp
from jax import lax
from jax.experimental import pallas as pl
from jax.experimental.pallas import tpu as pltpu

F32 = jnp.float32
BF16 = jnp.bfloat16

HEAD_DIM = 64
N_MAIN_HEADS = 12
MAIN_WIDTH = N_MAIN_HEADS * HEAD_DIM
N_MEM_HEADS = 4
MEM_WIDTH = N_MEM_HEADS * HEAD_DIM
RET_CHUNK = 128
ROPE_THETA = 10000.0
Q_LORA = 384
KV_LORA = 256
QK_NOPE = 64
QK_ROPE = 32
QK_HEAD = QK_NOPE + QK_ROPE
V_HEAD = 64
N_GROUPS = 4
EXPERTS_PER_GROUP = 8
N_EXPERTS = N_GROUPS * EXPERTS_PER_GROUP
D_EXPERT = 256
MOE_BLOCK = 512
EPS = 1e-6

LANES = 128
VMEM_LIMIT = 48 * 1024 * 1024
NEG_BIG = -1e30

N_PAIRS = N_MAIN_HEADS // 2
ROW_TILE = 512
MOE_TILE = 256
DMA_UNROLL = 8


def _cparams(sem):
    return pltpu.CompilerParams(dimension_semantics=sem, vmem_limit_bytes=VMEM_LIMIT)


def _dot(a, b):
    return jnp.dot(a, b, preferred_element_type=F32)


def _dot_nt(a, b):
    return lax.dot_general(a, b, (((1,), (1,)), ((), ())), preferred_element_type=F32)


def _dot_tn(a, b):
    return lax.dot_general(a, b, (((0,), (0,)), ((), ())), preferred_element_type=F32)


def _rmsnorm(xf, g):
    return xf * lax.rsqrt(jnp.mean(xf * xf, axis=-1, keepdims=True) + EPS) * g


def _rope_tables(pos_col, inv_row, sgn_row):
    ang = pos_col.astype(F32) * inv_row
    return jnp.cos(ang), jnp.sin(ang) * sgn_row


def _ret_inproj_kernel(x_ref, pos_ref, g_ref, inv_ref, sgn_ref, w_ref,
                       q_ref, k_ref, v_ref, gt_ref, qm_ref):
    h = _rmsnorm(x_ref[...], g_ref[...]).astype(BF16)
    cos, sin_s = _rope_tables(pos_ref[...], inv_ref[...], sgn_ref[...])
    first = sgn_ref[...] < 0.0
    mw = MAIN_WIDTH
    half = HEAD_DIM // 2
    qk = _dot(h, w_ref[:, :2 * mw])
    groups = range(2 * mw // LANES)
    ys = [qk[:, j * LANES:(j + 1) * LANES] for j in groups]
    fwd = [pltpu.roll(ys[j], LANES - half, 1) for j in groups]
    bwd = [pltpu.roll(ys[j], half, 1) for j in groups]
    outs = [(ys[j] * cos + jnp.where(first, fwd[j], bwd[j]) * sin_s).astype(BF16) for j in groups]
    n_q = mw // LANES
    for j in range(n_q):
        q_ref[:, j * LANES:(j + 1) * LANES] = outs[j]
        k_ref[:, j * LANES:(j + 1) * LANES] = outs[n_q + j]
    v_ref[...] = _dot(h, w_ref[:, 2 * mw:3 * mw]).astype(BF16)
    gt_ref[...] = _dot(h, w_ref[:, 3 * mw:4 * mw]).astype(BF16)
    qm_ref[...] = _dot(h, w_ref[:, 4 * mw:4 * mw + MEM_WIDTH]).astype(BF16)


def _ret_inproj(x2, pos_col, g, w_in):
    T, D = x2.shape
    tm = min(ROW_TILE, T)
    half = HEAD_DIM // 2
    inv = ROPE_THETA ** (-jnp.arange(half, dtype=F32) / half)
    lane = jnp.arange(LANES)
    inv_row = inv[lane % half][None, :]
    sgn_row = jnp.where((lane % HEAD_DIM) < half, -1.0, 1.0).astype(F32)[None, :]
    row = lambda i: (i, 0)
    fixed = lambda i: (0, 0)
    n_in = w_in.shape[1]
    outs = pl.pallas_call(
        _ret_inproj_kernel,
        grid=(T // tm,),
        in_specs=[
            pl.BlockSpec((tm, D), row),
            pl.BlockSpec((tm, 1), row),
            pl.BlockSpec((1, D), fixed),
            pl.BlockSpec((1, LANES), fixed),
            pl.BlockSpec((1, LANES), fixed),
            pl.BlockSpec((D, n_in), fixed),
        ],
        out_specs=[pl.BlockSpec((tm, MAIN_WIDTH), row)] * 4 + [pl.BlockSpec((tm, MEM_WIDTH), row)],
        out_shape=[jax.ShapeDtypeStruct((T, MAIN_WIDTH), BF16)] * 4
        + [jax.ShapeDtypeStruct((T, MEM_WIDTH), BF16)],
        compiler_params=_cparams(("parallel",)),
        name="ret_inproj",
    )(x2, pos_col, g[None, :], inv_row, sgn_row, w_in.astype(BF16))
    return outs


def _retention_tables():
    H, C, d = N_MAIN_HEADS, RET_CHUNK, HEAD_DIM
    log_g = jnp.log1p(-jnp.exp2(-5.0 - jnp.arange(H, dtype=F32)))
    idx = jnp.arange(C, dtype=F32)
    rel = idx[:, None] - idx[None, :]
    scale = d ** -0.5
    decay_in = jnp.where(rel[None] >= 0,
                         jnp.exp(log_g[:, None, None] * jnp.maximum(rel, 0.0)[None]), 0.0) * scale
    kdec = jnp.exp(log_g[None, :] * (C - 1.0 - idx)[:, None]) * scale
    qdec = jnp.exp(log_g[None, :] * (idx + 1.0)[:, None])
    cdec = jnp.exp(log_g * C)

    def lanes(t):
        return jnp.repeat(t, d, axis=1).reshape(C, N_PAIRS, 2 * d).transpose(1, 0, 2)

    head_of_lane = jnp.arange(2 * d) // d
    same = (head_of_lane[:, None] == head_of_lane[None, :]).astype(F32)
    cd_lane = jnp.repeat(cdec, d).reshape(N_PAIRS, 2 * d)
    state_decay = cd_lane[:, :, None] * same[None]
    decay_in = decay_in.reshape(N_PAIRS, 2, C, C)
    return decay_in, lanes(kdec), lanes(qdec), state_decay, same


RET_CHUNKS_PER_STEP = 4


def _retention_kernel(q_ref, k_ref, v_ref, gt_ref, dm_ref, kd_ref, qd_ref, sd_ref, same_ref,
                      gn_ref, o_ref, r_ref):
    n = pl.program_id(1)

    @pl.when(n == 0)
    def _():
        r_ref[...] = jnp.zeros_like(r_ref)

    lane = lax.broadcasted_iota(jnp.int32, (1, LANES), 1)
    head_a = lane < HEAD_DIM
    keep_a = jnp.where(head_a, 1.0, 0.0).astype(BF16)
    keep_b = jnp.where(head_a, 0.0, 1.0).astype(BF16)
    same = same_ref[...]
    avg = (same * (1.0 / HEAD_DIM)).astype(BF16)
    C = RET_CHUNK
    n_chunks = q_ref.shape[1] // C
    P = range(N_PAIRS)
    CP = [(c, p) for c in range(n_chunks) for p in P]
    rows = [pl.ds(c * C, C) for c in range(n_chunks)]
    sl = [slice(p * LANES, (p + 1) * LANES) for p in P]
    q = {cp: q_ref[0, rows[cp[0]], sl[cp[1]]] for cp in CP}
    k = {cp: k_ref[0, rows[cp[0]], sl[cp[1]]] for cp in CP}
    v = {cp: v_ref[0, rows[cp[0]], sl[cp[1]]] for cp in CP}
    s_a = {cp: _dot_nt(q[cp] * keep_a, k[cp]) * dm_ref[cp[1], 0] for cp in CP}
    s_b = {cp: _dot_nt(q[cp] * keep_b, k[cp]) * dm_ref[cp[1], 1] for cp in CP}
    u = {cp: _dot_tn((k[cp].astype(F32) * kd_ref[cp[1]]).astype(BF16), v[cp]) for cp in CP}
    qd = {cp: (q[cp].astype(F32) * qd_ref[cp[1]]).astype(BF16) for cp in CP}
    inner = {cp: _dot(s_a[cp].astype(BF16), v[cp] * keep_a) + _dot(s_b[cp].astype(BF16), v[cp] * keep_b)
             for cp in CP}
    state = [r_ref[p] for p in P]
    y = {}
    for c in range(n_chunks):
        for p in P:
            y[(c, p)] = inner[(c, p)] + _dot(qd[(c, p)], state[p].astype(BF16))
        state = [sd_ref[p] * state[p] + same * u[(c, p)] for p in P]
    for p in P:
        r_ref[p] = state[p]
    yc = {cp: y[cp] - _dot(y[cp].astype(BF16), avg) for cp in CP}
    var = {cp: _dot((yc[cp] * yc[cp]).astype(BF16), avg) for cp in CP}
    for c, p in CP:
        yn = yc[(c, p)] * lax.rsqrt(var[(c, p)] + EPS) * gn_ref[:, sl[p]]
        g = gt_ref[0, rows[c], sl[p]].astype(F32)
        o_ref[0, rows[c], sl[p]] = (g / (1.0 + jnp.exp(-g)) * yn).astype(BF16)


def _retention(q, k, v, gt, ret_gn):
    B, S, W = q.shape
    C = RET_CHUNK
    dm, kd, qd, sd, same = _retention_tables()
    tok = lambda b, n: (b, n, 0)
    fixed3 = lambda b, n: (0, 0, 0)
    step = RET_CHUNKS_PER_STEP * C
    return pl.pallas_call(
        _retention_kernel,
        grid=(B, S // step),
        in_specs=[pl.BlockSpec((1, step, W), tok)] * 4 + [
            pl.BlockSpec((N_PAIRS, 2, C, C), lambda b, n: (0, 0, 0, 0)),
            pl.BlockSpec((N_PAIRS, C, LANES), fixed3),
            pl.BlockSpec((N_PAIRS, C, LANES), fixed3),
            pl.BlockSpec((N_PAIRS, LANES, LANES), fixed3),
            pl.BlockSpec((LANES, LANES), lambda b, n: (0, 0)),
            pl.BlockSpec((1, W), lambda b, n: (0, 0)),
        ],
        out_specs=pl.BlockSpec((1, step, W), tok),
        out_shape=jax.ShapeDtypeStruct((B, S, W), BF16),
        scratch_shapes=[pltpu.VMEM((N_PAIRS, LANES, LANES), F32)],
        compiler_params=_cparams(("parallel", "arbitrary")),
        name="retention",
    )(q, k, v, gt, dm, kd, qd, sd, same, ret_gn[None, :])


def _mem_kv_kernel(mem_ref, w_ref, kg_ref, k_ref, v_ref):
    kv = _dot(mem_ref[...].astype(BF16), w_ref[...])
    lane = lax.broadcasted_iota(jnp.int32, (1, LANES), 1)
    head_a = lane < HEAD_DIM
    inv_d = 1.0 / HEAD_DIM
    for j in range(MEM_WIDTH // LANES):
        kj = kv[:, j * LANES:(j + 1) * LANES]
        k2 = kj * kj
        ms_a = jnp.sum(jnp.where(head_a, k2, 0.0), axis=-1, keepdims=True) * inv_d
        ms_b = jnp.sum(jnp.where(head_a, 0.0, k2), axis=-1, keepdims=True) * inv_d
        kn = kj * lax.rsqrt(jnp.where(head_a, ms_a, ms_b) + EPS) * kg_ref[...]
        k_ref[:, j * LANES:(j + 1) * LANES] = kn.astype(BF16)
    v_ref[...] = kv[:, MEM_WIDTH:].astype(BF16)


def _mem_kv(mem2, w_mem_kv, k_g):
    TM, D = mem2.shape
    tm = min(ROW_TILE, TM)
    kg_row = jnp.tile(k_g, LANES // HEAD_DIM)[None, :]
    row = lambda i: (i, 0)
    fixed = lambda i: (0, 0)
    return pl.pallas_call(
        _mem_kv_kernel,
        grid=(TM // tm,),
        in_specs=[pl.BlockSpec((tm, D), row), pl.BlockSpec((D, 2 * MEM_WIDTH), fixed),
                  pl.BlockSpec((1, LANES), fixed)],
        out_specs=[pl.BlockSpec((tm, MEM_WIDTH), row)] * 2,
        out_shape=[jax.ShapeDtypeStruct((TM, MEM_WIDTH), BF16)] * 2,
        compiler_params=_cparams(("parallel",)),
        name="mem_kv",
    )(mem2, w_mem_kv.astype(BF16), kg_row)


def _mem_attn_kernel(qm_ref, k_ref, v_ref, qg_ref, o_ref):
    lane = lax.broadcasted_iota(jnp.int32, (1, LANES), 1)
    head_a = lane < HEAD_DIM
    keep = [jnp.where(head_a, 1.0, 0.0).astype(BF16), jnp.where(head_a, 0.0, 1.0).astype(BF16)]
    r_head = lax.broadcasted_iota(jnp.int32, (LANES, LANES), 0) < HEAD_DIM
    c_head = lax.broadcasted_iota(jnp.int32, (LANES, LANES), 1) < HEAD_DIM
    avg = jnp.where(r_head == c_head, 1.0 / HEAD_DIM, 0.0).astype(BF16)
    n_mem = k_ref.shape[1]
    ones = jnp.ones((n_mem, LANES), BF16)
    scale = HEAD_DIM ** -0.5
    G = range(MEM_WIDTH // LANES)
    sl = [slice(j * LANES, (j + 1) * LANES) for j in G]
    q = [qm_ref[0, :, sl[j]].astype(F32) for j in G]
    ms = [_dot((q[j] * q[j]).astype(BF16), avg) for j in G]
    qn = [(q[j] * lax.rsqrt(ms[j] + EPS) * (qg_ref[...] * scale)).astype(BF16) for j in G]
    heads = [(j, hh) for j in G for hh in range(2)]
    s = {jh: _dot_nt(qn[jh[0]] * keep[jh[1]], k_ref[0, :, sl[jh[0]]]) for jh in heads}
    p = {jh: jnp.exp(s[jh] - jnp.max(s[jh], axis=-1, keepdims=True)).astype(BF16) for jh in heads}
    den = {jh: _dot(p[jh], ones) for jh in heads}
    pv = {jh: _dot(p[jh], v_ref[0, :, sl[jh[0]]] * keep[jh[1]]) for jh in heads}
    for j in G:
        o_ref[0, :, sl[j]] = (pv[(j, 0)] / den[(j, 0)] + pv[(j, 1)] / den[(j, 1)]).astype(BF16)


def _mem_attn(qm, k_m, v_m, q_g):
    B, S, _ = qm.shape
    M = k_m.shape[1]
    tm = min(2 * ROW_TILE, S)
    qg_row = jnp.tile(q_g, LANES // HEAD_DIM)[None, :]
    return pl.pallas_call(
        _mem_attn_kernel,
        grid=(B, S // tm),
        in_specs=[pl.BlockSpec((1, tm, MEM_WIDTH), lambda b, i: (b, i, 0)),
                  pl.BlockSpec((1, M, MEM_WIDTH), lambda b, i: (b, 0, 0)),
                  pl.BlockSpec((1, M, MEM_WIDTH), lambda b, i: (b, 0, 0)),
                  pl.BlockSpec((1, LANES), lambda b, i: (0, 0))],
        out_specs=pl.BlockSpec((1, tm, MEM_WIDTH), lambda b, i: (b, i, 0)),
        out_shape=jax.ShapeDtypeStruct((B, S, MEM_WIDTH), BF16),
        compiler_params=_cparams(("parallel", "parallel")),
        name="mem_attn",
    )(qm, k_m, v_m, qg_row)


ROUTER_LANE0 = N_GROUPS
RANK_BITS = 17
RANK_RADIX = 1 << RANK_BITS


def _proj_router_kernel(x_ref, y_ref, m_ref, wy_ref, wm_ref, g_ref, w2_ref, b_ref, lower_ref,
                        o_ref, info_ref, cnt_ref, tile_base_ref, base_ref):
    x1 = x_ref[...] + _dot(y_ref[...], wy_ref[...]) + _dot(m_ref[...], wm_ref[...])
    o_ref[...] = x1
    _route_tile(x1, g_ref, w2_ref, b_ref, lower_ref, info_ref, cnt_ref, tile_base_ref, base_ref)


def _route_tile(x, g_ref, w2_ref, b_ref, lower_ref, info_ref, cnt_ref, tile_base_ref, base_ref):
    i = pl.program_id(0)

    @pl.when(i == 0)
    def _():
        base_ref[...] = jnp.zeros_like(base_ref)

    tm = x.shape[0]
    h = _rmsnorm(x, g_ref[...])
    h_hi = h.astype(BF16)
    h_lo = (h - h_hi.astype(F32)).astype(BF16)
    both = _dot(h_hi, w2_ref[...])
    logits = both[:, :LANES] + both[:, LANES:] + _dot(h_lo, w2_ref[:, :LANES]) + b_ref[...]
    lane_i = lax.broadcasted_iota(jnp.int32, (tm, LANES), 1)
    lane = lane_i.astype(F32)
    big = float(LANES)

    is_g = lane_i < N_GROUPS
    lg = jnp.where(is_g, logits, NEG_BIG)
    mg = jnp.max(lg, axis=-1, keepdims=True)
    zg = jnp.sum(jnp.where(is_g, jnp.exp(lg - mg), 0.0), axis=-1, keepdims=True)
    p_grp = 1.0 / zg
    grp = jnp.min(jnp.where(is_g & (lg == mg), lane, big), axis=-1, keepdims=True)

    e_lane = lane_i - ROUTER_LANE0
    e_grp = (e_lane >> int(math.log2(EXPERTS_PER_GROUP))).astype(F32)
    is_e = (e_lane >= 0) & (e_lane < N_EXPERTS) & (e_grp == grp)
    le = jnp.where(is_e, logits, NEG_BIG)
    me = jnp.max(le, axis=-1, keepdims=True)
    ee = jnp.where(is_e, jnp.exp(le - me), 0.0)
    prob = ee / jnp.sum(ee, axis=-1, keepdims=True)
    p1 = jnp.max(prob, axis=-1, keepdims=True)
    i1 = jnp.min(jnp.where(is_e & (prob == p1), lane, big), axis=-1, keepdims=True)
    rest = is_e & (lane != i1)
    p2 = jnp.max(jnp.where(rest, prob, -1.0), axis=-1, keepdims=True)
    i2 = jnp.min(jnp.where(rest & (prob == p2), lane, big), axis=-1, keepdims=True)
    gate1 = p_grp * p1 / (p1 + p2)
    gate2 = p_grp * p2 / (p1 + p2)

    sel1 = lane == i1
    sel2 = lane == i2
    onehot = jnp.where(sel1 | sel2, 1.0, 0.0)
    tile_base_ref[0] = base_ref[...]
    before = _dot(lower_ref[...], onehot.astype(BF16)) + base_ref[...]
    rank1 = jnp.sum(jnp.where(sel1, before, 0.0), axis=-1, keepdims=True)
    rank2 = jnp.sum(jnp.where(sel2, before, 0.0), axis=-1, keepdims=True)
    base_ref[...] += jnp.sum(onehot, axis=0, keepdims=True)
    cnt_ref[...] = base_ref[...]

    code1 = (i1 - float(ROUTER_LANE0)) * float(RANK_RADIX) + rank1
    code2 = (i2 - float(ROUTER_LANE0)) * float(RANK_RADIX) + rank2
    info = jnp.zeros((tm, LANES), F32)
    for col, val in enumerate((gate1, gate2, code1, code2)):
        info = jnp.where(lane_i == col, val, info)
    info_ref[...] = info


def _router_operands(D, tm, g, w_grp, b_grp, w_exp, b_exp):
    w = jnp.zeros((D, LANES), F32)
    w = w.at[:, :N_GROUPS].set(w_grp).at[:, ROUTER_LANE0:ROUTER_LANE0 + N_EXPERTS].set(w_exp)
    b = jnp.zeros((1, LANES), F32)
    b = b.at[0, :N_GROUPS].set(b_grp).at[0, ROUTER_LANE0:ROUTER_LANE0 + N_EXPERTS].set(b_exp)
    w_hi = w.astype(BF16)
    w_lo = (w - w_hi.astype(F32)).astype(BF16)
    lower = (jnp.arange(tm)[:, None] > jnp.arange(tm)[None, :]).astype(BF16)
    fixed = lambda i: (0, 0)
    specs = [pl.BlockSpec((1, D), fixed), pl.BlockSpec((D, 2 * LANES), fixed), pl.BlockSpec((1, LANES), fixed),
             pl.BlockSpec((tm, tm), fixed)]
    return (g[None, :], jnp.concatenate([w_hi, w_lo], axis=1), b, lower), specs


def _router_outputs(T, tm):
    n_tiles = T // tm
    specs = [pl.BlockSpec((tm, LANES), lambda i: (i, 0)), pl.BlockSpec((1, LANES), lambda i: (0, 0)),
             pl.BlockSpec((1, 1, LANES), lambda i: (i, 0, 0))]
    shapes = [jax.ShapeDtypeStruct((T, LANES), F32), jax.ShapeDtypeStruct((1, LANES), F32),
              jax.ShapeDtypeStruct((n_tiles, 1, LANES), F32)]
    return specs, shapes


def _router_tables(cnt, tile_base):
    experts = slice(ROUTER_LANE0, ROUTER_LANE0 + N_EXPERTS)
    return cnt[0, experts].astype(jnp.int32), tile_base[:, 0, experts].astype(jnp.int32)


def _out_proj_router(x2, y2, m2, w_out, g, w_grp, b_grp, w_exp, b_exp):
    T, D = x2.shape
    tm = min(ROW_TILE, T)
    row = lambda i: (i, 0)
    fixed = lambda i: (0, 0)
    w = w_out.astype(BF16)
    ops, op_specs = _router_operands(D, tm, g, w_grp, b_grp, w_exp, b_exp)
    out_specs, out_shapes = _router_outputs(T, tm)
    x1, info, cnt, tile_base = pl.pallas_call(
        _proj_router_kernel,
        grid=(T // tm,),
        in_specs=[pl.BlockSpec((tm, D), row), pl.BlockSpec((tm, MAIN_WIDTH), row),
                  pl.BlockSpec((tm, MEM_WIDTH), row),
                  pl.BlockSpec((MAIN_WIDTH, D), fixed), pl.BlockSpec((MEM_WIDTH, D), fixed)] + op_specs,
        out_specs=[pl.BlockSpec((tm, D), row)] + out_specs,
        out_shape=[jax.ShapeDtypeStruct((T, D), F32)] + out_shapes,
        scratch_shapes=[pltpu.VMEM((1, LANES), F32)],
        compiler_params=_cparams(("arbitrary",)),
        name="out_proj_router",
    )(x2, y2, m2, w[:MAIN_WIDTH], w[MAIN_WIDTH:], *ops)
    return (x1, info) + _router_tables(cnt, tile_base)


TOK_ROWS = 4
U32 = jnp.uint32


def _tok_rows(r, n=1):
    start = r * TOK_ROWS
    if not isinstance(start, int):
        start = pl.multiple_of(start, TOK_ROWS)
    return pl.ds(start, n * TOK_ROWS)


def _pack_rows(h):
    bits = lax.bitcast_convert_type(h.astype(BF16).astype(F32), U32)
    half = h.shape[1] // 2
    return (bits[:, :half] >> 16) | bits[:, half:]


def _unpack_words(w):
    return (lax.bitcast_convert_type(w << 16, F32), lax.bitcast_convert_type(w & U32(0xFFFF0000), F32))


def _store_token_rows(ref, first_row, n, words):
    for c in range(TOK_ROWS):
        ref[pl.ds(first_row * TOK_ROWS + c, n, stride=TOK_ROWS), :] = words[:, c * LANES:(c + 1) * LANES]


def _load_token_rows(ref, n):
    parts = [_unpack_words(ref[pl.ds(c, n, stride=TOK_ROWS), :]) for c in range(TOK_ROWS)]
    return jnp.concatenate([p[0] for p in parts] + [p[1] for p in parts], axis=-1)


def _segment_copies(n, src, src_row, dst, dst_row, sem, top, op, grain=1):
    groups = (n + (grain - 1)) >> int(math.log2(grain))
    off = 0
    bit = top // grain
    while bit >= 1:
        take = groups & bit

        @pl.when(take != 0)
        def _(rows=bit * grain, off=off):
            op(pltpu.make_async_copy(src.at[_tok_rows(src_row + off, rows), :],
                                     dst.at[_tok_rows(dst_row + off, rows), :], sem))
        off = off + take * grain
        bit //= 2


ENTRIES_PER_ITER = 2 * DMA_UNROLL


def _local_rows(code_ref, a_ref, tbl, first_tok, dvec_ref, drow_ref, dsem):
    code = code_ref[...]
    e = code >> RANK_BITS
    d = (code & (RANK_RADIX - 1)) + first_tok
    for k in range(N_EXPERTS):
        d = d + jnp.where(e == k, a_ref[tbl + k], 0)
    dvec_ref[...] = d * TOK_ROWS
    copies = [pltpu.make_async_copy(dvec_ref.at[r], drow_ref.at[pl.ds(r * LANES, LANES)], dsem)
              for r in range(dvec_ref.shape[0])]
    for cp in copies:
        cp.start()
    for cp in copies:
        cp.wait()


def _run_copies(op, step, to_sorted, n_ref, ls_ref, gb_ref, local_ref, first_tok, sorted_ref, sem, top,
                grain=1):
    tbl = step * N_EXPERTS

    def run(e, c):
        loc = (local_ref, first_tok + ls_ref[tbl + e])
        glob = (sorted_ref, gb_ref[tbl + e])
        (src, src_row), (dst, dst_row) = (loc, glob) if to_sorted else (glob, loc)
        _segment_copies(n_ref[tbl + e], src, src_row, dst, dst_row, sem, top, op, grain)
        return c
    lax.fori_loop(0, N_EXPERTS, run, 0)


def _rows_at(first_row):
    return pl.ds(pl.multiple_of(first_row, TOK_ROWS), TOK_ROWS)


def _dma_start(cp):
    cp.start()


def _dma_wait(cp):
    cp.wait()


def _block_copy(src, dst, blk, sem):
    return pltpu.make_async_copy(src, dst.at[_tok_rows(blk * MOE_BLOCK, MOE_BLOCK), :], sem)


def _dispatch_kernel(a_ref, n_ref, ls_ref, gb_ref, zrow_ref, zcnt_ref, nblk_ref,
                     code_ref, x_ref, g_ref, xs_ref, hbuf_ref, cbuf_ref, zbuf_ref, dvec_ref, drow_ref,
                     sem, zsem, dsem):
    s = pl.program_id(0)
    ts = x_ref.shape[0]

    @pl.when(s == 0)
    def _():
        zbuf_ref[...] = jnp.zeros_like(zbuf_ref)
        n_blocks = xs_ref.shape[0] // (MOE_BLOCK * TOK_ROWS)
        for op in (_dma_start, _dma_wait):
            def tail(b, c, op=op):
                op(_block_copy(zbuf_ref, xs_ref, b, zsem))
                return c
            lax.fori_loop(nblk_ref[0], n_blocks, tail, 0)

            def pad(e, c, op=op):
                _segment_copies(zcnt_ref[e], zbuf_ref, 0, xs_ref, zrow_ref[e], zsem, MOE_BLOCK // 2, op)
                return c
            lax.fori_loop(0, N_EXPERTS, pad, 0)

    for sub in range(ts // MOE_TILE):
        h = _rmsnorm(x_ref[pl.ds(sub * MOE_TILE, MOE_TILE), :], g_ref[...])
        _store_token_rows(hbuf_ref, sub * MOE_TILE, MOE_TILE, _pack_rows(h))

    slot = s % 2
    first_tok = slot * (2 * ts)
    last = pl.num_programs(0) - 1

    def start_runs(step, step_slot):
        _run_copies(_dma_start, step, True, n_ref, ls_ref, gb_ref, cbuf_ref, step_slot * (2 * ts), xs_ref,
                    sem.at[step_slot], ts)

    def wait_runs(step_slot):
        pltpu.make_async_copy(cbuf_ref.at[_tok_rows(step_slot * (2 * ts), 2 * ts), :],
                              xs_ref.at[_tok_rows(0, 2 * ts), :], sem.at[step_slot]).wait()

    @pl.when(s >= 2)
    def _():
        wait_runs(slot)

    _local_rows(code_ref, a_ref, s * N_EXPERTS, first_tok, dvec_ref, drow_ref, dsem)

    def place(tb, c):
        first = tb * ENTRIES_PER_ITER
        for u in range(DMA_UNROLL):
            tile = hbuf_ref[_tok_rows(tb * DMA_UNROLL + u), :]
            for kk in range(2):
                cbuf_ref[_rows_at(drow_ref[first + u * 2 + kk]), :] = tile
        return c
    lax.fori_loop(0, ts // DMA_UNROLL, place, 0)

    start_runs(s, slot)

    @pl.when(s == last)
    def _():
        @pl.when(s >= 1)
        def _():
            wait_runs(1 - slot)
        wait_runs(slot)


def _dispatch(x2, g, codes, tables, zrow, zcnt, nblk_used, n_rows, ts):
    T, D = x2.shape
    return pl.pallas_call(
        _dispatch_kernel,
        grid_spec=pltpu.PrefetchScalarGridSpec(
            num_scalar_prefetch=7,
            grid=(T // ts,),
            in_specs=[pl.BlockSpec((ts * 2 // LANES, LANES), lambda i, *_: (i, 0)),
                      pl.BlockSpec((ts, D), lambda i, *_: (i, 0)),
                      pl.BlockSpec((1, D), lambda i, *_: (0, 0))],
            out_specs=pl.BlockSpec(memory_space=pl.ANY),
            scratch_shapes=[pltpu.VMEM((ts * TOK_ROWS, LANES), U32),
                            pltpu.VMEM((2 * 2 * ts * TOK_ROWS, LANES), U32),
                            pltpu.VMEM((MOE_BLOCK * TOK_ROWS, LANES), U32),
                            pltpu.VMEM((ts * 2 // LANES, LANES), jnp.int32),
                            pltpu.SMEM((ts * 2,), jnp.int32),
                            pltpu.SemaphoreType.DMA((2,)), pltpu.SemaphoreType.DMA(()),
                            pltpu.SemaphoreType.DMA(())],
        ),
        out_shape=jax.ShapeDtypeStruct((n_rows * TOK_ROWS, LANES), U32),
        compiler_params=_cparams(("arbitrary",)),
        name="moe_dispatch",
    )(*tables, zrow, zcnt, nblk_used, codes, x2, g[None, :])


def _expert_kernel(blk_e_ref, nblk_ref, xs_ref, wg_ref, wu_ref, wd_ref, ys_ref, wg_s, wu_s, wd_s):
    b = pl.program_id(0)

    @pl.when((b == 0) | (blk_e_ref[b] != blk_e_ref[jnp.maximum(b - 1, 0)]))
    def _():
        wg_s[...] = wg_ref[0, 0].astype(BF16)
        wu_s[...] = wu_ref[0, 0].astype(BF16)
        wd_s[...] = wd_ref[0, 0].astype(BF16)

    @pl.when(b < nblk_ref[0])
    def _():
        x = _load_token_rows(xs_ref, MOE_BLOCK).astype(BF16)
        a = _dot(x, wg_s[...])
        u = _dot(x, wu_s[...])
        hid = (a / (1.0 + jnp.exp(-a)) * u).astype(BF16)
        y = _dot(hid, wd_s[...])
        _store_token_rows(ys_ref, 0, MOE_BLOCK, _pack_rows(y))

    @pl.when(b >= nblk_ref[0])
    def _():
        ys_ref[...] = jnp.zeros_like(ys_ref)


def _experts(xs, blk_e, nblk_used, w_gate, w_up, w_down, layer):
    rows = xs.shape[0]
    nblk = rows // (MOE_BLOCK * TOK_ROWS)
    D = w_gate.shape[2]
    blk = lambda b, be, nb: (jnp.minimum(b, nb[0] - 1), 0)
    out_blk = lambda b, be, nb: (b, 0)
    wsel = lambda b, be, nb: (layer, be[b], 0, 0)
    return pl.pallas_call(
        _expert_kernel,
        grid_spec=pltpu.PrefetchScalarGridSpec(
            num_scalar_prefetch=2,
            grid=(nblk,),
            in_specs=[pl.BlockSpec((MOE_BLOCK * TOK_ROWS, LANES), blk),
                      pl.BlockSpec((1, 1, D, D_EXPERT), wsel),
                      pl.BlockSpec((1, 1, D, D_EXPERT), wsel),
                      pl.BlockSpec((1, 1, D_EXPERT, D), wsel)],
            out_specs=pl.BlockSpec((MOE_BLOCK * TOK_ROWS, LANES), out_blk),
            scratch_shapes=[pltpu.VMEM((D, D_EXPERT), BF16), pltpu.VMEM((D, D_EXPERT), BF16),
                            pltpu.VMEM((D_EXPERT, D), BF16)],
        ),
        out_shape=jax.ShapeDtypeStruct((rows, LANES), U32),
        compiler_params=_cparams(("arbitrary",)),
        name="moe_experts",
    )(blk_e, nblk_used, xs, w_gate, w_up, w_down)


def _combine_kernel(a_ref, n_ref, ls_ref, gb_ref, tot_ref, code_ref, x_ref, info_ref, ys_ref, o_ref,
                    ybuf_ref, pick0_ref, pick1_ref, dvec_ref, drow_ref, sem, dsem):
    s = pl.program_id(0)
    ts = x_ref.shape[0]
    slot = s % 2
    slot_rows = ybuf_ref.shape[0] // (2 * TOK_ROWS)
    first_tok = slot * slot_rows

    def start_fetch(step, step_slot):
        _run_copies(_dma_start, step, False, n_ref, ls_ref, gb_ref, ybuf_ref, step_slot * slot_rows, ys_ref,
                    sem.at[step_slot], ts, FETCH_GRAIN)

    @pl.when(s == 0)
    def _():
        start_fetch(0, 0)

    @pl.when(s + 1 < pl.num_programs(0))
    def _():
        start_fetch(s + 1, 1 - slot)

    _segment_copies(tot_ref[s], ys_ref, 0, ybuf_ref, first_tok, sem.at[slot], 2 * ts, _dma_wait, FETCH_GRAIN)
    _local_rows(code_ref, a_ref, s * N_EXPERTS, first_tok, dvec_ref, drow_ref, dsem)

    picks = (pick0_ref, pick1_ref)
    for sub in range(ts // MOE_TILE):
        def pick(tb, c, sub=sub):
            first = (sub * (MOE_TILE // DMA_UNROLL) + tb) * ENTRIES_PER_ITER
            for u in range(DMA_UNROLL):
                for kk in range(2):
                    picks[kk][_tok_rows(tb * DMA_UNROLL + u), :] = (
                        ybuf_ref[_rows_at(drow_ref[first + u * 2 + kk]), :])
            return c
        lax.fori_loop(0, MOE_TILE // DMA_UNROLL, pick, 0)

        rows = pl.ds(sub * MOE_TILE, MOE_TILE)
        info = info_ref[rows, :]
        g0 = info[:, 0:1]
        g1 = info[:, 1:2]
        half = x_ref.shape[1] // 2
        for c in range(TOK_ROWS):
            y0 = _unpack_words(pick0_ref[pl.ds(c, MOE_TILE, stride=TOK_ROWS), :])
            y1 = _unpack_words(pick1_ref[pl.ds(c, MOE_TILE, stride=TOK_ROWS), :])
            for part in range(2):
                sl = slice(part * half + c * LANES, part * half + (c + 1) * LANES)
                o_ref[rows, sl] = x_ref[rows, sl] + (y0[part] * g0 + y1[part] * g1)


def _combine(x2, info, ys, codes, tables, ts):
    T, D = x2.shape
    return pl.pallas_call(
        _combine_kernel,
        grid_spec=pltpu.PrefetchScalarGridSpec(
            num_scalar_prefetch=5,
            grid=(T // ts,),
            in_specs=[pl.BlockSpec((ts * 2 // LANES, LANES), lambda i, *_: (i, 0)),
                      pl.BlockSpec((ts, D), lambda i, *_: (i, 0)),
                      pl.BlockSpec((ts, LANES), lambda i, *_: (i, 0)),
                      pl.BlockSpec(memory_space=pl.ANY)],
            out_specs=pl.BlockSpec((ts, D), lambda i, *_: (i, 0)),
            scratch_shapes=[pltpu.VMEM((2 * (2 * ts + N_EXPERTS * FETCH_GRAIN) * TOK_ROWS, LANES), U32),
                            pltpu.VMEM((MOE_TILE * TOK_ROWS, LANES), U32),
                            pltpu.VMEM((MOE_TILE * TOK_ROWS, LANES), U32),
                            pltpu.VMEM((ts * 2 // LANES, LANES), jnp.int32),
                            pltpu.SMEM((ts * 2,), jnp.int32),
                            pltpu.SemaphoreType.DMA((2,)), pltpu.SemaphoreType.DMA(())],
        ),
        out_shape=jax.ShapeDtypeStruct((T, D), F32),
        compiler_params=_cparams(("arbitrary",)),
        name="moe_combine",
    )(*tables, codes, x2, info, ys)


def _supertile_tables(tile_base, counts, pad_start, ts, grain=1):
    per = ts // ROW_TILE
    base = tile_base[::per]
    nxt = jnp.concatenate([base[1:], counts[None, :]], axis=0)
    n = nxt - base
    room = (n + grain - 1) // grain * grain
    lstart = jnp.cumsum(room, axis=1) - room
    flat = lambda a: a.reshape(-1).astype(jnp.int32)
    return (flat(lstart - base), flat(n), flat(lstart), flat(pad_start[None, :] + base),
            flat(jnp.sum(room, axis=1)))


DISPATCH_TOKENS = 2048
COMBINE_TOKENS = 1024
FETCH_GRAIN = 64


def _moe_apply(x2, ln2, info, counts, tile_base, w_gate, w_up, w_down, layer):
    T, D = x2.shape
    padded = (counts + MOE_BLOCK - 1) // MOE_BLOCK * MOE_BLOCK
    pad_end = jnp.cumsum(padded)
    pad_start = pad_end - padded
    codes = info[:, 2:4].astype(jnp.int32).reshape(T * 2 // LANES, LANES)
    n_rows = T * 2 + (N_EXPERTS + 1) * MOE_BLOCK
    nblk = n_rows // MOE_BLOCK
    blk_row = jnp.arange(nblk, dtype=jnp.int32) * MOE_BLOCK
    blk_e = jnp.minimum(jnp.sum((pad_end[None, :] <= blk_row[:, None]).astype(jnp.int32), axis=1),
                        N_EXPERTS - 1).astype(jnp.int32)
    nblk_used = (pad_end[-1:] // MOE_BLOCK).astype(jnp.int32)
    zrow = (pad_start + counts).astype(jnp.int32)
    zcnt = (padded - counts).astype(jnp.int32)

    td = min(DISPATCH_TOKENS, T)
    tc = min(COMBINE_TOKENS, T)

    xs = _dispatch(x2, ln2, codes, _supertile_tables(tile_base, counts, pad_start, td)[:4],
                   zrow, zcnt, nblk_used, n_rows, td)
    ys = _experts(xs, blk_e, nblk_used, w_gate, w_up, w_down, layer)
    return _combine(x2, info, ys, codes, _supertile_tables(tile_base, counts, pad_start, tc, FETCH_GRAIN), tc)


HEAD_PAD = LANES
LATENT_PAD = 3 * LANES


def _rot_partner():
    half = QK_ROPE // 2
    r = jnp.arange(QK_ROPE)
    return jnp.where(r < half, r + half, r - half), jnp.where(r < half, -1.0, 1.0).astype(F32)


def _mla_rope_rows():
    half = QK_ROPE // 2
    inv = ROPE_THETA ** (-jnp.arange(half, dtype=F32) / half)
    lane = jnp.arange(LANES)
    r = lane - QK_NOPE
    in_rope = (r >= 0) & (r < QK_ROPE)
    inv_row = jnp.where(in_rope, inv[jnp.clip(r, 0, QK_ROPE - 1) % half], 0.0)[None, :]
    rope_row = in_rope.astype(F32)[None, :]
    real_row = (lane < QK_HEAD).astype(F32)[None, :]
    return inv_row, rope_row, real_row


def _head_gain_row(g):
    partner, _ = _rot_partner()
    return jnp.concatenate([g, g[QK_NOPE + partner]])[None, :]


def _with_partner_cols(w3):
    partner, sign = _rot_partner()
    rot = w3[:, :, QK_NOPE + partner] * sign
    return jnp.concatenate([w3, rot], axis=-1).reshape(w3.shape[0], N_MAIN_HEADS * HEAD_PAD)


def _heads_norm_rope(ys, gain_rows, scales, real_row, cos_real, sin_rope):
    n = range(len(ys))
    row_id = lax.broadcasted_iota(jnp.int32, (LANES, LANES), 0)
    ones_real = jnp.where(row_id < QK_HEAD, 1.0, 0.0).astype(BF16)
    ms = [_dot((ys[i] * ys[i]).astype(BF16), ones_real) * (1.0 / QK_HEAD) for i in n]
    yn = [ys[i] * (lax.rsqrt(ms[i] + EPS) * scales[i]) * gain_rows[i] for i in n]
    rolled = [pltpu.roll(yn[i], LANES - QK_ROPE, 1) for i in n]
    return [(yn[i] * cos_real + rolled[i] * sin_rope).astype(BF16) for i in n]


def _mla_qkv_kernel(x_ref, pos_ref, lnkv_ref, lnq_ref, wd_ref, scale_ref, wkv_ref, kg_ref,
                    win_ref, qlg_ref, wuq_ref, qg_ref, inv_ref, rope_ref, real_ref,
                    k_ref, v_ref, q_ref, qm_ref):
    x = x_ref[...]
    xr = x * lax.rsqrt(jnp.mean(x * x, axis=-1, keepdims=True) + EPS)
    ang = pos_ref[...].astype(F32) * inv_ref[...]
    real_row = real_ref[...]
    cos_real = jnp.cos(ang) * real_row
    sin_rope = jnp.sin(ang) * rope_ref[...]

    ckr = _dot((xr * lnkv_ref[...]).astype(BF16), wd_ref[...])
    c = ckr[:, :KV_LORA]
    r = lax.rsqrt(jnp.mean(c * c, axis=-1, keepdims=True) + EPS)
    lane = lax.broadcasted_iota(jnp.int32, (1, LATENT_PAD), 1)
    lhs = (ckr * jnp.where(lane < KV_LORA, r * scale_ref[...], 1.0)).astype(BF16)
    kv = _dot(lhs, wkv_ref[...])
    proj = _dot((xr * lnq_ref[...]).astype(BF16), win_ref[...])
    cq = _rmsnorm(proj[:, :Q_LORA], qlg_ref[...]).astype(BF16)
    q = _dot(cq, wuq_ref[...])
    qm_ref[...] = proj[:, Q_LORA:].astype(BF16)
    v_ref[...] = kv[:, N_MAIN_HEADS * HEAD_PAD:].astype(BF16)
    sls = [slice(hh * HEAD_PAD, (hh + 1) * HEAD_PAD) for hh in range(N_MAIN_HEADS)]
    nh = N_MAIN_HEADS
    outs = _heads_norm_rope([kv[:, sl] for sl in sls] + [q[:, sl] for sl in sls],
                            [kg_ref[...]] * nh + [qg_ref[...]] * nh,
                            [1.0] * nh + [QK_HEAD ** -0.5] * nh, real_row, cos_real, sin_rope)
    for hh, sl in enumerate(sls):
        k_ref[:, sl] = outs[hh]
        q_ref[:, sl] = outs[nh + hh]


def _mla_qkv(x2, pos_col, kv_ln, w_dkv, kv_lora_g, w_ukv, k_g, ln1, w_in, q_lora_g, w_uq, q_g):
    T, D = x2.shape
    tm = min(ROW_TILE, T)
    lat = LATENT_PAD
    wd = jnp.pad(w_dkv, ((0, 0), (0, lat - w_dkv.shape[1]))).astype(BF16)
    scale_row = jnp.pad(kv_lora_g, (0, lat - KV_LORA), constant_values=1.0)[None, :]
    w3 = w_ukv.reshape(KV_LORA, N_MAIN_HEADS, QK_NOPE + V_HEAD)
    wk = jnp.zeros((lat, N_MAIN_HEADS, QK_HEAD), F32)
    wk = wk.at[:KV_LORA, :, :QK_NOPE].set(w3[:, :, :QK_NOPE])
    eye = jnp.eye(QK_ROPE, dtype=F32)
    wk = wk.at[KV_LORA:KV_LORA + QK_ROPE, :, QK_NOPE:].set(
        jnp.broadcast_to(eye[:, None, :], (QK_ROPE, N_MAIN_HEADS, QK_ROPE)))
    wv = jnp.zeros((lat, N_MAIN_HEADS * V_HEAD), F32)
    wv = wv.at[:KV_LORA].set(w3[:, :, QK_NOPE:].reshape(KV_LORA, N_MAIN_HEADS * V_HEAD))
    wkv = jnp.concatenate([_with_partner_cols(wk), wv], axis=1).astype(BF16)
    wuq = _with_partner_cols(w_uq.reshape(Q_LORA, N_MAIN_HEADS, QK_HEAD)).astype(BF16)
    inv_row, rope_row, real_row = _mla_rope_rows()
    row = lambda i: (i, 0)
    fixed = lambda i: (0, 0)
    kw = N_MAIN_HEADS * HEAD_PAD
    n_in = w_in.shape[1]
    lane_row = pl.BlockSpec((1, LANES), fixed)
    return pl.pallas_call(
        _mla_qkv_kernel,
        grid=(T // tm,),
        in_specs=[pl.BlockSpec((tm, D), row), pl.BlockSpec((tm, 1), row),
                  pl.BlockSpec((1, D), fixed), pl.BlockSpec((1, D), fixed),
                  pl.BlockSpec((D, lat), fixed), pl.BlockSpec((1, lat), fixed),
                  pl.BlockSpec((lat, kw + MAIN_WIDTH), fixed), lane_row,
                  pl.BlockSpec((D, n_in), fixed), pl.BlockSpec((1, Q_LORA), fixed),
                  pl.BlockSpec((Q_LORA, kw), fixed), lane_row,
                  lane_row, lane_row, lane_row],
        out_specs=[pl.BlockSpec((tm, kw), row), pl.BlockSpec((tm, MAIN_WIDTH), row),
                   pl.BlockSpec((tm, kw), row), pl.BlockSpec((tm, MEM_WIDTH), row)],
        out_shape=[jax.ShapeDtypeStruct((T, kw), BF16), jax.ShapeDtypeStruct((T, MAIN_WIDTH), BF16),
                   jax.ShapeDtypeStruct((T, kw), BF16), jax.ShapeDtypeStruct((T, MEM_WIDTH), BF16)],
        compiler_params=_cparams(("parallel",)),
        name="mla_qkv",
    )(x2, pos_col, kv_ln[None, :], ln1[None, :], wd, scale_row, wkv, _head_gain_row(k_g),
      w_in.astype(BF16), q_lora_g[None, :], wuq, _head_gain_row(q_g), inv_row, rope_row, real_row)


ATTN_TILE = 1024
ATTN_PAIRS_PER_STEP = 3


def _flash_kernel(qi_ref, kj_ref, q_ref, k_ref, v_ref, o_ref, m_ref, acc_ref):
    t = pl.program_id(2)
    i = qi_ref[t]
    j = kj_ref[t]
    tq = q_ref.shape[1]
    tk = k_ref.shape[1]

    @pl.when(j == 0)
    def _():
        m_ref[...] = jnp.full_like(m_ref, NEG_BIG)
        acc_ref[...] = jnp.zeros_like(acc_ref)

    lane = lax.broadcasted_iota(jnp.int32, (1, LANES), 1)
    head_a = lane < V_HEAD
    den_lane = (V_HEAD, 0)

    def attend(pair, q0, nq, nk, diag_col):
        rows = pl.ds(q0, nq)
        v = v_ref[0, pl.ds(0, nk), pair * LANES:(pair + 1) * LANES]
        if diag_col is not None:
            q_idx = lax.broadcasted_iota(jnp.int32, (nq, nk), 0) + diag_col
            k_idx = lax.broadcasted_iota(jnp.int32, (nq, nk), 1)
            visible = k_idx <= q_idx
        H = range(2)
        st = [2 * pair + hh for hh in H]
        sl = [slice(h * HEAD_PAD, (h + 1) * HEAD_PAD) for h in st]
        s = [_dot_nt(q_ref[0, rows, sl[hh]], k_ref[0, pl.ds(0, nk), sl[hh]]) for hh in H]
        if diag_col is not None:
            s = [jnp.where(visible, s[hh], NEG_BIG) for hh in H]
        m_prev = [m_ref[st[hh], rows, :] for hh in H]
        acc_prev = [acc_ref[st[hh], rows, :] for hh in H]
        m_new = [jnp.maximum(m_prev[hh], jnp.max(s[hh], axis=-1, keepdims=True)) for hh in H]
        alpha = [jnp.exp(m_prev[hh] - m_new[hh]) for hh in H]
        m_wide = [jnp.concatenate([m_new[hh]] * (nk // LANES), axis=1) for hh in H]
        p = [jnp.exp((s[hh] - m_wide[hh]).astype(BF16)) for hh in H]
        keep_row = [jnp.where(head_a, 1.0, 0.0).astype(BF16), jnp.where(head_a, 0.0, 1.0).astype(BF16)]
        den_row = [jnp.where(lane == den_lane[hh], 1.0, 0.0).astype(BF16) for hh in H]
        pv = [_dot(p[hh], v * keep_row[hh] + den_row[hh]) for hh in H]
        for hh in H:
            acc_ref[st[hh], rows, :] = alpha[hh] * acc_prev[hh] + pv[hh]
            m_ref[st[hh], rows, :] = m_new[hh]

    n_pairs = v_ref.shape[2] // LANES

    @pl.when(j < i)
    def _():
        for pair in range(n_pairs):
            attend(pair, 0, tq, tk, None)

    @pl.when(j == i)
    def _():
        half = tq // 2
        for pair in range(n_pairs):
            attend(pair, 0, half, half, 0)
            attend(pair, half, half, tk, half)
            acc_a = acc_ref[2 * pair]
            acc_b = acc_ref[2 * pair + 1]
            out_a = acc_a / acc_a[:, den_lane[0]:den_lane[0] + 1]
            out_b = acc_b / acc_b[:, den_lane[1]:den_lane[1] + 1]
            o_ref[0, :, pair * LANES:(pair + 1) * LANES] = jnp.where(head_a, out_a, out_b).astype(BF16)


def _flash(q, k, v):
    B, S, _ = q.shape
    t = min(ATTN_TILE, S)
    n = S // t
    g = ATTN_PAIRS_PER_STEP
    pairs = [(i, j) for i in range(n) for j in range(i + 1)]
    qi = jnp.array([p[0] for p in pairs], jnp.int32)
    kj = jnp.array([p[1] for p in pairs], jnp.int32)
    return pl.pallas_call(
        _flash_kernel,
        grid_spec=pltpu.PrefetchScalarGridSpec(
            num_scalar_prefetch=2,
            grid=(B, N_PAIRS // g, len(pairs)),
            in_specs=[pl.BlockSpec((1, t, 2 * g * HEAD_PAD), lambda b, p, s, qi, kj: (b, qi[s], p)),
                      pl.BlockSpec((1, t, 2 * g * HEAD_PAD), lambda b, p, s, qi, kj: (b, kj[s], p)),
                      pl.BlockSpec((1, t, g * LANES), lambda b, p, s, qi, kj: (b, kj[s], p))],
            out_specs=pl.BlockSpec((1, t, g * LANES), lambda b, p, s, qi, kj: (b, qi[s], p)),
            scratch_shapes=[pltpu.VMEM((2 * g, t, LANES), F32), pltpu.VMEM((2 * g, t, LANES), F32)],
        ),
        out_shape=jax.ShapeDtypeStruct((B, S, MAIN_WIDTH), BF16),
        compiler_params=_cparams(("parallel", "parallel", "arbitrary")),
        name="mla_flash",
    )(qi, kj, q, k, v)


def kernel(x, mem, positions, ln1, ln2, w_out, mem_w_kv, mem_q_norm, mem_k_norm, router_group_w, router_group_b, router_expert_w, router_expert_b, expert_w_gate, expert_w_up, expert_w_down, ret_w_in, ret_gn, kv_ln, kv_w_down, kv_lora_norm, kv_w_up, k_norm, mla_w_in, q_lora_norm, mla_w_uq, q_norm):
    B, S, D = x.shape
    M = mem.shape[1]
    T = B * S
    x2 = x.reshape(T, D)
    mem2 = mem.reshape(B * M, D)
    pos_col = positions.reshape(T, 1).astype(jnp.int32)

    def mem_path(i, qm):
        k_m, v_m = _mem_kv(mem2, mem_w_kv[i], mem_k_norm[i])
        return _mem_attn(qm.reshape(B, S, MEM_WIDTH), k_m.reshape(B, M, MEM_WIDTH),
                         v_m.reshape(B, M, MEM_WIDTH), mem_q_norm[i]).reshape(T, MEM_WIDTH)

    def mix_out_and_moe(i, xin, y, m):
        x1, info, counts, tile_base = _out_proj_router(
            xin, y, m, w_out[i], ln2[i], router_group_w[i], router_group_b[i], router_expert_w[i],
            router_expert_b[i])
        return _moe_apply(x1, ln2[i], info, counts, tile_base, expert_w_gate, expert_w_up, expert_w_down, i)

    q, k, v, gt, qm = _ret_inproj(x2, pos_col, ln1[0], ret_w_in[0])
    shp = (B, S, MAIN_WIDTH)
    y = _retention(q.reshape(shp), k.reshape(shp), v.reshape(shp), gt.reshape(shp), ret_gn[0])
    x2 = mix_out_and_moe(0, x2, y.reshape(T, MAIN_WIDTH), mem_path(0, qm))

    k_sh, v_sh, q1, qm1 = _mla_qkv(x2, pos_col, kv_ln, kv_w_down, kv_lora_norm, kv_w_up, k_norm,
                                   ln1[1], mla_w_in[0], q_lora_norm[0], mla_w_uq[0], q_norm[0])
    kw = N_MAIN_HEADS * HEAD_PAD
    y1 = _flash(q1.reshape(B, S, kw), k_sh.reshape(B, S, kw), v_sh.reshape(shp))
    x2 = mix_out_and_moe(1, x2, y1.reshape(T, MAIN_WIDTH), mem_path(1, qm1))
    return x2.reshape(B, S, D)
```

```python
import math

import jax
import jax.numpy as jnp
from jax import lax
from jax.experimental import pallas as pl
from jax.experimental.pallas import tpu as pltpu

F32 = jnp.float32
BF16 = jnp.bfloat16

HEAD_DIM = 64
N_MAIN_HEADS = 12
MAIN_WIDTH = N_MAIN_HEADS * HEAD_DIM
N_MEM_HEADS = 4
MEM_WIDTH = N_MEM_HEADS * HEAD_DIM
RET_CHUNK = 128
ROPE_THETA = 10000.0
Q_LORA = 384
KV_LORA = 256
QK_NOPE = 64
QK_ROPE = 32
QK_HEAD = QK_NOPE + QK_ROPE
V_HEAD = 64
N_GROUPS = 4
EXPERTS_PER_GROUP = 8
N_EXPERTS = N_GROUPS * EXPERTS_PER_GROUP
D_EXPERT = 256
MOE_BLOCK = 512
EPS = 1e-6

LANES = 128
VMEM_LIMIT = 48 * 1024 * 1024
NEG_BIG = -1e30

N_PAIRS = N_MAIN_HEADS // 2
ROW_TILE = 512
MOE_TILE = 256
DMA_UNROLL = 8


def _cparams(sem):
    return pltpu.CompilerParams(dimension_semantics=sem, vmem_limit_bytes=VMEM_LIMIT)


def _dot(a, b):
    return jnp.dot(a, b, preferred_element_type=F32)


def _dot_nt(a, b):
    return lax.dot_general(a, b, (((1,), (1,)), ((), ())), preferred_element_type=F32)


def _dot_tn(a, b):
    return lax.dot_general(a, b, (((0,), (0,)), ((), ())), preferred_element_type=F32)


def _rmsnorm(xf, g):
    return xf * lax.rsqrt(jnp.mean(xf * xf, axis=-1, keepdims=True) + EPS) * g


def _rope_tables(pos_col, inv_row, sgn_row):
    ang = pos_col.astype(F32) * inv_row
    return jnp.cos(ang), jnp.sin(ang) * sgn_row


def _ret_inproj_kernel(x_ref, pos_ref, g_ref, inv_ref, sgn_ref, w_ref,
                       q_ref, k_ref, v_ref, gt_ref, qm_ref):
    h = _rmsnorm(x_ref[...], g_ref[...]).astype(BF16)
    cos, sin_s = _rope_tables(pos_ref[...], inv_ref[...], sgn_ref[...])
    first = sgn_ref[...] < 0.0
    mw = MAIN_WIDTH
    half = HEAD_DIM // 2
    qk = _dot(h, w_ref[:, :2 * mw])
    groups = range(2 * mw // LANES)
    ys = [qk[:, j * LANES:(j + 1) * LANES] for j in groups]
    fwd = [pltpu.roll(ys[j], LANES - half, 1) for j in groups]
    bwd = [pltpu.roll(ys[j], half, 1) for j in groups]
    outs = [(ys[j] * cos + jnp.where(first, fwd[j], bwd[j]) * sin_s).astype(BF16) for j in groups]
    n_q = mw // LANES
    for j in range(n_q):
        q_ref[:, j * LANES:(j + 1) * LANES] = outs[j]
        k_ref[:, j * LANES:(j + 1) * LANES] = outs[n_q + j]
    v_ref[...] = _dot(h, w_ref[:, 2 * mw:3 * mw]).astype(BF16)
    gt_ref[...] = _dot(h, w_ref[:, 3 * mw:4 * mw]).astype(BF16)
    qm_ref[...] = _dot(h, w_ref[:, 4 * mw:4 * mw + MEM_WIDTH]).astype(BF16)


def _ret_inproj(x2, pos_col, g, w_in):
    T, D = x2.shape
    tm = min(ROW_TILE, T)
    half = HEAD_DIM // 2
    inv = ROPE_THETA ** (-jnp.arange(half, dtype=F32) / half)
    lane = jnp.arange(LANES)
    inv_row = inv[lane % half][None, :]
    sgn_row = jnp.where((lane % HEAD_DIM) < half, -1.0, 1.0).astype(F32)[None, :]
    row = lambda i: (i, 0)
    fixed = lambda i: (0, 0)
    n_in = w_in.shape[1]
    outs = pl.pallas_call(
        _ret_inproj_kernel,
        grid=(T // tm,),
        in_specs=[
            pl.BlockSpec((tm, D), row),
            pl.BlockSpec((tm, 1), row),
            pl.BlockSpec((1, D), fixed),
            pl.BlockSpec((1, LANES), fixed),
            pl.BlockSpec((1, LANES), fixed),
            pl.BlockSpec((D, n_in), fixed),
        ],
        out_specs=[pl.BlockSpec((tm, MAIN_WIDTH), row)] * 4 + [pl.BlockSpec((tm, MEM_WIDTH), row)],
        out_shape=[jax.ShapeDtypeStruct((T, MAIN_WIDTH), BF16)] * 4
        + [jax.ShapeDtypeStruct((T, MEM_WIDTH), BF16)],
        compiler_params=_cparams(("parallel",)),
        name="ret_inproj",
    )(x2, pos_col, g[None, :], inv_row, sgn_row, w_in.astype(BF16))
    return outs


def _retention_tables():
    H, C, d = N_MAIN_HEADS, RET_CHUNK, HEAD_DIM
    log_g = jnp.log1p(-jnp.exp2(-5.0 - jnp.arange(H, dtype=F32)))
    idx = jnp.arange(C, dtype=F32)
    rel = idx[:, None] - idx[None, :]
    scale = d ** -0.5
    decay_in = jnp.where(rel[None] >= 0,
                         jnp.exp(log_g[:, None, None] * jnp.maximum(rel, 0.0)[None]), 0.0) * scale
    kdec = jnp.exp(log_g[None, :] * (C - 1.0 - idx)[:, None]) * scale
    qdec = jnp.exp(log_g[None, :] * (idx + 1.0)[:, None])
    cdec = jnp.exp(log_g * C)

    def lanes(t):
        return jnp.repeat(t, d, axis=1).reshape(C, N_PAIRS, 2 * d).transpose(1, 0, 2)

    head_of_lane = jnp.arange(2 * d) // d
    same = (head_of_lane[:, None] == head_of_lane[None, :]).astype(F32)
    cd_lane = jnp.repeat(cdec, d).reshape(N_PAIRS, 2 * d)
    state_decay = cd_lane[:, :, None] * same[None]
    decay_in = decay_in.reshape(N_PAIRS, 2, C, C)
    return decay_in, lanes(kdec), lanes(qdec), state_decay, same


RET_CHUNKS_PER_STEP = 4


def _retention_kernel(q_ref, k_ref, v_ref, gt_ref, dm_ref, kd_ref, qd_ref, sd_ref, same_ref,
                      gn_ref, o_ref, r_ref):
    n = pl.program_id(1)

    @pl.when(n == 0)
    def _():
        r_ref[...] = jnp.zeros_like(r_ref)

    lane = lax.broadcasted_iota(jnp.int32, (1, LANES), 1)
    head_a = lane < HEAD_DIM
    keep_a = jnp.where(head_a, 1.0, 0.0).astype(BF16)
    keep_b = jnp.where(head_a, 0.0, 1.0).astype(BF16)
    same = same_ref[...]
    avg = (same * (1.0 / HEAD_DIM)).astype(BF16)
    C = RET_CHUNK
    n_chunks = q_ref.shape[1] // C
    P = range(N_PAIRS)
    CP = [(c, p) for c in range(n_chunks) for p in P]
    rows = [pl.ds(c * C, C) for c in range(n_chunks)]
    sl = [slice(p * LANES, (p + 1) * LANES) for p in P]
    q = {cp: q_ref[0, rows[cp[0]], sl[cp[1]]] for cp in CP}
    k = {cp: k_ref[0, rows[cp[0]], sl[cp[1]]] for cp in CP}
    v = {cp: v_ref[0, rows[cp[0]], sl[cp[1]]] for cp in CP}
    s_a = {cp: _dot_nt(q[cp] * keep_a, k[cp]) * dm_ref[cp[1], 0] for cp in CP}
    s_b = {cp: _dot_nt(q[cp] * keep_b, k[cp]) * dm_ref[cp[1], 1] for cp in CP}
    u = {cp: _dot_tn((k[cp].astype(F32) * kd_ref[cp[1]]).astype(BF16), v[cp]) for cp in CP}
    qd = {cp: (q[cp].astype(F32) * qd_ref[cp[1]]).astype(BF16) for cp in CP}
    inner = {cp: _dot(s_a[cp].astype(BF16), v[cp] * keep_a) + _dot(s_b[cp].astype(BF16), v[cp] * keep_b)
             for cp in CP}
    state = [r_ref[p] for p in P]
    y = {}
    for c in range(n_chunks):
        for p in P:
            y[(c, p)] = inner[(c, p)] + _dot(qd[(c, p)], state[p].astype(BF16))
        state = [sd_ref[p] * state[p] + same * u[(c, p)] for p in P]
    for p in P:
        r_ref[p] = state[p]
    yc = {cp: y[cp] - _dot(y[cp].astype(BF16), avg) for cp in CP}
    var = {cp: _dot((yc[cp] * yc[cp]).astype(BF16), avg) for cp in CP}
    for c, p in CP:
        yn = yc[(c, p)] * lax.rsqrt(var[(c, p)] + EPS) * gn_ref[:, sl[p]]
        g = gt_ref[0, rows[c], sl[p]].astype(F32)
        o_ref[0, rows[c], sl[p]] = (g / (1.0 + jnp.exp(-g)) * yn).astype(BF16)


def _retention(q, k, v, gt, ret_gn):
    B, S, W = q.shape
    C = RET_CHUNK
    dm, kd, qd, sd, same = _retention_tables()
    tok = lambda b, n: (b, n, 0)
    fixed3 = lambda b, n: (0, 0, 0)
    step = min(RET_CHUNKS_PER_STEP * C, S)
    assert S % step == 0 and step % C == 0 and W == MAIN_WIDTH
    return pl.pallas_call(
        _retention_kernel,
        grid=(B, S // step),
        in_specs=[pl.BlockSpec((1, step, W), tok)] * 4 + [
            pl.BlockSpec((N_PAIRS, 2, C, C), lambda b, n: (0, 0, 0, 0)),
            pl.BlockSpec((N_PAIRS, C, LANES), fixed3),
            pl.BlockSpec((N_PAIRS, C, LANES), fixed3),
            pl.BlockSpec((N_PAIRS, LANES, LANES), fixed3),
            pl.BlockSpec((LANES, LANES), lambda b, n: (0, 0)),
            pl.BlockSpec((1, W), lambda b, n: (0, 0)),
        ],
        out_specs=pl.BlockSpec((1, step, W), tok),
        out_shape=jax.ShapeDtypeStruct((B, S, W), BF16),
        scratch_shapes=[pltpu.VMEM((N_PAIRS, LANES, LANES), F32)],
        compiler_params=_cparams(("parallel", "arbitrary")),
        name="retention",
    )(q, k, v, gt, dm, kd, qd, sd, same, ret_gn[None, :])


def _mem_kv_kernel(mem_ref, w_ref, kg_ref, k_ref, v_ref):
    kv = _dot(mem_ref[...].astype(BF16), w_ref[...])
    lane = lax.broadcasted_iota(jnp.int32, (1, LANES), 1)
    head_a = lane < HEAD_DIM
    inv_d = 1.0 / HEAD_DIM
    for j in range(MEM_WIDTH // LANES):
        kj = kv[:, j * LANES:(j + 1) * LANES]
        k2 = kj * kj
        ms_a = jnp.sum(jnp.where(head_a, k2, 0.0), axis=-1, keepdims=True) * inv_d
        ms_b = jnp.sum(jnp.where(head_a, 0.0, k2), axis=-1, keepdims=True) * inv_d
        kn = kj * lax.rsqrt(jnp.where(head_a, ms_a, ms_b) + EPS) * kg_ref[...]
        k_ref[:, j * LANES:(j + 1) * LANES] = kn.astype(BF16)
    v_ref[...] = kv[:, MEM_WIDTH:].astype(BF16)


def _mem_kv(mem2, w_mem_kv, k_g):
    TM, D = mem2.shape
    tm = min(ROW_TILE, TM)
    kg_row = jnp.tile(k_g, LANES // HEAD_DIM)[None, :]
    row = lambda i: (i, 0)
    fixed = lambda i: (0, 0)
    return pl.pallas_call(
        _mem_kv_kernel,
        grid=(TM // tm,),
        in_specs=[pl.BlockSpec((tm, D), row), pl.BlockSpec((D, 2 * MEM_WIDTH), fixed),
                  pl.BlockSpec((1, LANES), fixed)],
        out_specs=[pl.BlockSpec((tm, MEM_WIDTH), row)] * 2,
        out_shape=[jax.ShapeDtypeStruct((TM, MEM_WIDTH), BF16)] * 2,
        compiler_params=_cparams(("parallel",)),
        name="mem_kv",
    )(mem2, w_mem_kv.astype(BF16), kg_row)


def _mem_attn_kernel(qm_ref, k_ref, v_ref, qg_ref, o_ref):
    lane = lax.broadcasted_iota(jnp.int32, (1, LANES), 1)
    head_a = lane < HEAD_DIM
    keep = [jnp.where(head_a, 1.0, 0.0).astype(BF16), jnp.where(head_a, 0.0, 1.0).astype(BF16)]
    r_head = lax.broadcasted_iota(jnp.int32, (LANES, LANES), 0) < HEAD_DIM
    c_head = lax.broadcasted_iota(jnp.int32, (LANES, LANES), 1) < HEAD_DIM
    avg = jnp.where(r_head == c_head, 1.0 / HEAD_DIM, 0.0).astype(BF16)
    n_mem = k_ref.shape[1]
    ones = jnp.ones((n_mem, LANES), BF16)
    scale = HEAD_DIM ** -0.5
    G = range(MEM_WIDTH // LANES)
    sl = [slice(j * LANES, (j + 1) * LANES) for j in G]
    q = [qm_ref[0, :, sl[j]].astype(F32) for j in G]
    ms = [_dot((q[j] * q[j]).astype(BF16), avg) for j in G]
    qn = [(q[j] * lax.rsqrt(ms[j] + EPS) * (qg_ref[...] * scale)).astype(BF16) for j in G]
    heads = [(j, hh) for j in G for hh in range(2)]
    s = {jh: _dot_nt(qn[jh[0]] * keep[jh[1]], k_ref[0, :, sl[jh[0]]]) for jh in heads}
    p = {jh: jnp.exp(s[jh] - jnp.max(s[jh], axis=-1, keepdims=True)).astype(BF16) for jh in heads}
    den = {jh: _dot(p[jh], ones) for jh in heads}
    pv = {jh: _dot(p[jh], v_ref[0, :, sl[jh[0]]] * keep[jh[1]]) for jh in heads}
    for j in G:
        o_ref[0, :, sl[j]] = (pv[(j, 0)] / den[(j, 0)] + pv[(j, 1)] / den[(j, 1)]).astype(BF16)


def _mem_attn(qm, k_m, v_m, q_g):
    B, S, _ = qm.shape
    M = k_m.shape[1]
    tm = min(2 * ROW_TILE, S)
    qg_row = jnp.tile(q_g, LANES // HEAD_DIM)[None, :]
    return pl.pallas_call(
        _mem_attn_kernel,
        grid=(B, S // tm),
        in_specs=[pl.BlockSpec((1, tm, MEM_WIDTH), lambda b, i: (b, i, 0)),
                  pl.BlockSpec((1, M, MEM_WIDTH), lambda b, i: (b, 0, 0)),
                  pl.BlockSpec((1, M, MEM_WIDTH), lambda b, i: (b, 0, 0)),
                  pl.BlockSpec((1, LANES), lambda b, i: (0, 0))],
        out_specs=pl.BlockSpec((1, tm, MEM_WIDTH), lambda b, i: (b, i, 0)),
        out_shape=jax.ShapeDtypeStruct((B, S, MEM_WIDTH), BF16),
        compiler_params=_cparams(("parallel", "parallel")),
        name="mem_attn",
    )(qm, k_m, v_m, qg_row)


ROUTER_LANE0 = N_GROUPS
RANK_BITS = 17
RANK_RADIX = 1 << RANK_BITS


def _proj_router_kernel(x_ref, y_ref, m_ref, wy_ref, wm_ref, g_ref, w2_ref, b_ref, lower_ref,
                        o_ref, info_ref, cnt_ref, tile_base_ref, base_ref):
    x1 = x_ref[...] + _dot(y_ref[...], wy_ref[...]) + _dot(m_ref[...], wm_ref[...])
    o_ref[...] = x1
    _route_tile(x1, g_ref, w2_ref, b_ref, lower_ref, info_ref, cnt_ref, tile_base_ref, base_ref)


def _route_tile(x, g_ref, w2_ref, b_ref, lower_ref, info_ref, cnt_ref, tile_base_ref, base_ref):
    i = pl.program_id(0)

    @pl.when(i == 0)
    def _():
        base_ref[...] = jnp.zeros_like(base_ref)

    tm = x.shape[0]
    h = _rmsnorm(x, g_ref[...])
    h_hi = h.astype(BF16)
    h_lo = (h - h_hi.astype(F32)).astype(BF16)
    both = _dot(h_hi, w2_ref[...])
    logits = both[:, :LANES] + both[:, LANES:] + _dot(h_lo, w2_ref[:, :LANES]) + b_ref[...]
    lane_i = lax.broadcasted_iota(jnp.int32, (tm, LANES), 1)
    lane = lane_i.astype(F32)
    big = float(LANES)

    is_g = lane_i < N_GROUPS
    lg = jnp.where(is_g, logits, NEG_BIG)
    mg = jnp.max(lg, axis=-1, keepdims=True)
    zg = jnp.sum(jnp.where(is_g, jnp.exp(lg - mg), 0.0), axis=-1, keepdims=True)
    p_grp = 1.0 / zg
    grp = jnp.min(jnp.where(is_g & (lg == mg), lane, big), axis=-1, keepdims=True)

    e_lane = lane_i - ROUTER_LANE0
    e_grp = (e_lane >> int(math.log2(EXPERTS_PER_GROUP))).astype(F32)
    is_e = (e_lane >= 0) & (e_lane < N_EXPERTS) & (e_grp == grp)
    le = jnp.where(is_e, logits, NEG_BIG)
    me = jnp.max(le, axis=-1, keepdims=True)
    ee = jnp.where(is_e, jnp.exp(le - me), 0.0)
    prob = ee / jnp.sum(ee, axis=-1, keepdims=True)
    p1 = jnp.max(prob, axis=-1, keepdims=True)
    i1 = jnp.min(jnp.where(is_e & (prob == p1), lane, big), axis=-1, keepdims=True)
    rest = is_e & (lane != i1)
    p2 = jnp.max(jnp.where(rest, prob, -1.0), axis=-1, keepdims=True)
    i2 = jnp.min(jnp.where(rest & (prob == p2), lane, big), axis=-1, keepdims=True)
    gate1 = p_grp * p1 / (p1 + p2)
    gate2 = p_grp * p2 / (p1 + p2)

    sel1 = lane == i1
    sel2 = lane == i2
    onehot = jnp.where(sel1 | sel2, 1.0, 0.0)
    tile_base_ref[0] = base_ref[...]
    before = _dot(lower_ref[...], onehot.astype(BF16)) + base_ref[...]
    rank1 = jnp.sum(jnp.where(sel1, before, 0.0), axis=-1, keepdims=True)
    rank2 = jnp.sum(jnp.where(sel2, before, 0.0), axis=-1, keepdims=True)
    base_ref[...] += jnp.sum(onehot, axis=0, keepdims=True)
    cnt_ref[...] = base_ref[...]

    code1 = (i1 - float(ROUTER_LANE0)) * float(RANK_RADIX) + rank1
    code2 = (i2 - float(ROUTER_LANE0)) * float(RANK_RADIX) + rank2
    info = jnp.zeros((tm, LANES), F32)
    for col, val in enumerate((gate1, gate2, code1, code2)):
        info = jnp.where(lane_i == col, val, info)
    info_ref[...] = info


def _router_operands(D, tm, g, w_grp, b_grp, w_exp, b_exp):
    w = jnp.zeros((D, LANES), F32)
    w = w.at[:, :N_GROUPS].set(w_grp).at[:, ROUTER_LANE0:ROUTER_LANE0 + N_EXPERTS].set(w_exp)
    b = jnp.zeros((1, LANES), F32)
    b = b.at[0, :N_GROUPS].set(b_grp).at[0, ROUTER_LANE0:ROUTER_LANE0 + N_EXPERTS].set(b_exp)
    w_hi = w.astype(BF16)
    w_lo = (w - w_hi.astype(F32)).astype(BF16)
    lower = (jnp.arange(tm)[:, None] > jnp.arange(tm)[None, :]).astype(BF16)
    fixed = lambda i: (0, 0)
    specs = [pl.BlockSpec((1, D), fixed), pl.BlockSpec((D, 2 * LANES), fixed), pl.BlockSpec((1, LANES), fixed),
             pl.BlockSpec((tm, tm), fixed)]
    return (g[None, :], jnp.concatenate([w_hi, w_lo], axis=1), b, lower), specs


def _router_outputs(T, tm):
    n_tiles = T // tm
    specs = [pl.BlockSpec((tm, LANES), lambda i: (i, 0)), pl.BlockSpec((1, LANES), lambda i: (0, 0)),
             pl.BlockSpec((1, 1, LANES), lambda i: (i, 0, 0))]
    shapes = [jax.ShapeDtypeStruct((T, LANES), F32), jax.ShapeDtypeStruct((1, LANES), F32),
              jax.ShapeDtypeStruct((n_tiles, 1, LANES), F32)]
    return specs, shapes


def _router_tables(cnt, tile_base):
    experts = slice(ROUTER_LANE0, ROUTER_LANE0 + N_EXPERTS)
    return cnt[0, experts].astype(jnp.int32), tile_base[:, 0, experts].astype(jnp.int32)


def _out_proj_router(x2, y2, m2, w_out, g, w_grp, b_grp, w_exp, b_exp):
    T, D = x2.shape
    tm = min(ROW_TILE, T)
    row = lambda i: (i, 0)
    fixed = lambda i: (0, 0)
    w = w_out.astype(BF16)
    ops, op_specs = _router_operands(D, tm, g, w_grp, b_grp, w_exp, b_exp)
    out_specs, out_shapes = _router_outputs(T, tm)
    x1, info, cnt, tile_base = pl.pallas_call(
        _proj_router_kernel,
        grid=(T // tm,),
        in_specs=[pl.BlockSpec((tm, D), row), pl.BlockSpec((tm, MAIN_WIDTH), row),
                  pl.BlockSpec((tm, MEM_WIDTH), row),
                  pl.BlockSpec((MAIN_WIDTH, D), fixed), pl.BlockSpec((MEM_WIDTH, D), fixed)] + op_specs,
        out_specs=[pl.BlockSpec((tm, D), row)] + out_specs,
        out_shape=[jax.ShapeDtypeStruct((T, D), F32)] + out_shapes,
        scratch_shapes=[pltpu.VMEM((1, LANES), F32)],
        compiler_params=_cparams(("arbitrary",)),
        name="out_proj_router",
    )(x2, y2, m2, w[:MAIN_WIDTH], w[MAIN_WIDTH:], *ops)
    return (x1, info) + _router_tables(cnt, tile_base)


TOK_ROWS = 4
U32 = jnp.uint32


def _tok_rows(r, n=1):
    start = r * TOK_ROWS
    if not isinstance(start, int):
        start = pl.multiple_of(start, TOK_ROWS)
    return pl.ds(start, n * TOK_ROWS)


def _pack_rows(h):
    bits = lax.bitcast_convert_type(h.astype(BF16).astype(F32), U32)
    half = h.shape[1] // 2
    return (bits[:, :half] >> 16) | bits[:, half:]


def _unpack_words(w):
    return (lax.bitcast_convert_type(w << 16, F32), lax.bitcast_convert_type(w & U32(0xFFFF0000), F32))


def _store_token_rows(ref, first_row, n, words):
    for c in range(TOK_ROWS):
        ref[pl.ds(first_row * TOK_ROWS + c, n, stride=TOK_ROWS), :] = words[:, c * LANES:(c + 1) * LANES]


def _load_token_rows(ref, n):
    parts = [_unpack_words(ref[pl.ds(c, n, stride=TOK_ROWS), :]) for c in range(TOK_ROWS)]
    return jnp.concatenate([p[0] for p in parts] + [p[1] for p in parts], axis=-1)


def _segment_copies(n, src, src_row, dst, dst_row, sem, top, op, grain=1):
    groups = (n + (grain - 1)) >> int(math.log2(grain))
    off = 0
    bit = top // grain
    while bit >= 1:
        take = groups & bit

        @pl.when(take != 0)
        def _(rows=bit * grain, off=off):
            op(pltpu.make_async_copy(src.at[_tok_rows(src_row + off, rows), :],
                                     dst.at[_tok_rows(dst_row + off, rows), :], sem))
        off = off + take * grain
        bit //= 2


ENTRIES_PER_ITER = 2 * DMA_UNROLL


def _local_rows(code_ref, a_ref, tbl, first_tok, dvec_ref, drow_ref, dsem):
    code = code_ref[...]
    e = code >> RANK_BITS
    d = (code & (RANK_RADIX - 1)) + first_tok
    for k in range(N_EXPERTS):
        d = d + jnp.where(e == k, a_ref[tbl + k], 0)
    dvec_ref[...] = d * TOK_ROWS
    copies = [pltpu.make_async_copy(dvec_ref.at[r], drow_ref.at[pl.ds(r * LANES, LANES)], dsem)
              for r in range(dvec_ref.shape[0])]
    for cp in copies:
        cp.start()
    for cp in copies:
        cp.wait()


def _run_copies(op, step, to_sorted, n_ref, ls_ref, gb_ref, local_ref, first_tok, sorted_ref, sem, top,
                grain=1):
    tbl = step * N_EXPERTS

    def run(e, c):
        loc = (local_ref, first_tok + ls_ref[tbl + e])
        glob = (sorted_ref, gb_ref[tbl + e])
        (src, src_row), (dst, dst_row) = (loc, glob) if to_sorted else (glob, loc)
        _segment_copies(n_ref[tbl + e], src, src_row, dst, dst_row, sem, top, op, grain)
        return c
    lax.fori_loop(0, N_EXPERTS, run, 0)


def _rows_at(first_row):
    return pl.ds(pl.multiple_of(first_row, TOK_ROWS), TOK_ROWS)


def _dma_start(cp):
    cp.start()


def _dma_wait(cp):
    cp.wait()


def _block_copy(src, dst, blk, sem):
    return pltpu.make_async_copy(src, dst.at[_tok_rows(blk * MOE_BLOCK, MOE_BLOCK), :], sem)


def _dispatch_kernel(a_ref, n_ref, ls_ref, gb_ref, zrow_ref, zcnt_ref, nblk_ref,
                     code_ref, x_ref, g_ref, xs_ref, hbuf_ref, cbuf_ref, zbuf_ref, dvec_ref, drow_ref,
                     sem, zsem, dsem):
    s = pl.program_id(0)
    ts = x_ref.shape[0]

    def zero_fill(op):
        n_blocks = xs_ref.shape[0] // (MOE_BLOCK * TOK_ROWS)

        def tail(b, c):
            op(_block_copy(zbuf_ref, xs_ref, b, zsem))
            return c
        lax.fori_loop(nblk_ref[0], n_blocks, tail, 0)

        def pad(e, c):
            _segment_copies(zcnt_ref[e], zbuf_ref, 0, xs_ref, zrow_ref[e], zsem, MOE_BLOCK // 2, op)
            return c
        lax.fori_loop(0, N_EXPERTS, pad, 0)

    @pl.when(s == 0)
    def _():
        zbuf_ref[...] = jnp.zeros_like(zbuf_ref)
        zero_fill(_dma_start)

    for sub in range(ts // MOE_TILE):
        h = _rmsnorm(x_ref[pl.ds(sub * MOE_TILE, MOE_TILE), :], g_ref[...])
        _store_token_rows(hbuf_ref, sub * MOE_TILE, MOE_TILE, _pack_rows(h))

    slot = s % 2
    first_tok = slot * (2 * ts)
    last = pl.num_programs(0) - 1

    def start_runs(step, step_slot):
        _run_copies(_dma_start, step, True, n_ref, ls_ref, gb_ref, cbuf_ref, step_slot * (2 * ts), xs_ref,
                    sem.at[step_slot], ts)

    def wait_runs(step_slot):
        pltpu.make_async_copy(cbuf_ref.at[_tok_rows(step_slot * (2 * ts), 2 * ts), :],
                              xs_ref.at[_tok_rows(0, 2 * ts), :], sem.at[step_slot]).wait()

    @pl.when(s >= 2)
    def _():
        wait_runs(slot)

    _local_rows(code_ref, a_ref, s * N_EXPERTS, first_tok, dvec_ref, drow_ref, dsem)

    def place(tb, c):
        first = tb * ENTRIES_PER_ITER
        for u in range(DMA_UNROLL):
            tile = hbuf_ref[_tok_rows(tb * DMA_UNROLL + u), :]
            for kk in range(2):
                cbuf_ref[_rows_at(drow_ref[first + u * 2 + kk]), :] = tile
        return c
    lax.fori_loop(0, ts // DMA_UNROLL, place, 0)

    start_runs(s, slot)

    @pl.when(s == last)
    def _():
        @pl.when(s >= 1)
        def _():
            wait_runs(1 - slot)
        wait_runs(slot)
        zero_fill(_dma_wait)


def _dispatch(x2, g, codes, tables, zrow, zcnt, nblk_used, n_rows, ts):
    T, D = x2.shape
    return pl.pallas_call(
        _dispatch_kernel,
        grid_spec=pltpu.PrefetchScalarGridSpec(
            num_scalar_prefetch=7,
            grid=(T // ts,),
            in_specs=[pl.BlockSpec((ts * 2 // LANES, LANES), lambda i, *_: (i, 0)),
                      pl.BlockSpec((ts, D), lambda i, *_: (i, 0)),
                      pl.BlockSpec((1, D), lambda i, *_: (0, 0))],
            out_specs=pl.BlockSpec(memory_space=pl.ANY),
            scratch_shapes=[pltpu.VMEM((ts * TOK_ROWS, LANES), U32),
                            pltpu.VMEM((2 * 2 * ts * TOK_ROWS, LANES), U32),
                            pltpu.VMEM((MOE_BLOCK * TOK_ROWS, LANES), U32),
                            pltpu.VMEM((ts * 2 // LANES, LANES), jnp.int32),
                            pltpu.SMEM((ts * 2,), jnp.int32),
                            pltpu.SemaphoreType.DMA((2,)), pltpu.SemaphoreType.DMA(()),
                            pltpu.SemaphoreType.DMA(())],
        ),
        out_shape=jax.ShapeDtypeStruct((n_rows * TOK_ROWS, LANES), U32),
        compiler_params=_cparams(("arbitrary",)),
        name="moe_dispatch",
    )(*tables, zrow, zcnt, nblk_used, codes, x2, g[None, :])


def _expert_kernel(blk_e_ref, nblk_ref, xs_ref, wg_ref, wu_ref, wd_ref, ys_ref, wg_s, wu_s, wd_s):
    b = pl.program_id(0)

    @pl.when((b == 0) | (blk_e_ref[b] != blk_e_ref[jnp.maximum(b - 1, 0)]))
    def _():
        wg_s[...] = wg_ref[0, 0].astype(BF16)
        wu_s[...] = wu_ref[0, 0].astype(BF16)
        wd_s[...] = wd_ref[0, 0].astype(BF16)

    @pl.when(b < nblk_ref[0])
    def _():
        x = _load_token_rows(xs_ref, MOE_BLOCK).astype(BF16)
        a = _dot(x, wg_s[...])
        u = _dot(x, wu_s[...])
        hid = (a / (1.0 + jnp.exp(-a)) * u).astype(BF16)
        y = _dot(hid, wd_s[...])
        _store_token_rows(ys_ref, 0, MOE_BLOCK, _pack_rows(y))

    @pl.when(b >= nblk_ref[0])
    def _():
        ys_ref[...] = jnp.zeros_like(ys_ref)


def _experts(xs, blk_e, nblk_used, w_gate, w_up, w_down, layer):
    rows = xs.shape[0]
    nblk = rows // (MOE_BLOCK * TOK_ROWS)
    D = w_gate.shape[2]
    blk = lambda b, be, nb: (jnp.minimum(b, nb[0] - 1), 0)
    out_blk = lambda b, be, nb: (b, 0)
    wsel = lambda b, be, nb: (layer, be[b], 0, 0)
    return pl.pallas_call(
        _expert_kernel,
        grid_spec=pltpu.PrefetchScalarGridSpec(
            num_scalar_prefetch=2,
            grid=(nblk,),
            in_specs=[pl.BlockSpec((MOE_BLOCK * TOK_ROWS, LANES), blk),
                      pl.BlockSpec((1, 1, D, D_EXPERT), wsel),
                      pl.BlockSpec((1, 1, D, D_EXPERT), wsel),
                      pl.BlockSpec((1, 1, D_EXPERT, D), wsel)],
            out_specs=pl.BlockSpec((MOE_BLOCK * TOK_ROWS, LANES), out_blk),
            scratch_shapes=[pltpu.VMEM((D, D_EXPERT), BF16), pltpu.VMEM((D, D_EXPERT), BF16),
                            pltpu.VMEM((D_EXPERT, D), BF16)],
        ),
        out_shape=jax.ShapeDtypeStruct((rows, LANES), U32),
        compiler_params=_cparams(("arbitrary",)),
        name="moe_experts",
    )(blk_e, nblk_used, xs, w_gate, w_up, w_down)


def _combine_kernel(a_ref, n_ref, ls_ref, gb_ref, tot_ref, code_ref, x_ref, info_ref, ys_ref, o_ref,
                    ybuf_ref, pick0_ref, pick1_ref, dvec_ref, drow_ref, sem, dsem):
    s = pl.program_id(0)
    ts = x_ref.shape[0]
    slot = s % 2
    slot_rows = ybuf_ref.shape[0] // (2 * TOK_ROWS)
    first_tok = slot * slot_rows

    def start_fetch(step, step_slot):
        _run_copies(_dma_start, step, False, n_ref, ls_ref, gb_ref, ybuf_ref, step_slot * slot_rows, ys_ref,
                    sem.at[step_slot], ts, FETCH_GRAIN)

    @pl.when(s == 0)
    def _():
        start_fetch(0, 0)

    @pl.when(s + 1 < pl.num_programs(0))
    def _():
        start_fetch(s + 1, 1 - slot)

    _segment_copies(tot_ref[s], ys_ref, 0, ybuf_ref, first_tok, sem.at[slot], 2 * ts, _dma_wait, FETCH_GRAIN)
    _local_rows(code_ref, a_ref, s * N_EXPERTS, first_tok, dvec_ref, drow_ref, dsem)

    picks = (pick0_ref, pick1_ref)
    for sub in range(ts // MOE_TILE):
        def pick(tb, c, sub=sub):
            first = (sub * (MOE_TILE // DMA_UNROLL) + tb) * ENTRIES_PER_ITER
            for u in range(DMA_UNROLL):
                for kk in range(2):
                    picks[kk][_tok_rows(tb * DMA_UNROLL + u), :] = (
                        ybuf_ref[_rows_at(drow_ref[first + u * 2 + kk]), :])
            return c
        lax.fori_loop(0, MOE_TILE // DMA_UNROLL, pick, 0)

        rows = pl.ds(sub * MOE_TILE, MOE_TILE)
        info = info_ref[rows, :]
        g0 = info[:, 0:1]
        g1 = info[:, 1:2]
        half = x_ref.shape[1] // 2
        for c in range(TOK_ROWS):
            y0 = _unpack_words(pick0_ref[pl.ds(c, MOE_TILE, stride=TOK_ROWS), :])
            y1 = _unpack_words(pick1_ref[pl.ds(c, MOE_TILE, stride=TOK_ROWS), :])
            for part in range(2):
                sl = slice(part * half + c * LANES, part * half + (c + 1) * LANES)
                o_ref[rows, sl] = x_ref[rows, sl] + (y0[part] * g0 + y1[part] * g1)


def _combine(x2, info, ys, codes, tables, ts):
    T, D = x2.shape
    return pl.pallas_call(
        _combine_kernel,
        grid_spec=pltpu.PrefetchScalarGridSpec(
            num_scalar_prefetch=5,
            grid=(T // ts,),
            in_specs=[pl.BlockSpec((ts * 2 // LANES, LANES), lambda i, *_: (i, 0)),
                      pl.BlockSpec((ts, D), lambda i, *_: (i, 0)),
                      pl.BlockSpec((ts, LANES), lambda i, *_: (i, 0)),
                      pl.BlockSpec(memory_space=pl.ANY)],
            out_specs=pl.BlockSpec((ts, D), lambda i, *_: (i, 0)),
            scratch_shapes=[pltpu.VMEM((2 * (2 * ts + N_EXPERTS * FETCH_GRAIN) * TOK_ROWS, LANES), U32),
                            pltpu.VMEM((MOE_TILE * TOK_ROWS, LANES), U32),
                            pltpu.VMEM((MOE_TILE * TOK_ROWS, LANES), U32),
                            pltpu.VMEM((ts * 2 // LANES, LANES), jnp.int32),
                            pltpu.SMEM((ts * 2,), jnp.int32),
                            pltpu.SemaphoreType.DMA((2,)), pltpu.SemaphoreType.DMA(())],
        ),
        out_shape=jax.ShapeDtypeStruct((T, D), F32),
        compiler_params=_cparams(("arbitrary",)),
        name="moe_combine",
    )(*tables, codes, x2, info, ys)


def _supertile_tables(tile_base, counts, pad_start, ts, grain=1):
    per = ts // ROW_TILE
    base = tile_base[::per]
    nxt = jnp.concatenate([base[1:], counts[None, :]], axis=0)
    n = nxt - base
    room = (n + grain - 1) // grain * grain
    lstart = jnp.cumsum(room, axis=1) - room
    flat = lambda a: a.reshape(-1).astype(jnp.int32)
    return (flat(lstart - base), flat(n), flat(lstart), flat(pad_start[None, :] + base),
            flat(jnp.sum(room, axis=1)))


DISPATCH_TOKENS = 2048
COMBINE_TOKENS = 1024
FETCH_GRAIN = 64


def _moe_apply(x2, ln2, info, counts, tile_base, w_gate, w_up, w_down, layer):
    T, D = x2.shape
    padded = (counts + MOE_BLOCK - 1) // MOE_BLOCK * MOE_BLOCK
    pad_end = jnp.cumsum(padded)
    pad_start = pad_end - padded
    codes = info[:, 2:4].astype(jnp.int32).reshape(T * 2 // LANES, LANES)
    n_rows = T * 2 + (N_EXPERTS + 1) * MOE_BLOCK
    nblk = n_rows // MOE_BLOCK
    blk_row = jnp.arange(nblk, dtype=jnp.int32) * MOE_BLOCK
    blk_e = jnp.minimum(jnp.sum((pad_end[None, :] <= blk_row[:, None]).astype(jnp.int32), axis=1),
                        N_EXPERTS - 1).astype(jnp.int32)
    nblk_used = (pad_end[-1:] // MOE_BLOCK).astype(jnp.int32)
    zrow = (pad_start + counts).astype(jnp.int32)
    zcnt = (padded - counts).astype(jnp.int32)

    td = min(DISPATCH_TOKENS, T)
    tc = min(COMBINE_TOKENS, T)
    assert D == 2 * TOK_ROWS * LANES and 2 * T <= RANK_RADIX
    assert T % td == 0 and T % tc == 0 and td % ROW_TILE == 0 and tc % ROW_TILE == 0 and tc % MOE_TILE == 0

    xs = _dispatch(x2, ln2, codes, _supertile_tables(tile_base, counts, pad_start, td)[:4],
                   zrow, zcnt, nblk_used, n_rows, td)
    ys = _experts(xs, blk_e, nblk_used, w_gate, w_up, w_down, layer)
    return _combine(x2, info, ys, codes, _supertile_tables(tile_base, counts, pad_start, tc, FETCH_GRAIN), tc)


HEAD_PAD = LANES
LATENT_PAD = 3 * LANES


def _rot_partner():
    half = QK_ROPE // 2
    r = jnp.arange(QK_ROPE)
    return jnp.where(r < half, r + half, r - half), jnp.where(r < half, -1.0, 1.0).astype(F32)


def _mla_rope_rows():
    half = QK_ROPE // 2
    inv = ROPE_THETA ** (-jnp.arange(half, dtype=F32) / half)
    lane = jnp.arange(LANES)
    r = lane - QK_NOPE
    in_rope = (r >= 0) & (r < QK_ROPE)
    inv_row = jnp.where(in_rope, inv[jnp.clip(r, 0, QK_ROPE - 1) % half], 0.0)[None, :]
    rope_row = in_rope.astype(F32)[None, :]
    real_row = (lane < QK_HEAD).astype(F32)[None, :]
    return inv_row, rope_row, real_row


def _head_gain_row(g):
    partner, _ = _rot_partner()
    return jnp.concatenate([g, g[QK_NOPE + partner]])[None, :]


def _with_partner_cols(w3):
    partner, sign = _rot_partner()
    rot = w3[:, :, QK_NOPE + partner] * sign
    return jnp.concatenate([w3, rot], axis=-1).reshape(w3.shape[0], N_MAIN_HEADS * HEAD_PAD)


def _heads_norm_rope(ys, gain_rows, scales, real_row, cos_real, sin_rope):
    n = range(len(ys))
    row_id = lax.broadcasted_iota(jnp.int32, (LANES, LANES), 0)
    ones_real = jnp.where(row_id < QK_HEAD, 1.0, 0.0).astype(BF16)
    ms = [_dot((ys[i] * ys[i]).astype(BF16), ones_real) * (1.0 / QK_HEAD) for i in n]
    yn = [ys[i] * (lax.rsqrt(ms[i] + EPS) * scales[i]) * gain_rows[i] for i in n]
    rolled = [pltpu.roll(yn[i], LANES - QK_ROPE, 1) for i in n]
    return [(yn[i] * cos_real + rolled[i] * sin_rope).astype(BF16) for i in n]


def _mla_qkv_kernel(x_ref, pos_ref, lnkv_ref, lnq_ref, wd_ref, scale_ref, wkv_ref, kg_ref,
                    win_ref, qlg_ref, wuq_ref, qg_ref, inv_ref, rope_ref, real_ref,
                    k_ref, v_ref, q_ref, qm_ref):
    x = x_ref[...]
    xr = x * lax.rsqrt(jnp.mean(x * x, axis=-1, keepdims=True) + EPS)
    ang = pos_ref[...].astype(F32) * inv_ref[...]
    real_row = real_ref[...]
    cos_real = jnp.cos(ang) * real_row
    sin_rope = jnp.sin(ang) * rope_ref[...]

    ckr = _dot((xr * lnkv_ref[...]).astype(BF16), wd_ref[...])
    c = ckr[:, :KV_LORA]
    r = lax.rsqrt(jnp.mean(c * c, axis=-1, keepdims=True) + EPS)
    lane = lax.broadcasted_iota(jnp.int32, (1, LATENT_PAD), 1)
    lhs = (ckr * jnp.where(lane < KV_LORA, r * scale_ref[...], 1.0)).astype(BF16)
    kv = _dot(lhs, wkv_ref[...])
    proj = _dot((xr * lnq_ref[...]).astype(BF16), win_ref[...])
    cq = _rmsnorm(proj[:, :Q_LORA], qlg_ref[...]).astype(BF16)
    q = _dot(cq, wuq_ref[...])
    qm_ref[...] = proj[:, Q_LORA:].astype(BF16)
    v_ref[...] = kv[:, N_MAIN_HEADS * HEAD_PAD:].astype(BF16)
    sls = [slice(hh * HEAD_PAD, (hh + 1) * HEAD_PAD) for hh in range(N_MAIN_HEADS)]
    nh = N_MAIN_HEADS
    outs = _heads_norm_rope([kv[:, sl] for sl in sls] + [q[:, sl] for sl in sls],
                            [kg_ref[...]] * nh + [qg_ref[...]] * nh,
                            [1.0] * nh + [QK_HEAD ** -0.5] * nh, real_row, cos_real, sin_rope)
    for hh, sl in enumerate(sls):
        k_ref[:, sl] = outs[hh]
        q_ref[:, sl] = outs[nh + hh]


def _mla_qkv(x2, pos_col, kv_ln, w_dkv, kv_lora_g, w_ukv, k_g, ln1, w_in, q_lora_g, w_uq, q_g):
    T, D = x2.shape
    tm = min(ROW_TILE, T)
    lat = LATENT_PAD
    wd = jnp.pad(w_dkv, ((0, 0), (0, lat - w_dkv.shape[1]))).astype(BF16)
    scale_row = jnp.pad(kv_lora_g, (0, lat - KV_LORA), constant_values=1.0)[None, :]
    w3 = w_ukv.reshape(KV_LORA, N_MAIN_HEADS, QK_NOPE + V_HEAD)
    wk = jnp.zeros((lat, N_MAIN_HEADS, QK_HEAD), F32)
    wk = wk.at[:KV_LORA, :, :QK_NOPE].set(w3[:, :, :QK_NOPE])
    eye = jnp.eye(QK_ROPE, dtype=F32)
    wk = wk.at[KV_LORA:KV_LORA + QK_ROPE, :, QK_NOPE:].set(
        jnp.broadcast_to(eye[:, None, :], (QK_ROPE, N_MAIN_HEADS, QK_ROPE)))
    wv = jnp.zeros((lat, N_MAIN_HEADS * V_HEAD), F32)
    wv = wv.at[:KV_LORA].set(w3[:, :, QK_NOPE:].reshape(KV_LORA, N_MAIN_HEADS * V_HEAD))
    wkv = jnp.concatenate([_with_partner_cols(wk), wv], axis=1).astype(BF16)
    wuq = _with_partner_cols(w_uq.reshape(Q_LORA, N_MAIN_HEADS, QK_HEAD)).astype(BF16)
    inv_row, rope_row, real_row = _mla_rope_rows()
    row = lambda i: (i, 0)
    fixed = lambda i: (0, 0)
    kw = N_MAIN_HEADS * HEAD_PAD
    n_in = w_in.shape[1]
    lane_row = pl.BlockSpec((1, LANES), fixed)
    return pl.pallas_call(
        _mla_qkv_kernel,
        grid=(T // tm,),
        in_specs=[pl.BlockSpec((tm, D), row), pl.BlockSpec((tm, 1), row),
                  pl.BlockSpec((1, D), fixed), pl.BlockSpec((1, D), fixed),
                  pl.BlockSpec((D, lat), fixed), pl.BlockSpec((1, lat), fixed),
                  pl.BlockSpec((lat, kw + MAIN_WIDTH), fixed), lane_row,
                  pl.BlockSpec((D, n_in), fixed), pl.BlockSpec((1, Q_LORA), fixed),
                  pl.BlockSpec((Q_LORA, kw), fixed), lane_row,
                  lane_row, lane_row, lane_row],
        out_specs=[pl.BlockSpec((tm, kw), row), pl.BlockSpec((tm, MAIN_WIDTH), row),
                   pl.BlockSpec((tm, kw), row), pl.BlockSpec((tm, MEM_WIDTH), row)],
        out_shape=[jax.ShapeDtypeStruct((T, kw), BF16), jax.ShapeDtypeStruct((T, MAIN_WIDTH), BF16),
                   jax.ShapeDtypeStruct((T, kw), BF16), jax.ShapeDtypeStruct((T, MEM_WIDTH), BF16)],
        compiler_params=_cparams(("parallel",)),
        name="mla_qkv",
    )(x2, pos_col, kv_ln[None, :], ln1[None, :], wd, scale_row, wkv, _head_gain_row(k_g),
      w_in.astype(BF16), q_lora_g[None, :], wuq, _head_gain_row(q_g), inv_row, rope_row, real_row)


ATTN_TILE = 1024
ATTN_PAIRS_PER_STEP = 3


def _flash_kernel(qi_ref, kj_ref, q_ref, k_ref, v_ref, o_ref, m_ref, acc_ref):
    t = pl.program_id(2)
    i = qi_ref[t]
    j = kj_ref[t]
    tq = q_ref.shape[1]
    tk = k_ref.shape[1]

    @pl.when(j == 0)
    def _():
        m_ref[...] = jnp.full_like(m_ref, NEG_BIG)
        acc_ref[...] = jnp.zeros_like(acc_ref)

    lane = lax.broadcasted_iota(jnp.int32, (1, LANES), 1)
    head_a = lane < V_HEAD
    den_lane = (V_HEAD, 0)

    def attend(pair, q0, nq, nk, diag_col):
        rows = pl.ds(q0, nq)
        v = v_ref[0, pl.ds(0, nk), pair * LANES:(pair + 1) * LANES]
        if diag_col is not None:
            q_idx = lax.broadcasted_iota(jnp.int32, (nq, nk), 0) + diag_col
            k_idx = lax.broadcasted_iota(jnp.int32, (nq, nk), 1)
            visible = k_idx <= q_idx
        H = range(2)
        st = [2 * pair + hh for hh in H]
        sl = [slice(h * HEAD_PAD, (h + 1) * HEAD_PAD) for h in st]
        s = [_dot_nt(q_ref[0, rows, sl[hh]], k_ref[0, pl.ds(0, nk), sl[hh]]) for hh in H]
        if diag_col is not None:
            s = [jnp.where(visible, s[hh], NEG_BIG) for hh in H]
        m_prev = [m_ref[st[hh], rows, :] for hh in H]
        acc_prev = [acc_ref[st[hh], rows, :] for hh in H]
        m_new = [jnp.maximum(m_prev[hh], jnp.max(s[hh], axis=-1, keepdims=True)) for hh in H]
        alpha = [jnp.exp(m_prev[hh] - m_new[hh]) for hh in H]
        m_wide = [jnp.concatenate([m_new[hh]] * (nk // LANES), axis=1) for hh in H]
        p = [jnp.exp((s[hh] - m_wide[hh]).astype(BF16)) for hh in H]
        keep_row = [jnp.where(head_a, 1.0, 0.0).astype(BF16), jnp.where(head_a, 0.0, 1.0).astype(BF16)]
        den_row = [jnp.where(lane == den_lane[hh], 1.0, 0.0).astype(BF16) for hh in H]
        pv = [_dot(p[hh], v * keep_row[hh] + den_row[hh]) for hh in H]
        for hh in H:
            acc_ref[st[hh], rows, :] = alpha[hh] * acc_prev[hh] + pv[hh]
            m_ref[st[hh], rows, :] = m_new[hh]

    n_pairs = v_ref.shape[2] // LANES

    @pl.when(j < i)
    def _():
        for pair in range(n_pairs):
            attend(pair, 0, tq, tk, None)

    @pl.when(j == i)
    def _():
        half = tq // 2
        for pair in range(n_pairs):
            attend(pair, 0, half, half, 0)
            attend(pair, half, half, tk, half)
            acc_a = acc_ref[2 * pair]
            acc_b = acc_ref[2 * pair + 1]
            out_a = acc_a / acc_a[:, den_lane[0]:den_lane[0] + 1]
            out_b = acc_b / acc_b[:, den_lane[1]:den_lane[1] + 1]
            o_ref[0, :, pair * LANES:(pair + 1) * LANES] = jnp.where(head_a, out_a, out_b).astype(BF16)


def _flash(q, k, v):
    B, S, _ = q.shape
    t = min(ATTN_TILE, S)
    n = S // t
    g = ATTN_PAIRS_PER_STEP
    assert S % t == 0 and t % (2 * LANES) == 0 and N_PAIRS % g == 0
    pairs = [(i, j) for i in range(n) for j in range(i + 1)]
    qi = jnp.array([p[0] for p in pairs], jnp.int32)
    kj = jnp.array([p[1] for p in pairs], jnp.int32)
    return pl.pallas_call(
        _flash_kernel,
        grid_spec=pltpu.PrefetchScalarGridSpec(
            num_scalar_prefetch=2,
            grid=(B, N_PAIRS // g, len(pairs)),
            in_specs=[pl.BlockSpec((1, t, 2 * g * HEAD_PAD), lambda b, p, s, qi, kj: (b, qi[s], p)),
                      pl.BlockSpec((1, t, 2 * g * HEAD_PAD), lambda b, p, s, qi, kj: (b, kj[s], p)),
                      pl.BlockSpec((1, t, g * LANES), lambda b, p, s, qi, kj: (b, kj[s], p))],
            out_specs=pl.BlockSpec((1, t, g * LANES), lambda b, p, s, qi, kj: (b, qi[s], p)),
            scratch_shapes=[pltpu.VMEM((2 * g, t, LANES), F32), pltpu.VMEM((2 * g, t, LANES), F32)],
        ),
        out_shape=jax.ShapeDtypeStruct((B, S, MAIN_WIDTH), BF16),
        compiler_params=_cparams(("parallel", "parallel", "arbitrary")),
        name="mla_flash",
    )(qi, kj, q, k, v)


def kernel(x, mem, positions, ln1, ln2, w_out, mem_w_kv, mem_q_norm, mem_k_norm, router_group_w, router_group_b, router_expert_w, router_expert_b, expert_w_gate, expert_w_up, expert_w_down, ret_w_in, ret_gn, kv_ln, kv_w_down, kv_lora_norm, kv_w_up, k_norm, mla_w_in, q_lora_norm, mla_w_uq, q_norm):
    B, S, D = x.shape
    M = mem.shape[1]
    T = B * S
    assert T % ROW_TILE == 0 and (B * M) % ROW_TILE == 0 and S % ROW_TILE == 0 and D % LANES == 0
    x2 = x.reshape(T, D)
    mem2 = mem.reshape(B * M, D)
    pos_col = positions.reshape(T, 1).astype(jnp.int32)

    def mem_path(i, qm):
        k_m, v_m = _mem_kv(mem2, mem_w_kv[i], mem_k_norm[i])
        return _mem_attn(qm.reshape(B, S, MEM_WIDTH), k_m.reshape(B, M, MEM_WIDTH),
                         v_m.reshape(B, M, MEM_WIDTH), mem_q_norm[i]).reshape(T, MEM_WIDTH)

    def mix_out_and_moe(i, xin, y, m):
        x1, info, counts, tile_base = _out_proj_router(
            xin, y, m, w_out[i], ln2[i], router_group_w[i], router_group_b[i], router_expert_w[i],
            router_expert_b[i])
        return _moe_apply(x1, ln2[i], info, counts, tile_base, expert_w_gate, expert_w_up, expert_w_down, i)

    q, k, v, gt, qm = _ret_inproj(x2, pos_col, ln1[0], ret_w_in[0])
    shp = (B, S, MAIN_WIDTH)
    y = _retention(q.reshape(shp), k.reshape(shp), v.reshape(shp), gt.reshape(shp), ret_gn[0])
    x2 = mix_out_and_moe(0, x2, y.reshape(T, MAIN_WIDTH), mem_path(0, qm))

    k_sh, v_sh, q1, qm1 = _mla_qkv(x2, pos_col, kv_ln, kv_w_down, kv_lora_norm, kv_w_up, k_norm,
                                   ln1[1], mla_w_in[0], q_lora_norm[0], mla_w_uq[0], q_norm[0])
    kw = N_MAIN_HEADS * HEAD_PAD
    y1 = _flash(q1.reshape(B, S, kw), k_sh.reshape(B, S, kw), v_sh.reshape(shp))
    x2 = mix_out_and_moe(1, x2, y1.reshape(T, MAIN_WIDTH), mem_path(1, qm1))
    return x2.reshape(B, S, D)
```

```python
import math

import jax
import jax.numpy as jnp
from jax import lax
from jax.experimental import pallas as pl
from jax.experimental.pallas import tpu as pltpu

F32 = jnp.float32
BF16 = jnp.bfloat16

HEAD_DIM = 64
N_MAIN_HEADS = 12
MAIN_WIDTH = N_MAIN_HEADS * HEAD_DIM
N_MEM_HEADS = 4
MEM_WIDTH = N_MEM_HEADS * HEAD_DIM
RET_CHUNK = 128
ROPE_THETA = 10000.0
Q_LORA = 384
KV_LORA = 256
QK_NOPE = 64
QK_ROPE = 32
QK_HEAD = QK_NOPE + QK_ROPE
V_HEAD = 64
N_GROUPS = 4
EXPERTS_PER_GROUP = 8
N_EXPERTS = N_GROUPS * EXPERTS_PER_GROUP
D_EXPERT = 256
MOE_BLOCK = 512
EPS = 1e-6

LANES = 128
VMEM_LIMIT = 48 * 1024 * 1024
NEG_BIG = -1e30

N_PAIRS = N_MAIN_HEADS // 2
ROW_TILE = 512
MOE_TILE = 256
DMA_UNROLL = 16


def _cparams(sem):
    return pltpu.CompilerParams(dimension_semantics=sem, vmem_limit_bytes=VMEM_LIMIT)


def _dot(a, b):
    return jnp.dot(a, b, preferred_element_type=F32)


def _dot_nt(a, b):
    return lax.dot_general(a, b, (((1,), (1,)), ((), ())), preferred_element_type=F32)


def _dot_tn(a, b):
    return lax.dot_general(a, b, (((0,), (0,)), ((), ())), preferred_element_type=F32)


def _rmsnorm(xf, g):
    return xf * lax.rsqrt(jnp.mean(xf * xf, axis=-1, keepdims=True) + EPS) * g


def _rope_tables(pos_col, inv_row, sgn_row):
    ang = pos_col.astype(F32) * inv_row
    return jnp.cos(ang), jnp.sin(ang) * sgn_row


def _ret_inproj_kernel(x_ref, pos_ref, g_ref, inv_ref, sgn_ref, w_ref,
                       q_ref, k_ref, v_ref, gt_ref, qm_ref):
    h = _rmsnorm(x_ref[...], g_ref[...]).astype(BF16)
    cos, sin_s = _rope_tables(pos_ref[...], inv_ref[...], sgn_ref[...])
    first = sgn_ref[...] < 0.0
    mw = MAIN_WIDTH
    half = HEAD_DIM // 2
    qk = _dot(h, w_ref[:, :2 * mw])
    groups = range(2 * mw // LANES)
    ys = [qk[:, j * LANES:(j + 1) * LANES] for j in groups]
    fwd = [pltpu.roll(ys[j], LANES - half, 1) for j in groups]
    bwd = [pltpu.roll(ys[j], half, 1) for j in groups]
    outs = [(ys[j] * cos + jnp.where(first, fwd[j], bwd[j]) * sin_s).astype(BF16) for j in groups]
    n_q = mw // LANES
    for j in range(n_q):
        q_ref[:, j * LANES:(j + 1) * LANES] = outs[j]
        k_ref[:, j * LANES:(j + 1) * LANES] = outs[n_q + j]
    v_ref[...] = _dot(h, w_ref[:, 2 * mw:3 * mw]).astype(BF16)
    gt_ref[...] = _dot(h, w_ref[:, 3 * mw:4 * mw]).astype(BF16)
    qm_ref[...] = _dot(h, w_ref[:, 4 * mw:4 * mw + MEM_WIDTH]).astype(BF16)


def _ret_inproj(x2, pos_col, g, w_in):
    T, D = x2.shape
    tm = min(ROW_TILE, T)
    half = HEAD_DIM // 2
    inv = ROPE_THETA ** (-jnp.arange(half, dtype=F32) / half)
    lane = jnp.arange(LANES)
    inv_row = inv[lane % half][None, :]
    sgn_row = jnp.where((lane % HEAD_DIM) < half, -1.0, 1.0).astype(F32)[None, :]
    row = lambda i: (i, 0)
    fixed = lambda i: (0, 0)
    n_in = w_in.shape[1]
    outs = pl.pallas_call(
        _ret_inproj_kernel,
        grid=(T // tm,),
        in_specs=[
            pl.BlockSpec((tm, D), row),
            pl.BlockSpec((tm, 1), row),
            pl.BlockSpec((1, D), fixed),
            pl.BlockSpec((1, LANES), fixed),
            pl.BlockSpec((1, LANES), fixed),
            pl.BlockSpec((D, n_in), fixed),
        ],
        out_specs=[pl.BlockSpec((tm, MAIN_WIDTH), row)] * 4 + [pl.BlockSpec((tm, MEM_WIDTH), row)],
        out_shape=[jax.ShapeDtypeStruct((T, MAIN_WIDTH), BF16)] * 4
        + [jax.ShapeDtypeStruct((T, MEM_WIDTH), BF16)],
        compiler_params=_cparams(("parallel",)),
        name="ret_inproj",
    )(x2, pos_col, g[None, :], inv_row, sgn_row, w_in.astype(BF16))
    return outs


def _retention_tables():
    H, C, d = N_MAIN_HEADS, RET_CHUNK, HEAD_DIM
    log_g = jnp.log1p(-jnp.exp2(-5.0 - jnp.arange(H, dtype=F32)))
    idx = jnp.arange(C, dtype=F32)
    rel = idx[:, None] - idx[None, :]
    scale = d ** -0.5
    decay_in = jnp.where(rel[None] >= 0,
                         jnp.exp(log_g[:, None, None] * jnp.maximum(rel, 0.0)[None]), 0.0) * scale
    kdec = jnp.exp(log_g[None, :] * (C - 1.0 - idx)[:, None]) * scale
    qdec = jnp.exp(log_g[None, :] * (idx + 1.0)[:, None])
    cdec = jnp.exp(log_g * C)

    def lanes(t):
        return jnp.repeat(t, d, axis=1).reshape(C, N_PAIRS, 2 * d).transpose(1, 0, 2)

    head_of_lane = jnp.arange(2 * d) // d
    same = (head_of_lane[:, None] == head_of_lane[None, :]).astype(F32)
    cd_lane = jnp.repeat(cdec, d).reshape(N_PAIRS, 2 * d)
    state_decay = cd_lane[:, :, None] * same[None]
    decay_in = decay_in.reshape(N_PAIRS, 2, C, C)
    return decay_in, lanes(kdec), lanes(qdec), state_decay, same


RET_CHUNKS_PER_STEP = 4


def _retention_kernel(q_ref, k_ref, v_ref, gt_ref, dm_ref, kd_ref, qd_ref, sd_ref, same_ref,
                      gn_ref, o_ref, r_ref):
    n = pl.program_id(1)

    @pl.when(n == 0)
    def _():
        r_ref[...] = jnp.zeros_like(r_ref)

    lane = lax.broadcasted_iota(jnp.int32, (1, LANES), 1)
    head_a = lane < HEAD_DIM
    keep_a = jnp.where(head_a, 1.0, 0.0).astype(BF16)
    keep_b = jnp.where(head_a, 0.0, 1.0).astype(BF16)
    same = same_ref[...]
    avg = (same * (1.0 / HEAD_DIM)).astype(BF16)
    C = RET_CHUNK
    n_chunks = q_ref.shape[1] // C
    P = range(N_PAIRS)
    CP = [(c, p) for c in range(n_chunks) for p in P]
    rows = [pl.ds(c * C, C) for c in range(n_chunks)]
    sl = [slice(p * LANES, (p + 1) * LANES) for p in P]
    q = {cp: q_ref[0, rows[cp[0]], sl[cp[1]]] for cp in CP}
    k = {cp: k_ref[0, rows[cp[0]], sl[cp[1]]] for cp in CP}
    v = {cp: v_ref[0, rows[cp[0]], sl[cp[1]]] for cp in CP}
    s_a = {cp: _dot_nt(q[cp] * keep_a, k[cp]) * dm_ref[cp[1], 0] for cp in CP}
    s_b = {cp: _dot_nt(q[cp] * keep_b, k[cp]) * dm_ref[cp[1], 1] for cp in CP}
    u = {cp: _dot_tn((k[cp].astype(F32) * kd_ref[cp[1]]).astype(BF16), v[cp]) for cp in CP}
    qd = {cp: (q[cp].astype(F32) * qd_ref[cp[1]]).astype(BF16) for cp in CP}
    inner = {cp: _dot(s_a[cp].astype(BF16), v[cp] * keep_a) + _dot(s_b[cp].astype(BF16), v[cp] * keep_b)
             for cp in CP}
    state = [r_ref[p] for p in P]
    y = {}
    for c in range(n_chunks):
        for p in P:
            y[(c, p)] = inner[(c, p)] + _dot(qd[(c, p)], state[p].astype(BF16))
        state = [sd_ref[p] * state[p] + same * u[(c, p)] for p in P]
    for p in P:
        r_ref[p] = state[p]
    yc = {cp: y[cp] - _dot(y[cp].astype(BF16), avg) for cp in CP}
    var = {cp: _dot((yc[cp] * yc[cp]).astype(BF16), avg) for cp in CP}
    for c, p in CP:
        yn = yc[(c, p)] * lax.rsqrt(var[(c, p)] + EPS) * gn_ref[:, sl[p]]
        g = gt_ref[0, rows[c], sl[p]].astype(F32)
        o_ref[0, rows[c], sl[p]] = (g / (1.0 + jnp.exp(-g)) * yn).astype(BF16)


def _retention(q, k, v, gt, ret_gn):
    B, S, W = q.shape
    C = RET_CHUNK
    dm, kd, qd, sd, same = _retention_tables()
    tok = lambda b, n: (b, n, 0)
    fixed3 = lambda b, n: (0, 0, 0)
    step = min(RET_CHUNKS_PER_STEP * C, S)
    assert S % step == 0 and step % C == 0 and W == MAIN_WIDTH
    return pl.pallas_call(
        _retention_kernel,
        grid=(B, S // step),
        in_specs=[pl.BlockSpec((1, step, W), tok)] * 4 + [
            pl.BlockSpec((N_PAIRS, 2, C, C), lambda b, n: (0, 0, 0, 0)),
            pl.BlockSpec((N_PAIRS, C, LANES), fixed3),
            pl.BlockSpec((N_PAIRS, C, LANES), fixed3),
            pl.BlockSpec((N_PAIRS, LANES, LANES), fixed3),
            pl.BlockSpec((LANES, LANES), lambda b, n: (0, 0)),
            pl.BlockSpec((1, W), lambda b, n: (0, 0)),
        ],
        out_specs=pl.BlockSpec((1, step, W), tok),
        out_shape=jax.ShapeDtypeStruct((B, S, W), BF16),
        scratch_shapes=[pltpu.VMEM((N_PAIRS, LANES, LANES), F32)],
        compiler_params=_cparams(("parallel", "arbitrary")),
        name="retention",
    )(q, k, v, gt, dm, kd, qd, sd, same, ret_gn[None, :])


def _mem_kv_kernel(mem_ref, w_ref, kg_ref, k_ref, v_ref):
    kv = _dot(mem_ref[...].astype(BF16), w_ref[...])
    lane = lax.broadcasted_iota(jnp.int32, (1, LANES), 1)
    head_a = lane < HEAD_DIM
    inv_d = 1.0 / HEAD_DIM
    for j in range(MEM_WIDTH // LANES):
        kj = kv[:, j * LANES:(j + 1) * LANES]
        k2 = kj * kj
        ms_a = jnp.sum(jnp.where(head_a, k2, 0.0), axis=-1, keepdims=True) * inv_d
        ms_b = jnp.sum(jnp.where(head_a, 0.0, k2), axis=-1, keepdims=True) * inv_d
        kn = kj * lax.rsqrt(jnp.where(head_a, ms_a, ms_b) + EPS) * kg_ref[...]
        k_ref[:, j * LANES:(j + 1) * LANES] = kn.astype(BF16)
    v_ref[...] = kv[:, MEM_WIDTH:].astype(BF16)


def _mem_kv(mem2, w_mem_kv, k_g):
    TM, D = mem2.shape
    tm = min(ROW_TILE, TM)
    kg_row = jnp.tile(k_g, LANES // HEAD_DIM)[None, :]
    row = lambda i: (i, 0)
    fixed = lambda i: (0, 0)
    return pl.pallas_call(
        _mem_kv_kernel,
        grid=(TM // tm,),
        in_specs=[pl.BlockSpec((tm, D), row), pl.BlockSpec((D, 2 * MEM_WIDTH), fixed),
                  pl.BlockSpec((1, LANES), fixed)],
        out_specs=[pl.BlockSpec((tm, MEM_WIDTH), row)] * 2,
        out_shape=[jax.ShapeDtypeStruct((TM, MEM_WIDTH), BF16)] * 2,
        compiler_params=_cparams(("parallel",)),
        name="mem_kv",
    )(mem2, w_mem_kv.astype(BF16), kg_row)


def _mem_attn_kernel(qm_ref, k_ref, v_ref, qg_ref, o_ref):
    lane = lax.broadcasted_iota(jnp.int32, (1, LANES), 1)
    head_a = lane < HEAD_DIM
    keep = [jnp.where(head_a, 1.0, 0.0).astype(BF16), jnp.where(head_a, 0.0, 1.0).astype(BF16)]
    r_head = lax.broadcasted_iota(jnp.int32, (LANES, LANES), 0) < HEAD_DIM
    c_head = lax.broadcasted_iota(jnp.int32, (LANES, LANES), 1) < HEAD_DIM
    avg = jnp.where(r_head == c_head, 1.0 / HEAD_DIM, 0.0).astype(BF16)
    n_mem = k_ref.shape[1]
    ones = jnp.ones((n_mem, LANES), BF16)
    scale = HEAD_DIM ** -0.5
    G = range(MEM_WIDTH // LANES)
    sl = [slice(j * LANES, (j + 1) * LANES) for j in G]
    q = [qm_ref[0, :, sl[j]].astype(F32) for j in G]
    ms = [_dot((q[j] * q[j]).astype(BF16), avg) for j in G]
    qn = [(q[j] * lax.rsqrt(ms[j] + EPS) * (qg_ref[...] * scale)).astype(BF16) for j in G]
    heads = [(j, hh) for j in G for hh in range(2)]
    s = {jh: _dot_nt(qn[jh[0]] * keep[jh[1]], k_ref[0, :, sl[jh[0]]]) for jh in heads}
    p = {jh: jnp.exp(s[jh] - jnp.max(s[jh], axis=-1, keepdims=True)).astype(BF16) for jh in heads}
    den = {jh: _dot(p[jh], ones) for jh in heads}
    pv = {jh: _dot(p[jh], v_ref[0, :, sl[jh[0]]] * keep[jh[1]]) for jh in heads}
    for j in G:
        o_ref[0, :, sl[j]] = (pv[(j, 0)] / den[(j, 0)] + pv[(j, 1)] / den[(j, 1)]).astype(BF16)


def _mem_attn(qm, k_m, v_m, q_g):
    B, S, _ = qm.shape
    M = k_m.shape[1]
    tm = min(2 * ROW_TILE, S)
    qg_row = jnp.tile(q_g, LANES // HEAD_DIM)[None, :]
    return pl.pallas_call(
        _mem_attn_kernel,
        grid=(B, S // tm),
        in_specs=[pl.BlockSpec((1, tm, MEM_WIDTH), lambda b, i: (b, i, 0)),
                  pl.BlockSpec((1, M, MEM_WIDTH), lambda b, i: (b, 0, 0)),
                  pl.BlockSpec((1, M, MEM_WIDTH), lambda b, i: (b, 0, 0)),
                  pl.BlockSpec((1, LANES), lambda b, i: (0, 0))],
        out_specs=pl.BlockSpec((1, tm, MEM_WIDTH), lambda b, i: (b, i, 0)),
        out_shape=jax.ShapeDtypeStruct((B, S, MEM_WIDTH), BF16),
        compiler_params=_cparams(("parallel", "parallel")),
        name="mem_attn",
    )(qm, k_m, v_m, qg_row)


ROUTER_LANE0 = N_GROUPS
RANK_BITS = 17
RANK_RADIX = 1 << RANK_BITS


def _proj_router_kernel(x_ref, y_ref, m_ref, wy_ref, wm_ref, g_ref, w2_ref, b_ref, lower_ref,
                        o_ref, info_ref, cnt_ref, tile_base_ref, base_ref):
    x1 = x_ref[...] + _dot(y_ref[...], wy_ref[...]) + _dot(m_ref[...], wm_ref[...])
    o_ref[...] = x1
    _route_tile(x1, g_ref, w2_ref, b_ref, lower_ref, info_ref, cnt_ref, tile_base_ref, base_ref)


def _route_tile(x, g_ref, w2_ref, b_ref, lower_ref, info_ref, cnt_ref, tile_base_ref, base_ref):
    i = pl.program_id(0)

    @pl.when(i == 0)
    def _():
        base_ref[...] = jnp.zeros_like(base_ref)

    tm = x.shape[0]
    h = _rmsnorm(x, g_ref[...])
    h_hi = h.astype(BF16)
    h_lo = (h - h_hi.astype(F32)).astype(BF16)
    both = _dot(h_hi, w2_ref[...])
    logits = both[:, :LANES] + both[:, LANES:] + _dot(h_lo, w2_ref[:, :LANES]) + b_ref[...]
    lane_i = lax.broadcasted_iota(jnp.int32, (tm, LANES), 1)
    lane = lane_i.astype(F32)
    big = float(LANES)

    is_g = lane_i < N_GROUPS
    lg = jnp.where(is_g, logits, NEG_BIG)
    mg = jnp.max(lg, axis=-1, keepdims=True)
    zg = jnp.sum(jnp.where(is_g, jnp.exp(lg - mg), 0.0), axis=-1, keepdims=True)
    p_grp = 1.0 / zg
    grp = jnp.min(jnp.where(is_g & (lg == mg), lane, big), axis=-1, keepdims=True)

    e_lane = lane_i - ROUTER_LANE0
    e_grp = (e_lane >> int(math.log2(EXPERTS_PER_GROUP))).astype(F32)
    is_e = (e_lane >= 0) & (e_lane < N_EXPERTS) & (e_grp == grp)
    le = jnp.where(is_e, logits, NEG_BIG)
    me = jnp.max(le, axis=-1, keepdims=True)
    ee = jnp.where(is_e, jnp.exp(le - me), 0.0)
    prob = ee / jnp.sum(ee, axis=-1, keepdims=True)
    p1 = jnp.max(prob, axis=-1, keepdims=True)
    i1 = jnp.min(jnp.where(is_e & (prob == p1), lane, big), axis=-1, keepdims=True)
    rest = is_e & (lane != i1)
    p2 = jnp.max(jnp.where(rest, prob, -1.0), axis=-1, keepdims=True)
    i2 = jnp.min(jnp.where(rest & (prob == p2), lane, big), axis=-1, keepdims=True)
    gate1 = p_grp * p1 / (p1 + p2)
    gate2 = p_grp * p2 / (p1 + p2)

    sel1 = lane == i1
    sel2 = lane == i2
    onehot = jnp.where(sel1 | sel2, 1.0, 0.0)
    tile_base_ref[0] = base_ref[...]
    before = _dot(lower_ref[...], onehot.astype(BF16)) + base_ref[...]
    rank1 = jnp.sum(jnp.where(sel1, before, 0.0), axis=-1, keepdims=True)
    rank2 = jnp.sum(jnp.where(sel2, before, 0.0), axis=-1, keepdims=True)
    base_ref[...] += jnp.sum(onehot, axis=0, keepdims=True)
    cnt_ref[...] = base_ref[...]

    code1 = (i1 - float(ROUTER_LANE0)) * float(RANK_RADIX) + rank1
    code2 = (i2 - float(ROUTER_LANE0)) * float(RANK_RADIX) + rank2
    info = jnp.zeros((tm, LANES), F32)
    for col, val in enumerate((gate1, gate2, code1, code2)):
        info = jnp.where(lane_i == col, val, info)
    info_ref[...] = info


def _router_operands(D, tm, g, w_grp, b_grp, w_exp, b_exp):
    w = jnp.zeros((D, LANES), F32)
    w = w.at[:, :N_GROUPS].set(w_grp).at[:, ROUTER_LANE0:ROUTER_LANE0 + N_EXPERTS].set(w_exp)
    b = jnp.zeros((1, LANES), F32)
    b = b.at[0, :N_GROUPS].set(b_grp).at[0, ROUTER_LANE0:ROUTER_LANE0 + N_EXPERTS].set(b_exp)
    w_hi = w.astype(BF16)
    w_lo = (w - w_hi.astype(F32)).astype(BF16)
    lower = (jnp.arange(tm)[:, None] > jnp.arange(tm)[None, :]).astype(BF16)
    fixed = lambda i: (0, 0)
    specs = [pl.BlockSpec((1, D), fixed), pl.BlockSpec((D, 2 * LANES), fixed), pl.BlockSpec((1, LANES), fixed),
             pl.BlockSpec((tm, tm), fixed)]
    return (g[None, :], jnp.concatenate([w_hi, w_lo], axis=1), b, lower), specs


def _router_outputs(T, tm):
    n_tiles = T // tm
    specs = [pl.BlockSpec((tm, LANES), lambda i: (i, 0)), pl.BlockSpec((1, LANES), lambda i: (0, 0)),
             pl.BlockSpec((1, 1, LANES), lambda i: (i, 0, 0))]
    shapes = [jax.ShapeDtypeStruct((T, LANES), F32), jax.ShapeDtypeStruct((1, LANES), F32),
              jax.ShapeDtypeStruct((n_tiles, 1, LANES), F32)]
    return specs, shapes


def _router_tables(cnt, tile_base):
    experts = slice(ROUTER_LANE0, ROUTER_LANE0 + N_EXPERTS)
    return cnt[0, experts].astype(jnp.int32), tile_base[:, 0, experts].astype(jnp.int32)


def _out_proj_router(x2, y2, m2, w_out, g, w_grp, b_grp, w_exp, b_exp):
    T, D = x2.shape
    tm = min(ROW_TILE, T)
    row = lambda i: (i, 0)
    fixed = lambda i: (0, 0)
    w = w_out.astype(BF16)
    ops, op_specs = _router_operands(D, tm, g, w_grp, b_grp, w_exp, b_exp)
    out_specs, out_shapes = _router_outputs(T, tm)
    x1, info, cnt, tile_base = pl.pallas_call(
        _proj_router_kernel,
        grid=(T // tm,),
        in_specs=[pl.BlockSpec((tm, D), row), pl.BlockSpec((tm, MAIN_WIDTH), row),
                  pl.BlockSpec((tm, MEM_WIDTH), row),
                  pl.BlockSpec((MAIN_WIDTH, D), fixed), pl.BlockSpec((MEM_WIDTH, D), fixed)] + op_specs,
        out_specs=[pl.BlockSpec((tm, D), row)] + out_specs,
        out_shape=[jax.ShapeDtypeStruct((T, D), F32)] + out_shapes,
        scratch_shapes=[pltpu.VMEM((1, LANES), F32)],
        compiler_params=_cparams(("arbitrary",)),
        name="out_proj_router",
    )(x2, y2, m2, w[:MAIN_WIDTH], w[MAIN_WIDTH:], *ops)
    return (x1, info) + _router_tables(cnt, tile_base)


TOK_ROWS = 4
U32 = jnp.uint32


def _tok_rows(r, n=1):
    start = r * TOK_ROWS
    if not isinstance(start, int):
        start = pl.multiple_of(start, TOK_ROWS)
    return pl.ds(start, n * TOK_ROWS)


def _pack_rows(h):
    bits = lax.bitcast_convert_type(h.astype(BF16).astype(F32), U32)
    half = h.shape[1] // 2
    return (bits[:, :half] >> 16) | bits[:, half:]


def _unpack_words(w):
    return (lax.bitcast_convert_type(w << 16, F32), lax.bitcast_convert_type(w & U32(0xFFFF0000), F32))


def _store_token_rows(ref, first_row, n, words):
    for c in range(TOK_ROWS):
        ref[pl.ds(first_row * TOK_ROWS + c, n, stride=TOK_ROWS), :] = words[:, c * LANES:(c + 1) * LANES]


def _load_token_rows(ref, n):
    parts = [_unpack_words(ref[pl.ds(c, n, stride=TOK_ROWS), :]) for c in range(TOK_ROWS)]
    return jnp.concatenate([p[0] for p in parts] + [p[1] for p in parts], axis=-1)


def _segment_copies(n, src, src_row, dst, dst_row, sem, top, op, grain=1):
    groups = (n + (grain - 1)) >> int(math.log2(grain))
    off = 0
    bit = top // grain
    while bit >= 1:
        take = groups & bit

        @pl.when(take != 0)
        def _(rows=bit * grain, off=off):
            op(pltpu.make_async_copy(src.at[_tok_rows(src_row + off, rows), :],
                                     dst.at[_tok_rows(dst_row + off, rows), :], sem))
        off = off + take * grain
        bit //= 2


ENTRIES_PER_ITER = 2 * DMA_UNROLL


def _local_rows(code_ref, a_ref, tbl, first_tok, dvec_ref, drow_ref, dsem):
    code = code_ref[...]
    e = code >> RANK_BITS
    d = (code & (RANK_RADIX - 1)) + first_tok
    for k in range(N_EXPERTS):
        d = d + jnp.where(e == k, a_ref[tbl + k], 0)
    dvec_ref[...] = d * TOK_ROWS
    copies = [pltpu.make_async_copy(dvec_ref.at[r], drow_ref.at[pl.ds(r * LANES, LANES)], dsem)
              for r in range(dvec_ref.shape[0])]
    for cp in copies:
        cp.start()
    for cp in copies:
        cp.wait()


def _run_copies(op, step, to_sorted, n_ref, ls_ref, gb_ref, local_ref, first_tok, sorted_ref, sem, top,
                grain=1):
    tbl = step * N_EXPERTS

    def run(e, c):
        loc = (local_ref, first_tok + ls_ref[tbl + e])
        glob = (sorted_ref, gb_ref[tbl + e])
        (src, src_row), (dst, dst_row) = (loc, glob) if to_sorted else (glob, loc)
        _segment_copies(n_ref[tbl + e], src, src_row, dst, dst_row, sem, top, op, grain)
        return c
    lax.fori_loop(0, N_EXPERTS, run, 0)


def _rows_at(first_row):
    return pl.ds(pl.multiple_of(first_row, TOK_ROWS), TOK_ROWS)


def _dma_start(cp):
    cp.start()


def _dma_wait(cp):
    cp.wait()


def _block_copy(src, dst, blk, sem):
    return pltpu.make_async_copy(src, dst.at[_tok_rows(blk * MOE_BLOCK, MOE_BLOCK), :], sem)


def _dispatch_kernel(a_ref, n_ref, ls_ref, gb_ref, zrow_ref, zcnt_ref, nblk_ref,
                     code_ref, x_ref, g_ref, xs_ref, hbuf_ref, cbuf_ref, zbuf_ref, dvec_ref, drow_ref,
                     sem, zsem, dsem):
    s = pl.program_id(0)
    ts = x_ref.shape[0]

    def zero_fill(op):
        n_blocks = xs_ref.shape[0] // (MOE_BLOCK * TOK_ROWS)

        def tail(b, c):
            op(_block_copy(zbuf_ref, xs_ref, b, zsem))
            return c
        lax.fori_loop(nblk_ref[0], n_blocks, tail, 0)

        def pad(e, c):
            _segment_copies(zcnt_ref[e], zbuf_ref, 0, xs_ref, zrow_ref[e], zsem, MOE_BLOCK // 2, op)
            return c
        lax.fori_loop(0, N_EXPERTS, pad, 0)

    @pl.when(s == 0)
    def _():
        zbuf_ref[...] = jnp.zeros_like(zbuf_ref)
        zero_fill(_dma_start)

    for sub in range(ts // MOE_TILE):
        h = _rmsnorm(x_ref[pl.ds(sub * MOE_TILE, MOE_TILE), :], g_ref[...])
        _store_token_rows(hbuf_ref, sub * MOE_TILE, MOE_TILE, _pack_rows(h))

    slot = s % 2
    first_tok = slot * (2 * ts)
    last = pl.num_programs(0) - 1

    def start_runs(step, step_slot):
        _run_copies(_dma_start, step, True, n_ref, ls_ref, gb_ref, cbuf_ref, step_slot * (2 * ts), xs_ref,
                    sem.at[step_slot], ts)

    def wait_runs(step_slot):
        pltpu.make_async_copy(cbuf_ref.at[_tok_rows(step_slot * (2 * ts), 2 * ts), :],
                              xs_ref.at[_tok_rows(0, 2 * ts), :], sem.at[step_slot]).wait()

    @pl.when(s >= 2)
    def _():
        wait_runs(slot)

    _local_rows(code_ref, a_ref, s * N_EXPERTS, first_tok, dvec_ref, drow_ref, dsem)

    def place(tb, c):
        first = tb * ENTRIES_PER_ITER
        for u in range(DMA_UNROLL):
            tile = hbuf_ref[_tok_rows(tb * DMA_UNROLL + u), :]
            for kk in range(2):
                cbuf_ref[_rows_at(drow_ref[first + u * 2 + kk]), :] = tile
        return c
    lax.fori_loop(0, ts // DMA_UNROLL, place, 0)

    start_runs(s, slot)

    @pl.when(s == last)
    def _():
        @pl.when(s >= 1)
        def _():
            wait_runs(1 - slot)
        wait_runs(slot)
        zero_fill(_dma_wait)


def _dispatch(x2, g, codes, tables, zrow, zcnt, nblk_used, n_rows, ts):
    T, D = x2.shape
    return pl.pallas_call(
        _dispatch_kernel,
        grid_spec=pltpu.PrefetchScalarGridSpec(
            num_scalar_prefetch=7,
            grid=(T // ts,),
            in_specs=[pl.BlockSpec((ts * 2 // LANES, LANES), lambda i, *_: (i, 0)),
                      pl.BlockSpec((ts, D), lambda i, *_: (i, 0)),
                      pl.BlockSpec((1, D), lambda i, *_: (0, 0))],
            out_specs=pl.BlockSpec(memory_space=pl.ANY),
            scratch_shapes=[pltpu.VMEM((ts * TOK_ROWS, LANES), U32),
                            pltpu.VMEM((2 * 2 * ts * TOK_ROWS, LANES), U32),
                            pltpu.VMEM((MOE_BLOCK * TOK_ROWS, LANES), U32),
                            pltpu.VMEM((ts * 2 // LANES, LANES), jnp.int32),
                            pltpu.SMEM((ts * 2,), jnp.int32),
                            pltpu.SemaphoreType.DMA((2,)), pltpu.SemaphoreType.DMA(()),
                            pltpu.SemaphoreType.DMA(())],
        ),
        out_shape=jax.ShapeDtypeStruct((n_rows * TOK_ROWS, LANES), U32),
        compiler_params=_cparams(("arbitrary",)),
        name="moe_dispatch",
    )(*tables, zrow, zcnt, nblk_used, codes, x2, g[None, :])


def _expert_kernel(blk_e_ref, nblk_ref, xs_ref, wg_ref, wu_ref, wd_ref, ys_ref, wg_s, wu_s, wd_s):
    b = pl.program_id(0)

    @pl.when((b == 0) | (blk_e_ref[b] != blk_e_ref[jnp.maximum(b - 1, 0)]))
    def _():
        wg_s[...] = wg_ref[0, 0].astype(BF16)
        wu_s[...] = wu_ref[0, 0].astype(BF16)
        wd_s[...] = wd_ref[0, 0].astype(BF16)

    @pl.when(b < nblk_ref[0])
    def _():
        x = _load_token_rows(xs_ref, MOE_BLOCK).astype(BF16)
        a = _dot(x, wg_s[...])
        u = _dot(x, wu_s[...])
        hid = (a / (1.0 + jnp.exp(-a)) * u).astype(BF16)
        y = _dot(hid, wd_s[...])
        _store_token_rows(ys_ref, 0, MOE_BLOCK, _pack_rows(y))

    @pl.when(b >= nblk_ref[0])
    def _():
        ys_ref[...] = jnp.zeros_like(ys_ref)


def _experts(xs, blk_e, nblk_used, w_gate, w_up, w_down, layer):
    rows = xs.shape[0]
    nblk = rows // (MOE_BLOCK * TOK_ROWS)
    D = w_gate.shape[2]
    blk = lambda b, be, nb: (jnp.minimum(b, nb[0] - 1), 0)
    out_blk = lambda b, be, nb: (b, 0)
    wsel = lambda b, be, nb: (layer, be[b], 0, 0)
    return pl.pallas_call(
        _expert_kernel,
        grid_spec=pltpu.PrefetchScalarGridSpec(
            num_scalar_prefetch=2,
            grid=(nblk,),
            in_specs=[pl.BlockSpec((MOE_BLOCK * TOK_ROWS, LANES), blk),
                      pl.BlockSpec((1, 1, D, D_EXPERT), wsel),
                      pl.BlockSpec((1, 1, D, D_EXPERT), wsel),
                      pl.BlockSpec((1, 1, D_EXPERT, D), wsel)],
            out_specs=pl.BlockSpec((MOE_BLOCK * TOK_ROWS, LANES), out_blk),
            scratch_shapes=[pltpu.VMEM((D, D_EXPERT), BF16), pltpu.VMEM((D, D_EXPERT), BF16),
                            pltpu.VMEM((D_EXPERT, D), BF16)],
        ),
        out_shape=jax.ShapeDtypeStruct((rows, LANES), U32),
        compiler_params=_cparams(("arbitrary",)),
        name="moe_experts",
    )(blk_e, nblk_used, xs, w_gate, w_up, w_down)


def _combine_kernel(a_ref, n_ref, ls_ref, gb_ref, tot_ref, code_ref, x_ref, info_ref, ys_ref, o_ref,
                    ybuf_ref, pick0_ref, pick1_ref, dvec_ref, drow_ref, sem, dsem):
    s = pl.program_id(0)
    ts = x_ref.shape[0]
    slot = s % 2
    slot_rows = ybuf_ref.shape[0] // (2 * TOK_ROWS)
    first_tok = slot * slot_rows

    def start_fetch(step, step_slot):
        _run_copies(_dma_start, step, False, n_ref, ls_ref, gb_ref, ybuf_ref, step_slot * slot_rows, ys_ref,
                    sem.at[step_slot], ts, FETCH_GRAIN)

    @pl.when(s == 0)
    def _():
        start_fetch(0, 0)

    @pl.when(s + 1 < pl.num_programs(0))
    def _():
        start_fetch(s + 1, 1 - slot)

    _segment_copies(tot_ref[s], ys_ref, 0, ybuf_ref, first_tok, sem.at[slot], 2 * ts, _dma_wait, FETCH_GRAIN)
    _local_rows(code_ref, a_ref, s * N_EXPERTS, first_tok, dvec_ref, drow_ref, dsem)

    picks = (pick0_ref, pick1_ref)
    for sub in range(ts // MOE_TILE):
        def pick(tb, c, sub=sub):
            first = (sub * (MOE_TILE // DMA_UNROLL) + tb) * ENTRIES_PER_ITER
            for u in range(DMA_UNROLL):
                for kk in range(2):
                    picks[kk][_tok_rows(tb * DMA_UNROLL + u), :] = (
                        ybuf_ref[_rows_at(drow_ref[first + u * 2 + kk]), :])
            return c
        lax.fori_loop(0, MOE_TILE // DMA_UNROLL, pick, 0)

        rows = pl.ds(sub * MOE_TILE, MOE_TILE)
        info = info_ref[rows, :]
        g0 = info[:, 0:1]
        g1 = info[:, 1:2]
        half = x_ref.shape[1] // 2
        for c in range(TOK_ROWS):
            y0 = _unpack_words(pick0_ref[pl.ds(c, MOE_TILE, stride=TOK_ROWS), :])
            y1 = _unpack_words(pick1_ref[pl.ds(c, MOE_TILE, stride=TOK_ROWS), :])
            for part in range(2):
                sl = slice(part * half + c * LANES, part * half + (c + 1) * LANES)
                o_ref[rows, sl] = x_ref[rows, sl] + (y0[part] * g0 + y1[part] * g1)


def _combine(x2, info, ys, codes, tables, ts):
    T, D = x2.shape
    return pl.pallas_call(
        _combine_kernel,
        grid_spec=pltpu.PrefetchScalarGridSpec(
            num_scalar_prefetch=5,
            grid=(T // ts,),
            in_specs=[pl.BlockSpec((ts * 2 // LANES, LANES), lambda i, *_: (i, 0)),
                      pl.BlockSpec((ts, D), lambda i, *_: (i, 0)),
                      pl.BlockSpec((ts, LANES), lambda i, *_: (i, 0)),
                      pl.BlockSpec(memory_space=pl.ANY)],
            out_specs=pl.BlockSpec((ts, D), lambda i, *_: (i, 0)),
            scratch_shapes=[pltpu.VMEM((2 * (2 * ts + N_EXPERTS * FETCH_GRAIN) * TOK_ROWS, LANES), U32),
                            pltpu.VMEM((MOE_TILE * TOK_ROWS, LANES), U32),
                            pltpu.VMEM((MOE_TILE * TOK_ROWS, LANES), U32),
                            pltpu.VMEM((ts * 2 // LANES, LANES), jnp.int32),
                            pltpu.SMEM((ts * 2,), jnp.int32),
                            pltpu.SemaphoreType.DMA((2,)), pltpu.SemaphoreType.DMA(())],
        ),
        out_shape=jax.ShapeDtypeStruct((T, D), F32),
        compiler_params=_cparams(("arbitrary",)),
        name="moe_combine",
    )(*tables, codes, x2, info, ys)


def _supertile_tables(tile_base, counts, pad_start, ts, grain=1):
    per = ts // ROW_TILE
    base = tile_base[::per]
    nxt = jnp.concatenate([base[1:], counts[None, :]], axis=0)
    n = nxt - base
    room = (n + grain - 1) // grain * grain
    lstart = jnp.cumsum(room, axis=1) - room
    flat = lambda a: a.reshape(-1).astype(jnp.int32)
    return (flat(lstart - base), flat(n), flat(lstart), flat(pad_start[None, :] + base),
            flat(jnp.sum(room, axis=1)))


DISPATCH_TOKENS = 2048
COMBINE_TOKENS = 1024
FETCH_GRAIN = 64


def _moe_apply(x2, ln2, info, counts, tile_base, w_gate, w_up, w_down, layer):
    T, D = x2.shape
    padded = (counts + MOE_BLOCK - 1) // MOE_BLOCK * MOE_BLOCK
    pad_end = jnp.cumsum(padded)
    pad_start = pad_end - padded
    codes = info[:, 2:4].astype(jnp.int32).reshape(T * 2 // LANES, LANES)
    n_rows = T * 2 + (N_EXPERTS + 1) * MOE_BLOCK
    nblk = n_rows // MOE_BLOCK
    blk_row = jnp.arange(nblk, dtype=jnp.int32) * MOE_BLOCK
    blk_e = jnp.minimum(jnp.sum((pad_end[None, :] <= blk_row[:, None]).astype(jnp.int32), axis=1),
                        N_EXPERTS - 1).astype(jnp.int32)
    nblk_used = (pad_end[-1:] // MOE_BLOCK).astype(jnp.int32)
    zrow = (pad_start + counts).astype(jnp.int32)
    zcnt = (padded - counts).astype(jnp.int32)

    td = min(DISPATCH_TOKENS, T)
    tc = min(COMBINE_TOKENS, T)
    assert D == 2 * TOK_ROWS * LANES and 2 * T <= RANK_RADIX
    assert T % td == 0 and T % tc == 0 and td % ROW_TILE == 0 and tc % ROW_TILE == 0 and tc % MOE_TILE == 0

    xs = _dispatch(x2, ln2, codes, _supertile_tables(tile_base, counts, pad_start, td)[:4],
                   zrow, zcnt, nblk_used, n_rows, td)
    ys = _experts(xs, blk_e, nblk_used, w_gate, w_up, w_down, layer)
    return _combine(x2, info, ys, codes, _supertile_tables(tile_base, counts, pad_start, tc, FETCH_GRAIN), tc)


HEAD_PAD = LANES
LATENT_PAD = 3 * LANES


def _rot_partner():
    half = QK_ROPE // 2
    r = jnp.arange(QK_ROPE)
    return jnp.where(r < half, r + half, r - half), jnp.where(r < half, -1.0, 1.0).astype(F32)


def _mla_rope_rows():
    half = QK_ROPE // 2
    inv = ROPE_THETA ** (-jnp.arange(half, dtype=F32) / half)
    lane = jnp.arange(LANES)
    r = lane - QK_NOPE
    in_rope = (r >= 0) & (r < QK_ROPE)
    inv_row = jnp.where(in_rope, inv[jnp.clip(r, 0, QK_ROPE - 1) % half], 0.0)[None, :]
    rope_row = in_rope.astype(F32)[None, :]
    real_row = (lane < QK_HEAD).astype(F32)[None, :]
    return inv_row, rope_row, real_row


def _head_gain_row(g):
    partner, _ = _rot_partner()
    return jnp.concatenate([g, g[QK_NOPE + partner]])[None, :]


def _with_partner_cols(w3):
    partner, sign = _rot_partner()
    rot = w3[:, :, QK_NOPE + partner] * sign
    return jnp.concatenate([w3, rot], axis=-1).reshape(w3.shape[0], N_MAIN_HEADS * HEAD_PAD)


def _heads_norm_rope(ys, gain_rows, scales, real_row, cos_real, sin_rope):
    n = range(len(ys))
    row_id = lax.broadcasted_iota(jnp.int32, (LANES, LANES), 0)
    ones_real = jnp.where(row_id < QK_HEAD, 1.0, 0.0).astype(BF16)
    ms = [_dot((ys[i] * ys[i]).astype(BF16), ones_real) * (1.0 / QK_HEAD) for i in n]
    yn = [ys[i] * (lax.rsqrt(ms[i] + EPS) * scales[i]) * gain_rows[i] for i in n]
    rolled = [pltpu.roll(yn[i], LANES - QK_ROPE, 1) for i in n]
    return [(yn[i] * cos_real + rolled[i] * sin_rope).astype(BF16) for i in n]


def _mla_qkv_kernel(x_ref, pos_ref, lnkv_ref, lnq_ref, wd_ref, scale_ref, wkv_ref, kg_ref,
                    win_ref, qlg_ref, wuq_ref, qg_ref, inv_ref, rope_ref, real_ref,
                    k_ref, v_ref, q_ref, qm_ref):
    x = x_ref[...]
    xr = x * lax.rsqrt(jnp.mean(x * x, axis=-1, keepdims=True) + EPS)
    ang = pos_ref[...].astype(F32) * inv_ref[...]
    real_row = real_ref[...]
    cos_real = jnp.cos(ang) * real_row
    sin_rope = jnp.sin(ang) * rope_ref[...]

    ckr = _dot((xr * lnkv_ref[...]).astype(BF16), wd_ref[...])
    c = ckr[:, :KV_LORA]
    r = lax.rsqrt(jnp.mean(c * c, axis=-1, keepdims=True) + EPS)
    lane = lax.broadcasted_iota(jnp.int32, (1, LATENT_PAD), 1)
    lhs = (ckr * jnp.where(lane < KV_LORA, r * scale_ref[...], 1.0)).astype(BF16)
    kv = _dot(lhs, wkv_ref[...])
    proj = _dot((xr * lnq_ref[...]).astype(BF16), win_ref[...])
    cq = _rmsnorm(proj[:, :Q_LORA], qlg_ref[...]).astype(BF16)
    q = _dot(cq, wuq_ref[...])
    qm_ref[...] = proj[:, Q_LORA:].astype(BF16)
    v_ref[...] = kv[:, N_MAIN_HEADS * HEAD_PAD:].astype(BF16)
    sls = [slice(hh * HEAD_PAD, (hh + 1) * HEAD_PAD) for hh in range(N_MAIN_HEADS)]
    nh = N_MAIN_HEADS
    outs = _heads_norm_rope([kv[:, sl] for sl in sls] + [q[:, sl] for sl in sls],
                            [kg_ref[...]] * nh + [qg_ref[...]] * nh,
                            [1.0] * nh + [QK_HEAD ** -0.5] * nh, real_row, cos_real, sin_rope)
    for hh, sl in enumerate(sls):
        k_ref[:, sl] = outs[hh]
        q_ref[:, sl] = outs[nh + hh]


def _mla_qkv(x2, pos_col, kv_ln, w_dkv, kv_lora_g, w_ukv, k_g, ln1, w_in, q_lora_g, w_uq, q_g):
    T, D = x2.shape
    tm = min(ROW_TILE, T)
    lat = LATENT_PAD
    wd = jnp.pad(w_dkv, ((0, 0), (0, lat - w_dkv.shape[1]))).astype(BF16)
    scale_row = jnp.pad(kv_lora_g, (0, lat - KV_LORA), constant_values=1.0)[None, :]
    w3 = w_ukv.reshape(KV_LORA, N_MAIN_HEADS, QK_NOPE + V_HEAD)
    wk = jnp.zeros((lat, N_MAIN_HEADS, QK_HEAD), F32)
    wk = wk.at[:KV_LORA, :, :QK_NOPE].set(w3[:, :, :QK_NOPE])
    eye = jnp.eye(QK_ROPE, dtype=F32)
    wk = wk.at[KV_LORA:KV_LORA + QK_ROPE, :, QK_NOPE:].set(
        jnp.broadcast_to(eye[:, None, :], (QK_ROPE, N_MAIN_HEADS, QK_ROPE)))
    wv = jnp.zeros((lat, N_MAIN_HEADS * V_HEAD), F32)
    wv = wv.at[:KV_LORA].set(w3[:, :, QK_NOPE:].reshape(KV_LORA, N_MAIN_HEADS * V_HEAD))
    wkv = jnp.concatenate([_with_partner_cols(wk), wv], axis=1).astype(BF16)
    wuq = _with_partner_cols(w_uq.reshape(Q_LORA, N_MAIN_HEADS, QK_HEAD)).astype(BF16)
    inv_row, rope_row, real_row = _mla_rope_rows()
    row = lambda i: (i, 0)
    fixed = lambda i: (0, 0)
    kw = N_MAIN_HEADS * HEAD_PAD
    n_in = w_in.shape[1]
    lane_row = pl.BlockSpec((1, LANES), fixed)
    return pl.pallas_call(
        _mla_qkv_kernel,
        grid=(T // tm,),
        in_specs=[pl.BlockSpec((tm, D), row), pl.BlockSpec((tm, 1), row),
                  pl.BlockSpec((1, D), fixed), pl.BlockSpec((1, D), fixed),
                  pl.BlockSpec((D, lat), fixed), pl.BlockSpec((1, lat), fixed),
                  pl.BlockSpec((lat, kw + MAIN_WIDTH), fixed), lane_row,
                  pl.BlockSpec((D, n_in), fixed), pl.BlockSpec((1, Q_LORA), fixed),
                  pl.BlockSpec((Q_LORA, kw), fixed), lane_row,
                  lane_row, lane_row, lane_row],
        out_specs=[pl.BlockSpec((tm, kw), row), pl.BlockSpec((tm, MAIN_WIDTH), row),
                   pl.BlockSpec((tm, kw), row), pl.BlockSpec((tm, MEM_WIDTH), row)],
        out_shape=[jax.ShapeDtypeStruct((T, kw), BF16), jax.ShapeDtypeStruct((T, MAIN_WIDTH), BF16),
                   jax.ShapeDtypeStruct((T, kw), BF16), jax.ShapeDtypeStruct((T, MEM_WIDTH), BF16)],
        compiler_params=_cparams(("parallel",)),
        name="mla_qkv",
    )(x2, pos_col, kv_ln[None, :], ln1[None, :], wd, scale_row, wkv, _head_gain_row(k_g),
      w_in.astype(BF16), q_lora_g[None, :], wuq, _head_gain_row(q_g), inv_row, rope_row, real_row)


ATTN_TILE = 1024
ATTN_PAIRS_PER_STEP = 3


def _flash_kernel(qi_ref, kj_ref, q_ref, k_ref, v_ref, o_ref, m_ref, acc_ref):
    t = pl.program_id(2)
    i = qi_ref[t]
    j = kj_ref[t]
    tq = q_ref.shape[1]
    tk = k_ref.shape[1]

    @pl.when(j == 0)
    def _():
        m_ref[...] = jnp.full_like(m_ref, NEG_BIG)
        acc_ref[...] = jnp.zeros_like(acc_ref)

    lane = lax.broadcasted_iota(jnp.int32, (1, LANES), 1)
    head_a = lane < V_HEAD
    den_lane = (V_HEAD, 0)

    def attend(pair, q0, nq, nk, diag_col):
        rows = pl.ds(q0, nq)
        v = v_ref[0, pl.ds(0, nk), pair * LANES:(pair + 1) * LANES]
        if diag_col is not None:
            q_idx = lax.broadcasted_iota(jnp.int32, (nq, nk), 0) + diag_col
            k_idx = lax.broadcasted_iota(jnp.int32, (nq, nk), 1)
            visible = k_idx <= q_idx
        H = range(2)
        st = [2 * pair + hh for hh in H]
        sl = [slice(h * HEAD_PAD, (h + 1) * HEAD_PAD) for h in st]
        s = [_dot_nt(q_ref[0, rows, sl[hh]], k_ref[0, pl.ds(0, nk), sl[hh]]) for hh in H]
        if diag_col is not None:
            s = [jnp.where(visible, s[hh], NEG_BIG) for hh in H]
        m_prev = [m_ref[st[hh], rows, :] for hh in H]
        acc_prev = [acc_ref[st[hh], rows, :] for hh in H]
        m_new = [jnp.maximum(m_prev[hh], jnp.max(s[hh], axis=-1, keepdims=True)) for hh in H]
        alpha = [jnp.exp(m_prev[hh] - m_new[hh]) for hh in H]
        m_wide = [jnp.concatenate([m_new[hh]] * (nk // LANES), axis=1) for hh in H]
        p = [jnp.exp((s[hh] - m_wide[hh]).astype(BF16)) for hh in H]
        keep_row = [jnp.where(head_a, 1.0, 0.0).astype(BF16), jnp.where(head_a, 0.0, 1.0).astype(BF16)]
        den_row = [jnp.where(lane == den_lane[hh], 1.0, 0.0).astype(BF16) for hh in H]
        pv = [_dot(p[hh], v * keep_row[hh] + den_row[hh]) for hh in H]
        for hh in H:
            acc_ref[st[hh], rows, :] = alpha[hh] * acc_prev[hh] + pv[hh]
            m_ref[st[hh], rows, :] = m_new[hh]

    n_pairs = v_ref.shape[2] // LANES

    @pl.when(j < i)
    def _():
        for pair in range(n_pairs):
            attend(pair, 0, tq, tk, None)

    @pl.when(j == i)
    def _():
        half = tq // 2
        for pair in range(n_pairs):
            attend(pair, 0, half, half, 0)
            attend(pair, half, half, tk, half)
            acc_a = acc_ref[2 * pair]
            acc_b = acc_ref[2 * pair + 1]
            out_a = acc_a / acc_a[:, den_lane[0]:den_lane[0] + 1]
            out_b = acc_b / acc_b[:, den_lane[1]:den_lane[1] + 1]
            o_ref[0, :, pair * LANES:(pair + 1) * LANES] = jnp.where(head_a, out_a, out_b).astype(BF16)


def _flash(q, k, v):
    B, S, _ = q.shape
    t = min(ATTN_TILE, S)
    n = S // t
    g = ATTN_PAIRS_PER_STEP
    assert S % t == 0 and t % (2 * LANES) == 0 and N_PAIRS % g == 0
    pairs = [(i, j) for i in range(n) for j in range(i + 1)]
    qi = jnp.array([p[0] for p in pairs], jnp.int32)
    kj = jnp.array([p[1] for p in pairs], jnp.int32)
    return pl.pallas_call(
        _flash_kernel,
        grid_spec=pltpu.PrefetchScalarGridSpec(
            num_scalar_prefetch=2,
            grid=(B, N_PAIRS // g, len(pairs)),
            in_specs=[pl.BlockSpec((1, t, 2 * g * HEAD_PAD), lambda b, p, s, qi, kj: (b, qi[s], p)),
                      pl.BlockSpec((1, t, 2 * g * HEAD_PAD), lambda b, p, s, qi, kj: (b, kj[s], p)),
                      pl.BlockSpec((1, t, g * LANES), lambda b, p, s, qi, kj: (b, kj[s], p))],
            out_specs=pl.BlockSpec((1, t, g * LANES), lambda b, p, s, qi, kj: (b, qi[s], p)),
            scratch_shapes=[pltpu.VMEM((2 * g, t, LANES), F32), pltpu.VMEM((2 * g, t, LANES), F32)],
        ),
        out_shape=jax.ShapeDtypeStruct((B, S, MAIN_WIDTH), BF16),
        compiler_params=_cparams(("parallel", "parallel", "arbitrary")),
        name="mla_flash",
    )(qi, kj, q, k, v)


def kernel(x, mem, positions, ln1, ln2, w_out, mem_w_kv, mem_q_norm, mem_k_norm, router_group_w, router_group_b, router_expert_w, router_expert_b, expert_w_gate, expert_w_up, expert_w_down, ret_w_in, ret_gn, kv_ln, kv_w_down, kv_lora_norm, kv_w_up, k_norm, mla_w_in, q_lora_norm, mla_w_uq, q_norm):
    B, S, D = x.shape
    M = mem.shape[1]
    T = B * S
    assert T % ROW_TILE == 0 and (B * M) % ROW_TILE == 0 and S % ROW_TILE == 0 and D % LANES == 0
    x2 = x.reshape(T, D)
    mem2 = mem.reshape(B * M, D)
    pos_col = positions.reshape(T, 1).astype(jnp.int32)

    def mem_path(i, qm):
        k_m, v_m = _mem_kv(mem2, mem_w_kv[i], mem_k_norm[i])
        return _mem_attn(qm.reshape(B, S, MEM_WIDTH), k_m.reshape(B, M, MEM_WIDTH),
                         v_m.reshape(B, M, MEM_WIDTH), mem_q_norm[i]).reshape(T, MEM_WIDTH)

    def mix_out_and_moe(i, xin, y, m):
        x1, info, counts, tile_base = _out_proj_router(
            xin, y, m, w_out[i], ln2[i], router_group_w[i], router_group_b[i], router_expert_w[i],
            router_expert_b[i])
        return _moe_apply(x1, ln2[i], info, counts, tile_base, expert_w_gate, expert_w_up, expert_w_down, i)

    q, k, v, gt, qm = _ret_inproj(x2, pos_col, ln1[0], ret_w_in[0])
    shp = (B, S, MAIN_WIDTH)
    y = _retention(q.reshape(shp), k.reshape(shp), v.reshape(shp), gt.reshape(shp), ret_gn[0])
    x2 = mix_out_and_moe(0, x2, y.reshape(T, MAIN_WIDTH), mem_path(0, qm))

    k_sh, v_sh, q1, qm1 = _mla_qkv(x2, pos_col, kv_ln, kv_w_down, kv_lora_norm, kv_w_up, k_norm,
                                   ln1[1], mla_w_in[0], q_lora_norm[0], mla_w_uq[0], q_norm[0])
    kw = N_MAIN_HEADS * HEAD_PAD
    y1 = _flash(q1.reshape(B, S, kw), k_sh.reshape(B, S, kw), v_sh.reshape(shp))
    x2 = mix_out_and_moe(1, x2, y1.reshape(T, MAIN_WIDTH), mem_path(1, qm1))
    return x2.reshape(B, S, D)
```

```python
import math

import jax
import jax.numpy as jnp
from jax import lax
from jax.experimental import pallas as pl
from jax.experimental.pallas import tpu as pltpu

F32 = jnp.float32
BF16 = jnp.bfloat16

HEAD_DIM = 64
N_MAIN_HEADS = 12
MAIN_WIDTH = N_MAIN_HEADS * HEAD_DIM
N_MEM_HEADS = 4
MEM_WIDTH = N_MEM_HEADS * HEAD_DIM
RET_CHUNK = 128
ROPE_THETA = 10000.0
Q_LORA = 384
KV_LORA = 256
QK_NOPE = 64
QK_ROPE = 32
QK_HEAD = QK_NOPE + QK_ROPE
V_HEAD = 64
N_GROUPS = 4
EXPERTS_PER_GROUP = 8
N_EXPERTS = N_GROUPS * EXPERTS_PER_GROUP
D_EXPERT = 256
MOE_BLOCK = 512
EPS = 1e-6

LANES = 128
VMEM_LIMIT = 48 * 1024 * 1024
NEG_BIG = -1e30

N_PAIRS = N_MAIN_HEADS // 2
ROW_TILE = 512
MOE_TILE = 256
DMA_UNROLL = 16


def _cparams(sem):
    return pltpu.CompilerParams(dimension_semantics=sem, vmem_limit_bytes=VMEM_LIMIT)


def _dot(a, b):
    return jnp.dot(a, b, preferred_element_type=F32)


def _dot_nt(a, b):
    return lax.dot_general(a, b, (((1,), (1,)), ((), ())), preferred_element_type=F32)


def _dot_tn(a, b):
    return lax.dot_general(a, b, (((0,), (0,)), ((), ())), preferred_element_type=F32)


def _rmsnorm(xf, g):
    return xf * lax.rsqrt(jnp.mean(xf * xf, axis=-1, keepdims=True) + EPS) * g


def _rope_tables(pos_col, inv_row, sgn_row):
    ang = pos_col.astype(F32) * inv_row
    return jnp.cos(ang), jnp.sin(ang) * sgn_row


def _ret_inproj_kernel(x_ref, pos_ref, g_ref, inv_ref, sgn_ref, w_ref,
                       q_ref, k_ref, v_ref, gt_ref, qm_ref):
    h = _rmsnorm(x_ref[...], g_ref[...]).astype(BF16)
    cos, sin_s = _rope_tables(pos_ref[...], inv_ref[...], sgn_ref[...])
    first = sgn_ref[...] < 0.0
    mw = MAIN_WIDTH
    half = HEAD_DIM // 2
    qk = _dot(h, w_ref[:, :2 * mw])
    groups = range(2 * mw // LANES)
    ys = [qk[:, j * LANES:(j + 1) * LANES] for j in groups]
    fwd = [pltpu.roll(ys[j], LANES - half, 1) for j in groups]
    bwd = [pltpu.roll(ys[j], half, 1) for j in groups]
    outs = [(ys[j] * cos + jnp.where(first, fwd[j], bwd[j]) * sin_s).astype(BF16) for j in groups]
    n_q = mw // LANES
    for j in range(n_q):
        q_ref[:, j * LANES:(j + 1) * LANES] = outs[j]
        k_ref[:, j * LANES:(j + 1) * LANES] = outs[n_q + j]
    v_ref[...] = _dot(h, w_ref[:, 2 * mw:3 * mw]).astype(BF16)
    gt_ref[...] = _dot(h, w_ref[:, 3 * mw:4 * mw]).astype(BF16)
    qm_ref[...] = _dot(h, w_ref[:, 4 * mw:4 * mw + MEM_WIDTH]).astype(BF16)


def _ret_inproj(x2, pos_col, g, w_in):
    T, D = x2.shape
    tm = min(ROW_TILE, T)
    half = HEAD_DIM // 2
    inv = ROPE_THETA ** (-jnp.arange(half, dtype=F32) / half)
    lane = jnp.arange(LANES)
    inv_row = inv[lane % half][None, :]
    sgn_row = jnp.where((lane % HEAD_DIM) < half, -1.0, 1.0).astype(F32)[None, :]
    row = lambda i: (i, 0)
    fixed = lambda i: (0, 0)
    n_in = w_in.shape[1]
    outs = pl.pallas_call(
        _ret_inproj_kernel,
        grid=(T // tm,),
        in_specs=[
            pl.BlockSpec((tm, D), row),
            pl.BlockSpec((tm, 1), row),
            pl.BlockSpec((1, D), fixed),
            pl.BlockSpec((1, LANES), fixed),
            pl.BlockSpec((1, LANES), fixed),
            pl.BlockSpec((D, n_in), fixed),
        ],
        out_specs=[pl.BlockSpec((tm, MAIN_WIDTH), row)] * 4 + [pl.BlockSpec((tm, MEM_WIDTH), row)],
        out_shape=[jax.ShapeDtypeStruct((T, MAIN_WIDTH), BF16)] * 4
        + [jax.ShapeDtypeStruct((T, MEM_WIDTH), BF16)],
        compiler_params=_cparams(("parallel",)),
        name="ret_inproj",
    )(x2, pos_col, g[None, :], inv_row, sgn_row, w_in.astype(BF16))
    return outs


def _retention_tables():
    H, C, d = N_MAIN_HEADS, RET_CHUNK, HEAD_DIM
    log_g = jnp.log1p(-jnp.exp2(-5.0 - jnp.arange(H, dtype=F32)))
    idx = jnp.arange(C, dtype=F32)
    rel = idx[:, None] - idx[None, :]
    scale = d ** -0.5
    decay_in = jnp.where(rel[None] >= 0,
                         jnp.exp(log_g[:, None, None] * jnp.maximum(rel, 0.0)[None]), 0.0) * scale
    kdec = jnp.exp(log_g[None, :] * (C - 1.0 - idx)[:, None]) * scale
    qdec = jnp.exp(log_g[None, :] * (idx + 1.0)[:, None])
    cdec = jnp.exp(log_g * C)

    def lanes(t):
        return jnp.repeat(t, d, axis=1).reshape(C, N_PAIRS, 2 * d).transpose(1, 0, 2)

    head_of_lane = jnp.arange(2 * d) // d
    same = (head_of_lane[:, None] == head_of_lane[None, :]).astype(F32)
    cd_lane = jnp.repeat(cdec, d).reshape(N_PAIRS, 2 * d)
    state_decay = cd_lane[:, :, None] * same[None]
    decay_in = decay_in.reshape(N_PAIRS, 2, C, C)
    return decay_in, lanes(kdec), lanes(qdec), state_decay, same


RET_CHUNKS_PER_STEP = 4


def _retention_kernel(q_ref, k_ref, v_ref, gt_ref, dm_ref, kd_ref, qd_ref, sd_ref, same_ref,
                      gn_ref, o_ref, r_ref):
    n = pl.program_id(1)

    @pl.when(n == 0)
    def _():
        r_ref[...] = jnp.zeros_like(r_ref)

    lane = lax.broadcasted_iota(jnp.int32, (1, LANES), 1)
    head_a = lane < HEAD_DIM
    keep_a = jnp.where(head_a, 1.0, 0.0).astype(BF16)
    keep_b = jnp.where(head_a, 0.0, 1.0).astype(BF16)
    same = same_ref[...]
    avg = (same * (1.0 / HEAD_DIM)).astype(BF16)
    C = RET_CHUNK
    n_chunks = q_ref.shape[1] // C
    P = range(N_PAIRS)
    CP = [(c, p) for c in range(n_chunks) for p in P]
    rows = [pl.ds(c * C, C) for c in range(n_chunks)]
    sl = [slice(p * LANES, (p + 1) * LANES) for p in P]
    q = {cp: q_ref[0, rows[cp[0]], sl[cp[1]]] for cp in CP}
    k = {cp: k_ref[0, rows[cp[0]], sl[cp[1]]] for cp in CP}
    v = {cp: v_ref[0, rows[cp[0]], sl[cp[1]]] for cp in CP}
    s_a = {cp: _dot_nt(q[cp] * keep_a, k[cp]) * dm_ref[cp[1], 0] for cp in CP}
    s_b = {cp: _dot_nt(q[cp] * keep_b, k[cp]) * dm_ref[cp[1], 1] for cp in CP}
    u = {cp: _dot_tn((k[cp].astype(F32) * kd_ref[cp[1]]).astype(BF16), v[cp]) for cp in CP}
    qd = {cp: (q[cp].astype(F32) * qd_ref[cp[1]]).astype(BF16) for cp in CP}
    inner = {cp: _dot(s_a[cp].astype(BF16), v[cp] * keep_a) + _dot(s_b[cp].astype(BF16), v[cp] * keep_b)
             for cp in CP}
    state = [r_ref[p] for p in P]
    y = {}
    for c in range(n_chunks):
        for p in P:
            y[(c, p)] = inner[(c, p)] + _dot(qd[(c, p)], state[p].astype(BF16))
        state = [sd_ref[p] * state[p] + same * u[(c, p)] for p in P]
    for p in P:
        r_ref[p] = state[p]
    yc = {cp: y[cp] - _dot(y[cp].astype(BF16), avg) for cp in CP}
    var = {cp: _dot((yc[cp] * yc[cp]).astype(BF16), avg) for cp in CP}
    for c, p in CP:
        yn = yc[(c, p)] * lax.rsqrt(var[(c, p)] + EPS) * gn_ref[:, sl[p]]
        g = gt_ref[0, rows[c], sl[p]].astype(F32)
        o_ref[0, rows[c], sl[p]] = (g / (1.0 + jnp.exp(-g)) * yn).astype(BF16)


def _retention(q, k, v, gt, ret_gn):
    B, S, W = q.shape
    C = RET_CHUNK
    dm, kd, qd, sd, same = _retention_tables()
    tok = lambda b, n: (b, n, 0)
    fixed3 = lambda b, n: (0, 0, 0)
    step = min(RET_CHUNKS_PER_STEP * C, S)
    assert S % step == 0 and step % C == 0 and W == MAIN_WIDTH
    return pl.pallas_call(
        _retention_kernel,
        grid=(B, S // step),
        in_specs=[pl.BlockSpec((1, step, W), tok)] * 4 + [
            pl.BlockSpec((N_PAIRS, 2, C, C), lambda b, n: (0, 0, 0, 0)),
            pl.BlockSpec((N_PAIRS, C, LANES), fixed3),
            pl.BlockSpec((N_PAIRS, C, LANES), fixed3),
            pl.BlockSpec((N_PAIRS, LANES, LANES), fixed3),
            pl.BlockSpec((LANES, LANES), lambda b, n: (0, 0)),
            pl.BlockSpec((1, W), lambda b, n: (0, 0)),
        ],
        out_specs=pl.BlockSpec((1, step, W), tok),
        out_shape=jax.ShapeDtypeStruct((B, S, W), BF16),
        scratch_shapes=[pltpu.VMEM((N_PAIRS, LANES, LANES), F32)],
        compiler_params=_cparams(("parallel", "arbitrary")),
        name="retention",
    )(q, k, v, gt, dm, kd, qd, sd, same, ret_gn[None, :])


def _mem_kv_kernel(mem_ref, w_ref, kg_ref, k_ref, v_ref):
    kv = _dot(mem_ref[...].astype(BF16), w_ref[...])
    lane = lax.broadcasted_iota(jnp.int32, (1, LANES), 1)
    head_a = lane < HEAD_DIM
    inv_d = 1.0 / HEAD_DIM
    for j in range(MEM_WIDTH // LANES):
        kj = kv[:, j * LANES:(j + 1) * LANES]
        k2 = kj * kj
        ms_a = jnp.sum(jnp.where(head_a, k2, 0.0), axis=-1, keepdims=True) * inv_d
        ms_b = jnp.sum(jnp.where(head_a, 0.0, k2), axis=-1, keepdims=True) * inv_d
        kn = kj * lax.rsqrt(jnp.where(head_a, ms_a, ms_b) + EPS) * kg_ref[...]
        k_ref[:, j * LANES:(j + 1) * LANES] = kn.astype(BF16)
    v_ref[...] = kv[:, MEM_WIDTH:].astype(BF16)


def _mem_kv(mem2, w_mem_kv, k_g):
    TM, D = mem2.shape
    tm = min(ROW_TILE, TM)
    kg_row = jnp.tile(k_g, LANES // HEAD_DIM)[None, :]
    row = lambda i: (i, 0)
    fixed = lambda i: (0, 0)
    return pl.pallas_call(
        _mem_kv_kernel,
        grid=(TM // tm,),
        in_specs=[pl.BlockSpec((tm, D), row), pl.BlockSpec((D, 2 * MEM_WIDTH), fixed),
                  pl.BlockSpec((1, LANES), fixed)],
        out_specs=[pl.BlockSpec((tm, MEM_WIDTH), row)] * 2,
        out_shape=[jax.ShapeDtypeStruct((TM, MEM_WIDTH), BF16)] * 2,
        compiler_params=_cparams(("parallel",)),
        name="mem_kv",
    )(mem2, w_mem_kv.astype(BF16), kg_row)


def _mem_attn_kernel(qm_ref, k_ref, v_ref, qg_ref, o_ref):
    lane = lax.broadcasted_iota(jnp.int32, (1, LANES), 1)
    head_a = lane < HEAD_DIM
    keep = [jnp.where(head_a, 1.0, 0.0).astype(BF16), jnp.where(head_a, 0.0, 1.0).astype(BF16)]
    r_head = lax.broadcasted_iota(jnp.int32, (LANES, LANES), 0) < HEAD_DIM
    c_head = lax.broadcasted_iota(jnp.int32, (LANES, LANES), 1) < HEAD_DIM
    avg = jnp.where(r_head == c_head, 1.0 / HEAD_DIM, 0.0).astype(BF16)
    n_mem = k_ref.shape[1]
    ones = jnp.ones((n_mem, LANES), BF16)
    scale = HEAD_DIM ** -0.5
    G = range(MEM_WIDTH // LANES)
    sl = [slice(j * LANES, (j + 1) * LANES) for j in G]
    q = [qm_ref[0, :, sl[j]].astype(F32) for j in G]
    ms = [_dot((q[j] * q[j]).astype(BF16), avg) for j in G]
    qn = [(q[j] * lax.rsqrt(ms[j] + EPS) * (qg_ref[...] * scale)).astype(BF16) for j in G]
    heads = [(j, hh) for j in G for hh in range(2)]
    s = {jh: _dot_nt(qn[jh[0]] * keep[jh[1]], k_ref[0, :, sl[jh[0]]]) for jh in heads}
    p = {jh: jnp.exp(s[jh] - jnp.max(s[jh], axis=-1, keepdims=True)).astype(BF16) for jh in heads}
    den = {jh: _dot(p[jh], ones) for jh in heads}
    pv = {jh: _dot(p[jh], v_ref[0, :, sl[jh[0]]] * keep[jh[1]]) for jh in heads}
    for j in G:
        o_ref[0, :, sl[j]] = (pv[(j, 0)] / den[(j, 0)] + pv[(j, 1)] / den[(j, 1)]).astype(BF16)


def _mem_attn(qm, k_m, v_m, q_g):
    B, S, _ = qm.shape
    M = k_m.shape[1]
    tm = min(2 * ROW_TILE, S)
    qg_row = jnp.tile(q_g, LANES // HEAD_DIM)[None, :]
    return pl.pallas_call(
        _mem_attn_kernel,
        grid=(B, S // tm),
        in_specs=[pl.BlockSpec((1, tm, MEM_WIDTH), lambda b, i: (b, i, 0)),
                  pl.BlockSpec((1, M, MEM_WIDTH), lambda b, i: (b, 0, 0)),
                  pl.BlockSpec((1, M, MEM_WIDTH), lambda b, i: (b, 0, 0)),
                  pl.BlockSpec((1, LANES), lambda b, i: (0, 0))],
        out_specs=pl.BlockSpec((1, tm, MEM_WIDTH), lambda b, i: (b, i, 0)),
        out_shape=jax.ShapeDtypeStruct((B, S, MEM_WIDTH), BF16),
        compiler_params=_cparams(("parallel", "parallel")),
        name="mem_attn",
    )(qm, k_m, v_m, qg_row)


ROUTER_LANE0 = N_GROUPS
RANK_BITS = 17
RANK_RADIX = 1 << RANK_BITS


def _proj_router_kernel(x_ref, y_ref, m_ref, wy_ref, wm_ref, g_ref, w2_ref, b_ref, lower_ref,
                        o_ref, info_ref, cnt_ref, tile_base_ref, base_ref):
    x1 = x_ref[...] + _dot(y_ref[...], wy_ref[...]) + _dot(m_ref[...], wm_ref[...])
    o_ref[...] = x1
    _route_tile(x1, g_ref, w2_ref, b_ref, lower_ref, info_ref, cnt_ref, tile_base_ref, base_ref)


def _route_tile(x, g_ref, w2_ref, b_ref, lower_ref, info_ref, cnt_ref, tile_base_ref, base_ref):
    i = pl.program_id(0)

    @pl.when(i == 0)
    def _():
        base_ref[...] = jnp.zeros_like(base_ref)

    tm = x.shape[0]
    h = _rmsnorm(x, g_ref[...])
    h_hi = h.astype(BF16)
    h_lo = (h - h_hi.astype(F32)).astype(BF16)
    both = _dot(h_hi, w2_ref[...])
    logits = both[:, :LANES] + both[:, LANES:] + _dot(h_lo, w2_ref[:, :LANES]) + b_ref[...]
    lane_i = lax.broadcasted_iota(jnp.int32, (tm, LANES), 1)
    lane = lane_i.astype(F32)
    big = float(LANES)

    is_g = lane_i < N_GROUPS
    lg = jnp.where(is_g, logits, NEG_BIG)
    mg = jnp.max(lg, axis=-1, keepdims=True)
    zg = jnp.sum(jnp.where(is_g, jnp.exp(lg - mg), 0.0), axis=-1, keepdims=True)
    p_grp = 1.0 / zg
    grp = jnp.min(jnp.where(is_g & (lg == mg), lane, big), axis=-1, keepdims=True)

    e_lane = lane_i - ROUTER_LANE0
    e_grp = (e_lane >> int(math.log2(EXPERTS_PER_GROUP))).astype(F32)
    is_e = (e_lane >= 0) & (e_lane < N_EXPERTS) & (e_grp == grp)
    le = jnp.where(is_e, logits, NEG_BIG)
    me = jnp.max(le, axis=-1, keepdims=True)
    ee = jnp.where(is_e, jnp.exp(le - me), 0.0)
    prob = ee / jnp.sum(ee, axis=-1, keepdims=True)
    p1 = jnp.max(prob, axis=-1, keepdims=True)
    i1 = jnp.min(jnp.where(is_e & (prob == p1), lane, big), axis=-1, keepdims=True)
    rest = is_e & (lane != i1)
    p2 = jnp.max(jnp.where(rest, prob, -1.0), axis=-1, keepdims=True)
    i2 = jnp.min(jnp.where(rest & (prob == p2), lane, big), axis=-1, keepdims=True)
    gate1 = p_grp * p1 / (p1 + p2)
    gate2 = p_grp * p2 / (p1 + p2)

    sel1 = lane == i1
    sel2 = lane == i2
    onehot = jnp.where(sel1 | sel2, 1.0, 0.0)
    tile_base_ref[0] = base_ref[...]
    before = _dot(lower_ref[...], onehot.astype(BF16)) + base_ref[...]
    rank1 = jnp.sum(jnp.where(sel1, before, 0.0), axis=-1, keepdims=True)
    rank2 = jnp.sum(jnp.where(sel2, before, 0.0), axis=-1, keepdims=True)
    base_ref[...] += jnp.sum(onehot, axis=0, keepdims=True)
    cnt_ref[...] = base_ref[...]

    code1 = (i1 - float(ROUTER_LANE0)) * float(RANK_RADIX) + rank1
    code2 = (i2 - float(ROUTER_LANE0)) * float(RANK_RADIX) + rank2
    info = jnp.zeros((tm, LANES), F32)
    for col, val in enumerate((gate1, gate2, code1, code2)):
        info = jnp.where(lane_i == col, val, info)
    info_ref[...] = info


def _router_operands(D, tm, g, w_grp, b_grp, w_exp, b_exp):
    w = jnp.zeros((D, LANES), F32)
    w = w.at[:, :N_GROUPS].set(w_grp).at[:, ROUTER_LANE0:ROUTER_LANE0 + N_EXPERTS].set(w_exp)
    b = jnp.zeros((1, LANES), F32)
    b = b.at[0, :N_GROUPS].set(b_grp).at[0, ROUTER_LANE0:ROUTER_LANE0 + N_EXPERTS].set(b_exp)
    w_hi = w.astype(BF16)
    w_lo = (w - w_hi.astype(F32)).astype(BF16)
    lower = (jnp.arange(tm)[:, None] > jnp.arange(tm)[None, :]).astype(BF16)
    fixed = lambda i: (0, 0)
    specs = [pl.BlockSpec((1, D), fixed), pl.BlockSpec((D, 2 * LANES), fixed), pl.BlockSpec((1, LANES), fixed),
             pl.BlockSpec((tm, tm), fixed)]
    return (g[None, :], jnp.concatenate([w_hi, w_lo], axis=1), b, lower), specs


def _router_outputs(T, tm):
    n_tiles = T // tm
    specs = [pl.BlockSpec((tm, LANES), lambda i: (i, 0)), pl.BlockSpec((1, LANES), lambda i: (0, 0)),
             pl.BlockSpec((1, 1, LANES), lambda i: (i, 0, 0))]
    shapes = [jax.ShapeDtypeStruct((T, LANES), F32), jax.ShapeDtypeStruct((1, LANES), F32),
              jax.ShapeDtypeStruct((n_tiles, 1, LANES), F32)]
    return specs, shapes


def _router_tables(cnt, tile_base):
    experts = slice(ROUTER_LANE0, ROUTER_LANE0 + N_EXPERTS)
    return cnt[0, experts].astype(jnp.int32), tile_base[:, 0, experts].astype(jnp.int32)


def _out_proj_router(x2, y2, m2, w_out, g, w_grp, b_grp, w_exp, b_exp):
    T, D = x2.shape
    tm = min(ROW_TILE, T)
    row = lambda i: (i, 0)
    fixed = lambda i: (0, 0)
    w = w_out.astype(BF16)
    ops, op_specs = _router_operands(D, tm, g, w_grp, b_grp, w_exp, b_exp)
    out_specs, out_shapes = _router_outputs(T, tm)
    x1, info, cnt, tile_base = pl.pallas_call(
        _proj_router_kernel,
        grid=(T // tm,),
        in_specs=[pl.BlockSpec((tm, D), row), pl.BlockSpec((tm, MAIN_WIDTH), row),
                  pl.BlockSpec((tm, MEM_WIDTH), row),
                  pl.BlockSpec((MAIN_WIDTH, D), fixed), pl.BlockSpec((MEM_WIDTH, D), fixed)] + op_specs,
        out_specs=[pl.BlockSpec((tm, D), row)] + out_specs,
        out_shape=[jax.ShapeDtypeStruct((T, D), F32)] + out_shapes,
        scratch_shapes=[pltpu.VMEM((1, LANES), F32)],
        compiler_params=_cparams(("arbitrary",)),
        name="out_proj_router",
    )(x2, y2, m2, w[:MAIN_WIDTH], w[MAIN_WIDTH:], *ops)
    return (x1, info) + _router_tables(cnt, tile_base)


TOK_ROWS = 4
U32 = jnp.uint32


def _tok_rows(r, n=1):
    start = r * TOK_ROWS
    if not isinstance(start, int):
        start = pl.multiple_of(start, TOK_ROWS)
    return pl.ds(start, n * TOK_ROWS)


def _pack_rows(h):
    bits = lax.bitcast_convert_type(h.astype(BF16).astype(F32), U32)
    half = h.shape[1] // 2
    return (bits[:, :half] >> 16) | bits[:, half:]


def _unpack_words(w):
    return (lax.bitcast_convert_type(w << 16, F32), lax.bitcast_convert_type(w & U32(0xFFFF0000), F32))


def _store_token_rows(ref, first_row, n, words):
    for c in range(TOK_ROWS):
        ref[pl.ds(first_row * TOK_ROWS + c, n, stride=TOK_ROWS), :] = words[:, c * LANES:(c + 1) * LANES]


def _load_token_rows(ref, n):
    parts = [_unpack_words(ref[pl.ds(c, n, stride=TOK_ROWS), :]) for c in range(TOK_ROWS)]
    return jnp.concatenate([p[0] for p in parts] + [p[1] for p in parts], axis=-1)


def _segment_copies(n, src, src_row, dst, dst_row, sem, top, op, grain=1):
    groups = (n + (grain - 1)) >> int(math.log2(grain))
    off = 0
    bit = top // grain
    while bit >= 1:
        take = groups & bit

        @pl.when(take != 0)
        def _(rows=bit * grain, off=off, priority=int(math.log2(bit)) % 2):
            op(pltpu.make_async_copy(src.at[_tok_rows(src_row + off, rows), :],
                                     dst.at[_tok_rows(dst_row + off, rows), :], sem), priority)
        off = off + take * grain
        bit //= 2


ENTRIES_PER_ITER = 2 * DMA_UNROLL


def _local_rows(code_ref, a_ref, tbl, first_tok, dvec_ref, drow_ref, dsem):
    code = code_ref[...]
    e = code >> RANK_BITS
    d = (code & (RANK_RADIX - 1)) + first_tok
    for k in range(N_EXPERTS):
        d = d + jnp.where(e == k, a_ref[tbl + k], 0)
    dvec_ref[...] = d * TOK_ROWS
    copies = [pltpu.make_async_copy(dvec_ref.at[r], drow_ref.at[pl.ds(r * LANES, LANES)], dsem)
              for r in range(dvec_ref.shape[0])]
    for cp in copies:
        cp.start()
    for cp in copies:
        cp.wait()


def _run_copies(op, step, to_sorted, n_ref, ls_ref, gb_ref, local_ref, first_tok, sorted_ref, sem, top,
                grain=1):
    tbl = step * N_EXPERTS

    def run(e, c):
        loc = (local_ref, first_tok + ls_ref[tbl + e])
        glob = (sorted_ref, gb_ref[tbl + e])
        (src, src_row), (dst, dst_row) = (loc, glob) if to_sorted else (glob, loc)
        _segment_copies(n_ref[tbl + e], src, src_row, dst, dst_row, sem, top, op, grain)
        return c
    lax.fori_loop(0, N_EXPERTS, run, 0)


def _rows_at(first_row):
    return pl.ds(pl.multiple_of(first_row, TOK_ROWS), TOK_ROWS)


def _dma_start(cp, priority=0):
    cp.start(priority=priority)


def _dma_wait(cp, priority=0):
    cp.wait()


def _block_copy(src, dst, blk, sem):
    return pltpu.make_async_copy(src, dst.at[_tok_rows(blk * MOE_BLOCK, MOE_BLOCK), :], sem)


def _dispatch_kernel(a_ref, n_ref, ls_ref, gb_ref, zrow_ref, zcnt_ref, nblk_ref,
                     code_ref, x_ref, g_ref, xs_ref, hbuf_ref, cbuf_ref, zbuf_ref, dvec_ref, drow_ref,
                     sem, zsem, dsem):
    s = pl.program_id(0)
    ts = x_ref.shape[0]

    def zero_fill(op):
        n_blocks = xs_ref.shape[0] // (MOE_BLOCK * TOK_ROWS)

        def tail(b, c):
            op(_block_copy(zbuf_ref, xs_ref, b, zsem))
            return c
        lax.fori_loop(nblk_ref[0], n_blocks, tail, 0)

        def pad(e, c):
            _segment_copies(zcnt_ref[e], zbuf_ref, 0, xs_ref, zrow_ref[e], zsem, MOE_BLOCK // 2, op)
            return c
        lax.fori_loop(0, N_EXPERTS, pad, 0)

    @pl.when(s == 0)
    def _():
        zbuf_ref[...] = jnp.zeros_like(zbuf_ref)
        zero_fill(_dma_start)

    for sub in range(ts // MOE_TILE):
        h = _rmsnorm(x_ref[pl.ds(sub * MOE_TILE, MOE_TILE), :], g_ref[...])
        _store_token_rows(hbuf_ref, sub * MOE_TILE, MOE_TILE, _pack_rows(h))

    slot = s % 2
    first_tok = slot * (2 * ts)
    last = pl.num_programs(0) - 1

    def start_runs(step, step_slot):
        _run_copies(_dma_start, step, True, n_ref, ls_ref, gb_ref, cbuf_ref, step_slot * (2 * ts), xs_ref,
                    sem.at[step_slot], ts)

    def wait_runs(step_slot):
        pltpu.make_async_copy(cbuf_ref.at[_tok_rows(step_slot * (2 * ts), 2 * ts), :],
                              xs_ref.at[_tok_rows(0, 2 * ts), :], sem.at[step_slot]).wait()

    @pl.when(s >= 2)
    def _():
        wait_runs(slot)

    _local_rows(code_ref, a_ref, s * N_EXPERTS, first_tok, dvec_ref, drow_ref, dsem)

    def place(tb, c):
        first = tb * ENTRIES_PER_ITER
        for u in range(DMA_UNROLL):
            tile = hbuf_ref[_tok_rows(tb * DMA_UNROLL + u), :]
            for kk in range(2):
                cbuf_ref[_rows_at(drow_ref[first + u * 2 + kk]), :] = tile
        return c
    lax.fori_loop(0, ts // DMA_UNROLL, place, 0)

    start_runs(s, slot)

    @pl.when(s == last)
    def _():
        @pl.when(s >= 1)
        def _():
            wait_runs(1 - slot)
        wait_runs(slot)
        zero_fill(_dma_wait)


def _dispatch(x2, g, codes, tables, zrow, zcnt, nblk_used, n_rows, ts):
    T, D = x2.shape
    return pl.pallas_call(
        _dispatch_kernel,
        grid_spec=pltpu.PrefetchScalarGridSpec(
            num_scalar_prefetch=7,
            grid=(T // ts,),
            in_specs=[pl.BlockSpec((ts * 2 // LANES, LANES), lambda i, *_: (i, 0)),
                      pl.BlockSpec((ts, D), lambda i, *_: (i, 0)),
                      pl.BlockSpec((1, D), lambda i, *_: (0, 0))],
            out_specs=pl.BlockSpec(memory_space=pl.ANY),
            scratch_shapes=[pltpu.VMEM((ts * TOK_ROWS, LANES), U32),
                            pltpu.VMEM((2 * 2 * ts * TOK_ROWS, LANES), U32),
                            pltpu.VMEM((MOE_BLOCK * TOK_ROWS, LANES), U32),
                            pltpu.VMEM((ts * 2 // LANES, LANES), jnp.int32),
                            pltpu.SMEM((ts * 2,), jnp.int32),
                            pltpu.SemaphoreType.DMA((2,)), pltpu.SemaphoreType.DMA(()),
                            pltpu.SemaphoreType.DMA(())],
        ),
        out_shape=jax.ShapeDtypeStruct((n_rows * TOK_ROWS, LANES), U32),
        compiler_params=_cparams(("arbitrary",)),
        name="moe_dispatch",
    )(*tables, zrow, zcnt, nblk_used, codes, x2, g[None, :])


def _expert_kernel(blk_e_ref, nblk_ref, xs_ref, wg_ref, wu_ref, wd_ref, ys_ref, wg_s, wu_s, wd_s):
    b = pl.program_id(0)

    @pl.when((b == 0) | (blk_e_ref[b] != blk_e_ref[jnp.maximum(b - 1, 0)]))
    def _():
        wg_s[...] = wg_ref[0, 0].astype(BF16)
        wu_s[...] = wu_ref[0, 0].astype(BF16)
        wd_s[...] = wd_ref[0, 0].astype(BF16)

    @pl.when(b < nblk_ref[0])
    def _():
        x = _load_token_rows(xs_ref, MOE_BLOCK).astype(BF16)
        a = _dot(x, wg_s[...])
        u = _dot(x, wu_s[...])
        hid = (a / (1.0 + jnp.exp(-a)) * u).astype(BF16)
        y = _dot(hid, wd_s[...])
        _store_token_rows(ys_ref, 0, MOE_BLOCK, _pack_rows(y))

    @pl.when(b >= nblk_ref[0])
    def _():
        ys_ref[...] = jnp.zeros_like(ys_ref)


def _experts(xs, blk_e, nblk_used, w_gate, w_up, w_down, layer):
    rows = xs.shape[0]
    nblk = rows // (MOE_BLOCK * TOK_ROWS)
    D = w_gate.shape[2]
    blk = lambda b, be, nb: (jnp.minimum(b, nb[0] - 1), 0)
    out_blk = lambda b, be, nb: (b, 0)
    wsel = lambda b, be, nb: (layer, be[b], 0, 0)
    return pl.pallas_call(
        _expert_kernel,
        grid_spec=pltpu.PrefetchScalarGridSpec(
            num_scalar_prefetch=2,
            grid=(nblk,),
            in_specs=[pl.BlockSpec((MOE_BLOCK * TOK_ROWS, LANES), blk),
                      pl.BlockSpec((1, 1, D, D_EXPERT), wsel),
                      pl.BlockSpec((1, 1, D, D_EXPERT), wsel),
                      pl.BlockSpec((1, 1, D_EXPERT, D), wsel)],
            out_specs=pl.BlockSpec((MOE_BLOCK * TOK_ROWS, LANES), out_blk),
            scratch_shapes=[pltpu.VMEM((D, D_EXPERT), BF16), pltpu.VMEM((D, D_EXPERT), BF16),
                            pltpu.VMEM((D_EXPERT, D), BF16)],
        ),
        out_shape=jax.ShapeDtypeStruct((rows, LANES), U32),
        compiler_params=_cparams(("arbitrary",)),
        name="moe_experts",
    )(blk_e, nblk_used, xs, w_gate, w_up, w_down)


def _combine_kernel(a_ref, n_ref, ls_ref, gb_ref, tot_ref, code_ref, x_ref, info_ref, ys_ref, o_ref,
                    ybuf_ref, pick0_ref, pick1_ref, dvec_ref, drow_ref, sem, dsem):
    s = pl.program_id(0)
    ts = x_ref.shape[0]
    slot = s % 2
    slot_rows = ybuf_ref.shape[0] // (2 * TOK_ROWS)
    first_tok = slot * slot_rows

    def start_fetch(step, step_slot):
        _run_copies(_dma_start, step, False, n_ref, ls_ref, gb_ref, ybuf_ref, step_slot * slot_rows, ys_ref,
                    sem.at[step_slot], ts, FETCH_GRAIN)

    @pl.when(s == 0)
    def _():
        start_fetch(0, 0)

    @pl.when(s + 1 < pl.num_programs(0))
    def _():
        start_fetch(s + 1, 1 - slot)

    _segment_copies(tot_ref[s], ys_ref, 0, ybuf_ref, first_tok, sem.at[slot], 2 * ts, _dma_wait, FETCH_GRAIN)
    _local_rows(code_ref, a_ref, s * N_EXPERTS, first_tok, dvec_ref, drow_ref, dsem)

    picks = (pick0_ref, pick1_ref)
    for sub in range(ts // MOE_TILE):
        def pick(tb, c, sub=sub):
            first = (sub * (MOE_TILE // DMA_UNROLL) + tb) * ENTRIES_PER_ITER
            for u in range(DMA_UNROLL):
                for kk in range(2):
                    picks[kk][_tok_rows(tb * DMA_UNROLL + u), :] = (
                        ybuf_ref[_rows_at(drow_ref[first + u * 2 + kk]), :])
            return c
        lax.fori_loop(0, MOE_TILE // DMA_UNROLL, pick, 0)

        rows = pl.ds(sub * MOE_TILE, MOE_TILE)
        info = info_ref[rows, :]
        g0 = info[:, 0:1]
        g1 = info[:, 1:2]
        half = x_ref.shape[1] // 2
        for c in range(TOK_ROWS):
            y0 = _unpack_words(pick0_ref[pl.ds(c, MOE_TILE, stride=TOK_ROWS), :])
            y1 = _unpack_words(pick1_ref[pl.ds(c, MOE_TILE, stride=TOK_ROWS), :])
            for part in range(2):
                sl = slice(part * half + c * LANES, part * half + (c + 1) * LANES)
                o_ref[rows, sl] = x_ref[rows, sl] + (y0[part] * g0 + y1[part] * g1)


def _combine(x2, info, ys, codes, tables, ts):
    T, D = x2.shape
    return pl.pallas_call(
        _combine_kernel,
        grid_spec=pltpu.PrefetchScalarGridSpec(
            num_scalar_prefetch=5,
            grid=(T // ts,),
            in_specs=[pl.BlockSpec((ts * 2 // LANES, LANES), lambda i, *_: (i, 0)),
                      pl.BlockSpec((ts, D), lambda i, *_: (i, 0)),
                      pl.BlockSpec((ts, LANES), lambda i, *_: (i, 0)),
                      pl.BlockSpec(memory_space=pl.ANY)],
            out_specs=pl.BlockSpec((ts, D), lambda i, *_: (i, 0)),
            scratch_shapes=[pltpu.VMEM((2 * (2 * ts + N_EXPERTS * FETCH_GRAIN) * TOK_ROWS, LANES), U32),
                            pltpu.VMEM((MOE_TILE * TOK_ROWS, LANES), U32),
                            pltpu.VMEM((MOE_TILE * TOK_ROWS, LANES), U32),
                            pltpu.VMEM((ts * 2 // LANES, LANES), jnp.int32),
                            pltpu.SMEM((ts * 2,), jnp.int32),
                            pltpu.SemaphoreType.DMA((2,)), pltpu.SemaphoreType.DMA(())],
        ),
        out_shape=jax.ShapeDtypeStruct((T, D), F32),
        compiler_params=_cparams(("arbitrary",)),
        name="moe_combine",
    )(*tables, codes, x2, info, ys)


def _supertile_tables(tile_base, counts, pad_start, ts, grain=1):
    per = ts // ROW_TILE
    base = tile_base[::per]
    nxt = jnp.concatenate([base[1:], counts[None, :]], axis=0)
    n = nxt - base
    room = (n + grain - 1) // grain * grain
    lstart = jnp.cumsum(room, axis=1) - room
    flat = lambda a: a.reshape(-1).astype(jnp.int32)
    return (flat(lstart - base), flat(n), flat(lstart), flat(pad_start[None, :] + base),
            flat(jnp.sum(room, axis=1)))


DISPATCH_TOKENS = 2048
COMBINE_TOKENS = 1024
FETCH_GRAIN = 64


def _moe_apply(x2, ln2, info, counts, tile_base, w_gate, w_up, w_down, layer):
    T, D = x2.shape
    padded = (counts + MOE_BLOCK - 1) // MOE_BLOCK * MOE_BLOCK
    pad_end = jnp.cumsum(padded)
    pad_start = pad_end - padded
    codes = info[:, 2:4].astype(jnp.int32).reshape(T * 2 // LANES, LANES)
    n_rows = T * 2 + (N_EXPERTS + 1) * MOE_BLOCK
    nblk = n_rows // MOE_BLOCK
    blk_row = jnp.arange(nblk, dtype=jnp.int32) * MOE_BLOCK
    blk_e = jnp.minimum(jnp.sum((pad_end[None, :] <= blk_row[:, None]).astype(jnp.int32), axis=1),
                        N_EXPERTS - 1).astype(jnp.int32)
    nblk_used = (pad_end[-1:] // MOE_BLOCK).astype(jnp.int32)
    zrow = (pad_start + counts).astype(jnp.int32)
    zcnt = (padded - counts).astype(jnp.int32)

    td = min(DISPATCH_TOKENS, T)
    tc = min(COMBINE_TOKENS, T)
    assert D == 2 * TOK_ROWS * LANES and 2 * T <= RANK_RADIX
    assert T % td == 0 and T % tc == 0 and td % ROW_TILE == 0 and tc % ROW_TILE == 0 and tc % MOE_TILE == 0

    xs = _dispatch(x2, ln2, codes, _supertile_tables(tile_base, counts, pad_start, td)[:4],
                   zrow, zcnt, nblk_used, n_rows, td)
    ys = _experts(xs, blk_e, nblk_used, w_gate, w_up, w_down, layer)
    return _combine(x2, info, ys, codes, _supertile_tables(tile_base, counts, pad_start, tc, FETCH_GRAIN), tc)


HEAD_PAD = LANES
LATENT_PAD = 3 * LANES


def _rot_partner():
    half = QK_ROPE // 2
    r = jnp.arange(QK_ROPE)
    return jnp.where(r < half, r + half, r - half), jnp.where(r < half, -1.0, 1.0).astype(F32)


def _mla_rope_rows():
    half = QK_ROPE // 2
    inv = ROPE_THETA ** (-jnp.arange(half, dtype=F32) / half)
    lane = jnp.arange(LANES)
    r = lane - QK_NOPE
    in_rope = (r >= 0) & (r < QK_ROPE)
    inv_row = jnp.where(in_rope, inv[jnp.clip(r, 0, QK_ROPE - 1) % half], 0.0)[None, :]
    rope_row = in_rope.astype(F32)[None, :]
    real_row = (lane < QK_HEAD).astype(F32)[None, :]
    return inv_row, rope_row, real_row


def _head_gain_row(g):
    partner, _ = _rot_partner()
    return jnp.concatenate([g, g[QK_NOPE + partner]])[None, :]


def _with_partner_cols(w3):
    partner, sign = _rot_partner()
    rot = w3[:, :, QK_NOPE + partner] * sign
    return jnp.concatenate([w3, rot], axis=-1).reshape(w3.shape[0], N_MAIN_HEADS * HEAD_PAD)


def _heads_norm_rope(ys, gain_rows, scales, real_row, cos_real, sin_rope):
    n = range(len(ys))
    row_id = lax.broadcasted_iota(jnp.int32, (LANES, LANES), 0)
    ones_real = jnp.where(row_id < QK_HEAD, 1.0, 0.0).astype(BF16)
    ms = [_dot((ys[i] * ys[i]).astype(BF16), ones_real) * (1.0 / QK_HEAD) for i in n]
    yn = [ys[i] * (lax.rsqrt(ms[i] + EPS) * scales[i]) * gain_rows[i] for i in n]
    rolled = [pltpu.roll(yn[i], LANES - QK_ROPE, 1) for i in n]
    return [(yn[i] * cos_real + rolled[i] * sin_rope).astype(BF16) for i in n]


def _mla_qkv_kernel(x_ref, pos_ref, lnkv_ref, lnq_ref, wd_ref, scale_ref, wkv_ref, kg_ref,
                    win_ref, qlg_ref, wuq_ref, qg_ref, inv_ref, rope_ref, real_ref,
                    k_ref, v_ref, q_ref, qm_ref):
    x = x_ref[...]
    xr = x * lax.rsqrt(jnp.mean(x * x, axis=-1, keepdims=True) + EPS)
    ang = pos_ref[...].astype(F32) * inv_ref[...]
    real_row = real_ref[...]
    cos_real = jnp.cos(ang) * real_row
    sin_rope = jnp.sin(ang) * rope_ref[...]

    ckr = _dot((xr * lnkv_ref[...]).astype(BF16), wd_ref[...])
    c = ckr[:, :KV_LORA]
    r = lax.rsqrt(jnp.mean(c * c, axis=-1, keepdims=True) + EPS)
    lane = lax.broadcasted_iota(jnp.int32, (1, LATENT_PAD), 1)
    lhs = (ckr * jnp.where(lane < KV_LORA, r * scale_ref[...], 1.0)).astype(BF16)
    kv = _dot(lhs, wkv_ref[...])
    proj = _dot((xr * lnq_ref[...]).astype(BF16), win_ref[...])
    cq = _rmsnorm(proj[:, :Q_LORA], qlg_ref[...]).astype(BF16)
    q = _dot(cq, wuq_ref[...])
    qm_ref[...] = proj[:, Q_LORA:].astype(BF16)
    v_ref[...] = kv[:, N_MAIN_HEADS * HEAD_PAD:].astype(BF16)
    sls = [slice(hh * HEAD_PAD, (hh + 1) * HEAD_PAD) for hh in range(N_MAIN_HEADS)]
    nh = N_MAIN_HEADS
    outs = _heads_norm_rope([kv[:, sl] for sl in sls] + [q[:, sl] for sl in sls],
                            [kg_ref[...]] * nh + [qg_ref[...]] * nh,
                            [1.0] * nh + [QK_HEAD ** -0.5] * nh, real_row, cos_real, sin_rope)
    for hh, sl in enumerate(sls):
        k_ref[:, sl] = outs[hh]
        q_ref[:, sl] = outs[nh + hh]


def _mla_qkv(x2, pos_col, kv_ln, w_dkv, kv_lora_g, w_ukv, k_g, ln1, w_in, q_lora_g, w_uq, q_g):
    T, D = x2.shape
    tm = min(ROW_TILE, T)
    lat = LATENT_PAD
    wd = jnp.pad(w_dkv, ((0, 0), (0, lat - w_dkv.shape[1]))).astype(BF16)
    scale_row = jnp.pad(kv_lora_g, (0, lat - KV_LORA), constant_values=1.0)[None, :]
    w3 = w_ukv.reshape(KV_LORA, N_MAIN_HEADS, QK_NOPE + V_HEAD)
    wk = jnp.zeros((lat, N_MAIN_HEADS, QK_HEAD), F32)
    wk = wk.at[:KV_LORA, :, :QK_NOPE].set(w3[:, :, :QK_NOPE])
    eye = jnp.eye(QK_ROPE, dtype=F32)
    wk = wk.at[KV_LORA:KV_LORA + QK_ROPE, :, QK_NOPE:].set(
        jnp.broadcast_to(eye[:, None, :], (QK_ROPE, N_MAIN_HEADS, QK_ROPE)))
    wv = jnp.zeros((lat, N_MAIN_HEADS * V_HEAD), F32)
    wv = wv.at[:KV_LORA].set(w3[:, :, QK_NOPE:].reshape(KV_LORA, N_MAIN_HEADS * V_HEAD))
    wkv = jnp.concatenate([_with_partner_cols(wk), wv], axis=1).astype(BF16)
    wuq = _with_partner_cols(w_uq.reshape(Q_LORA, N_MAIN_HEADS, QK_HEAD)).astype(BF16)
    inv_row, rope_row, real_row = _mla_rope_rows()
    row = lambda i: (i, 0)
    fixed = lambda i: (0, 0)
    kw = N_MAIN_HEADS * HEAD_PAD
    n_in = w_in.shape[1]
    lane_row = pl.BlockSpec((1, LANES), fixed)
    return pl.pallas_call(
        _mla_qkv_kernel,
        grid=(T // tm,),
        in_specs=[pl.BlockSpec((tm, D), row), pl.BlockSpec((tm, 1), row),
                  pl.BlockSpec((1, D), fixed), pl.BlockSpec((1, D), fixed),
                  pl.BlockSpec((D, lat), fixed), pl.BlockSpec((1, lat), fixed),
                  pl.BlockSpec((lat, kw + MAIN_WIDTH), fixed), lane_row,
                  pl.BlockSpec((D, n_in), fixed), pl.BlockSpec((1, Q_LORA), fixed),
                  pl.BlockSpec((Q_LORA, kw), fixed), lane_row,
                  lane_row, lane_row, lane_row],
        out_specs=[pl.BlockSpec((tm, kw), row), pl.BlockSpec((tm, MAIN_WIDTH), row),
                   pl.BlockSpec((tm, kw), row), pl.BlockSpec((tm, MEM_WIDTH), row)],
        out_shape=[jax.ShapeDtypeStruct((T, kw), BF16), jax.ShapeDtypeStruct((T, MAIN_WIDTH), BF16),
                   jax.ShapeDtypeStruct((T, kw), BF16), jax.ShapeDtypeStruct((T, MEM_WIDTH), BF16)],
        compiler_params=_cparams(("parallel",)),
        name="mla_qkv",
    )(x2, pos_col, kv_ln[None, :], ln1[None, :], wd, scale_row, wkv, _head_gain_row(k_g),
      w_in.astype(BF16), q_lora_g[None, :], wuq, _head_gain_row(q_g), inv_row, rope_row, real_row)


ATTN_TILE = 1024
ATTN_PAIRS_PER_STEP = 3


def _flash_kernel(qi_ref, kj_ref, q_ref, k_ref, v_ref, o_ref, m_ref, acc_ref):
    t = pl.program_id(2)
    i = qi_ref[t]
    j = kj_ref[t]
    tq = q_ref.shape[1]
    tk = k_ref.shape[1]

    @pl.when(j == 0)
    def _():
        m_ref[...] = jnp.full_like(m_ref, NEG_BIG)
        acc_ref[...] = jnp.zeros_like(acc_ref)

    lane = lax.broadcasted_iota(jnp.int32, (1, LANES), 1)
    head_a = lane < V_HEAD
    den_lane = (V_HEAD, 0)

    def attend(pair, q0, nq, nk, diag_col):
        rows = pl.ds(q0, nq)
        v = v_ref[0, pl.ds(0, nk), pair * LANES:(pair + 1) * LANES]
        if diag_col is not None:
            q_idx = lax.broadcasted_iota(jnp.int32, (nq, nk), 0) + diag_col
            k_idx = lax.broadcasted_iota(jnp.int32, (nq, nk), 1)
            visible = k_idx <= q_idx
        H = range(2)
        st = [2 * pair + hh for hh in H]
        sl = [slice(h * HEAD_PAD, (h + 1) * HEAD_PAD) for h in st]
        s = [_dot_nt(q_ref[0, rows, sl[hh]], k_ref[0, pl.ds(0, nk), sl[hh]]) for hh in H]
        if diag_col is not None:
            s = [jnp.where(visible, s[hh], NEG_BIG) for hh in H]
        m_prev = [m_ref[st[hh], rows, :] for hh in H]
        acc_prev = [acc_ref[st[hh], rows, :] for hh in H]
        m_new = [jnp.maximum(m_prev[hh], jnp.max(s[hh], axis=-1, keepdims=True)) for hh in H]
        alpha = [jnp.exp(m_prev[hh] - m_new[hh]) for hh in H]
        m_wide = [jnp.concatenate([m_new[hh]] * (nk // LANES), axis=1) for hh in H]
        p = [jnp.exp((s[hh] - m_wide[hh]).astype(BF16)) for hh in H]
        keep_row = [jnp.where(head_a, 1.0, 0.0).astype(BF16), jnp.where(head_a, 0.0, 1.0).astype(BF16)]
        den_row = [jnp.where(lane == den_lane[hh], 1.0, 0.0).astype(BF16) for hh in H]
        pv = [_dot(p[hh], v * keep_row[hh] + den_row[hh]) for hh in H]
        for hh in H:
            acc_ref[st[hh], rows, :] = alpha[hh] * acc_prev[hh] + pv[hh]
            m_ref[st[hh], rows, :] = m_new[hh]

    n_pairs = v_ref.shape[2] // LANES

    @pl.when(j < i)
    def _():
        for pair in range(n_pairs):
            attend(pair, 0, tq, tk, None)

    @pl.when(j == i)
    def _():
        half = tq // 2
        for pair in range(n_pairs):
            attend(pair, 0, half, half, 0)
            attend(pair, half, half, tk, half)
            acc_a = acc_ref[2 * pair]
            acc_b = acc_ref[2 * pair + 1]
            out_a = acc_a / acc_a[:, den_lane[0]:den_lane[0] + 1]
            out_b = acc_b / acc_b[:, den_lane[1]:den_lane[1] + 1]
            o_ref[0, :, pair * LANES:(pair + 1) * LANES] = jnp.where(head_a, out_a, out_b).astype(BF16)


def _flash(q, k, v):
    B, S, _ = q.shape
    t = min(ATTN_TILE, S)
    n = S // t
    g = ATTN_PAIRS_PER_STEP
    assert S % t == 0 and t % (2 * LANES) == 0 and N_PAIRS % g == 0
    pairs = [(i, j) for i in range(n) for j in range(i + 1)]
    qi = jnp.array([p[0] for p in pairs], jnp.int32)
    kj = jnp.array([p[1] for p in pairs], jnp.int32)
    return pl.pallas_call(
        _flash_kernel,
        grid_spec=pltpu.PrefetchScalarGridSpec(
            num_scalar_prefetch=2,
            grid=(B, N_PAIRS // g, len(pairs)),
            in_specs=[pl.BlockSpec((1, t, 2 * g * HEAD_PAD), lambda b, p, s, qi, kj: (b, qi[s], p)),
                      pl.BlockSpec((1, t, 2 * g * HEAD_PAD), lambda b, p, s, qi, kj: (b, kj[s], p)),
                      pl.BlockSpec((1, t, g * LANES), lambda b, p, s, qi, kj: (b, kj[s], p))],
            out_specs=pl.BlockSpec((1, t, g * LANES), lambda b, p, s, qi, kj: (b, qi[s], p)),
            scratch_shapes=[pltpu.VMEM((2 * g, t, LANES), F32), pltpu.VMEM((2 * g, t, LANES), F32)],
        ),
        out_shape=jax.ShapeDtypeStruct((B, S, MAIN_WIDTH), BF16),
        compiler_params=_cparams(("parallel", "parallel", "arbitrary")),
        name="mla_flash",
    )(qi, kj, q, k, v)


def kernel(x, mem, positions, ln1, ln2, w_out, mem_w_kv, mem_q_norm, mem_k_norm, router_group_w, router_group_b, router_expert_w, router_expert_b, expert_w_gate, expert_w_up, expert_w_down, ret_w_in, ret_gn, kv_ln, kv_w_down, kv_lora_norm, kv_w_up, k_norm, mla_w_in, q_lora_norm, mla_w_uq, q_norm):
    B, S, D = x.shape
    M = mem.shape[1]
    T = B * S
    assert T % ROW_TILE == 0 and (B * M) % ROW_TILE == 0 and S % ROW_TILE == 0 and D % LANES == 0
    x2 = x.reshape(T, D)
    mem2 = mem.reshape(B * M, D)
    pos_col = positions.reshape(T, 1).astype(jnp.int32)

    def mem_path(i, qm):
        k_m, v_m = _mem_kv(mem2, mem_w_kv[i], mem_k_norm[i])
        return _mem_attn(qm.reshape(B, S, MEM_WIDTH), k_m.reshape(B, M, MEM_WIDTH),
                         v_m.reshape(B, M, MEM_WIDTH), mem_q_norm[i]).reshape(T, MEM_WIDTH)

    def mix_out_and_moe(i, xin, y, m):
        x1, info, counts, tile_base = _out_proj_router(
            xin, y, m, w_out[i], ln2[i], router_group_w[i], router_group_b[i], router_expert_w[i],
            router_expert_b[i])
        return _moe_apply(x1, ln2[i], info, counts, tile_base, expert_w_gate, expert_w_up, expert_w_down, i)

    q, k, v, gt, qm = _ret_inproj(x2, pos_col, ln1[0], ret_w_in[0])
    shp = (B, S, MAIN_WIDTH)
    y = _retention(q.reshape(shp), k.reshape(shp), v.reshape(shp), gt.reshape(shp), ret_gn[0])
    x2 = mix_out_and_moe(0, x2, y.reshape(T, MAIN_WIDTH), mem_path(0, qm))

    k_sh, v_sh, q1, qm1 = _mla_qkv(x2, pos_col, kv_ln, kv_w_down, kv_lora_norm, kv_w_up, k_norm,
                                   ln1[1], mla_w_in[0], q_lora_norm[0], mla_w_uq[0], q_norm[0])
    kw = N_MAIN_HEADS * HEAD_PAD
    y1 = _flash(q1.reshape(B, S, kw), k_sh.reshape(B, S, kw), v_sh.reshape(shp))
    x2 = mix_out_and_moe(1, x2, y1.reshape(T, MAIN_WIDTH), mem_path(1, qm1))
    return x2.reshape(B, S, D)
```
